```python
import math
import jax, jax.numpy as jnp
from jax import lax
import numpy as np

D_MODEL = 1024
BATCH = 8
SEQ = 2048
DEPTH = 1

D_CONV = D_MODEL
CONV_K = 3
HEAD_DIM = 64
N_HEADS = 16
N_KV_HEADS = 4
GROUP = N_HEADS // N_KV_HEADS
D_ATTN = N_HEADS * HEAD_DIM
D_KV = N_KV_HEADS * HEAD_DIM
WINDOW = 128
BLOCK = 128
ROT_DIM = HEAD_DIM // 4
ROPE_THETA = 500000.0
ATTN_SCALE = 1.0 / math.sqrt(HEAD_DIM)
NEG_INF = -1e30
D_FF = ((8 * D_MODEL // 3 + 255) // 256) * 256
EPS = 1e-5

IN_WIDTHS = (D_CONV, D_CONV, D_CONV, D_ATTN, D_KV, D_KV, D_MODEL, D_MODEL)
N_IN = sum(IN_WIDTHS)
SPLIT_POINTS = tuple(int(v) for v in np.cumsum(IN_WIDTHS)[:-1])

kernel_name = "hybrid_gated_conv_swa_sink_block"


def rms_norm(x, g):
    xf = x.astype(jnp.float32)
    y = xf * lax.rsqrt(jnp.mean(xf * xf, axis=-1, keepdims=True) + EPS)
    return (y * g.astype(jnp.float32)).astype(x.dtype)


def rotary_tables(seq, dtype):
    inv_freq = ROPE_THETA ** (-jnp.arange(0, ROT_DIM, 2, dtype=jnp.float32) / ROT_DIM)
    ang = jnp.arange(seq, dtype=jnp.float32)[:, None] * inv_freq[None, :]
    return jnp.cos(ang).astype(dtype), jnp.sin(ang).astype(dtype)


def partial_rotary(t, cos, sin):
    rot, rest = t[..., :ROT_DIM], t[..., ROT_DIM:]
    r1, r2 = rot[..., :ROT_DIM // 2], rot[..., ROT_DIM // 2:]
    c = cos[None, :, None, :]
    s = sin[None, :, None, :]
    rot = jnp.concatenate([r1 * c - r2 * s, r2 * c + r1 * s], axis=-1)
    return jnp.concatenate([rot, rest], axis=-1)


def causal_depthwise_conv(u, w):
    rhs = w[:, None, :].astype(u.dtype)
    return lax.conv_general_dilated(
        u, rhs, window_strides=(1,), padding=[(CONV_K - 1, 0)],
        dimension_numbers=('NWC', 'WIO', 'NWC'), feature_group_count=u.shape[-1])


def sliding_window_sink_attention(q, k, v, sinks):
    b, s = q.shape[0], q.shape[1]
    nb = s // BLOCK
    qb = q.reshape(b, nb, BLOCK, N_KV_HEADS, GROUP, HEAD_DIM)

    def band(t):
        tp = jnp.pad(t, ((0, 0), (BLOCK, 0), (0, 0), (0, 0)))
        tp = tp.reshape(b, nb + 1, BLOCK, N_KV_HEADS, HEAD_DIM)
        return jnp.concatenate([tp[:, :-1], tp[:, 1:]], axis=2)

    kb, vb = band(k), band(v)
    scores = jnp.einsum('bnqhgd,bnkhd->bnhgqk', qb, kb,
                        preferred_element_type=jnp.float32) * ATTN_SCALE
    qi = jnp.arange(BLOCK)[:, None]
    kj = jnp.arange(2 * BLOCK)[None, :]
    rel = qi + BLOCK - kj
    in_window = (rel >= 0) & (rel < WINDOW)
    key_pos = jnp.arange(nb)[:, None] * BLOCK - BLOCK + jnp.arange(2 * BLOCK)[None, :]
    mask = in_window[None] & (key_pos >= 0)[:, None, :]
    scores = jnp.where(mask[None, :, None, None], scores, NEG_INF)
    sink = jnp.broadcast_to(sinks.astype(jnp.float32).reshape(1, 1, N_KV_HEADS, GROUP, 1, 1),
                            scores.shape[:-1] + (1,))
    probs = jax.nn.softmax(jnp.concatenate([scores, sink], axis=-1), axis=-1)[..., :-1]
    out = jnp.einsum('bnhgqk,bnkhd->bnqhgd', probs.astype(v.dtype), vb)
    return out.reshape(b, s, D_ATTN)


def setup_inputs(seed: int = 0) -> dict:
    key = jax.random.key(seed)
    ks = jax.random.split(key, 13)
    f32 = jnp.float32

    def w(k, shape, fan_in):
        return jax.random.normal(k, shape, f32) * (fan_in ** -0.5)

    def gain(k, shape):
        return 1.0 + 0.05 * jax.random.normal(k, shape, f32)

    return {
        "x": jax.random.normal(ks[0], (BATCH, SEQ, D_MODEL), f32),
        "g_mix": gain(ks[1], (DEPTH, D_MODEL)),
        "w_in": w(ks[2], (DEPTH, D_MODEL, N_IN), D_MODEL),
        "conv_w": w(ks[3], (DEPTH, CONV_K, D_CONV), CONV_K),
        "attn_sinks": 0.5 * jax.random.normal(ks[4], (DEPTH, N_HEADS), f32),
        "w_conv_out": w(ks[5], (DEPTH, D_CONV, D_MODEL), D_CONV),
        "w_attn_out": w(ks[6], (DEPTH, D_ATTN, D_MODEL), D_ATTN),
        "w_o": w(ks[7], (DEPTH, D_MODEL, D_MODEL), D_MODEL),
        "g_ffn": gain(ks[8], (DEPTH, D_MODEL)),
        "w_gate_up": w(ks[9], (DEPTH, D_MODEL, 2 * D_FF), D_MODEL),
        "w_down": w(ks[10], (DEPTH, D_FF, D_MODEL), D_FF),
        "g_final": gain(ks[11], (D_MODEL,)),
    }


def reference(x, g_mix, w_in, conv_w, attn_sinks, w_conv_out, w_attn_out, w_o,
              g_ffn, w_gate_up, w_down, g_final):
    b, s, _ = x.shape
    cos, sin = rotary_tables(s, x.dtype)
    for l in range(DEPTH):
        h = rms_norm(x, g_mix[l])
        proj = jnp.einsum('bsd,dn->bsn', h, w_in[l])
        cb, cc, cx, q, k, v, gate_c, gate_a = jnp.split(proj, SPLIT_POINTS, axis=-1)

        conv_y = cb * causal_depthwise_conv(cc * cx, conv_w[l])
        conv_out = jnp.einsum('bsc,cd->bsd', conv_y, w_conv_out[l])

        q = partial_rotary(q.reshape(b, s, N_HEADS, HEAD_DIM), cos, sin)
        k = partial_rotary(k.reshape(b, s, N_KV_HEADS, HEAD_DIM), cos, sin)
        v = v.reshape(b, s, N_KV_HEADS, HEAD_DIM)
        attn = sliding_window_sink_attention(q, k, v, attn_sinks[l])
        attn_out = jnp.einsum('bsc,cd->bsd', attn, w_attn_out[l])

        merged = jax.nn.sigmoid(gate_c) * conv_out + jax.nn.sigmoid(gate_a) * attn_out
        x = x + jnp.einsum('bsd,de->bse', merged, w_o[l])

        h = rms_norm(x, g_ffn[l])
        gu = jnp.einsum('bsd,df->bsf', h, w_gate_up[l])
        g_act, up = gu[..., :D_FF], gu[..., D_FF:]
        x = x + jnp.einsum('bsf,fd->bsd', jax.nn.silu(g_act) * up, w_down[l])
    return rms_norm(x, g_final)
```

```python
import functools
import math

import jax
import jax.numpy as jnp
from jax import lax
from jax.experimental import pallas as pl
from jax.experimental.pallas import tpu as pltpu

D_MODEL = 1024
CONV_K = 3
HEAD_DIM = 64
N_HEADS = 16
N_KV_HEADS = 4
GROUP = N_HEADS // N_KV_HEADS
D_ATTN = N_HEADS * HEAD_DIM
D_KV = N_KV_HEADS * HEAD_DIM
WINDOW = 128
BLOCK = 128
ROT_DIM = HEAD_DIM // 4
ROPE_THETA = 500000.0
ATTN_SCALE = 1.0 / math.sqrt(HEAD_DIM)
NEG_INF = -1e30
D_FF = 2816
EPS = 1e-5

OFF_CB = 0
OFF_CC = OFF_CB + D_MODEL
OFF_CX = OFF_CC + D_MODEL
OFF_Q = OFF_CX + D_MODEL
OFF_K = OFF_Q + D_ATTN
OFF_V = OFF_K + D_KV
OFF_GC = OFF_V + D_KV
OFF_GA = OFF_GC + D_MODEL
N_IN = OFF_GA + D_MODEL

LANES = 128
SUBLANES = 8
SEQ_TILE = 512
FF_CHUNKS = ((0, 1024), (1024, 1024), (2048, 768))
VMEM_LIMIT_BYTES = 58 * 1024 * 1024

F32 = jnp.float32
BF16 = jnp.bfloat16


def _rms_norm(x, g):
    ms = jnp.mean(x * x, axis=-1, keepdims=True)
    return x * lax.rsqrt(ms + EPS) * g


def _sigmoid(x):
    return 1.0 / (1.0 + jnp.exp(-x))


def _rope(t, cos, sin_next, sin_prev):
    nxt = pltpu.roll(t, LANES - ROT_DIM // 2, 1)
    prv = pltpu.roll(t, ROT_DIM // 2, 1)
    return t * cos + nxt * sin_next + prv * sin_prev


def _mixer_kernel(sinks_ref, x_ref, g_ref, rope_ref, convw_ref, w_in_ref,
                  w_co_ref, w_ao_ref, w_o_ref, out_ref,
                  u_scr, q_scr, k_scr, v_scr, attn_scr):
    ts = x_ref.shape[0]
    n_blk = ts // BLOCK
    s_idx = pl.program_id(1)

    @pl.when(s_idx == 0)
    def _():
        u_scr[0:SUBLANES, :] = jnp.zeros((SUBLANES, D_MODEL), F32)
        k_scr[0:BLOCK, :] = jnp.zeros((BLOCK, D_KV), BF16)
        v_scr[0:BLOCK, :] = jnp.zeros((BLOCK, D_KV), BF16)

    x = x_ref[...]
    h = _rms_norm(x, g_ref[...]).astype(BF16)

    def proj(off, width):
        return jnp.dot(h, w_in_ref[:, off:off + width], preferred_element_type=F32)

    u = proj(OFF_CC, D_MODEL) * proj(OFF_CX, D_MODEL)
    u_scr[SUBLANES:SUBLANES + ts, :] = u
    conv = (convw_ref[0:1, :] * u_scr[SUBLANES - 2:SUBLANES - 2 + ts, :]
            + convw_ref[1:2, :] * u_scr[SUBLANES - 1:SUBLANES - 1 + ts, :]
            + convw_ref[2:3, :] * u)
    u_scr[0:SUBLANES, :] = u_scr[ts:ts + SUBLANES, :]
    conv_y = (proj(OFF_CB, D_MODEL) * conv).astype(BF16)
    conv_out = jnp.dot(conv_y, w_co_ref[...], preferred_element_type=F32)
    merged = _sigmoid(proj(OFF_GC, D_MODEL)) * conv_out

    q = proj(OFF_Q, D_ATTN)
    for t in range(D_ATTN // LANES):
        sl = slice(t * LANES, (t + 1) * LANES)
        q_scr[:, sl] = _rope(q[:, sl], rope_ref[0], rope_ref[1], rope_ref[2]).astype(BF16)
    k = proj(OFF_K, D_KV)
    for t in range(D_KV // LANES):
        sl = slice(t * LANES, (t + 1) * LANES)
        k_scr[BLOCK:BLOCK + ts, sl] = _rope(
            k[:, sl], rope_ref[3], rope_ref[4], rope_ref[5]).astype(BF16)
    v_scr[BLOCK:BLOCK + ts, :] = proj(OFF_V, D_KV).astype(BF16)

    row = lax.broadcasted_iota(jnp.int32, (BLOCK, 2 * BLOCK), 0)
    col = lax.broadcasted_iota(jnp.int32, (BLOCK, 2 * BLOCK), 1)

    def attend_block(j, carry):
        r0 = pl.multiple_of(j * BLOCK, BLOCK)
        first_col = jnp.where(s_idx * n_blk + j == 0, BLOCK, 0)
        mask = (col >= jnp.maximum(row + 1, first_col)) & (col <= row + WINDOW)
        for kv in range(N_KV_HEADS):
            kv_sl = slice(kv * HEAD_DIM, (kv + 1) * HEAD_DIM)
            heads = [kv * GROUP + g for g in range(GROUP)]
            q_grp = jnp.concatenate(
                [q_scr[pl.ds(r0, BLOCK), hd * HEAD_DIM:(hd + 1) * HEAD_DIM] for hd in heads],
                axis=0)
            k_cat = k_scr[pl.ds(r0, 2 * BLOCK), kv_sl]
            v_cat = v_scr[pl.ds(r0, 2 * BLOCK), kv_sl]
            s = lax.dot_general(q_grp, k_cat, (((1,), (1,)), ((), ())),
                                preferred_element_type=F32)
            probs, inv_den = [], []
            for g, hd in enumerate(heads):
                sg = jnp.where(mask, s[g * BLOCK:(g + 1) * BLOCK], NEG_INF)
                sink = sinks_ref[hd]
                m = jnp.maximum(jnp.max(sg, axis=-1, keepdims=True), sink)
                p = jnp.exp(sg - m)
                den = jnp.sum(p, axis=-1, keepdims=True) + jnp.exp(sink - m)
                probs.append(p.astype(BF16))
                inv_den.append(1.0 / den)
            o = jnp.dot(jnp.concatenate(probs, axis=0), v_cat, preferred_element_type=F32)
            for g, hd in enumerate(heads):
                attn_scr[pl.ds(r0, BLOCK), hd * HEAD_DIM:(hd + 1) * HEAD_DIM] = (
                    o[g * BLOCK:(g + 1) * BLOCK] * inv_den[g]).astype(BF16)
        return carry

    lax.fori_loop(0, n_blk, attend_block, 0)
    k_scr[0:BLOCK, :] = k_scr[ts:ts + BLOCK, :]
    v_scr[0:BLOCK, :] = v_scr[ts:ts + BLOCK, :]

    attn_out = jnp.dot(attn_scr[...], w_ao_ref[...], preferred_element_type=F32)
    merged = merged + _sigmoid(proj(OFF_GA, D_MODEL)) * attn_out
    out_ref[...] = x + jnp.dot(merged.astype(BF16), w_o_ref[...], preferred_element_type=F32)


def _ffn_kernel(x_ref, g_ref, gf_ref, w_gu_ref, w_d_ref, out_ref, act_scr, *, final_norm):
    x = x_ref[...]
    h = _rms_norm(x, g_ref[...]).astype(BF16)
    for off, width in FF_CHUNKS:
        gate = jnp.dot(h, w_gu_ref[:, off:off + width], preferred_element_type=F32)
        up = jnp.dot(h, w_gu_ref[:, D_FF + off:D_FF + off + width], preferred_element_type=F32)
        act_scr[:, off:off + width] = (gate * _sigmoid(gate) * up).astype(BF16)
    y = x + jnp.dot(act_scr[...], w_d_ref[...], preferred_element_type=F32)
    if final_norm:
        y = _rms_norm(y, gf_ref[...])
    out_ref[...] = y


def _resident(shape):
    return pl.BlockSpec(shape, lambda b, s: (0,) * len(shape), pipeline_mode=pl.Buffered(1))


def _rope_tables(seq):
    half = ROT_DIM // 2
    inv_freq = ROPE_THETA ** (-jnp.arange(0, ROT_DIM, 2, dtype=F32) / ROT_DIM)
    ang = jnp.arange(seq, dtype=F32)[:, None] * inv_freq[None, :]
    cos, sin = jnp.cos(ang), jnp.sin(ang)
    rest = HEAD_DIM - ROT_DIM
    zeros_h = jnp.zeros((seq, half), F32)
    zeros_r = jnp.zeros((seq, rest), F32)
    c = jnp.concatenate([cos, cos, jnp.ones((seq, rest), F32)], axis=1)
    s_next = jnp.concatenate([-sin, zeros_h, zeros_r], axis=1)
    s_prev = jnp.concatenate([zeros_h, sin, zeros_r], axis=1)
    per_tile = LANES // HEAD_DIM
    k_tabs = [jnp.tile(t, (1, per_tile)) for t in (c, s_next, s_prev)]
    q_tabs = [t * ATTN_SCALE for t in k_tabs]
    return jnp.stack(q_tabs + k_tabs)


def _mixer(x, g, rope, conv_w, sinks, w_in, w_co, w_ao, w_o):
    b, s, d = x.shape
    ts = SEQ_TILE
    tile = pl.BlockSpec((None, ts, d), lambda i, j: (i, j, 0))
    return pl.pallas_call(
        _mixer_kernel,
        grid=(b, s // ts),
        in_specs=[
            pl.BlockSpec(memory_space=pltpu.SMEM),
            tile,
            _resident((1, d)),
            pl.BlockSpec((6, ts, LANES), lambda i, j: (0, j, 0)),
            _resident((CONV_K, d)),
            _resident(w_in.shape),
            _resident(w_co.shape),
            _resident(w_ao.shape),
            _resident(w_o.shape),
        ],
        out_specs=tile,
        out_shape=jax.ShapeDtypeStruct(x.shape, x.dtype),
        scratch_shapes=[
            pltpu.VMEM((SUBLANES + ts, d), F32),
            pltpu.VMEM((ts, D_ATTN), BF16),
            pltpu.VMEM((BLOCK + ts, D_KV), BF16),
            pltpu.VMEM((BLOCK + ts, D_KV), BF16),
            pltpu.VMEM((ts, D_ATTN), BF16),
        ],
        compiler_params=pltpu.CompilerParams(
            dimension_semantics=("arbitrary", "arbitrary"),
            vmem_limit_bytes=VMEM_LIMIT_BYTES),
        name="mixer",
    )(sinks, x, g, rope, conv_w, w_in, w_co, w_ao, w_o)


def _ffn(x, g, g_final, w_gu, w_d, final_norm):
    b, s, d = x.shape
    ts = SEQ_TILE
    tile = pl.BlockSpec((None, ts, d), lambda i, j: (i, j, 0))
    return pl.pallas_call(
        functools.partial(_ffn_kernel, final_norm=final_norm),
        grid=(b, s // ts),
        in_specs=[tile, _resident((1, d)), _resident((1, d)),
                  _resident(w_gu.shape), _resident(w_d.shape)],
        out_specs=tile,
        out_shape=jax.ShapeDtypeStruct(x.shape, x.dtype),
        scratch_shapes=[pltpu.VMEM((ts, D_FF), BF16)],
        compiler_params=pltpu.CompilerParams(
            dimension_semantics=("arbitrary", "arbitrary"),
            vmem_limit_bytes=VMEM_LIMIT_BYTES),
        name="ffn",
    )(x, g, g_final, w_gu, w_d)


def kernel(x, g_mix, w_in, conv_w, attn_sinks, w_conv_out, w_attn_out, w_o,
           g_ffn, w_gate_up, w_down, g_final):
    b, s, d = x.shape
    depth = w_in.shape[0]
    assert d == D_MODEL and s % SEQ_TILE == 0 and w_in.shape[-1] == N_IN
    rope = _rope_tables(s)
    g_fin = g_final.reshape(1, d)
    for l in range(depth):
        x = _mixer(x, g_mix[l].reshape(1, d), rope, conv_w[l], attn_sinks[l],
                   w_in[l].astype(BF16), w_conv_out[l].astype(BF16),
                   w_attn_out[l].astype(BF16), w_o[l].astype(BF16))
        x = _ffn(x, g_ffn[l].reshape(1, d), g_fin, w_gate_up[l].astype(BF16),
                 w_down[l].astype(BF16), final_norm=(l == depth - 1))
    return x
```

```python
import functools
import math

import jax
import jax.numpy as jnp
from jax import lax
from jax.experimental import pallas as pl
from jax.experimental.pallas import tpu as pltpu

D_MODEL = 1024
CONV_K = 3
HEAD_DIM = 64
N_HEADS = 16
N_KV_HEADS = 4
GROUP = N_HEADS // N_KV_HEADS
D_ATTN = N_HEADS * HEAD_DIM
D_KV = N_KV_HEADS * HEAD_DIM
WINDOW = 128
BLOCK = 128
ROT_DIM = HEAD_DIM // 4
ROPE_THETA = 500000.0
ATTN_SCALE = 1.0 / math.sqrt(HEAD_DIM)
LOG2E = math.log2(math.e)
NEG_INF = -1e30
D_FF = 2816
EPS = 1e-5

OFF_CB = 0
OFF_CC = OFF_CB + D_MODEL
OFF_CX = OFF_CC + D_MODEL
OFF_Q = OFF_CX + D_MODEL
OFF_K = OFF_Q + D_ATTN
OFF_V = OFF_K + D_KV
OFF_GC = OFF_V + D_KV
OFF_GA = OFF_GC + D_MODEL
N_IN = OFF_GA + D_MODEL

LANES = 128
SUBLANES = 8
SEQ_TILE = 512
FF_CHUNKS = ((0, 1024), (1024, 1024), (2048, 768))
VMEM_LIMIT_BYTES = 58 * 1024 * 1024

F32 = jnp.float32
BF16 = jnp.bfloat16

assert WINDOW == BLOCK and 2 * HEAD_DIM == LANES and GROUP == 4


def _rms_norm(x, g):
    ms = jnp.mean(x * x, axis=-1, keepdims=True)
    return x * lax.rsqrt(ms + EPS) * g


def _sigmoid(x):
    return 1.0 / (1.0 + jnp.exp(-x))


def _rope(t, cos, sin_next, sin_prev):
    nxt = pltpu.roll(t, LANES - ROT_DIM // 2, 1)
    prv = pltpu.roll(t, ROT_DIM // 2, 1)
    return t * cos + nxt * sin_next + prv * sin_prev


def _split_head_pair(t, lane_lo):
    sw = pltpu.roll(t, HEAD_DIM, 1)
    zero = jnp.zeros_like(t)
    even = (jnp.where(lane_lo, t, zero), jnp.where(lane_lo, zero, sw))
    odd = (jnp.where(lane_lo, sw, zero), jnp.where(lane_lo, zero, t))
    return even, odd


def _mixer_kernel(sinks_ref, x_ref, g_ref, rope_ref, convw_ref, w_in_ref,
                  w_co_ref, w_ao_ref, w_o_ref, out_ref,
                  u_scr, q_scr, klo_scr, khi_scr, vlo_scr, vhi_scr, attn_scr):
    ts = x_ref.shape[0]
    n_blk = ts // BLOCK
    s_idx = pl.program_id(1)
    kv_scrs = (klo_scr, khi_scr, vlo_scr, vhi_scr)

    @pl.when(s_idx == 0)
    def _():
        u_scr[0:SUBLANES, :] = jnp.zeros((SUBLANES, D_MODEL), F32)
        for scr in kv_scrs:
            scr[0:BLOCK, :] = jnp.zeros((BLOCK, N_KV_HEADS * LANES), BF16)

    x = x_ref[...]
    h = _rms_norm(x, g_ref[...]).astype(BF16)

    def proj(off, width):
        return jnp.dot(h, w_in_ref[:, off:off + width], preferred_element_type=F32)

    u = proj(OFF_CC, D_MODEL) * proj(OFF_CX, D_MODEL)
    u_scr[SUBLANES:SUBLANES + ts, :] = u
    conv = (convw_ref[0:1, :] * u_scr[SUBLANES - 2:SUBLANES - 2 + ts, :]
            + convw_ref[1:2, :] * u_scr[SUBLANES - 1:SUBLANES - 1 + ts, :]
            + convw_ref[2:3, :] * u)
    u_scr[0:SUBLANES, :] = u_scr[ts:ts + SUBLANES, :]
    conv_y = (proj(OFF_CB, D_MODEL) * conv).astype(BF16)
    conv_out = jnp.dot(conv_y, w_co_ref[...], preferred_element_type=F32)
    merged = _sigmoid(proj(OFF_GC, D_MODEL)) * conv_out

    q = proj(OFF_Q, D_ATTN)
    for t in range(D_ATTN // LANES):
        sl = slice(t * LANES, (t + 1) * LANES)
        q_scr[:, sl] = _rope(q[:, sl], rope_ref[0], rope_ref[1], rope_ref[2]).astype(BF16)
    lane_lo_t = lax.broadcasted_iota(jnp.int32, (ts, LANES), 1) < HEAD_DIM
    k = proj(OFF_K, D_KV)
    v = proj(OFF_V, D_KV)
    for t in range(D_KV // LANES):
        sl = slice(t * LANES, (t + 1) * LANES)
        k_t = _rope(k[:, sl], rope_ref[3], rope_ref[4], rope_ref[5])
        for src, lo_scr, hi_scr in ((k_t, klo_scr, khi_scr), (v[:, sl], vlo_scr, vhi_scr)):
            for hd, (lo, hi) in zip((2 * t, 2 * t + 1), _split_head_pair(src, lane_lo_t)):
                hsl = slice(hd * LANES, (hd + 1) * LANES)
                lo_scr[BLOCK:BLOCK + ts, hsl] = lo.astype(BF16)
                hi_scr[BLOCK:BLOCK + ts, hsl] = hi.astype(BF16)

    lane = lax.broadcasted_iota(jnp.int32, (BLOCK, LANES), 1)
    rowi = lax.broadcasted_iota(jnp.int32, (BLOCK, LANES), 0)
    from_cur = lane <= rowi
    lane_lo = lane < HEAD_DIM
    lane2 = lax.broadcasted_iota(jnp.int32, (2 * BLOCK, LANES), 1)
    ones_lo = jnp.where(lane2 < HEAD_DIM, 1.0, 0.0).astype(BF16)
    ones_hi = jnp.where(lane2 < HEAD_DIM, 0.0, 1.0).astype(BF16)
    nt_dims = (((1,), (1,)), ((), ()))

    for j in range(n_blk):
        r0 = j * BLOCK
        for kv in range(N_KV_HEADS):
            kv_sl = slice(kv * LANES, (kv + 1) * LANES)
            q_sl = [slice((2 * kv + pr) * LANES, (2 * kv + pr + 1) * LANES) for pr in range(2)]
            q2 = jnp.concatenate([q_scr[r0:r0 + BLOCK, sl] for sl in q_sl], axis=0)
            k_rhs = jnp.concatenate([klo_scr[r0:r0 + 2 * BLOCK, kv_sl],
                                     khi_scr[r0:r0 + 2 * BLOCK, kv_sl]], axis=0)
            s = lax.dot_general(q2, k_rhs, nt_dims, preferred_element_type=F32)
            v_rhs = jnp.concatenate(
                [jnp.concatenate([vlo_scr[r0:r0 + 2 * BLOCK, kv_sl], ones_lo], axis=1),
                 jnp.concatenate([vhi_scr[r0:r0 + 2 * BLOCK, kv_sl], ones_hi], axis=1)], axis=0)
            p_rows, sink_terms = [], []
            for pr in range(2):
                rows = slice(pr * BLOCK, (pr + 1) * BLOCK)
                p_tiles, e_sink = [], []
                for half in range(2):
                    hd = kv * GROUP + 2 * pr + half
                    s_prev = s[rows, (2 * half) * BLOCK:(2 * half + 1) * BLOCK]
                    s_cur = s[rows, (2 * half + 1) * BLOCK:(2 * half + 2) * BLOCK]
                    if j == 0:
                        s_prev = jnp.where(s_idx > 0, s_prev, NEG_INF)
                    t = jnp.where(from_cur, s_cur, s_prev)
                    m = jnp.max(t, axis=-1, keepdims=True)
                    p = jnp.exp2(t - m).astype(BF16)
                    zero = jnp.zeros_like(p)
                    p_tiles += [jnp.where(from_cur, zero, p), jnp.where(from_cur, p, zero)]
                    e_sink.append(jnp.exp2(sinks_ref[hd] - m))
                p_rows.append(jnp.concatenate(p_tiles, axis=1))
                sink_terms.append(jnp.where(lane_lo, e_sink[0], e_sink[1]))
            o = jnp.dot(jnp.concatenate(p_rows, axis=0), v_rhs, preferred_element_type=F32)
            for pr in range(2):
                rows = slice(pr * BLOCK, (pr + 1) * BLOCK)
                den = o[rows, LANES:2 * LANES] + sink_terms[pr]
                attn_scr[r0:r0 + BLOCK, q_sl[pr]] = (o[rows, 0:LANES] / den).astype(BF16)

    for scr in kv_scrs:
        scr[0:BLOCK, :] = scr[ts:ts + BLOCK, :]

    attn_out = jnp.dot(attn_scr[...], w_ao_ref[...], preferred_element_type=F32)
    merged = merged + _sigmoid(proj(OFF_GA, D_MODEL)) * attn_out
    out_ref[...] = x + jnp.dot(merged.astype(BF16), w_o_ref[...], preferred_element_type=F32)


def _ffn_kernel(x_ref, g_ref, gf_ref, w_gu_ref, w_d_ref, out_ref, act_scr, *, final_norm):
    x = x_ref[...]
    h = _rms_norm(x, g_ref[...]).astype(BF16)
    for off, width in FF_CHUNKS:
        gate = jnp.dot(h, w_gu_ref[:, off:off + width], preferred_element_type=F32)
        up = jnp.dot(h, w_gu_ref[:, D_FF + off:D_FF + off + width], preferred_element_type=F32)
        act_scr[:, off:off + width] = (gate * _sigmoid(gate) * up).astype(BF16)
    y = x + jnp.dot(act_scr[...], w_d_ref[...], preferred_element_type=F32)
    if final_norm:
        y = _rms_norm(y, gf_ref[...])
    out_ref[...] = y


def _resident(shape):
    return pl.BlockSpec(shape, lambda b, s: (0,) * len(shape), pipeline_mode=pl.Buffered(1))


def _rope_tables(seq):
    half = ROT_DIM // 2
    inv_freq = ROPE_THETA ** (-jnp.arange(0, ROT_DIM, 2, dtype=F32) / ROT_DIM)
    ang = jnp.arange(seq, dtype=F32)[:, None] * inv_freq[None, :]
    cos, sin = jnp.cos(ang), jnp.sin(ang)
    rest = HEAD_DIM - ROT_DIM
    zeros_h = jnp.zeros((seq, half), F32)
    zeros_r = jnp.zeros((seq, rest), F32)
    c = jnp.concatenate([cos, cos, jnp.ones((seq, rest), F32)], axis=1)
    s_next = jnp.concatenate([-sin, zeros_h, zeros_r], axis=1)
    s_prev = jnp.concatenate([zeros_h, sin, zeros_r], axis=1)
    per_tile = LANES // HEAD_DIM
    k_tabs = [jnp.tile(t, (1, per_tile)) for t in (c, s_next, s_prev)]
    q_tabs = [t * (ATTN_SCALE * LOG2E) for t in k_tabs]
    return jnp.stack(q_tabs + k_tabs)


def _mixer(x, g, rope, conv_w, sinks, w_in, w_co, w_ao, w_o):
    b, s, d = x.shape
    ts = SEQ_TILE
    tile = pl.BlockSpec((None, ts, d), lambda i, j: (i, j, 0))
    kv_scratch = pltpu.VMEM((BLOCK + ts, N_KV_HEADS * LANES), BF16)
    return pl.pallas_call(
        _mixer_kernel,
        grid=(b, s // ts),
        in_specs=[
            pl.BlockSpec(memory_space=pltpu.SMEM),
            tile,
            _resident((1, d)),
            pl.BlockSpec((6, ts, LANES), lambda i, j: (0, j, 0)),
            _resident((CONV_K, d)),
            _resident(w_in.shape),
            _resident(w_co.shape),
            _resident(w_ao.shape),
            _resident(w_o.shape),
        ],
        out_specs=tile,
        out_shape=jax.ShapeDtypeStruct(x.shape, x.dtype),
        scratch_shapes=[
            pltpu.VMEM((SUBLANES + ts, d), F32),
            pltpu.VMEM((ts, D_ATTN), BF16),
            kv_scratch, kv_scratch, kv_scratch, kv_scratch,
            pltpu.VMEM((ts, D_ATTN), BF16),
        ],
        compiler_params=pltpu.CompilerParams(
            dimension_semantics=("arbitrary", "arbitrary"),
            vmem_limit_bytes=VMEM_LIMIT_BYTES),
        name="mixer",
    )(sinks, x, g, rope, conv_w, w_in, w_co, w_ao, w_o)


def _ffn(x, g, g_final, w_gu, w_d, final_norm):
    b, s, d = x.shape
    ts = SEQ_TILE
    tile = pl.BlockSpec((None, ts, d), lambda i, j: (i, j, 0))
    return pl.pallas_call(
        functools.partial(_ffn_kernel, final_norm=final_norm),
        grid=(b, s // ts),
        in_specs=[tile, _resident((1, d)), _resident((1, d)),
                  _resident(w_gu.shape), _resident(w_d.shape)],
        out_specs=tile,
        out_shape=jax.ShapeDtypeStruct(x.shape, x.dtype),
        scratch_shapes=[pltpu.VMEM((ts, D_FF), BF16)],
        compiler_params=pltpu.CompilerParams(
            dimension_semantics=("arbitrary", "arbitrary"),
            vmem_limit_bytes=VMEM_LIMIT_BYTES),
        name="ffn",
    )(x, g, g_final, w_gu, w_d)


def kernel(x, g_mix, w_in, conv_w, attn_sinks, w_conv_out, w_attn_out, w_o,
           g_ffn, w_gate_up, w_down, g_final):
    b, s, d = x.shape
    depth = w_in.shape[0]
    assert d == D_MODEL and s % SEQ_TILE == 0 and w_in.shape[-1] == N_IN
    rope = _rope_tables(s)
    g_fin = g_final.reshape(1, d)
    for l in range(depth):
        x = _mixer(x, g_mix[l].reshape(1, d), rope, conv_w[l], attn_sinks[l] * LOG2E,
                   w_in[l].astype(BF16), w_conv_out[l].astype(BF16),
                   w_attn_out[l].astype(BF16), w_o[l].astype(BF16))
        x = _ffn(x, g_ffn[l].reshape(1, d), g_fin, w_gate_up[l].astype(BF16),
                 w_down[l].astype(BF16), final_norm=(l == depth - 1))
    return x
```

```python
import functools
import math

import jax
import jax.numpy as jnp
from jax import lax
from jax.experimental import pallas as pl
from jax.experimental.pallas import tpu as pltpu

D_MODEL = 1024
CONV_K = 3
HEAD_DIM = 64
N_HEADS = 16
N_KV_HEADS = 4
GROUP = N_HEADS // N_KV_HEADS
D_ATTN = N_HEADS * HEAD_DIM
D_KV = N_KV_HEADS * HEAD_DIM
WINDOW = 128
BLOCK = 128
ROT_DIM = HEAD_DIM // 4
ROPE_THETA = 500000.0
ATTN_SCALE = 1.0 / math.sqrt(HEAD_DIM)
LOG2E = math.log2(math.e)
NEG_INF = -1e30
D_FF = 2816
EPS = 1e-5

OFF_CB = 0
OFF_CC = OFF_CB + D_MODEL
OFF_CX = OFF_CC + D_MODEL
OFF_Q = OFF_CX + D_MODEL
OFF_K = OFF_Q + D_ATTN
OFF_V = OFF_K + D_KV
OFF_GC = OFF_V + D_KV
OFF_GA = OFF_GC + D_MODEL
N_IN = OFF_GA + D_MODEL

LANES = 128
SUBLANES = 8
BF16_SUBLANES = 16
SEQ_TILE = 512
FFN_SEQ_TILE = 1024
FF_CHUNKS = ((0, 1024), (1024, 1024), (2048, 768))
VMEM_LIMIT_BYTES = 58 * 1024 * 1024

F32 = jnp.float32
BF16 = jnp.bfloat16

assert WINDOW == BLOCK and 2 * HEAD_DIM == LANES and GROUP == 4


def _rms_norm(x, g):
    ms = jnp.mean(x * x, axis=-1, keepdims=True)
    return x * lax.rsqrt(ms + EPS) * g


def _sigmoid(x):
    return 1.0 / (1.0 + jnp.exp(-x))


def _rope(t, cos, sin_next, sin_prev):
    nxt = pltpu.roll(t, LANES - ROT_DIM // 2, 1)
    prv = pltpu.roll(t, ROT_DIM // 2, 1)
    return t * cos + nxt * sin_next + prv * sin_prev


def _split_head_pair(t, lane_lo):
    sw = pltpu.roll(t, HEAD_DIM, 1)
    zero = jnp.zeros_like(t)
    even = (jnp.where(lane_lo, t, zero), jnp.where(lane_lo, zero, sw))
    odd = (jnp.where(lane_lo, sw, zero), jnp.where(lane_lo, zero, t))
    return even, odd


def _mixer_kernel(sinks_ref, x_ref, g_ref, rope_ref, convw_ref, w_in_ref,
                  w_co_ref, w_ao_ref, w_o_ref, w_gu_f32_ref, w_d_f32_ref,
                  out_ref, w_gu_bf16_ref, w_d_bf16_ref,
                  u_scr, q_scr, klo_scr, khi_scr, vlo_scr, vhi_scr, attn_scr):
    ts = x_ref.shape[0]
    n_blk = ts // BLOCK
    s_idx = pl.program_id(1)
    kv_scrs = (klo_scr, khi_scr, vlo_scr, vhi_scr)

    @pl.when(s_idx == 0)
    def _():
        u_scr[0:SUBLANES, :] = jnp.zeros((SUBLANES, D_MODEL), F32)
        for scr in kv_scrs:
            scr[0:BLOCK, :] = jnp.zeros((BLOCK, N_KV_HEADS * LANES), BF16)

    w_gu_bf16_ref[...] = w_gu_f32_ref[...].astype(BF16)
    w_d_bf16_ref[...] = w_d_f32_ref[...].astype(BF16)

    x = x_ref[...]
    h = _rms_norm(x, g_ref[...]).astype(BF16)

    def proj(off, width):
        return jnp.dot(h, w_in_ref[:, off:off + width], preferred_element_type=F32)

    u = proj(OFF_CC, D_MODEL) * proj(OFF_CX, D_MODEL)
    u_scr[SUBLANES:SUBLANES + ts, :] = u
    conv = (convw_ref[0:1, :] * u_scr[SUBLANES - 2:SUBLANES - 2 + ts, :]
            + convw_ref[1:2, :] * u_scr[SUBLANES - 1:SUBLANES - 1 + ts, :]
            + convw_ref[2:3, :] * u)
    u_scr[0:SUBLANES, :] = u_scr[ts:ts + SUBLANES, :]
    conv_y = (proj(OFF_CB, D_MODEL) * conv).astype(BF16)
    conv_out = jnp.dot(conv_y, w_co_ref[...], preferred_element_type=F32)
    merged = _sigmoid(proj(OFF_GC, D_MODEL)) * conv_out

    q_tabs = [rope_ref[i] * (ATTN_SCALE * LOG2E) for i in range(3)]
    q = proj(OFF_Q, D_ATTN)
    for t in range(D_ATTN // LANES):
        sl = slice(t * LANES, (t + 1) * LANES)
        q_scr[:, sl] = _rope(q[:, sl], *q_tabs).astype(BF16)
    lane_lo_t = lax.broadcasted_iota(jnp.int32, (ts, LANES), 1) < HEAD_DIM
    k = proj(OFF_K, D_KV)
    v = proj(OFF_V, D_KV)
    for t in range(D_KV // LANES):
        sl = slice(t * LANES, (t + 1) * LANES)
        k_t = _rope(k[:, sl], rope_ref[0], rope_ref[1], rope_ref[2])
        for src, lo_scr, hi_scr in ((k_t, klo_scr, khi_scr), (v[:, sl], vlo_scr, vhi_scr)):
            for hd, (lo, hi) in zip((2 * t, 2 * t + 1), _split_head_pair(src, lane_lo_t)):
                hsl = slice(hd * LANES, (hd + 1) * LANES)
                lo_scr[BLOCK:BLOCK + ts, hsl] = lo.astype(BF16)
                hi_scr[BLOCK:BLOCK + ts, hsl] = hi.astype(BF16)

    lane = lax.broadcasted_iota(jnp.int32, (BLOCK, LANES), 1)
    rowi = lax.broadcasted_iota(jnp.int32, (BLOCK, LANES), 0)
    from_cur = lane <= rowi
    lane_lo = lane < HEAD_DIM
    lane2 = lax.broadcasted_iota(jnp.int32, (2 * BLOCK, LANES), 1)
    ones_lo = jnp.where(lane2 < HEAD_DIM, 1.0, 0.0).astype(BF16)
    ones_hi = jnp.where(lane2 < HEAD_DIM, 0.0, 1.0).astype(BF16)
    nt_dims = (((1,), (1,)), ((), ()))

    for j in range(n_blk):
        r0 = j * BLOCK
        for kv in range(N_KV_HEADS):
            kv_sl = slice(kv * LANES, (kv + 1) * LANES)
            q_sl = [slice((2 * kv + pr) * LANES, (2 * kv + pr + 1) * LANES) for pr in range(2)]
            q2 = jnp.concatenate([q_scr[r0:r0 + BLOCK, sl] for sl in q_sl], axis=0)
            k_rhs = jnp.concatenate([klo_scr[r0:r0 + 2 * BLOCK, kv_sl],
                                     khi_scr[r0:r0 + 2 * BLOCK, kv_sl]], axis=0)
            s = lax.dot_general(q2, k_rhs, nt_dims, preferred_element_type=F32)
            v_rhs = jnp.concatenate(
                [jnp.concatenate([vlo_scr[r0:r0 + 2 * BLOCK, kv_sl], ones_lo], axis=1),
                 jnp.concatenate([vhi_scr[r0:r0 + 2 * BLOCK, kv_sl], ones_hi], axis=1)], axis=0)
            p_rows, sink_terms = [], []
            for pr in range(2):
                rows = slice(pr * BLOCK, (pr + 1) * BLOCK)
                p_tiles, e_sink = [], []
                for half in range(2):
                    hd = kv * GROUP + 2 * pr + half
                    s_prev = s[rows, (2 * half) * BLOCK:(2 * half + 1) * BLOCK]
                    s_cur = s[rows, (2 * half + 1) * BLOCK:(2 * half + 2) * BLOCK]
                    if j == 0:
                        s_prev = jnp.where(s_idx > 0, s_prev, NEG_INF)
                    t = jnp.where(from_cur, s_cur, s_prev)
                    m = jnp.max(t, axis=-1, keepdims=True)
                    p = jnp.exp2(t - m).astype(BF16)
                    zero = jnp.zeros_like(p)
                    p_tiles += [jnp.where(from_cur, zero, p), jnp.where(from_cur, p, zero)]
                    e_sink.append(jnp.exp2(sinks_ref[hd] - m))
                p_rows.append(jnp.concatenate(p_tiles, axis=1))
                sink_terms.append(jnp.where(lane_lo, e_sink[0], e_sink[1]))
            o = jnp.dot(jnp.concatenate(p_rows, axis=0), v_rhs, preferred_element_type=F32)
            for pr in range(2):
                rows = slice(pr * BLOCK, (pr + 1) * BLOCK)
                den = o[rows, LANES:2 * LANES] + sink_terms[pr]
                attn_scr[r0:r0 + BLOCK, q_sl[pr]] = (o[rows, 0:LANES] / den).astype(BF16)

    for scr in kv_scrs:
        scr[0:BLOCK, :] = scr[ts:ts + BLOCK, :]

    attn_out = jnp.dot(attn_scr[...], w_ao_ref[...], preferred_element_type=F32)
    merged = merged + _sigmoid(proj(OFF_GA, D_MODEL)) * attn_out
    out_ref[...] = x + jnp.dot(merged.astype(BF16), w_o_ref[...], preferred_element_type=F32)


def _ffn_kernel(x_ref, g_ref, gf_ref, w_gu_ref, w_d_ref, out_ref, act_scr, *, final_norm):
    x = x_ref[...]
    h = _rms_norm(x, g_ref[...]).astype(BF16)
    for off, width in FF_CHUNKS:
        gate = jnp.dot(h, w_gu_ref[:, off:off + width], preferred_element_type=F32)
        up = jnp.dot(h, w_gu_ref[:, D_FF + off:D_FF + off + width], preferred_element_type=F32)
        act_scr[:, off:off + width] = (gate * _sigmoid(gate) * up).astype(BF16)
    y = x + jnp.dot(act_scr[...], w_d_ref[...], preferred_element_type=F32)
    if final_norm:
        y = _rms_norm(y, gf_ref[...])
    out_ref[...] = y


def _resident(shape):
    return pl.BlockSpec(shape, lambda b, s: (0,) * len(shape), pipeline_mode=pl.Buffered(1))


def _rope_tables(seq):
    half = ROT_DIM // 2
    d = jnp.arange(LANES) % HEAD_DIM
    inv_freq = ROPE_THETA ** (-(2 * (d % half)).astype(F32) / ROT_DIM)
    ang = jnp.arange(seq, dtype=F32)[:, None] * inv_freq[None, :]
    cos, sin = jnp.cos(ang), jnp.sin(ang)
    c = jnp.where(d < ROT_DIM, cos, 1.0)
    s_next = jnp.where(d < half, -sin, 0.0)
    s_prev = jnp.where((d >= half) & (d < ROT_DIM), sin, 0.0)
    return jnp.stack([c, s_next, s_prev])


def _mixer(x, g, rope, conv_w, sinks, w_in, w_co, w_ao, w_o, w_gu, w_d):
    b, s, d = x.shape
    ts = SEQ_TILE
    n_s = s // ts
    gu_rows = w_gu.shape[0] // (b * n_s)
    d_rows = 2 * w_d.shape[0] // (b * n_s)
    assert gu_rows % BF16_SUBLANES == 0 and d_rows % BF16_SUBLANES == 0
    gu_slab = pl.BlockSpec((gu_rows, w_gu.shape[1]), lambda i, j: (i * n_s + j, 0))
    d_slab = pl.BlockSpec((d_rows, w_d.shape[1]), lambda i, j: ((i * n_s + j) // 2, 0))
    tile = pl.BlockSpec((None, ts, d), lambda i, j: (i, j, 0))
    kv_scratch = pltpu.VMEM((BLOCK + ts, N_KV_HEADS * LANES), BF16)
    return pl.pallas_call(
        _mixer_kernel,
        grid=(b, n_s),
        in_specs=[
            pl.BlockSpec(memory_space=pltpu.SMEM),
            tile,
            _resident((1, d)),
            pl.BlockSpec((3, ts, LANES), lambda i, j: (0, j, 0)),
            _resident((CONV_K, d)),
            _resident(w_in.shape),
            _resident(w_co.shape),
            _resident(w_ao.shape),
            _resident(w_o.shape),
            gu_slab,
            d_slab,
        ],
        out_specs=[tile, gu_slab, d_slab],
        out_shape=[jax.ShapeDtypeStruct(x.shape, x.dtype),
                   jax.ShapeDtypeStruct(w_gu.shape, BF16),
                   jax.ShapeDtypeStruct(w_d.shape, BF16)],
        scratch_shapes=[
            pltpu.VMEM((SUBLANES + ts, d), F32),
            pltpu.VMEM((ts, D_ATTN), BF16),
            kv_scratch, kv_scratch, kv_scratch, kv_scratch,
            pltpu.VMEM((ts, D_ATTN), BF16),
        ],
        compiler_params=pltpu.CompilerParams(
            dimension_semantics=("arbitrary", "arbitrary"),
            vmem_limit_bytes=VMEM_LIMIT_BYTES),
        name="mixer",
    )(sinks, x, g, rope, conv_w, w_in, w_co, w_ao, w_o, w_gu, w_d)


def _ffn(x, g, g_final, w_gu, w_d, final_norm):
    b, s, d = x.shape
    ts = FFN_SEQ_TILE
    tile = pl.BlockSpec((None, ts, d), lambda i, j: (i, j, 0))
    return pl.pallas_call(
        functools.partial(_ffn_kernel, final_norm=final_norm),
        grid=(b, s // ts),
        in_specs=[tile, _resident((1, d)), _resident((1, d)),
                  _resident(w_gu.shape), _resident(w_d.shape)],
        out_specs=tile,
        out_shape=jax.ShapeDtypeStruct(x.shape, x.dtype),
        scratch_shapes=[pltpu.VMEM((ts, D_FF), BF16)],
        compiler_params=pltpu.CompilerParams(
            dimension_semantics=("arbitrary", "arbitrary"),
            vmem_limit_bytes=VMEM_LIMIT_BYTES),
        name="ffn",
    )(x, g, g_final, w_gu, w_d)


def kernel(x, g_mix, w_in, conv_w, attn_sinks, w_conv_out, w_attn_out, w_o,
           g_ffn, w_gate_up, w_down, g_final):
    b, s, d = x.shape
    depth = w_in.shape[0]
    assert d == D_MODEL and s % SEQ_TILE == 0 and s % FFN_SEQ_TILE == 0
    assert w_in.shape[-1] == N_IN and w_gate_up.shape[-1] == 2 * D_FF
    rope = _rope_tables(s)
    g_fin = g_final.reshape(1, d)
    for l in range(depth):
        x, w_gu, w_d = _mixer(
            x, g_mix[l].reshape(1, d), rope, conv_w[l], attn_sinks[l] * LOG2E,
            w_in[l].astype(BF16), w_conv_out[l].astype(BF16), w_attn_out[l].astype(BF16),
            w_o[l].astype(BF16), w_gate_up[l], w_down[l])
        x = _ffn(x, g_ffn[l].reshape(1, d), g_fin, w_gu, w_d, final_norm=(l == depth - 1))
    return x
```

```python
import functools
import math

import jax
import jax.numpy as jnp
from jax import lax
from jax.experimental import pallas as pl
from jax.experimental.pallas import tpu as pltpu

D_MODEL = 1024
CONV_K = 3
HEAD_DIM = 64
N_HEADS = 16
N_KV_HEADS = 4
GROUP = N_HEADS // N_KV_HEADS
D_ATTN = N_HEADS * HEAD_DIM
D_KV = N_KV_HEADS * HEAD_DIM
WINDOW = 128
BLOCK = 128
ROT_DIM = HEAD_DIM // 4
ROPE_THETA = 500000.0
ATTN_SCALE = 1.0 / math.sqrt(HEAD_DIM)
LOG2E = math.log2(math.e)
NEG_INF = -1e30
D_FF = 2816
EPS = 1e-5

OFF_CB = 0
OFF_CC = OFF_CB + D_MODEL
OFF_CX = OFF_CC + D_MODEL
OFF_Q = OFF_CX + D_MODEL
OFF_K = OFF_Q + D_ATTN
OFF_V = OFF_K + D_KV
OFF_GC = OFF_V + D_KV
OFF_GA = OFF_GC + D_MODEL
N_IN = OFF_GA + D_MODEL

LANES = 128
SUBLANES = 8
BF16_SUBLANES = 16
SEQ_TILE = 512
STEP_ROWS = 1024
FF_CHUNKS = ((0, 1024), (1024, 1024), (2048, 768))
VMEM_LIMIT_BYTES = 58 * 1024 * 1024

F32 = jnp.float32
BF16 = jnp.bfloat16

assert WINDOW == BLOCK and 2 * HEAD_DIM == LANES and GROUP == 4


def _rms_norm(x, g):
    ms = jnp.mean(x * x, axis=-1, keepdims=True)
    return x * lax.rsqrt(ms + EPS) * g


def _sigmoid(x):
    return 1.0 / (1.0 + jnp.exp(-x))


def _rope(t, cos, sin_next, sin_prev):
    nxt = pltpu.roll(t, LANES - ROT_DIM // 2, 1)
    prv = pltpu.roll(t, ROT_DIM // 2, 1)
    return t * cos + nxt * sin_next + prv * sin_prev


def _split_head_pair(t, lane_lo):
    sw = pltpu.roll(t, HEAD_DIM, 1)
    zero = jnp.zeros_like(t)
    even = (jnp.where(lane_lo, t, zero), jnp.where(lane_lo, zero, sw))
    odd = (jnp.where(lane_lo, sw, zero), jnp.where(lane_lo, zero, t))
    return even, odd


def _mixer_kernel(sinks_ref, x_ref, g_ref, rope_ref, convw_ref, w_in_ref,
                  w_co_ref, w_ao_ref, w_o_ref, w_gu_f32_ref, w_d_f32_ref,
                  out_ref, w_gu_bf16_ref, w_d_bf16_ref,
                  u_scr, q_scr, klo_scr, khi_scr, vlo_scr, vhi_scr, attn_scr):
    ts = SEQ_TILE
    kv_scrs = (klo_scr, khi_scr, vlo_scr, vhi_scr)
    at_seq_start = pl.program_id(1) == 0

    @pl.when(at_seq_start)
    def _():
        u_scr[0:SUBLANES, :] = jnp.zeros((SUBLANES, D_MODEL), F32)
        for scr in kv_scrs:
            scr[0:BLOCK, :] = jnp.zeros((BLOCK, N_KV_HEADS * LANES), BF16)

    w_gu_bf16_ref[...] = w_gu_f32_ref[...].astype(BF16)
    w_d_bf16_ref[...] = w_d_f32_ref[...].astype(BF16)

    for sub in range(x_ref.shape[0] // ts):
        _mixer_tile(sub * ts, at_seq_start if sub == 0 else None,
                    sinks_ref, x_ref, g_ref, rope_ref, convw_ref, w_in_ref,
                    w_co_ref, w_ao_ref, w_o_ref, out_ref,
                    u_scr, q_scr, kv_scrs, attn_scr)


def _mixer_tile(row0, at_seq_start, sinks_ref, x_ref, g_ref, rope_ref, convw_ref, w_in_ref,
                w_co_ref, w_ao_ref, w_o_ref, out_ref, u_scr, q_scr, kv_scrs, attn_scr):
    ts = SEQ_TILE
    n_blk = ts // BLOCK
    klo_scr, khi_scr, vlo_scr, vhi_scr = kv_scrs
    tile_rows = slice(row0, row0 + ts)
    tabs = [rope_ref[i, tile_rows, :] for i in range(3)]

    x = x_ref[tile_rows, :]
    h = _rms_norm(x, g_ref[...]).astype(BF16)

    def proj(off, width):
        return jnp.dot(h, w_in_ref[:, off:off + width], preferred_element_type=F32)

    u = proj(OFF_CC, D_MODEL) * proj(OFF_CX, D_MODEL)
    u_scr[SUBLANES:SUBLANES + ts, :] = u
    conv = (convw_ref[0:1, :] * u_scr[SUBLANES - 2:SUBLANES - 2 + ts, :]
            + convw_ref[1:2, :] * u_scr[SUBLANES - 1:SUBLANES - 1 + ts, :]
            + convw_ref[2:3, :] * u)
    u_scr[0:SUBLANES, :] = u_scr[ts:ts + SUBLANES, :]
    conv_y = (proj(OFF_CB, D_MODEL) * conv).astype(BF16)
    conv_out = jnp.dot(conv_y, w_co_ref[...], preferred_element_type=F32)
    merged = _sigmoid(proj(OFF_GC, D_MODEL)) * conv_out

    q_tabs = [t * (ATTN_SCALE * LOG2E) for t in tabs]
    q = proj(OFF_Q, D_ATTN)
    for t in range(D_ATTN // LANES):
        sl = slice(t * LANES, (t + 1) * LANES)
        q_scr[:, sl] = _rope(q[:, sl], *q_tabs).astype(BF16)
    lane_lo_t = lax.broadcasted_iota(jnp.int32, (ts, LANES), 1) < HEAD_DIM
    k = proj(OFF_K, D_KV)
    v = proj(OFF_V, D_KV)
    for t in range(D_KV // LANES):
        sl = slice(t * LANES, (t + 1) * LANES)
        k_t = _rope(k[:, sl], *tabs)
        for src, lo_scr, hi_scr in ((k_t, klo_scr, khi_scr), (v[:, sl], vlo_scr, vhi_scr)):
            for hd, (lo, hi) in zip((2 * t, 2 * t + 1), _split_head_pair(src, lane_lo_t)):
                hsl = slice(hd * LANES, (hd + 1) * LANES)
                lo_scr[BLOCK:BLOCK + ts, hsl] = lo.astype(BF16)
                hi_scr[BLOCK:BLOCK + ts, hsl] = hi.astype(BF16)

    lane = lax.broadcasted_iota(jnp.int32, (BLOCK, LANES), 1)
    rowi = lax.broadcasted_iota(jnp.int32, (BLOCK, LANES), 0)
    from_cur = lane <= rowi
    lane_lo = lane < HEAD_DIM
    lane2 = lax.broadcasted_iota(jnp.int32, (2 * BLOCK, LANES), 1)
    ones_lo = jnp.where(lane2 < HEAD_DIM, 1.0, 0.0).astype(BF16)
    ones_hi = jnp.where(lane2 < HEAD_DIM, 0.0, 1.0).astype(BF16)
    nt_dims = (((1,), (1,)), ((), ()))

    for j in range(n_blk):
        r0 = j * BLOCK
        for kv in range(N_KV_HEADS):
            kv_sl = slice(kv * LANES, (kv + 1) * LANES)
            q_sl = [slice((2 * kv + pr) * LANES, (2 * kv + pr + 1) * LANES) for pr in range(2)]
            q2 = jnp.concatenate([q_scr[r0:r0 + BLOCK, sl] for sl in q_sl], axis=0)
            k_rhs = jnp.concatenate([klo_scr[r0:r0 + 2 * BLOCK, kv_sl],
                                     khi_scr[r0:r0 + 2 * BLOCK, kv_sl]], axis=0)
            s = lax.dot_general(q2, k_rhs, nt_dims, preferred_element_type=F32)
            v_rhs = jnp.concatenate(
                [jnp.concatenate([vlo_scr[r0:r0 + 2 * BLOCK, kv_sl], ones_lo], axis=1),
                 jnp.concatenate([vhi_scr[r0:r0 + 2 * BLOCK, kv_sl], ones_hi], axis=1)], axis=0)
            p_rows, sink_terms = [], []
            for pr in range(2):
                rows = slice(pr * BLOCK, (pr + 1) * BLOCK)
                p_tiles, e_sink = [], []
                for half in range(2):
                    hd = kv * GROUP + 2 * pr + half
                    s_prev = s[rows, (2 * half) * BLOCK:(2 * half + 1) * BLOCK]
                    s_cur = s[rows, (2 * half + 1) * BLOCK:(2 * half + 2) * BLOCK]
                    if j == 0 and at_seq_start is not None:
                        s_prev = jnp.where(at_seq_start, NEG_INF, s_prev)
                    t = jnp.where(from_cur, s_cur, s_prev)
                    m = jnp.max(t, axis=-1, keepdims=True)
                    p = jnp.exp2(t - m).astype(BF16)
                    zero = jnp.zeros_like(p)
                    p_tiles += [jnp.where(from_cur, zero, p), jnp.where(from_cur, p, zero)]
                    e_sink.append(jnp.exp2(sinks_ref[hd] - m))
                p_rows.append(jnp.concatenate(p_tiles, axis=1))
                sink_terms.append(jnp.where(lane_lo, e_sink[0], e_sink[1]))
            o = jnp.dot(jnp.concatenate(p_rows, axis=0), v_rhs, preferred_element_type=F32)
            for pr in range(2):
                rows = slice(pr * BLOCK, (pr + 1) * BLOCK)
                den = o[rows, LANES:2 * LANES] + sink_terms[pr]
                attn_scr[r0:r0 + BLOCK, q_sl[pr]] = (o[rows, 0:LANES] / den).astype(BF16)

    for scr in kv_scrs:
        scr[0:BLOCK, :] = scr[ts:ts + BLOCK, :]

    attn_out = jnp.dot(attn_scr[...], w_ao_ref[...], preferred_element_type=F32)
    merged = merged + _sigmoid(proj(OFF_GA, D_MODEL)) * attn_out
    out_ref[tile_rows, :] = x + jnp.dot(
        merged.astype(BF16), w_o_ref[...], preferred_element_type=F32)


def _ffn_kernel(x_ref, g_ref, gf_ref, w_gu_ref, w_d_ref, out_ref, act_scr, *, final_norm):
    ts = SEQ_TILE
    for sub in range(x_ref.shape[0] // ts):
        tile_rows = slice(sub * ts, (sub + 1) * ts)
        x = x_ref[tile_rows, :]
        h = _rms_norm(x, g_ref[...]).astype(BF16)
        for off, width in FF_CHUNKS:
            gate = jnp.dot(h, w_gu_ref[:, off:off + width], preferred_element_type=F32)
            up = jnp.dot(h, w_gu_ref[:, D_FF + off:D_FF + off + width],
                         preferred_element_type=F32)
            act_scr[:, off:off + width] = (gate * _sigmoid(gate) * up).astype(BF16)
        y = x + jnp.dot(act_scr[...], w_d_ref[...], preferred_element_type=F32)
        if final_norm:
            y = _rms_norm(y, gf_ref[...])
        out_ref[tile_rows, :] = y


def _resident(shape):
    return pl.BlockSpec(shape, lambda b, s: (0,) * len(shape), pipeline_mode=pl.Buffered(1))


def _rope_tables(seq):
    half = ROT_DIM // 2
    d = jnp.arange(LANES) % HEAD_DIM
    inv_freq = ROPE_THETA ** (-(2 * (d % half)).astype(F32) / ROT_DIM)
    ang = jnp.arange(seq, dtype=F32)[:, None] * inv_freq[None, :]
    cos, sin = jnp.cos(ang), jnp.sin(ang)
    c = jnp.where(d < ROT_DIM, cos, 1.0)
    s_next = jnp.where(d < half, -sin, 0.0)
    s_prev = jnp.where((d >= half) & (d < ROT_DIM), sin, 0.0)
    return jnp.stack([c, s_next, s_prev])


def _mixer(x, g, rope, conv_w, sinks, w_in, w_co, w_ao, w_o, w_gu, w_d):
    b, s, d = x.shape
    ts = SEQ_TILE
    n_s = s // STEP_ROWS
    gu_rows = w_gu.shape[0] // (b * n_s)
    d_rows = w_d.shape[0] // (b * n_s)
    assert gu_rows % BF16_SUBLANES == 0 and d_rows % BF16_SUBLANES == 0
    gu_slab = pl.BlockSpec((gu_rows, w_gu.shape[1]), lambda i, j: (i * n_s + j, 0))
    d_slab = pl.BlockSpec((d_rows, w_d.shape[1]), lambda i, j: (i * n_s + j, 0))
    tile = pl.BlockSpec((None, STEP_ROWS, d), lambda i, j: (i, j, 0))
    kv_scratch = pltpu.VMEM((BLOCK + ts, N_KV_HEADS * LANES), BF16)
    return pl.pallas_call(
        _mixer_kernel,
        grid=(b, n_s),
        in_specs=[
            pl.BlockSpec(memory_space=pltpu.SMEM),
            tile,
            _resident((1, d)),
            pl.BlockSpec((3, STEP_ROWS, LANES), lambda i, j: (0, j, 0)),
            _resident((CONV_K, d)),
            _resident(w_in.shape),
            _resident(w_co.shape),
            _resident(w_ao.shape),
            _resident(w_o.shape),
            gu_slab,
            d_slab,
        ],
        out_specs=[tile, gu_slab, d_slab],
        out_shape=[jax.ShapeDtypeStruct(x.shape, x.dtype),
                   jax.ShapeDtypeStruct(w_gu.shape, BF16),
                   jax.ShapeDtypeStruct(w_d.shape, BF16)],
        scratch_shapes=[
            pltpu.VMEM((SUBLANES + ts, d), F32),
            pltpu.VMEM((ts, D_ATTN), BF16),
            kv_scratch, kv_scratch, kv_scratch, kv_scratch,
            pltpu.VMEM((ts, D_ATTN), BF16),
        ],
        compiler_params=pltpu.CompilerParams(
            dimension_semantics=("arbitrary", "arbitrary"),
            vmem_limit_bytes=VMEM_LIMIT_BYTES),
        name="mixer",
    )(sinks, x, g, rope, conv_w, w_in, w_co, w_ao, w_o, w_gu, w_d)


def _ffn(x, g, g_final, w_gu, w_d, final_norm):
    b, s, d = x.shape
    ts = SEQ_TILE
    tile = pl.BlockSpec((None, STEP_ROWS, d), lambda i, j: (i, j, 0))
    return pl.pallas_call(
        functools.partial(_ffn_kernel, final_norm=final_norm),
        grid=(b, s // STEP_ROWS),
        in_specs=[tile, _resident((1, d)), _resident((1, d)),
                  _resident(w_gu.shape), _resident(w_d.shape)],
        out_specs=tile,
        out_shape=jax.ShapeDtypeStruct(x.shape, x.dtype),
        scratch_shapes=[pltpu.VMEM((ts, D_FF), BF16)],
        compiler_params=pltpu.CompilerParams(
            dimension_semantics=("arbitrary", "arbitrary"),
            vmem_limit_bytes=VMEM_LIMIT_BYTES),
        name="ffn",
    )(x, g, g_final, w_gu, w_d)


def kernel(x, g_mix, w_in, conv_w, attn_sinks, w_conv_out, w_attn_out, w_o,
           g_ffn, w_gate_up, w_down, g_final):
    b, s, d = x.shape
    depth = w_in.shape[0]
    assert d == D_MODEL and s % STEP_ROWS == 0 and STEP_ROWS % SEQ_TILE == 0
    assert w_in.shape[-1] == N_IN and w_gate_up.shape[-1] == 2 * D_FF
    rope = _rope_tables(s)
    g_fin = g_final.reshape(1, d)
    for l in range(depth):
        x, w_gu, w_d = _mixer(
            x, g_mix[l].reshape(1, d), rope, conv_w[l], attn_sinks[l] * LOG2E,
            w_in[l].astype(BF16), w_conv_out[l].astype(BF16), w_attn_out[l].astype(BF16),
            w_o[l].astype(BF16), w_gate_up[l], w_down[l])
        x = _ffn(x, g_ffn[l].reshape(1, d), g_fin, w_gu, w_d, final_norm=(l == depth - 1))
    return x
```

```python
import functools
import math

import jax
import jax.numpy as jnp
from jax import lax
from jax.experimental import pallas as pl
from jax.experimental.pallas import tpu as pltpu

D_MODEL = 1024
CONV_K = 3
HEAD_DIM = 64
N_HEADS = 16
N_KV_HEADS = 4
GROUP = N_HEADS // N_KV_HEADS
D_ATTN = N_HEADS * HEAD_DIM
D_KV = N_KV_HEADS * HEAD_DIM
WINDOW = 128
BLOCK = 128
ROT_DIM = HEAD_DIM // 4
ROPE_THETA = 500000.0
ATTN_SCALE = 1.0 / math.sqrt(HEAD_DIM)
LOG2E = math.log2(math.e)
NEG_INF = -1e30
D_FF = 2816
EPS = 1e-5

OFF_CB = 0
OFF_CC = OFF_CB + D_MODEL
OFF_CX = OFF_CC + D_MODEL
OFF_Q = OFF_CX + D_MODEL
OFF_K = OFF_Q + D_ATTN
OFF_V = OFF_K + D_KV
OFF_GC = OFF_V + D_KV
OFF_GA = OFF_GC + D_MODEL
N_IN = OFF_GA + D_MODEL

LANES = 128
SUBLANES = 8
BF16_SUBLANES = 16
MXU_TILE = 256
SEQ_TILE = 512
FFN_SEQ_TILE = 1024
FF_CHUNKS = ((0, 1024), (1024, 1024), (2048, 768))
VMEM_LIMIT_BYTES = 58 * 1024 * 1024

F32 = jnp.float32
BF16 = jnp.bfloat16

assert WINDOW == BLOCK and 2 * HEAD_DIM == LANES and GROUP == 4


def _rms_norm(x, g):
    ms = jnp.mean(x * x, axis=-1, keepdims=True)
    return x * lax.rsqrt(ms + EPS) * g


def _sigmoid(x):
    return 1.0 / (1.0 + jnp.exp(-x))


def _rope(t, cos, sin_next, sin_prev):
    nxt = pltpu.roll(t, LANES - ROT_DIM // 2, 1)
    prv = pltpu.roll(t, ROT_DIM // 2, 1)
    return t * cos + nxt * sin_next + prv * sin_prev


def _split_head_pair(t, lane_lo):
    sw = pltpu.roll(t, HEAD_DIM, 1)
    zero = jnp.zeros_like(t)
    even = (jnp.where(lane_lo, t, zero), jnp.where(lane_lo, zero, sw))
    odd = (jnp.where(lane_lo, sw, zero), jnp.where(lane_lo, zero, t))
    return even, odd


def _mixer_kernel(sinks_ref, x_ref, g_ref, rope_ref, convw_ref, w_in_ref,
                  w_co_ref, w_ao_ref, w_o_ref, w_gu_f32_ref, w_d_f32_ref,
                  out_ref, w_gu_bf16_ref, w_d_bf16_ref,
                  u_scr, q_scr, klo_scr, khi_scr, vlo_scr, vhi_scr, attn_scr,
                  convy_scr, merged_scr, gate_a_scr):
    ts = x_ref.shape[0]
    n_blk = ts // BLOCK
    s_idx = pl.program_id(1)
    kv_scrs = (klo_scr, khi_scr, vlo_scr, vhi_scr)

    @pl.when(s_idx == 0)
    def _():
        u_scr[0:SUBLANES, :] = jnp.zeros((SUBLANES, D_MODEL), F32)
        for scr in kv_scrs:
            scr[0:BLOCK, :] = jnp.zeros((BLOCK, N_KV_HEADS * LANES), BF16)

    w_gu_bf16_ref[...] = w_gu_f32_ref[...].astype(BF16)
    w_d_bf16_ref[...] = w_d_f32_ref[...].astype(BF16)

    x = x_ref[...]
    h = _rms_norm(x, g_ref[...]).astype(BF16)

    def proj(off, width):
        return jnp.dot(h, w_in_ref[:, off:off + width], preferred_element_type=F32)

    q_tabs = [rope_ref[i] * (ATTN_SCALE * LOG2E) for i in range(3)]
    q = proj(OFF_Q, D_ATTN)
    for t in range(D_ATTN // LANES):
        sl = slice(t * LANES, (t + 1) * LANES)
        q_scr[:, sl] = _rope(q[:, sl], *q_tabs).astype(BF16)
    lane_lo_t = lax.broadcasted_iota(jnp.int32, (ts, LANES), 1) < HEAD_DIM
    k = proj(OFF_K, D_KV)
    v = proj(OFF_V, D_KV)
    for t in range(D_KV // LANES):
        sl = slice(t * LANES, (t + 1) * LANES)
        k_t = _rope(k[:, sl], rope_ref[0], rope_ref[1], rope_ref[2])
        for src, lo_scr, hi_scr in ((k_t, klo_scr, khi_scr), (v[:, sl], vlo_scr, vhi_scr)):
            for hd, (lo, hi) in zip((2 * t, 2 * t + 1), _split_head_pair(src, lane_lo_t)):
                hsl = slice(hd * LANES, (hd + 1) * LANES)
                lo_scr[BLOCK:BLOCK + ts, hsl] = lo.astype(BF16)
                hi_scr[BLOCK:BLOCK + ts, hsl] = hi.astype(BF16)

    u = proj(OFF_CC, D_MODEL) * proj(OFF_CX, D_MODEL)
    u_scr[SUBLANES:SUBLANES + ts, :] = u
    conv = (convw_ref[0:1, :] * u_scr[SUBLANES - 2:SUBLANES - 2 + ts, :]
            + convw_ref[1:2, :] * u_scr[SUBLANES - 1:SUBLANES - 1 + ts, :]
            + convw_ref[2:3, :] * u)
    u_scr[0:SUBLANES, :] = u_scr[ts:ts + SUBLANES, :]

    def chunk(n):
        return slice(n * MXU_TILE, (n + 1) * MXU_TILE)

    def conv_gate_chunk(n):
        convy_scr[:, chunk(n)] = (
            proj(OFF_CB + n * MXU_TILE, MXU_TILE) * conv[:, chunk(n)]).astype(BF16)

    def merge_gate_chunk(n):
        merged_scr[:, chunk(n)] = _sigmoid(proj(OFF_GC + n * MXU_TILE, MXU_TILE))

    def conv_out_chunk(n):
        merged_scr[:, chunk(n)] = merged_scr[:, chunk(n)] * jnp.dot(
            convy_scr[...], w_co_ref[:, chunk(n)], preferred_element_type=F32)

    def attn_gate_chunk(n):
        gate_a_scr[:, chunk(n)] = _sigmoid(proj(OFF_GA + n * MXU_TILE, MXU_TILE))

    n_chunks = D_MODEL // MXU_TILE
    fillers = [functools.partial(f, n)
               for f in (conv_gate_chunk, merge_gate_chunk, conv_out_chunk, attn_gate_chunk)
               for n in range(n_chunks)]

    lane = lax.broadcasted_iota(jnp.int32, (BLOCK, LANES), 1)
    rowi = lax.broadcasted_iota(jnp.int32, (BLOCK, LANES), 0)
    from_cur = lane <= rowi
    lane_lo = lane < HEAD_DIM
    lane2 = lax.broadcasted_iota(jnp.int32, (2 * BLOCK, LANES), 1)
    ones_lo = jnp.where(lane2 < HEAD_DIM, 1.0, 0.0).astype(BF16)
    ones_hi = jnp.where(lane2 < HEAD_DIM, 0.0, 1.0).astype(BF16)
    nt_dims = (((1,), (1,)), ((), ()))

    for j in range(n_blk):
        r0 = j * BLOCK
        for kv in range(N_KV_HEADS):
            kv_sl = slice(kv * LANES, (kv + 1) * LANES)
            q_sl = [slice((2 * kv + pr) * LANES, (2 * kv + pr + 1) * LANES) for pr in range(2)]
            q2 = jnp.concatenate([q_scr[r0:r0 + BLOCK, sl] for sl in q_sl], axis=0)
            k_rhs = jnp.concatenate([klo_scr[r0:r0 + 2 * BLOCK, kv_sl],
                                     khi_scr[r0:r0 + 2 * BLOCK, kv_sl]], axis=0)
            s = lax.dot_general(q2, k_rhs, nt_dims, preferred_element_type=F32)
            v_rhs = jnp.concatenate(
                [jnp.concatenate([vlo_scr[r0:r0 + 2 * BLOCK, kv_sl], ones_lo], axis=1),
                 jnp.concatenate([vhi_scr[r0:r0 + 2 * BLOCK, kv_sl], ones_hi], axis=1)], axis=0)
            p_rows, sink_terms = [], []
            for pr in range(2):
                rows = slice(pr * BLOCK, (pr + 1) * BLOCK)
                p_tiles, e_sink = [], []
                for half in range(2):
                    hd = kv * GROUP + 2 * pr + half
                    s_prev = s[rows, (2 * half) * BLOCK:(2 * half + 1) * BLOCK]
                    s_cur = s[rows, (2 * half + 1) * BLOCK:(2 * half + 2) * BLOCK]
                    if j == 0:
                        s_prev = jnp.where(s_idx > 0, s_prev, NEG_INF)
                    t = jnp.where(from_cur, s_cur, s_prev)
                    m = jnp.max(t, axis=-1, keepdims=True)
                    p = jnp.exp2(t - m).astype(BF16)
                    zero = jnp.zeros_like(p)
                    p_tiles += [jnp.where(from_cur, zero, p), jnp.where(from_cur, p, zero)]
                    e_sink.append(jnp.exp2(sinks_ref[hd] - m))
                p_rows.append(jnp.concatenate(p_tiles, axis=1))
                sink_terms.append(jnp.where(lane_lo, e_sink[0], e_sink[1]))
            o = jnp.dot(jnp.concatenate(p_rows, axis=0), v_rhs, preferred_element_type=F32)
            for pr in range(2):
                rows = slice(pr * BLOCK, (pr + 1) * BLOCK)
                den = o[rows, LANES:2 * LANES] + sink_terms[pr]
                attn_scr[r0:r0 + BLOCK, q_sl[pr]] = (o[rows, 0:LANES] / den).astype(BF16)
            if fillers:
                fillers.pop(0)()

    for f in fillers:
        f()
    for scr in kv_scrs:
        scr[0:BLOCK, :] = scr[ts:ts + BLOCK, :]

    attn_out = jnp.dot(attn_scr[...], w_ao_ref[...], preferred_element_type=F32)
    merged = merged_scr[...] + gate_a_scr[...] * attn_out
    out_ref[...] = x + jnp.dot(merged.astype(BF16), w_o_ref[...], preferred_element_type=F32)


def _ffn_kernel(x_ref, g_ref, gf_ref, w_gu_ref, w_d_ref, out_ref, act_scr, *, final_norm):
    x = x_ref[...]
    h = _rms_norm(x, g_ref[...]).astype(BF16)
    for off, width in FF_CHUNKS:
        gate = jnp.dot(h, w_gu_ref[:, off:off + width], preferred_element_type=F32)
        up = jnp.dot(h, w_gu_ref[:, D_FF + off:D_FF + off + width], preferred_element_type=F32)
        act_scr[:, off:off + width] = (gate * _sigmoid(gate) * up).astype(BF16)
    y = x + jnp.dot(act_scr[...], w_d_ref[...], preferred_element_type=F32)
    if final_norm:
        y = _rms_norm(y, gf_ref[...])
    out_ref[...] = y


def _resident(shape):
    return pl.BlockSpec(shape, lambda b, s: (0,) * len(shape), pipeline_mode=pl.Buffered(1))


def _rope_tables(seq):
    half = ROT_DIM // 2
    d = jnp.arange(LANES) % HEAD_DIM
    inv_freq = ROPE_THETA ** (-(2 * (d % half)).astype(F32) / ROT_DIM)
    ang = jnp.arange(seq, dtype=F32)[:, None] * inv_freq[None, :]
    cos, sin = jnp.cos(ang), jnp.sin(ang)
    c = jnp.where(d < ROT_DIM, cos, 1.0)
    s_next = jnp.where(d < half, -sin, 0.0)
    s_prev = jnp.where((d >= half) & (d < ROT_DIM), sin, 0.0)
    return jnp.stack([c, s_next, s_prev])


def _mixer(x, g, rope, conv_w, sinks, w_in, w_co, w_ao, w_o, w_gu, w_d):
    b, s, d = x.shape
    ts = SEQ_TILE
    n_s = s // ts
    gu_rows = w_gu.shape[0] // (b * n_s)
    d_rows = 2 * w_d.shape[0] // (b * n_s)
    assert gu_rows % BF16_SUBLANES == 0 and d_rows % BF16_SUBLANES == 0
    gu_slab = pl.BlockSpec((gu_rows, w_gu.shape[1]), lambda i, j: (i * n_s + j, 0))
    d_slab = pl.BlockSpec((d_rows, w_d.shape[1]), lambda i, j: ((i * n_s + j) // 2, 0))
    tile = pl.BlockSpec((None, ts, d), lambda i, j: (i, j, 0))
    kv_scratch = pltpu.VMEM((BLOCK + ts, N_KV_HEADS * LANES), BF16)
    return pl.pallas_call(
        _mixer_kernel,
        grid=(b, n_s),
        in_specs=[
            pl.BlockSpec(memory_space=pltpu.SMEM),
            tile,
            _resident((1, d)),
            pl.BlockSpec((3, ts, LANES), lambda i, j: (0, j, 0)),
            _resident((CONV_K, d)),
            _resident(w_in.shape),
            _resident(w_co.shape),
            _resident(w_ao.shape),
            _resident(w_o.shape),
            gu_slab,
            d_slab,
        ],
        out_specs=[tile, gu_slab, d_slab],
        out_shape=[jax.ShapeDtypeStruct(x.shape, x.dtype),
                   jax.ShapeDtypeStruct(w_gu.shape, BF16),
                   jax.ShapeDtypeStruct(w_d.shape, BF16)],
        scratch_shapes=[
            pltpu.VMEM((SUBLANES + ts, d), F32),
            pltpu.VMEM((ts, D_ATTN), BF16),
            kv_scratch, kv_scratch, kv_scratch, kv_scratch,
            pltpu.VMEM((ts, D_ATTN), BF16),
            pltpu.VMEM((ts, d), BF16),
            pltpu.VMEM((ts, d), F32),
            pltpu.VMEM((ts, d), F32),
        ],
        compiler_params=pltpu.CompilerParams(
            dimension_semantics=("arbitrary", "arbitrary"),
            vmem_limit_bytes=VMEM_LIMIT_BYTES),
        name="mixer",
    )(sinks, x, g, rope, conv_w, w_in, w_co, w_ao, w_o, w_gu, w_d)


def _ffn(x, g, g_final, w_gu, w_d, final_norm):
    b, s, d = x.shape
    ts = FFN_SEQ_TILE
    tile = pl.BlockSpec((None, ts, d), lambda i, j: (i, j, 0))
    return pl.pallas_call(
        functools.partial(_ffn_kernel, final_norm=final_norm),
        grid=(b, s // ts),
        in_specs=[tile, _resident((1, d)), _resident((1, d)),
                  _resident(w_gu.shape), _resident(w_d.shape)],
        out_specs=tile,
        out_shape=jax.ShapeDtypeStruct(x.shape, x.dtype),
        scratch_shapes=[pltpu.VMEM((ts, D_FF), BF16)],
        compiler_params=pltpu.CompilerParams(
            dimension_semantics=("arbitrary", "arbitrary"),
            vmem_limit_bytes=VMEM_LIMIT_BYTES),
        name="ffn",
    )(x, g, g_final, w_gu, w_d)


def kernel(x, g_mix, w_in, conv_w, attn_sinks, w_conv_out, w_attn_out, w_o,
           g_ffn, w_gate_up, w_down, g_final):
    b, s, d = x.shape
    depth = w_in.shape[0]
    assert d == D_MODEL and s % SEQ_TILE == 0 and s % FFN_SEQ_TILE == 0
    assert w_in.shape[-1] == N_IN and w_gate_up.shape[-1] == 2 * D_FF
    rope = _rope_tables(s)
    g_fin = g_final.reshape(1, d)
    for l in range(depth):
        x, w_gu, w_d = _mixer(
            x, g_mix[l].reshape(1, d), rope, conv_w[l], attn_sinks[l] * LOG2E,
            w_in[l].astype(BF16), w_conv_out[l].astype(BF16), w_attn_out[l].astype(BF16),
            w_o[l].astype(BF16), w_gate_up[l], w_down[l])
        x = _ffn(x, g_ffn[l].reshape(1, d), g_fin, w_gu, w_d, final_norm=(l == depth - 1))
    return x
```

```python
import functools
import math

import jax
import jax.numpy as jnp
from jax import lax
from jax.experimental import pallas as pl
from jax.experimental.pallas import tpu as pltpu

D_MODEL = 1024
CONV_K = 3
HEAD_DIM = 64
N_HEADS = 16
N_KV_HEADS = 4
GROUP = N_HEADS // N_KV_HEADS
D_ATTN = N_HEADS * HEAD_DIM
D_KV = N_KV_HEADS * HEAD_DIM
WINDOW = 128
BLOCK = 128
ROT_DIM = HEAD_DIM // 4
ROPE_THETA = 500000.0
ATTN_SCALE = 1.0 / math.sqrt(HEAD_DIM)
LOG2E = math.log2(math.e)
NEG_INF = -1e30
D_FF = 2816
EPS = 1e-5

OFF_CB = 0
OFF_CC = OFF_CB + D_MODEL
OFF_CX = OFF_CC + D_MODEL
OFF_Q = OFF_CX + D_MODEL
OFF_K = OFF_Q + D_ATTN
OFF_V = OFF_K + D_KV
OFF_GC = OFF_V + D_KV
OFF_GA = OFF_GC + D_MODEL
N_IN = OFF_GA + D_MODEL

LANES = 128
SUBLANES = 8
BF16_SUBLANES = 16
SEQ_TILE = 512
FFN_SEQ_TILE = 1024
FF_CHUNKS = ((0, 1024), (1024, 1024), (2048, 768))
VMEM_LIMIT_BYTES = 58 * 1024 * 1024

F32 = jnp.float32
BF16 = jnp.bfloat16

assert WINDOW == BLOCK and 2 * HEAD_DIM == LANES and GROUP == 4


def _rms_norm(x, g):
    ms = jnp.mean(x * x, axis=-1, keepdims=True)
    return x * lax.rsqrt(ms + EPS) * g


def _sigmoid(x):
    return 1.0 / (1.0 + jnp.exp(-x))


def _rope(t, cos, sin_next, sin_prev):
    nxt = pltpu.roll(t, LANES - ROT_DIM // 2, 1)
    prv = pltpu.roll(t, ROT_DIM // 2, 1)
    return t * cos + nxt * sin_next + prv * sin_prev


def _split_head_pair(t, lane_lo):
    sw = pltpu.roll(t, HEAD_DIM, 1)
    zero = jnp.zeros_like(t)
    even = (jnp.where(lane_lo, t, zero), jnp.where(lane_lo, zero, sw))
    odd = (jnp.where(lane_lo, sw, zero), jnp.where(lane_lo, zero, t))
    return even, odd


def _mixer_kernel(sinks_ref, x_ref, g_ref, rope_ref, convw_ref, w_in_ref,
                  w_co_ref, w_ao_ref, w_o_ref, w_gu_f32_ref, w_d_f32_ref,
                  out_ref, w_gu_bf16_ref, w_d_bf16_ref,
                  u_scr, q_scr, klo_scr, khi_scr, vlo_scr, vhi_scr, attn_scr):
    ts = x_ref.shape[0]
    n_blk = ts // BLOCK
    s_idx = pl.program_id(1)
    kv_scrs = (klo_scr, khi_scr, vlo_scr, vhi_scr)

    @pl.when(s_idx == 0)
    def _():
        u_scr[0:SUBLANES, :] = jnp.zeros((SUBLANES, D_MODEL), F32)
        for scr in kv_scrs:
            scr[0:BLOCK, :] = jnp.zeros((BLOCK, N_KV_HEADS * LANES), BF16)

    w_gu_bf16_ref[...] = w_gu_f32_ref[...].astype(BF16)
    w_d_bf16_ref[...] = w_d_f32_ref[...].astype(BF16)

    x = x_ref[...]
    h = _rms_norm(x, g_ref[...]).astype(BF16)

    def proj(off, width):
        return jnp.dot(h, w_in_ref[:, off:off + width], preferred_element_type=F32)

    u = proj(OFF_CC, D_MODEL) * proj(OFF_CX, D_MODEL)
    u_scr[SUBLANES:SUBLANES + ts, :] = u
    conv = (convw_ref[0:1, :] * u_scr[SUBLANES - 2:SUBLANES - 2 + ts, :]
            + convw_ref[1:2, :] * u_scr[SUBLANES - 1:SUBLANES - 1 + ts, :]
            + convw_ref[2:3, :] * u)
    u_scr[0:SUBLANES, :] = u_scr[ts:ts + SUBLANES, :]
    conv_y = proj(OFF_CB, D_MODEL) * conv
    conv_out = jnp.dot(conv_y, w_co_ref[...], preferred_element_type=F32)
    merged = _sigmoid(proj(OFF_GC, D_MODEL)) * conv_out

    q_tabs = [rope_ref[i] * (ATTN_SCALE * LOG2E) for i in range(3)]
    q = proj(OFF_Q, D_ATTN)
    for t in range(D_ATTN // LANES):
        sl = slice(t * LANES, (t + 1) * LANES)
        q_scr[:, sl] = _rope(q[:, sl], *q_tabs).astype(BF16)
    lane_lo_t = lax.broadcasted_iota(jnp.int32, (ts, LANES), 1) < HEAD_DIM
    k = proj(OFF_K, D_KV)
    v = proj(OFF_V, D_KV)
    for t in range(D_KV // LANES):
        sl = slice(t * LANES, (t + 1) * LANES)
        k_t = _rope(k[:, sl], rope_ref[0], rope_ref[1], rope_ref[2])
        for src, lo_scr, hi_scr in ((k_t, klo_scr, khi_scr), (v[:, sl], vlo_scr, vhi_scr)):
            for hd, (lo, hi) in zip((2 * t, 2 * t + 1), _split_head_pair(src, lane_lo_t)):
                hsl = slice(hd * LANES, (hd + 1) * LANES)
                lo_scr[BLOCK:BLOCK + ts, hsl] = lo.astype(BF16)
                hi_scr[BLOCK:BLOCK + ts, hsl] = hi.astype(BF16)

    lane = lax.broadcasted_iota(jnp.int32, (BLOCK, LANES), 1)
    rowi = lax.broadcasted_iota(jnp.int32, (BLOCK, LANES), 0)
    from_cur = lane <= rowi
    lane_lo = lane < HEAD_DIM
    lane2 = lax.broadcasted_iota(jnp.int32, (2 * BLOCK, LANES), 1)
    ones_lo = jnp.where(lane2 < HEAD_DIM, 1.0, 0.0).astype(BF16)
    ones_hi = jnp.where(lane2 < HEAD_DIM, 0.0, 1.0).astype(BF16)
    nt_dims = (((1,), (1,)), ((), ()))

    for j in range(n_blk):
        r0 = j * BLOCK
        for kv in range(N_KV_HEADS):
            kv_sl = slice(kv * LANES, (kv + 1) * LANES)
            q_sl = [slice((2 * kv + pr) * LANES, (2 * kv + pr + 1) * LANES) for pr in range(2)]
            q2 = jnp.concatenate([q_scr[r0:r0 + BLOCK, sl] for sl in q_sl], axis=0)
            k_rhs = jnp.concatenate([klo_scr[r0:r0 + 2 * BLOCK, kv_sl],
                                     khi_scr[r0:r0 + 2 * BLOCK, kv_sl]], axis=0)
            s = lax.dot_general(q2, k_rhs, nt_dims, preferred_element_type=F32)
            v_rhs = jnp.concatenate(
                [jnp.concatenate([vlo_scr[r0:r0 + 2 * BLOCK, kv_sl], ones_lo], axis=1),
                 jnp.concatenate([vhi_scr[r0:r0 + 2 * BLOCK, kv_sl], ones_hi], axis=1)], axis=0)
            p_rows, sink_terms = [], []
            for pr in range(2):
                rows = slice(pr * BLOCK, (pr + 1) * BLOCK)
                p_tiles, e_sink = [], []
                for half in range(2):
                    hd = kv * GROUP + 2 * pr + half
                    s_prev = s[rows, (2 * half) * BLOCK:(2 * half + 1) * BLOCK]
                    s_cur = s[rows, (2 * half + 1) * BLOCK:(2 * half + 2) * BLOCK]
                    if j == 0:
                        s_prev = jnp.where(s_idx > 0, s_prev, NEG_INF)
                    t = jnp.where(from_cur, s_cur, s_prev)
                    m = jnp.max(t, axis=-1, keepdims=True)
                    p = jnp.exp2(t - m).astype(BF16)
                    zero = jnp.zeros_like(p)
                    p_tiles += [jnp.where(from_cur, zero, p), jnp.where(from_cur, p, zero)]
                    e_sink.append(jnp.exp2(sinks_ref[hd] - m))
                p_rows.append(jnp.concatenate(p_tiles, axis=1))
                sink_terms.append(jnp.where(lane_lo, e_sink[0], e_sink[1]))
            o = jnp.dot(jnp.concatenate(p_rows, axis=0), v_rhs, preferred_element_type=F32)
            for pr in range(2):
                rows = slice(pr * BLOCK, (pr + 1) * BLOCK)
                den = o[rows, LANES:2 * LANES] + sink_terms[pr]
                attn_scr[r0:r0 + BLOCK, q_sl[pr]] = o[rows, 0:LANES] / den

    for scr in kv_scrs:
        scr[0:BLOCK, :] = scr[ts:ts + BLOCK, :]

    attn_out = jnp.dot(attn_scr[...], w_ao_ref[...], preferred_element_type=F32)
    merged = merged + _sigmoid(proj(OFF_GA, D_MODEL)) * attn_out
    out_ref[...] = x + jnp.dot(merged, w_o_ref[...], preferred_element_type=F32)


def _ffn_kernel(x_ref, g_ref, gf_ref, w_gu_ref, w_d_ref, out_ref, act_scr, *, final_norm):
    x = x_ref[...]
    h = _rms_norm(x, g_ref[...]).astype(BF16)
    for off, width in FF_CHUNKS:
        gate = jnp.dot(h, w_gu_ref[:, off:off + width], preferred_element_type=F32)
        up = jnp.dot(h, w_gu_ref[:, D_FF + off:D_FF + off + width], preferred_element_type=F32)
        act_scr[:, off:off + width] = (gate * _sigmoid(gate) * up).astype(BF16)
    y = x + jnp.dot(act_scr[...], w_d_ref[...], preferred_element_type=F32)
    if final_norm:
        y = _rms_norm(y, gf_ref[...])
    out_ref[...] = y


def _resident(shape):
    return pl.BlockSpec(shape, lambda b, s: (0,) * len(shape), pipeline_mode=pl.Buffered(1))


def _rope_tables(seq):
    half = ROT_DIM // 2
    d = jnp.arange(LANES) % HEAD_DIM
    inv_freq = ROPE_THETA ** (-(2 * (d % half)).astype(F32) / ROT_DIM)
    ang = jnp.arange(seq, dtype=F32)[:, None] * inv_freq[None, :]
    cos, sin = jnp.cos(ang), jnp.sin(ang)
    c = jnp.where(d < ROT_DIM, cos, 1.0)
    s_next = jnp.where(d < half, -sin, 0.0)
    s_prev = jnp.where((d >= half) & (d < ROT_DIM), sin, 0.0)
    return jnp.stack([c, s_next, s_prev])


def _mixer(x, g, rope, conv_w, sinks, w_in, w_co, w_ao, w_o, w_gu, w_d):
    b, s, d = x.shape
    ts = SEQ_TILE
    n_s = s // ts
    gu_rows = w_gu.shape[0] // (b * n_s)
    d_rows = 2 * w_d.shape[0] // (b * n_s)
    assert gu_rows % BF16_SUBLANES == 0 and d_rows % BF16_SUBLANES == 0
    gu_slab = pl.BlockSpec((gu_rows, w_gu.shape[1]), lambda i, j: (i * n_s + j, 0))
    d_slab = pl.BlockSpec((d_rows, w_d.shape[1]), lambda i, j: ((i * n_s + j) // 2, 0))
    tile = pl.BlockSpec((None, ts, d), lambda i, j: (i, j, 0))
    kv_scratch = pltpu.VMEM((BLOCK + ts, N_KV_HEADS * LANES), BF16)
    return pl.pallas_call(
        _mixer_kernel,
        grid=(b, n_s),
        in_specs=[
            pl.BlockSpec(memory_space=pltpu.SMEM),
            tile,
            _resident((1, d)),
            pl.BlockSpec((3, ts, LANES), lambda i, j: (0, j, 0)),
            _resident((CONV_K, d)),
            _resident(w_in.shape),
            _resident(w_co.shape),
            _resident(w_ao.shape),
            _resident(w_o.shape),
            gu_slab,
            d_slab,
        ],
        out_specs=[tile, gu_slab, d_slab],
        out_shape=[jax.ShapeDtypeStruct(x.shape, x.dtype),
                   jax.ShapeDtypeStruct(w_gu.shape, BF16),
                   jax.ShapeDtypeStruct(w_d.shape, BF16)],
        scratch_shapes=[
            pltpu.VMEM((SUBLANES + ts, d), F32),
            pltpu.VMEM((ts, D_ATTN), BF16),
            kv_scratch, kv_scratch, kv_scratch, kv_scratch,
            pltpu.VMEM((ts, D_ATTN), F32),
        ],
        compiler_params=pltpu.CompilerParams(
            dimension_semantics=("arbitrary", "arbitrary"),
            vmem_limit_bytes=VMEM_LIMIT_BYTES),
        name="mixer",
    )(sinks, x, g, rope, conv_w, w_in, w_co, w_ao, w_o, w_gu, w_d)


def _ffn(x, g, g_final, w_gu, w_d, final_norm):
    b, s, d = x.shape
    ts = FFN_SEQ_TILE
    tile = pl.BlockSpec((None, ts, d), lambda i, j: (i, j, 0))
    return pl.pallas_call(
        functools.partial(_ffn_kernel, final_norm=final_norm),
        grid=(b, s // ts),
        in_specs=[tile, _resident((1, d)), _resident((1, d)),
                  _resident(w_gu.shape), _resident(w_d.shape)],
        out_specs=tile,
        out_shape=jax.ShapeDtypeStruct(x.shape, x.dtype),
        scratch_shapes=[pltpu.VMEM((ts, D_FF), BF16)],
        compiler_params=pltpu.CompilerParams(
            dimension_semantics=("arbitrary", "arbitrary"),
            vmem_limit_bytes=VMEM_LIMIT_BYTES),
        name="ffn",
    )(x, g, g_final, w_gu, w_d)


def kernel(x, g_mix, w_in, conv_w, attn_sinks, w_conv_out, w_attn_out, w_o,
           g_ffn, w_gate_up, w_down, g_final):
    b, s, d = x.shape
    depth = w_in.shape[0]
    assert d == D_MODEL and s % SEQ_TILE == 0 and s % FFN_SEQ_TILE == 0
    assert w_in.shape[-1] == N_IN and w_gate_up.shape[-1] == 2 * D_FF
    rope = _rope_tables(s)
    g_fin = g_final.reshape(1, d)
    for l in range(depth):
        x, w_gu, w_d = _mixer(
            x, g_mix[l].reshape(1, d), rope, conv_w[l], attn_sinks[l] * LOG2E,
            w_in[l].astype(BF16), w_conv_out[l], w_attn_out[l], w_o[l],
            w_gate_up[l], w_down[l])
        x = _ffn(x, g_ffn[l].reshape(1, d), g_fin, w_gu, w_d, final_norm=(l == depth - 1))
    return x
```

```python
import functools
import math

import jax
import jax.numpy as jnp
from jax import lax
from jax.experimental import pallas as pl
from jax.experimental.pallas import tpu as pltpu

D_MODEL = 1024
CONV_K = 3
HEAD_DIM = 64
N_HEADS = 16
N_KV_HEADS = 4
GROUP = N_HEADS // N_KV_HEADS
D_ATTN = N_HEADS * HEAD_DIM
D_KV = N_KV_HEADS * HEAD_DIM
WINDOW = 128
BLOCK = 128
ROT_DIM = HEAD_DIM // 4
ROPE_THETA = 500000.0
ATTN_SCALE = 1.0 / math.sqrt(HEAD_DIM)
LOG2E = math.log2(math.e)
NEG_INF = -1e30
D_FF = 2816
EPS = 1e-5

OFF_CB = 0
OFF_CC = OFF_CB + D_MODEL
OFF_CX = OFF_CC + D_MODEL
OFF_Q = OFF_CX + D_MODEL
OFF_K = OFF_Q + D_ATTN
OFF_V = OFF_K + D_KV
OFF_GC = OFF_V + D_KV
OFF_GA = OFF_GC + D_MODEL
N_IN = OFF_GA + D_MODEL

LANES = 128
SUBLANES = 8
BF16_SUBLANES = 16
SEQ_TILE = 512
FFN_SEQ_TILE = 1024
STAGE_COLS = 512
FF_CHUNKS = ((0, 1024), (1024, 1024), (2048, 768))
VMEM_LIMIT_BYTES = 58 * 1024 * 1024

F32 = jnp.float32
BF16 = jnp.bfloat16

assert WINDOW == BLOCK and 2 * HEAD_DIM == LANES and GROUP == 4


def _rms_norm(x, g):
    ms = jnp.mean(x * x, axis=-1, keepdims=True)
    return x * lax.rsqrt(ms + EPS) * g


def _sigmoid(x):
    return 1.0 / (1.0 + jnp.exp(-x))


def _rope(t, cos, sin_next, sin_prev):
    nxt = pltpu.roll(t, LANES - ROT_DIM // 2, 1)
    prv = pltpu.roll(t, ROT_DIM // 2, 1)
    return t * cos + nxt * sin_next + prv * sin_prev


def _split_head_pair(t, lane_lo):
    sw = pltpu.roll(t, HEAD_DIM, 1)
    zero = jnp.zeros_like(t)
    even = (jnp.where(lane_lo, t, zero), jnp.where(lane_lo, zero, sw))
    odd = (jnp.where(lane_lo, sw, zero), jnp.where(lane_lo, zero, t))
    return even, odd


def _stage_weights(pairs, stage_ref, sem_ref):
    jobs = [(src, dst, c * STAGE_COLS)
            for src, dst in pairs for c in range(src.shape[1] // STAGE_COLS)]

    def slab_copy(n):
        src, _, col = jobs[n]
        return pltpu.make_async_copy(
            src.at[:, pl.ds(col, STAGE_COLS)], stage_ref.at[n % 2], sem_ref.at[n % 2])

    slab_copy(0).start()
    for n, (_, dst, col) in enumerate(jobs):
        if n + 1 < len(jobs):
            slab_copy(n + 1).start()
        slab_copy(n).wait()
        dst[:, col:col + STAGE_COLS] = stage_ref[n % 2].astype(BF16)


def _mixer_kernel(sinks_ref, x_ref, g_ref, rope_ref, convw_ref, w_in_hbm,
                  w_co_hbm, w_ao_hbm, w_o_hbm, w_gu_f32_ref, w_d_f32_ref,
                  out_ref, w_gu_bf16_ref, w_d_bf16_ref,
                  w_in_ref, w_co_ref, w_ao_ref, w_o_ref, stage_ref, stage_sem,
                  u_scr, q_scr, klo_scr, khi_scr, vlo_scr, vhi_scr, attn_scr):
    ts = x_ref.shape[0]
    n_blk = ts // BLOCK
    s_idx = pl.program_id(1)
    kv_scrs = (klo_scr, khi_scr, vlo_scr, vhi_scr)

    @pl.when((pl.program_id(0) == 0) & (s_idx == 0))
    def _():
        _stage_weights(((w_in_hbm, w_in_ref), (w_co_hbm, w_co_ref),
                        (w_ao_hbm, w_ao_ref), (w_o_hbm, w_o_ref)), stage_ref, stage_sem)

    @pl.when(s_idx == 0)
    def _():
        u_scr[0:SUBLANES, :] = jnp.zeros((SUBLANES, D_MODEL), F32)
        for scr in kv_scrs:
            scr[0:BLOCK, :] = jnp.zeros((BLOCK, N_KV_HEADS * LANES), BF16)

    w_gu_bf16_ref[...] = w_gu_f32_ref[...].astype(BF16)
    w_d_bf16_ref[...] = w_d_f32_ref[...].astype(BF16)

    x = x_ref[...]
    h = _rms_norm(x, g_ref[...]).astype(BF16)

    def proj(off, width):
        return jnp.dot(h, w_in_ref[:, off:off + width], preferred_element_type=F32)

    u = proj(OFF_CC, D_MODEL) * proj(OFF_CX, D_MODEL)
    u_scr[SUBLANES:SUBLANES + ts, :] = u
    conv = (convw_ref[0:1, :] * u_scr[SUBLANES - 2:SUBLANES - 2 + ts, :]
            + convw_ref[1:2, :] * u_scr[SUBLANES - 1:SUBLANES - 1 + ts, :]
            + convw_ref[2:3, :] * u)
    u_scr[0:SUBLANES, :] = u_scr[ts:ts + SUBLANES, :]
    conv_y = (proj(OFF_CB, D_MODEL) * conv).astype(BF16)
    conv_out = jnp.dot(conv_y, w_co_ref[...], preferred_element_type=F32)
    merged = _sigmoid(proj(OFF_GC, D_MODEL)) * conv_out

    q_tabs = [rope_ref[i] * (ATTN_SCALE * LOG2E) for i in range(3)]
    q = proj(OFF_Q, D_ATTN)
    for t in range(D_ATTN // LANES):
        sl = slice(t * LANES, (t + 1) * LANES)
        q_scr[:, sl] = _rope(q[:, sl], *q_tabs).astype(BF16)
    lane_lo_t = lax.broadcasted_iota(jnp.int32, (ts, LANES), 1) < HEAD_DIM
    k = proj(OFF_K, D_KV)
    v = proj(OFF_V, D_KV)
    for t in range(D_KV // LANES):
        sl = slice(t * LANES, (t + 1) * LANES)
        k_t = _rope(k[:, sl], rope_ref[0], rope_ref[1], rope_ref[2])
        for src, lo_scr, hi_scr in ((k_t, klo_scr, khi_scr), (v[:, sl], vlo_scr, vhi_scr)):
            for hd, (lo, hi) in zip((2 * t, 2 * t + 1), _split_head_pair(src, lane_lo_t)):
                hsl = slice(hd * LANES, (hd + 1) * LANES)
                lo_scr[BLOCK:BLOCK + ts, hsl] = lo.astype(BF16)
                hi_scr[BLOCK:BLOCK + ts, hsl] = hi.astype(BF16)

    lane = lax.broadcasted_iota(jnp.int32, (BLOCK, LANES), 1)
    rowi = lax.broadcasted_iota(jnp.int32, (BLOCK, LANES), 0)
    from_cur = lane <= rowi
    lane_lo = lane < HEAD_DIM
    lane2 = lax.broadcasted_iota(jnp.int32, (2 * BLOCK, LANES), 1)
    ones_lo = jnp.where(lane2 < HEAD_DIM, 1.0, 0.0).astype(BF16)
    ones_hi = jnp.where(lane2 < HEAD_DIM, 0.0, 1.0).astype(BF16)
    nt_dims = (((1,), (1,)), ((), ()))

    for j in range(n_blk):
        r0 = j * BLOCK
        for kv in range(N_KV_HEADS):
            kv_sl = slice(kv * LANES, (kv + 1) * LANES)
            q_sl = [slice((2 * kv + pr) * LANES, (2 * kv + pr + 1) * LANES) for pr in range(2)]
            q2 = jnp.concatenate([q_scr[r0:r0 + BLOCK, sl] for sl in q_sl], axis=0)
            k_rhs = jnp.concatenate([klo_scr[r0:r0 + 2 * BLOCK, kv_sl],
                                     khi_scr[r0:r0 + 2 * BLOCK, kv_sl]], axis=0)
            s = lax.dot_general(q2, k_rhs, nt_dims, preferred_element_type=F32)
            v_rhs = jnp.concatenate(
                [jnp.concatenate([vlo_scr[r0:r0 + 2 * BLOCK, kv_sl], ones_lo], axis=1),
                 jnp.concatenate([vhi_scr[r0:r0 + 2 * BLOCK, kv_sl], ones_hi], axis=1)], axis=0)
            p_rows, sink_terms = [], []
            for pr in range(2):
                rows = slice(pr * BLOCK, (pr + 1) * BLOCK)
                p_tiles, e_sink = [], []
                for half in range(2):
                    hd = kv * GROUP + 2 * pr + half
                    s_prev = s[rows, (2 * half) * BLOCK:(2 * half + 1) * BLOCK]
                    s_cur = s[rows, (2 * half + 1) * BLOCK:(2 * half + 2) * BLOCK]
                    if j == 0:
                        s_prev = jnp.where(s_idx > 0, s_prev, NEG_INF)
                    t = jnp.where(from_cur, s_cur, s_prev)
                    m = jnp.max(t, axis=-1, keepdims=True)
                    p = jnp.exp2(t - m).astype(BF16)
                    zero = jnp.zeros_like(p)
                    p_tiles += [jnp.where(from_cur, zero, p), jnp.where(from_cur, p, zero)]
                    e_sink.append(jnp.exp2(sinks_ref[hd] - m))
                p_rows.append(jnp.concatenate(p_tiles, axis=1))
                sink_terms.append(jnp.where(lane_lo, e_sink[0], e_sink[1]))
            o = jnp.dot(jnp.concatenate(p_rows, axis=0), v_rhs, preferred_element_type=F32)
            for pr in range(2):
                rows = slice(pr * BLOCK, (pr + 1) * BLOCK)
                den = o[rows, LANES:2 * LANES] + sink_terms[pr]
                attn_scr[r0:r0 + BLOCK, q_sl[pr]] = (o[rows, 0:LANES] / den).astype(BF16)

    for scr in kv_scrs:
        scr[0:BLOCK, :] = scr[ts:ts + BLOCK, :]

    attn_out = jnp.dot(attn_scr[...], w_ao_ref[...], preferred_element_type=F32)
    merged = merged + _sigmoid(proj(OFF_GA, D_MODEL)) * attn_out
    out_ref[...] = x + jnp.dot(merged.astype(BF16), w_o_ref[...], preferred_element_type=F32)


def _ffn_kernel(x_ref, g_ref, gf_ref, w_gu_ref, w_d_ref, out_ref, act_scr, *, final_norm):
    x = x_ref[...]
    h = _rms_norm(x, g_ref[...]).astype(BF16)
    for off, width in FF_CHUNKS:
        gate = jnp.dot(h, w_gu_ref[:, off:off + width], preferred_element_type=F32)
        up = jnp.dot(h, w_gu_ref[:, D_FF + off:D_FF + off + width], preferred_element_type=F32)
        act_scr[:, off:off + width] = (gate * _sigmoid(gate) * up).astype(BF16)
    y = x + jnp.dot(act_scr[...], w_d_ref[...], preferred_element_type=F32)
    if final_norm:
        y = _rms_norm(y, gf_ref[...])
    out_ref[...] = y


def _resident(shape):
    return pl.BlockSpec(shape, lambda b, s: (0,) * len(shape), pipeline_mode=pl.Buffered(1))


def _rope_tables(seq):
    half = ROT_DIM // 2
    d = jnp.arange(LANES) % HEAD_DIM
    inv_freq = ROPE_THETA ** (-(2 * (d % half)).astype(F32) / ROT_DIM)
    ang = jnp.arange(seq, dtype=F32)[:, None] * inv_freq[None, :]
    cos, sin = jnp.cos(ang), jnp.sin(ang)
    c = jnp.where(d < ROT_DIM, cos, 1.0)
    s_next = jnp.where(d < half, -sin, 0.0)
    s_prev = jnp.where((d >= half) & (d < ROT_DIM), sin, 0.0)
    return jnp.stack([c, s_next, s_prev])


def _mixer(x, g, rope, conv_w, sinks, w_in, w_co, w_ao, w_o, w_gu, w_d):
    b, s, d = x.shape
    ts = SEQ_TILE
    n_s = s // ts
    gu_rows = w_gu.shape[0] // (b * n_s)
    d_rows = 2 * w_d.shape[0] // (b * n_s)
    assert gu_rows % BF16_SUBLANES == 0 and d_rows % BF16_SUBLANES == 0
    gu_slab = pl.BlockSpec((gu_rows, w_gu.shape[1]), lambda i, j: (i * n_s + j, 0))
    d_slab = pl.BlockSpec((d_rows, w_d.shape[1]), lambda i, j: ((i * n_s + j) // 2, 0))
    tile = pl.BlockSpec((None, ts, d), lambda i, j: (i, j, 0))
    kv_scratch = pltpu.VMEM((BLOCK + ts, N_KV_HEADS * LANES), BF16)
    return pl.pallas_call(
        _mixer_kernel,
        grid=(b, n_s),
        in_specs=[
            pl.BlockSpec(memory_space=pltpu.SMEM),
            tile,
            _resident((1, d)),
            pl.BlockSpec((3, ts, LANES), lambda i, j: (0, j, 0)),
            _resident((CONV_K, d)),
            pl.BlockSpec(memory_space=pl.ANY),
            pl.BlockSpec(memory_space=pl.ANY),
            pl.BlockSpec(memory_space=pl.ANY),
            pl.BlockSpec(memory_space=pl.ANY),
            gu_slab,
            d_slab,
        ],
        out_specs=[tile, gu_slab, d_slab],
        out_shape=[jax.ShapeDtypeStruct(x.shape, x.dtype),
                   jax.ShapeDtypeStruct(w_gu.shape, BF16),
                   jax.ShapeDtypeStruct(w_d.shape, BF16)],
        scratch_shapes=[
            pltpu.VMEM(w_in.shape, BF16),
            pltpu.VMEM(w_co.shape, BF16),
            pltpu.VMEM(w_ao.shape, BF16),
            pltpu.VMEM(w_o.shape, BF16),
            pltpu.VMEM((2, d, STAGE_COLS), F32),
            pltpu.SemaphoreType.DMA((2,)),
            pltpu.VMEM((SUBLANES + ts, d), F32),
            pltpu.VMEM((ts, D_ATTN), BF16),
            kv_scratch, kv_scratch, kv_scratch, kv_scratch,
            pltpu.VMEM((ts, D_ATTN), BF16),
        ],
        compiler_params=pltpu.CompilerParams(
            dimension_semantics=("arbitrary", "arbitrary"),
            vmem_limit_bytes=VMEM_LIMIT_BYTES),
        name="mixer",
    )(sinks, x, g, rope, conv_w, w_in, w_co, w_ao, w_o, w_gu, w_d)


def _ffn(x, g, g_final, w_gu, w_d, final_norm):
    b, s, d = x.shape
    ts = FFN_SEQ_TILE
    tile = pl.BlockSpec((None, ts, d), lambda i, j: (i, j, 0))
    return pl.pallas_call(
        functools.partial(_ffn_kernel, final_norm=final_norm),
        grid=(b, s // ts),
        in_specs=[tile, _resident((1, d)), _resident((1, d)),
                  _resident(w_gu.shape), _resident(w_d.shape)],
        out_specs=tile,
        out_shape=jax.ShapeDtypeStruct(x.shape, x.dtype),
        scratch_shapes=[pltpu.VMEM((ts, D_FF), BF16)],
        compiler_params=pltpu.CompilerParams(
            dimension_semantics=("arbitrary", "arbitrary"),
            vmem_limit_bytes=VMEM_LIMIT_BYTES),
        name="ffn",
    )(x, g, g_final, w_gu, w_d)


def kernel(x, g_mix, w_in, conv_w, attn_sinks, w_conv_out, w_attn_out, w_o,
           g_ffn, w_gate_up, w_down, g_final):
    b, s, d = x.shape
    depth = w_in.shape[0]
    assert d == D_MODEL and s % SEQ_TILE == 0 and s % FFN_SEQ_TILE == 0
    assert w_in.shape[-1] == N_IN and w_gate_up.shape[-1] == 2 * D_FF
    assert N_IN % STAGE_COLS == 0 and d % STAGE_COLS == 0
    rope = _rope_tables(s)
    g_fin = g_final.reshape(1, d)
    for l in range(depth):
        x, w_gu, w_d = _mixer(
            x, g_mix[l].reshape(1, d), rope, conv_w[l], attn_sinks[l] * LOG2E,
            w_in[l], w_conv_out[l], w_attn_out[l], w_o[l],
            w_gate_up[l], w_down[l])
        x = _ffn(x, g_ffn[l].reshape(1, d), g_fin, w_gu, w_d, final_norm=(l == depth - 1))
    return x
```

```python
import functools
import math

import jax
import jax.numpy as jnp
from jax import lax
from jax.experimental import pallas as pl
from jax.experimental.pallas import tpu as pltpu

D_MODEL = 1024
CONV_K = 3
HEAD_DIM = 64
N_HEADS = 16
N_KV_HEADS = 4
GROUP = N_HEADS // N_KV_HEADS
D_ATTN = N_HEADS * HEAD_DIM
D_KV = N_KV_HEADS * HEAD_DIM
WINDOW = 128
BLOCK = 128
ROT_DIM = HEAD_DIM // 4
ROPE_THETA = 500000.0
ATTN_SCALE = 1.0 / math.sqrt(HEAD_DIM)
LOG2E = math.log2(math.e)
NEG_INF = -1e30
D_FF = 2816
EPS = 1e-5

OFF_CB = 0
OFF_CC = OFF_CB + D_MODEL
OFF_CX = OFF_CC + D_MODEL
OFF_Q = OFF_CX + D_MODEL
OFF_K = OFF_Q + D_ATTN
OFF_V = OFF_K + D_KV
OFF_GC = OFF_V + D_KV
OFF_GA = OFF_GC + D_MODEL
N_IN = OFF_GA + D_MODEL

LANES = 128
SUBLANES = 8
BF16_SUBLANES = 16
SEQ_TILE = 512
FFN_SEQ_TILE = 1024
STAGE_COLS = 512
STAGE_SLOTS = 4
FF_CHUNKS = ((0, 1024), (1024, 1024), (2048, 768))
VMEM_LIMIT_BYTES = 58 * 1024 * 1024

F32 = jnp.float32
BF16 = jnp.bfloat16

assert WINDOW == BLOCK and 2 * HEAD_DIM == LANES and GROUP == 4


def _rms_norm(x, g):
    ms = jnp.mean(x * x, axis=-1, keepdims=True)
    return x * lax.rsqrt(ms + EPS) * g


def _sigmoid(x):
    return 1.0 / (1.0 + jnp.exp(-x))


def _rope(t, cos, sin_next, sin_prev):
    nxt = pltpu.roll(t, LANES - ROT_DIM // 2, 1)
    prv = pltpu.roll(t, ROT_DIM // 2, 1)
    return t * cos + nxt * sin_next + prv * sin_prev


def _split_head_pair(t, lane_lo):
    sw = pltpu.roll(t, HEAD_DIM, 1)
    zero = jnp.zeros_like(t)
    even = (jnp.where(lane_lo, t, zero), jnp.where(lane_lo, zero, sw))
    odd = (jnp.where(lane_lo, sw, zero), jnp.where(lane_lo, zero, t))
    return even, odd


def _stage_weights(pairs, stage_ref, sem_ref):
    slots = stage_ref.shape[0]
    jobs = [(src, dst, c * STAGE_COLS)
            for src, dst in pairs for c in range(src.shape[1] // STAGE_COLS)]

    def slab_copy(n):
        src, _, col = jobs[n]
        return pltpu.make_async_copy(
            src.at[:, pl.ds(col, STAGE_COLS)], stage_ref.at[n % slots], sem_ref.at[n % slots])

    for n in range(min(slots - 1, len(jobs))):
        slab_copy(n).start()
    for n, (_, dst, col) in enumerate(jobs):
        if n + slots - 1 < len(jobs):
            slab_copy(n + slots - 1).start()
        slab_copy(n).wait()
        dst[:, col:col + STAGE_COLS] = stage_ref[n % slots].astype(BF16)


def _mixer_kernel(sinks_ref, x_ref, g_ref, rope_ref, convw_ref, w_in_hbm,
                  w_co_hbm, w_ao_hbm, w_o_hbm, w_gu_f32_ref, w_d_f32_ref,
                  out_ref, w_gu_bf16_ref, w_d_bf16_ref,
                  w_in_ref, w_co_ref, w_ao_ref, w_o_ref, stage_ref, stage_sem,
                  u_scr, q_scr, klo_scr, khi_scr, vlo_scr, vhi_scr, attn_scr):
    ts = x_ref.shape[0]
    n_blk = ts // BLOCK
    s_idx = pl.program_id(1)
    kv_scrs = (klo_scr, khi_scr, vlo_scr, vhi_scr)

    @pl.when((pl.program_id(0) == 0) & (s_idx == 0))
    def _():
        _stage_weights(((w_in_hbm, w_in_ref), (w_co_hbm, w_co_ref),
                        (w_ao_hbm, w_ao_ref), (w_o_hbm, w_o_ref)), stage_ref, stage_sem)

    @pl.when(s_idx == 0)
    def _():
        u_scr[0:SUBLANES, :] = jnp.zeros((SUBLANES, D_MODEL), F32)
        for scr in kv_scrs:
            scr[0:BLOCK, :] = jnp.zeros((BLOCK, N_KV_HEADS * LANES), BF16)

    w_gu_bf16_ref[...] = w_gu_f32_ref[...].astype(BF16)
    w_d_bf16_ref[...] = w_d_f32_ref[...].astype(BF16)

    x = x_ref[...]
    h = _rms_norm(x, g_ref[...]).astype(BF16)

    def proj(off, width):
        return jnp.dot(h, w_in_ref[:, off:off + width], preferred_element_type=F32)

    u = proj(OFF_CC, D_MODEL) * proj(OFF_CX, D_MODEL)
    u_scr[SUBLANES:SUBLANES + ts, :] = u
    conv = (convw_ref[0:1, :] * u_scr[SUBLANES - 2:SUBLANES - 2 + ts, :]
            + convw_ref[1:2, :] * u_scr[SUBLANES - 1:SUBLANES - 1 + ts, :]
            + convw_ref[2:3, :] * u)
    u_scr[0:SUBLANES, :] = u_scr[ts:ts + SUBLANES, :]
    conv_y = (proj(OFF_CB, D_MODEL) * conv).astype(BF16)
    conv_out = jnp.dot(conv_y, w_co_ref[...], preferred_element_type=F32)
    merged = _sigmoid(proj(OFF_GC, D_MODEL)) * conv_out

    q_tabs = [rope_ref[i] * (ATTN_SCALE * LOG2E) for i in range(3)]
    q = proj(OFF_Q, D_ATTN)
    for t in range(D_ATTN // LANES):
        sl = slice(t * LANES, (t + 1) * LANES)
        q_scr[:, sl] = _rope(q[:, sl], *q_tabs).astype(BF16)
    lane_lo_t = lax.broadcasted_iota(jnp.int32, (ts, LANES), 1) < HEAD_DIM
    k = proj(OFF_K, D_KV)
    v = proj(OFF_V, D_KV)
    for t in range(D_KV // LANES):
        sl = slice(t * LANES, (t + 1) * LANES)
        k_t = _rope(k[:, sl], rope_ref[0], rope_ref[1], rope_ref[2])
        for src, lo_scr, hi_scr in ((k_t, klo_scr, khi_scr), (v[:, sl], vlo_scr, vhi_scr)):
            for hd, (lo, hi) in zip((2 * t, 2 * t + 1), _split_head_pair(src, lane_lo_t)):
                hsl = slice(hd * LANES, (hd + 1) * LANES)
                lo_scr[BLOCK:BLOCK + ts, hsl] = lo.astype(BF16)
                hi_scr[BLOCK:BLOCK + ts, hsl] = hi.astype(BF16)

    lane = lax.broadcasted_iota(jnp.int32, (BLOCK, LANES), 1)
    rowi = lax.broadcasted_iota(jnp.int32, (BLOCK, LANES), 0)
    from_cur = lane <= rowi
    lane_lo = lane < HEAD_DIM
    lane2 = lax.broadcasted_iota(jnp.int32, (2 * BLOCK, LANES), 1)
    ones_lo = jnp.where(lane2 < HEAD_DIM, 1.0, 0.0).astype(BF16)
    ones_hi = jnp.where(lane2 < HEAD_DIM, 0.0, 1.0).astype(BF16)
    nt_dims = (((1,), (1,)), ((), ()))

    for j in range(n_blk):
        r0 = j * BLOCK
        for kv in range(N_KV_HEADS):
            kv_sl = slice(kv * LANES, (kv + 1) * LANES)
            q_sl = [slice((2 * kv + pr) * LANES, (2 * kv + pr + 1) * LANES) for pr in range(2)]
            q2 = jnp.concatenate([q_scr[r0:r0 + BLOCK, sl] for sl in q_sl], axis=0)
            k_rhs = jnp.concatenate([klo_scr[r0:r0 + 2 * BLOCK, kv_sl],
                                     khi_scr[r0:r0 + 2 * BLOCK, kv_sl]], axis=0)
            s = lax.dot_general(q2, k_rhs, nt_dims, preferred_element_type=F32)
            v_rhs = jnp.concatenate(
                [jnp.concatenate([vlo_scr[r0:r0 + 2 * BLOCK, kv_sl], ones_lo], axis=1),
                 jnp.concatenate([vhi_scr[r0:r0 + 2 * BLOCK, kv_sl], ones_hi], axis=1)], axis=0)
            p_rows, sink_terms = [], []
            for pr in range(2):
                rows = slice(pr * BLOCK, (pr + 1) * BLOCK)
                p_tiles, e_sink = [], []
                for half in range(2):
                    hd = kv * GROUP + 2 * pr + half
                    s_prev = s[rows, (2 * half) * BLOCK:(2 * half + 1) * BLOCK]
                    s_cur = s[rows, (2 * half + 1) * BLOCK:(2 * half + 2) * BLOCK]
                    if j == 0:
                        s_prev = jnp.where(s_idx > 0, s_prev, NEG_INF)
                    t = jnp.where(from_cur, s_cur, s_prev)
                    m = jnp.max(t, axis=-1, keepdims=True)
                    p = jnp.exp2(t - m).astype(BF16)
                    zero = jnp.zeros_like(p)
                    p_tiles += [jnp.where(from_cur, zero, p), jnp.where(from_cur, p, zero)]
                    e_sink.append(jnp.exp2(sinks_ref[hd] - m))
                p_rows.append(jnp.concatenate(p_tiles, axis=1))
                sink_terms.append(jnp.where(lane_lo, e_sink[0], e_sink[1]))
            o = jnp.dot(jnp.concatenate(p_rows, axis=0), v_rhs, preferred_element_type=F32)
            for pr in range(2):
                rows = slice(pr * BLOCK, (pr + 1) * BLOCK)
                den = o[rows, LANES:2 * LANES] + sink_terms[pr]
                attn_scr[r0:r0 + BLOCK, q_sl[pr]] = (o[rows, 0:LANES] / den).astype(BF16)

    for scr in kv_scrs:
        scr[0:BLOCK, :] = scr[ts:ts + BLOCK, :]

    attn_out = jnp.dot(attn_scr[...], w_ao_ref[...], preferred_element_type=F32)
    merged = merged + _sigmoid(proj(OFF_GA, D_MODEL)) * attn_out
    out_ref[...] = x + jnp.dot(merged.astype(BF16), w_o_ref[...], preferred_element_type=F32)


def _ffn_kernel(x_ref, g_ref, gf_ref, w_gu_ref, w_d_ref, out_ref, act_scr, *, final_norm):
    x = x_ref[...]
    h = _rms_norm(x, g_ref[...]).astype(BF16)
    for off, width in FF_CHUNKS:
        gate = jnp.dot(h, w_gu_ref[:, off:off + width], preferred_element_type=F32)
        up = jnp.dot(h, w_gu_ref[:, D_FF + off:D_FF + off + width], preferred_element_type=F32)
        act_scr[:, off:off + width] = (gate * _sigmoid(gate) * up).astype(BF16)
    y = x + jnp.dot(act_scr[...], w_d_ref[...], preferred_element_type=F32)
    if final_norm:
        y = _rms_norm(y, gf_ref[...])
    out_ref[...] = y


def _resident(shape):
    return pl.BlockSpec(shape, lambda b, s: (0,) * len(shape), pipeline_mode=pl.Buffered(1))


def _rope_tables(seq):
    half = ROT_DIM // 2
    d = jnp.arange(LANES) % HEAD_DIM
    inv_freq = ROPE_THETA ** (-(2 * (d % half)).astype(F32) / ROT_DIM)
    ang = jnp.arange(seq, dtype=F32)[:, None] * inv_freq[None, :]
    cos, sin = jnp.cos(ang), jnp.sin(ang)
    c = jnp.where(d < ROT_DIM, cos, 1.0)
    s_next = jnp.where(d < half, -sin, 0.0)
    s_prev = jnp.where((d >= half) & (d < ROT_DIM), sin, 0.0)
    return jnp.stack([c, s_next, s_prev])


def _mixer(x, g, rope, conv_w, sinks, w_in, w_co, w_ao, w_o, w_gu, w_d):
    b, s, d = x.shape
    ts = SEQ_TILE
    n_s = s // ts
    gu_rows = w_gu.shape[0] // (b * n_s)
    d_rows = 2 * w_d.shape[0] // (b * n_s)
    assert gu_rows % BF16_SUBLANES == 0 and d_rows % BF16_SUBLANES == 0
    gu_slab = pl.BlockSpec((gu_rows, w_gu.shape[1]), lambda i, j: (i * n_s + j, 0))
    d_slab = pl.BlockSpec((d_rows, w_d.shape[1]), lambda i, j: ((i * n_s + j) // 2, 0))
    tile = pl.BlockSpec((None, ts, d), lambda i, j: (i, j, 0))
    kv_scratch = pltpu.VMEM((BLOCK + ts, N_KV_HEADS * LANES), BF16)
    return pl.pallas_call(
        _mixer_kernel,
        grid=(b, n_s),
        in_specs=[
            pl.BlockSpec(memory_space=pltpu.SMEM),
            tile,
            _resident((1, d)),
            pl.BlockSpec((3, ts, LANES), lambda i, j: (0, j, 0)),
            _resident((CONV_K, d)),
            pl.BlockSpec(memory_space=pl.ANY),
            pl.BlockSpec(memory_space=pl.ANY),
            pl.BlockSpec(memory_space=pl.ANY),
            pl.BlockSpec(memory_space=pl.ANY),
            gu_slab,
            d_slab,
        ],
        out_specs=[tile, gu_slab, d_slab],
        out_shape=[jax.ShapeDtypeStruct(x.shape, x.dtype),
                   jax.ShapeDtypeStruct(w_gu.shape, BF16),
                   jax.ShapeDtypeStruct(w_d.shape, BF16)],
        scratch_shapes=[
            pltpu.VMEM(w_in.shape, BF16),
            pltpu.VMEM(w_co.shape, BF16),
            pltpu.VMEM(w_ao.shape, BF16),
            pltpu.VMEM(w_o.shape, BF16),
            pltpu.VMEM((STAGE_SLOTS, d, STAGE_COLS), F32),
            pltpu.SemaphoreType.DMA((STAGE_SLOTS,)),
            pltpu.VMEM((SUBLANES + ts, d), F32),
            pltpu.VMEM((ts, D_ATTN), BF16),
            kv_scratch, kv_scratch, kv_scratch, kv_scratch,
            pltpu.VMEM((ts, D_ATTN), BF16),
        ],
        compiler_params=pltpu.CompilerParams(
            dimension_semantics=("arbitrary", "arbitrary"),
            vmem_limit_bytes=VMEM_LIMIT_BYTES),
        name="mixer",
    )(sinks, x, g, rope, conv_w, w_in, w_co, w_ao, w_o, w_gu, w_d)


def _ffn(x, g, g_final, w_gu, w_d, final_norm):
    b, s, d = x.shape
    ts = FFN_SEQ_TILE
    tile = pl.BlockSpec((None, ts, d), lambda i, j: (i, j, 0))
    return pl.pallas_call(
        functools.partial(_ffn_kernel, final_norm=final_norm),
        grid=(b, s // ts),
        in_specs=[tile, _resident((1, d)), _resident((1, d)),
                  _resident(w_gu.shape), _resident(w_d.shape)],
        out_specs=tile,
        out_shape=jax.ShapeDtypeStruct(x.shape, x.dtype),
        scratch_shapes=[pltpu.VMEM((ts, D_FF), BF16)],
        compiler_params=pltpu.CompilerParams(
            dimension_semantics=("arbitrary", "arbitrary"),
            vmem_limit_bytes=VMEM_LIMIT_BYTES),
        name="ffn",
    )(x, g, g_final, w_gu, w_d)


def kernel(x, g_mix, w_in, conv_w, attn_sinks, w_conv_out, w_attn_out, w_o,
           g_ffn, w_gate_up, w_down, g_final):
    b, s, d = x.shape
    depth = w_in.shape[0]
    assert d == D_MODEL and s % SEQ_TILE == 0 and s % FFN_SEQ_TILE == 0
    assert w_in.shape[-1] == N_IN and w_gate_up.shape[-1] == 2 * D_FF
    assert N_IN % STAGE_COLS == 0 and d % STAGE_COLS == 0
    rope = _rope_tables(s)
    g_fin = g_final.reshape(1, d)
    for l in range(depth):
        x, w_gu, w_d = _mixer(
            x, g_mix[l].reshape(1, d), rope, conv_w[l], attn_sinks[l] * LOG2E,
            w_in[l], w_conv_out[l], w_attn_out[l], w_o[l],
            w_gate_up[l], w_down[l])
        x = _ffn(x, g_ffn[l].reshape(1, d), g_fin, w_gu, w_d, final_norm=(l == depth - 1))
    return x
```

```python
import functools
import math

import jax
import jax.numpy as jnp
from jax import lax
from jax.experimental import pallas as pl
from jax.experimental.pallas import tpu as pltpu

D_MODEL = 1024
CONV_K = 3
HEAD_DIM = 64
N_HEADS = 16
N_KV_HEADS = 4
GROUP = N_HEADS // N_KV_HEADS
D_ATTN = N_HEADS * HEAD_DIM
D_KV = N_KV_HEADS * HEAD_DIM
WINDOW = 128
BLOCK = 128
ROT_DIM = HEAD_DIM // 4
ROPE_THETA = 500000.0
ATTN_SCALE = 1.0 / math.sqrt(HEAD_DIM)
LOG2E = math.log2(math.e)
NEG_INF = -1e30
D_FF = 2816
EPS = 1e-5

OFF_CB = 0
OFF_CC = OFF_CB + D_MODEL
OFF_CX = OFF_CC + D_MODEL
OFF_Q = OFF_CX + D_MODEL
OFF_K = OFF_Q + D_ATTN
OFF_V = OFF_K + D_KV
OFF_GC = OFF_V + D_KV
OFF_GA = OFF_GC + D_MODEL
N_IN = OFF_GA + D_MODEL

LANES = 128
SUBLANES = 8
BF16_SUBLANES = 16
SEQ_TILE = 512
FFN_SEQ_TILE = 1024
STAGE_COLS = 512
STAGE_SLOTS = 4
FF_CHUNKS = ((0, 1024), (1024, 1024), (2048, 768))
VMEM_LIMIT_BYTES = 58 * 1024 * 1024

F32 = jnp.float32
BF16 = jnp.bfloat16

assert WINDOW == BLOCK and 2 * HEAD_DIM == LANES and GROUP == 4


def _rms_norm(x, g):
    ms = jnp.mean(x * x, axis=-1, keepdims=True)
    return x * lax.rsqrt(ms + EPS) * g


def _sigmoid(x):
    return 1.0 / (1.0 + jnp.exp(-x))


def _rope(t, cos, sin_next, sin_prev):
    nxt = pltpu.roll(t, LANES - ROT_DIM // 2, 1)
    prv = pltpu.roll(t, ROT_DIM // 2, 1)
    return t * cos + nxt * sin_next + prv * sin_prev


def _split_head_pair(t, lane_lo):
    sw = pltpu.roll(t, HEAD_DIM, 1)
    zero = jnp.zeros_like(t)
    even = (jnp.where(lane_lo, t, zero), jnp.where(lane_lo, zero, sw))
    odd = (jnp.where(lane_lo, sw, zero), jnp.where(lane_lo, zero, t))
    return even, odd


def _stage_weights(pairs, stage_ref, sem_ref):
    slots = stage_ref.shape[0]
    jobs = [(src, dst, c * STAGE_COLS)
            for src, dst in pairs for c in range(src.shape[1] // STAGE_COLS)]

    def slab_copy(n):
        src, _, col = jobs[n]
        return pltpu.make_async_copy(
            src.at[:, pl.ds(col, STAGE_COLS)], stage_ref.at[n % slots], sem_ref.at[n % slots])

    for n in range(min(slots - 1, len(jobs))):
        slab_copy(n).start()
    for n, (_, dst, col) in enumerate(jobs):
        if n + slots - 1 < len(jobs):
            slab_copy(n + slots - 1).start()
        slab_copy(n).wait()
        dst[:, col:col + STAGE_COLS] = stage_ref[n % slots].astype(BF16)


def _mixer_kernel(sinks_ref, x_ref, g_ref, rope_ref, convw_ref, w_in_hbm,
                  w_co_hbm, w_ao_hbm, w_o_hbm, w_gu_f32_ref, w_d_f32_ref,
                  out_ref, w_gu_bf16_ref, w_d_bf16_ref,
                  w_in_ref, w_co_ref, w_ao_ref, w_o_ref, stage_ref, stage_sem,
                  u_scr, q_scr, klo_scr, khi_scr, vlo_scr, vhi_scr, attn_scr):
    ts = x_ref.shape[0]
    n_blk = ts // BLOCK
    s_idx = pl.program_id(1)
    kv_scrs = (klo_scr, khi_scr, vlo_scr, vhi_scr)

    @pl.when((pl.program_id(0) == 0) & (s_idx == 0))
    def _():
        _stage_weights(((w_in_hbm, w_in_ref), (w_co_hbm, w_co_ref),
                        (w_ao_hbm, w_ao_ref), (w_o_hbm, w_o_ref)), stage_ref, stage_sem)

    @pl.when(s_idx == 0)
    def _():
        u_scr[0:SUBLANES, :] = jnp.zeros((SUBLANES, D_MODEL), F32)
        for scr in kv_scrs:
            scr[0:BLOCK, :] = jnp.zeros((BLOCK, N_KV_HEADS * LANES), BF16)

    w_gu_bf16_ref[...] = w_gu_f32_ref[...].astype(BF16)
    w_d_bf16_ref[...] = w_d_f32_ref[...].astype(BF16)

    x = x_ref[...]
    h = _rms_norm(x, g_ref[...]).astype(BF16)

    def proj(off, width):
        return jnp.dot(h, w_in_ref[:, off:off + width], preferred_element_type=F32)

    u = proj(OFF_CC, D_MODEL) * proj(OFF_CX, D_MODEL)
    u_scr[SUBLANES:SUBLANES + ts, :] = u
    conv = (convw_ref[0:1, :] * u_scr[SUBLANES - 2:SUBLANES - 2 + ts, :]
            + convw_ref[1:2, :] * u_scr[SUBLANES - 1:SUBLANES - 1 + ts, :]
            + convw_ref[2:3, :] * u)
    u_scr[0:SUBLANES, :] = u_scr[ts:ts + SUBLANES, :]
    conv_y = (proj(OFF_CB, D_MODEL) * conv).astype(BF16)
    conv_out = jnp.dot(conv_y, w_co_ref[...], preferred_element_type=F32)
    merged = _sigmoid(proj(OFF_GC, D_MODEL)) * conv_out

    q_tabs = [rope_ref[i] * (ATTN_SCALE * LOG2E) for i in range(3)]
    q = proj(OFF_Q, D_ATTN)
    for t in range(D_ATTN // LANES):
        sl = slice(t * LANES, (t + 1) * LANES)
        q_scr[:, sl] = _rope(q[:, sl], *q_tabs).astype(BF16)
    lane_lo_t = lax.broadcasted_iota(jnp.int32, (ts, LANES), 1) < HEAD_DIM
    k = proj(OFF_K, D_KV)
    v = proj(OFF_V, D_KV)
    for t in range(D_KV // LANES):
        sl = slice(t * LANES, (t + 1) * LANES)
        k_t = _rope(k[:, sl], rope_ref[0], rope_ref[1], rope_ref[2])
        for src, lo_scr, hi_scr in ((k_t, klo_scr, khi_scr), (v[:, sl], vlo_scr, vhi_scr)):
            for hd, (lo, hi) in zip((2 * t, 2 * t + 1), _split_head_pair(src, lane_lo_t)):
                hsl = slice(hd * LANES, (hd + 1) * LANES)
                lo_scr[BLOCK:BLOCK + ts, hsl] = lo.astype(BF16)
                hi_scr[BLOCK:BLOCK + ts, hsl] = hi.astype(BF16)

    lane = lax.broadcasted_iota(jnp.int32, (BLOCK, LANES), 1)
    rowi = lax.broadcasted_iota(jnp.int32, (BLOCK, LANES), 0)
    from_cur = lane <= rowi
    lane_lo = lane < HEAD_DIM
    lane2 = lax.broadcasted_iota(jnp.int32, (2 * BLOCK, LANES), 1)
    ones_lo = jnp.where(lane2 < HEAD_DIM, 1.0, 0.0).astype(BF16)
    ones_hi = jnp.where(lane2 < HEAD_DIM, 0.0, 1.0).astype(BF16)
    nt_dims = (((1,), (1,)), ((), ()))

    for j in range(n_blk):
        r0 = j * BLOCK
        for kv in range(N_KV_HEADS):
            kv_sl = slice(kv * LANES, (kv + 1) * LANES)
            q_sl = [slice((2 * kv + pr) * LANES, (2 * kv + pr + 1) * LANES) for pr in range(2)]
            q2 = jnp.concatenate([q_scr[r0:r0 + BLOCK, sl] for sl in q_sl], axis=0)
            k_rhs = jnp.concatenate([klo_scr[r0:r0 + 2 * BLOCK, kv_sl],
                                     khi_scr[r0:r0 + 2 * BLOCK, kv_sl]], axis=0)
            s = lax.dot_general(q2, k_rhs, nt_dims, preferred_element_type=F32)
            v_rhs = jnp.concatenate(
                [jnp.concatenate([vlo_scr[r0:r0 + 2 * BLOCK, kv_sl], ones_lo], axis=1),
                 jnp.concatenate([vhi_scr[r0:r0 + 2 * BLOCK, kv_sl], ones_hi], axis=1)], axis=0)
            p_rows, sink_terms = [], []
            for pr in range(2):
                rows = slice(pr * BLOCK, (pr + 1) * BLOCK)
                p_tiles, e_sink = [], []
                for half in range(2):
                    hd = kv * GROUP + 2 * pr + half
                    s_prev = s[rows, (2 * half) * BLOCK:(2 * half + 1) * BLOCK]
                    s_cur = s[rows, (2 * half + 1) * BLOCK:(2 * half + 2) * BLOCK]
                    if j == 0:
                        s_prev = jnp.where(s_idx > 0, s_prev, NEG_INF)
                    t = jnp.where(from_cur, s_cur, s_prev)
                    m = jnp.max(t, axis=-1, keepdims=True)
                    p = jnp.exp2(t - m).astype(BF16)
                    zero = jnp.zeros_like(p)
                    p_tiles += [jnp.where(from_cur, zero, p), jnp.where(from_cur, p, zero)]
                    e_sink.append(jnp.exp2(sinks_ref[hd] * LOG2E - m))
                p_rows.append(jnp.concatenate(p_tiles, axis=1))
                sink_terms.append(jnp.where(lane_lo, e_sink[0], e_sink[1]))
            o = jnp.dot(jnp.concatenate(p_rows, axis=0), v_rhs, preferred_element_type=F32)
            for pr in range(2):
                rows = slice(pr * BLOCK, (pr + 1) * BLOCK)
                den = o[rows, LANES:2 * LANES] + sink_terms[pr]
                attn_scr[r0:r0 + BLOCK, q_sl[pr]] = (o[rows, 0:LANES] / den).astype(BF16)

    for scr in kv_scrs:
        scr[0:BLOCK, :] = scr[ts:ts + BLOCK, :]

    attn_out = jnp.dot(attn_scr[...], w_ao_ref[...], preferred_element_type=F32)
    merged = merged + _sigmoid(proj(OFF_GA, D_MODEL)) * attn_out
    out_ref[...] = x + jnp.dot(merged.astype(BF16), w_o_ref[...], preferred_element_type=F32)


def _ffn_kernel(x_ref, g_ref, gf_ref, w_gu_ref, w_d_ref, out_ref, act_scr, *, final_norm):
    x = x_ref[...]
    h = _rms_norm(x, g_ref[...]).astype(BF16)
    for off, width in FF_CHUNKS:
        gate = jnp.dot(h, w_gu_ref[:, off:off + width], preferred_element_type=F32)
        up = jnp.dot(h, w_gu_ref[:, D_FF + off:D_FF + off + width], preferred_element_type=F32)
        act_scr[:, off:off + width] = (gate * _sigmoid(gate) * up).astype(BF16)
    y = x + jnp.dot(act_scr[...], w_d_ref[...], preferred_element_type=F32)
    if final_norm:
        y = _rms_norm(y, gf_ref[...])
    out_ref[...] = y


def _resident(shape):
    return pl.BlockSpec(shape, lambda b, s: (0,) * len(shape), pipeline_mode=pl.Buffered(1))


def _rope_tables(seq):
    half = ROT_DIM // 2
    d = jnp.arange(LANES) % HEAD_DIM
    inv_freq = ROPE_THETA ** (-(2 * (d % half)).astype(F32) / ROT_DIM)
    ang = jnp.arange(seq, dtype=F32)[:, None] * inv_freq[None, :]
    cos, sin = jnp.cos(ang), jnp.sin(ang)
    c = jnp.where(d < ROT_DIM, cos, 1.0)
    s_next = jnp.where(d < half, -sin, 0.0)
    s_prev = jnp.where((d >= half) & (d < ROT_DIM), sin, 0.0)
    kind = jnp.arange(3)[:, None, None]
    return jnp.where(kind == 0, c, jnp.where(kind == 1, s_next, s_prev))


def _mixer(x, g, rope, conv_w, sinks, layer, w_in, w_co, w_ao, w_o, w_gu, w_d):
    b, s, d = x.shape
    ts = SEQ_TILE
    n_s = s // ts
    gu_rows = w_gu.shape[0] // (b * n_s)
    d_rows = 2 * w_d.shape[0] // (b * n_s)
    assert gu_rows % BF16_SUBLANES == 0 and d_rows % BF16_SUBLANES == 0
    gu_slab = pl.BlockSpec((gu_rows, w_gu.shape[1]), lambda i, j: (i * n_s + j, 0))
    d_slab = pl.BlockSpec((d_rows, w_d.shape[1]), lambda i, j: ((i * n_s + j) // 2, 0))
    tile = pl.BlockSpec((None, ts, d), lambda i, j: (i, j, 0))
    kv_scratch = pltpu.VMEM((BLOCK + ts, N_KV_HEADS * LANES), BF16)
    return pl.pallas_call(
        _mixer_kernel,
        grid=(b, n_s),
        in_specs=[
            pl.BlockSpec(memory_space=pltpu.SMEM),
            tile,
            _resident((1, d)),
            pl.BlockSpec((3, ts, LANES), lambda i, j: (0, j, 0)),
            pl.BlockSpec((None, CONV_K, d), lambda i, j: (layer, 0, 0),
                         pipeline_mode=pl.Buffered(1)),
            pl.BlockSpec(memory_space=pl.ANY),
            pl.BlockSpec(memory_space=pl.ANY),
            pl.BlockSpec(memory_space=pl.ANY),
            pl.BlockSpec(memory_space=pl.ANY),
            gu_slab,
            d_slab,
        ],
        out_specs=[tile, gu_slab, d_slab],
        out_shape=[jax.ShapeDtypeStruct(x.shape, x.dtype),
                   jax.ShapeDtypeStruct(w_gu.shape, BF16),
                   jax.ShapeDtypeStruct(w_d.shape, BF16)],
        scratch_shapes=[
            pltpu.VMEM(w_in.shape, BF16),
            pltpu.VMEM(w_co.shape, BF16),
            pltpu.VMEM(w_ao.shape, BF16),
            pltpu.VMEM(w_o.shape, BF16),
            pltpu.VMEM((STAGE_SLOTS, d, STAGE_COLS), F32),
            pltpu.SemaphoreType.DMA((STAGE_SLOTS,)),
            pltpu.VMEM((SUBLANES + ts, d), F32),
            pltpu.VMEM((ts, D_ATTN), BF16),
            kv_scratch, kv_scratch, kv_scratch, kv_scratch,
            pltpu.VMEM((ts, D_ATTN), BF16),
        ],
        compiler_params=pltpu.CompilerParams(
            dimension_semantics=("arbitrary", "arbitrary"),
            vmem_limit_bytes=VMEM_LIMIT_BYTES),
        name="mixer",
    )(sinks, x, g, rope, conv_w, w_in, w_co, w_ao, w_o, w_gu, w_d)


def _ffn(x, g, g_final, w_gu, w_d, final_norm):
    b, s, d = x.shape
    ts = FFN_SEQ_TILE
    tile = pl.BlockSpec((None, ts, d), lambda i, j: (i, j, 0))
    return pl.pallas_call(
        functools.partial(_ffn_kernel, final_norm=final_norm),
        grid=(b, s // ts),
        in_specs=[tile, _resident((1, d)), _resident((1, d)),
                  _resident(w_gu.shape), _resident(w_d.shape)],
        out_specs=tile,
        out_shape=jax.ShapeDtypeStruct(x.shape, x.dtype),
        scratch_shapes=[pltpu.VMEM((ts, D_FF), BF16)],
        compiler_params=pltpu.CompilerParams(
            dimension_semantics=("arbitrary", "arbitrary"),
            vmem_limit_bytes=VMEM_LIMIT_BYTES),
        name="ffn",
    )(x, g, g_final, w_gu, w_d)


def kernel(x, g_mix, w_in, conv_w, attn_sinks, w_conv_out, w_attn_out, w_o,
           g_ffn, w_gate_up, w_down, g_final):
    b, s, d = x.shape
    depth = w_in.shape[0]
    assert d == D_MODEL and s % SEQ_TILE == 0 and s % FFN_SEQ_TILE == 0
    assert w_in.shape[-1] == N_IN and w_gate_up.shape[-1] == 2 * D_FF
    assert N_IN % STAGE_COLS == 0 and d % STAGE_COLS == 0
    rope = _rope_tables(s)
    g_fin = g_final.reshape(1, d)
    for l in range(depth):
        x, w_gu, w_d = _mixer(
            x, g_mix[l].reshape(1, d), rope, conv_w, attn_sinks[l], l,
            w_in[l], w_conv_out[l], w_attn_out[l], w_o[l],
            w_gate_up[l], w_down[l])
        x = _ffn(x, g_ffn[l].reshape(1, d), g_fin, w_gu, w_d, final_norm=(l == depth - 1))
    return x
```

```python
import functools
import math

import jax
import jax.numpy as jnp
from jax import lax
from jax.experimental import pallas as pl
from jax.experimental.pallas import tpu as pltpu

D_MODEL = 1024
CONV_K = 3
HEAD_DIM = 64
N_HEADS = 16
N_KV_HEADS = 4
GROUP = N_HEADS // N_KV_HEADS
D_ATTN = N_HEADS * HEAD_DIM
D_KV = N_KV_HEADS * HEAD_DIM
WINDOW = 128
BLOCK = 128
ROT_DIM = HEAD_DIM // 4
ROPE_THETA = 500000.0
ATTN_SCALE = 1.0 / math.sqrt(HEAD_DIM)
LOG2E = math.log2(math.e)
NEG_INF = -1e30
D_FF = 2816
EPS = 1e-5

OFF_CB = 0
OFF_CC = OFF_CB + D_MODEL
OFF_CX = OFF_CC + D_MODEL
OFF_Q = OFF_CX + D_MODEL
OFF_K = OFF_Q + D_ATTN
OFF_V = OFF_K + D_KV
OFF_GC = OFF_V + D_KV
OFF_GA = OFF_GC + D_MODEL
N_IN = OFF_GA + D_MODEL

LANES = 128
SUBLANES = 8
BF16_SUBLANES = 16
SEQ_TILE = 512
FFN_SEQ_TILE = 1024
STAGE_COLS = 512
STAGE_SLOTS = 4
FF_CHUNKS = ((0, 1024), (1024, 1024), (2048, 768))
VMEM_LIMIT_BYTES = 58 * 1024 * 1024

F32 = jnp.float32
BF16 = jnp.bfloat16

assert WINDOW == BLOCK and 2 * HEAD_DIM == LANES and GROUP == 4


def _rms_norm(x, g):
    ms = jnp.mean(x * x, axis=-1, keepdims=True)
    return x * lax.rsqrt(ms + EPS) * g


def _sigmoid(x):
    return 1.0 / (1.0 + jnp.exp(-x))


def _rope(t, cos, sin_next, sin_prev):
    nxt = pltpu.roll(t, LANES - ROT_DIM // 2, 1)
    prv = pltpu.roll(t, ROT_DIM // 2, 1)
    return t * cos + nxt * sin_next + prv * sin_prev


def _split_head_pair(t, lane_lo):
    sw = pltpu.roll(t, HEAD_DIM, 1)
    zero = jnp.zeros_like(t)
    even = (jnp.where(lane_lo, t, zero), jnp.where(lane_lo, zero, sw))
    odd = (jnp.where(lane_lo, sw, zero), jnp.where(lane_lo, zero, t))
    return even, odd


def _stage_weights(pairs, stage_ref, sem_ref):
    slots = stage_ref.shape[0]
    jobs = [(src, dst, c * STAGE_COLS)
            for src, dst in pairs for c in range(src.shape[1] // STAGE_COLS)]

    def slab_copy(n):
        src, _, col = jobs[n]
        return pltpu.make_async_copy(
            src.at[:, pl.ds(col, STAGE_COLS)], stage_ref.at[n % slots], sem_ref.at[n % slots])

    for n in range(min(slots - 1, len(jobs))):
        slab_copy(n).start()
    for n, (_, dst, col) in enumerate(jobs):
        if n + slots - 1 < len(jobs):
            slab_copy(n + slots - 1).start()
        slab_copy(n).wait()
        dst[:, col:col + STAGE_COLS] = stage_ref[n % slots].astype(BF16)


def _mixer_kernel(sinks_ref, x_ref, g_ref, rope_ref, convw_ref, w_in_hbm,
                  w_co_hbm, w_ao_hbm, w_o_hbm, w_gu_f32_ref, w_d_f32_ref,
                  out_ref, w_gu_bf16_ref, w_d_bf16_ref,
                  w_in_ref, w_co_ref, w_ao_ref, w_o_ref, stage_ref, stage_sem,
                  u_scr, q_scr, klo_scr, khi_scr, vlo_scr, vhi_scr, attn_scr):
    ts = x_ref.shape[0]
    n_blk = ts // BLOCK
    s_idx = pl.program_id(1)
    kv_scrs = (klo_scr, khi_scr, vlo_scr, vhi_scr)

    @pl.when((pl.program_id(0) == 0) & (s_idx == 0))
    def _():
        _stage_weights(((w_in_hbm, w_in_ref), (w_co_hbm, w_co_ref),
                        (w_ao_hbm, w_ao_ref), (w_o_hbm, w_o_ref)), stage_ref, stage_sem)

    @pl.when(s_idx == 0)
    def _():
        u_scr[0:SUBLANES, :] = jnp.zeros((SUBLANES, D_MODEL), F32)
        for scr in kv_scrs:
            scr[0:BLOCK, :] = jnp.zeros((BLOCK, N_KV_HEADS * LANES), BF16)

    w_gu_bf16_ref[...] = w_gu_f32_ref[...].astype(BF16)
    w_d_bf16_ref[...] = w_d_f32_ref[...].astype(BF16)

    x = x_ref[...]
    h = _rms_norm(x, g_ref[...]).astype(BF16)

    def proj(off, width):
        return jnp.dot(h, w_in_ref[:, off:off + width], preferred_element_type=F32)

    u = proj(OFF_CC, D_MODEL) * proj(OFF_CX, D_MODEL)
    u_scr[SUBLANES:SUBLANES + ts, :] = u
    conv = (convw_ref[0:1, :] * u_scr[SUBLANES - 2:SUBLANES - 2 + ts, :]
            + convw_ref[1:2, :] * u_scr[SUBLANES - 1:SUBLANES - 1 + ts, :]
            + convw_ref[2:3, :] * u)
    u_scr[0:SUBLANES, :] = u_scr[ts:ts + SUBLANES, :]
    conv_y = (proj(OFF_CB, D_MODEL) * conv).astype(BF16)
    conv_out = jnp.dot(conv_y, w_co_ref[...], preferred_element_type=F32)

    q_tabs = [rope_ref[i] * (ATTN_SCALE * LOG2E) for i in range(3)]
    q = proj(OFF_Q, D_ATTN)
    for t in range(D_ATTN // LANES):
        sl = slice(t * LANES, (t + 1) * LANES)
        q_scr[:, sl] = _rope(q[:, sl], *q_tabs).astype(BF16)
    lane_lo_t = lax.broadcasted_iota(jnp.int32, (ts, LANES), 1) < HEAD_DIM
    k = proj(OFF_K, D_KV)
    v = proj(OFF_V, D_KV)
    for t in range(D_KV // LANES):
        sl = slice(t * LANES, (t + 1) * LANES)
        k_t = _rope(k[:, sl], rope_ref[0], rope_ref[1], rope_ref[2])
        for src, lo_scr, hi_scr in ((k_t, klo_scr, khi_scr), (v[:, sl], vlo_scr, vhi_scr)):
            for hd, (lo, hi) in zip((2 * t, 2 * t + 1), _split_head_pair(src, lane_lo_t)):
                hsl = slice(hd * LANES, (hd + 1) * LANES)
                lo_scr[BLOCK:BLOCK + ts, hsl] = lo.astype(BF16)
                hi_scr[BLOCK:BLOCK + ts, hsl] = hi.astype(BF16)
    merged = _sigmoid(proj(OFF_GC, D_MODEL)) * conv_out

    lane = lax.broadcasted_iota(jnp.int32, (BLOCK, LANES), 1)
    rowi = lax.broadcasted_iota(jnp.int32, (BLOCK, LANES), 0)
    from_cur = lane <= rowi
    lane_lo = lane < HEAD_DIM
    lane2 = lax.broadcasted_iota(jnp.int32, (2 * BLOCK, LANES), 1)
    ones_lo = jnp.where(lane2 < HEAD_DIM, 1.0, 0.0).astype(BF16)
    ones_hi = jnp.where(lane2 < HEAD_DIM, 0.0, 1.0).astype(BF16)
    nt_dims = (((1,), (1,)), ((), ()))

    for j in range(n_blk):
        r0 = j * BLOCK
        for kv in range(N_KV_HEADS):
            kv_sl = slice(kv * LANES, (kv + 1) * LANES)
            q_sl = [slice((2 * kv + pr) * LANES, (2 * kv + pr + 1) * LANES) for pr in range(2)]
            q2 = jnp.concatenate([q_scr[r0:r0 + BLOCK, sl] for sl in q_sl], axis=0)
            k_rhs = jnp.concatenate([klo_scr[r0:r0 + 2 * BLOCK, kv_sl],
                                     khi_scr[r0:r0 + 2 * BLOCK, kv_sl]], axis=0)
            s = lax.dot_general(q2, k_rhs, nt_dims, preferred_element_type=F32)
            v_rhs = jnp.concatenate(
                [jnp.concatenate([vlo_scr[r0:r0 + 2 * BLOCK, kv_sl], ones_lo], axis=1),
                 jnp.concatenate([vhi_scr[r0:r0 + 2 * BLOCK, kv_sl], ones_hi], axis=1)], axis=0)
            p_rows, sink_terms = [], []
            for pr in range(2):
                rows = slice(pr * BLOCK, (pr + 1) * BLOCK)
                p_tiles, e_sink = [], []
                for half in range(2):
                    hd = kv * GROUP + 2 * pr + half
                    s_prev = s[rows, (2 * half) * BLOCK:(2 * half + 1) * BLOCK]
                    s_cur = s[rows, (2 * half + 1) * BLOCK:(2 * half + 2) * BLOCK]
                    if j == 0:
                        s_prev = jnp.where(s_idx > 0, s_prev, NEG_INF)
                    t = jnp.where(from_cur, s_cur, s_prev)
                    m = jnp.max(t, axis=-1, keepdims=True)
                    p = jnp.exp2(t - m).astype(BF16)
                    zero = jnp.zeros_like(p)
                    p_tiles += [jnp.where(from_cur, zero, p), jnp.where(from_cur, p, zero)]
                    e_sink.append(jnp.exp2(sinks_ref[hd] * LOG2E - m))
                p_rows.append(jnp.concatenate(p_tiles, axis=1))
                sink_terms.append(jnp.where(lane_lo, e_sink[0], e_sink[1]))
            o = jnp.dot(jnp.concatenate(p_rows, axis=0), v_rhs, preferred_element_type=F32)
            for pr in range(2):
                rows = slice(pr * BLOCK, (pr + 1) * BLOCK)
                den = o[rows, LANES:2 * LANES] + sink_terms[pr]
                attn_scr[r0:r0 + BLOCK, q_sl[pr]] = (o[rows, 0:LANES] / den).astype(BF16)

    for scr in kv_scrs:
        scr[0:BLOCK, :] = scr[ts:ts + BLOCK, :]

    gate_a = _sigmoid(proj(OFF_GA, D_MODEL))
    attn_out = jnp.dot(attn_scr[...], w_ao_ref[...], preferred_element_type=F32)
    merged = merged + gate_a * attn_out
    out_ref[...] = x + jnp.dot(merged.astype(BF16), w_o_ref[...], preferred_element_type=F32)


def _ffn_kernel(x_ref, g_ref, gf_ref, w_gu_ref, w_d_ref, out_ref, act_scr, *, final_norm):
    x = x_ref[...]
    h = _rms_norm(x, g_ref[...]).astype(BF16)
    for off, width in FF_CHUNKS:
        gate = jnp.dot(h, w_gu_ref[:, off:off + width], preferred_element_type=F32)
        up = jnp.dot(h, w_gu_ref[:, D_FF + off:D_FF + off + width], preferred_element_type=F32)
        act_scr[:, off:off + width] = (gate * _sigmoid(gate) * up).astype(BF16)
    y = x + jnp.dot(act_scr[...], w_d_ref[...], preferred_element_type=F32)
    if final_norm:
        y = _rms_norm(y, gf_ref[...])
    out_ref[...] = y


def _resident(shape):
    return pl.BlockSpec(shape, lambda b, s: (0,) * len(shape), pipeline_mode=pl.Buffered(1))


def _rope_tables(seq):
    half = ROT_DIM // 2
    d = jnp.arange(LANES) % HEAD_DIM
    inv_freq = ROPE_THETA ** (-(2 * (d % half)).astype(F32) / ROT_DIM)
    ang = jnp.arange(seq, dtype=F32)[:, None] * inv_freq[None, :]
    cos, sin = jnp.cos(ang), jnp.sin(ang)
    c = jnp.where(d < ROT_DIM, cos, 1.0)
    s_next = jnp.where(d < half, -sin, 0.0)
    s_prev = jnp.where((d >= half) & (d < ROT_DIM), sin, 0.0)
    kind = jnp.arange(3)[:, None, None]
    return jnp.where(kind == 0, c, jnp.where(kind == 1, s_next, s_prev))


def _mixer(x, g, rope, conv_w, sinks, layer, w_in, w_co, w_ao, w_o, w_gu, w_d):
    b, s, d = x.shape
    ts = SEQ_TILE
    n_s = s // ts
    gu_rows = w_gu.shape[0] // (b * n_s)
    d_rows = 2 * w_d.shape[0] // (b * n_s)
    assert gu_rows % BF16_SUBLANES == 0 and d_rows % BF16_SUBLANES == 0
    gu_slab = pl.BlockSpec((gu_rows, w_gu.shape[1]), lambda i, j: (i * n_s + j, 0))
    d_slab = pl.BlockSpec((d_rows, w_d.shape[1]), lambda i, j: ((i * n_s + j) // 2, 0))
    tile = pl.BlockSpec((None, ts, d), lambda i, j: (i, j, 0))
    kv_scratch = pltpu.VMEM((BLOCK + ts, N_KV_HEADS * LANES), BF16)
    return pl.pallas_call(
        _mixer_kernel,
        grid=(b, n_s),
        in_specs=[
            pl.BlockSpec(memory_space=pltpu.SMEM),
            tile,
            _resident((1, d)),
            pl.BlockSpec((3, ts, LANES), lambda i, j: (0, j, 0)),
            pl.BlockSpec((None, CONV_K, d), lambda i, j: (layer, 0, 0),
                         pipeline_mode=pl.Buffered(1)),
            pl.BlockSpec(memory_space=pl.ANY),
            pl.BlockSpec(memory_space=pl.ANY),
            pl.BlockSpec(memory_space=pl.ANY),
            pl.BlockSpec(memory_space=pl.ANY),
            gu_slab,
            d_slab,
        ],
        out_specs=[tile, gu_slab, d_slab],
        out_shape=[jax.ShapeDtypeStruct(x.shape, x.dtype),
                   jax.ShapeDtypeStruct(w_gu.shape, BF16),
                   jax.ShapeDtypeStruct(w_d.shape, BF16)],
        scratch_shapes=[
            pltpu.VMEM(w_in.shape, BF16),
            pltpu.VMEM(w_co.shape, BF16),
            pltpu.VMEM(w_ao.shape, BF16),
            pltpu.VMEM(w_o.shape, BF16),
            pltpu.VMEM((STAGE_SLOTS, d, STAGE_COLS), F32),
            pltpu.SemaphoreType.DMA((STAGE_SLOTS,)),
            pltpu.VMEM((SUBLANES + ts, d), F32),
            pltpu.VMEM((ts, D_ATTN), BF16),
            kv_scratch, kv_scratch, kv_scratch, kv_scratch,
            pltpu.VMEM((ts, D_ATTN), BF16),
        ],
        compiler_params=pltpu.CompilerParams(
            dimension_semantics=("arbitrary", "arbitrary"),
            vmem_limit_bytes=VMEM_LIMIT_BYTES),
        name="mixer",
    )(sinks, x, g, rope, conv_w, w_in, w_co, w_ao, w_o, w_gu, w_d)


def _ffn(x, g, g_final, w_gu, w_d, final_norm):
    b, s, d = x.shape
    ts = FFN_SEQ_TILE
    tile = pl.BlockSpec((None, ts, d), lambda i, j: (i, j, 0))
    return pl.pallas_call(
        functools.partial(_ffn_kernel, final_norm=final_norm),
        grid=(b, s // ts),
        in_specs=[tile, _resident((1, d)), _resident((1, d)),
                  _resident(w_gu.shape), _resident(w_d.shape)],
        out_specs=tile,
        out_shape=jax.ShapeDtypeStruct(x.shape, x.dtype),
        scratch_shapes=[pltpu.VMEM((ts, D_FF), BF16)],
        compiler_params=pltpu.CompilerParams(
            dimension_semantics=("arbitrary", "arbitrary"),
            vmem_limit_bytes=VMEM_LIMIT_BYTES),
        name="ffn",
    )(x, g, g_final, w_gu, w_d)


def kernel(x, g_mix, w_in, conv_w, attn_sinks, w_conv_out, w_attn_out, w_o,
           g_ffn, w_gate_up, w_down, g_final):
    b, s, d = x.shape
    depth = w_in.shape[0]
    assert d == D_MODEL and s % SEQ_TILE == 0 and s % FFN_SEQ_TILE == 0
    assert w_in.shape[-1] == N_IN and w_gate_up.shape[-1] == 2 * D_FF
    assert N_IN % STAGE_COLS == 0 and d % STAGE_COLS == 0
    rope = _rope_tables(s)
    g_fin = g_final.reshape(1, d)
    for l in range(depth):
        x, w_gu, w_d = _mixer(
            x, g_mix[l].reshape(1, d), rope, conv_w, attn_sinks[l], l,
            w_in[l], w_conv_out[l], w_attn_out[l], w_o[l],
            w_gate_up[l], w_down[l])
        x = _ffn(x, g_ffn[l].reshape(1, d), g_fin, w_gu, w_d, final_norm=(l == depth - 1))
    return x
```

```python
import functools
import math

import jax
import jax.numpy as jnp
from jax import lax
from jax.experimental import pallas as pl
from jax.experimental.pallas import tpu as pltpu

D_MODEL = 1024
CONV_K = 3
HEAD_DIM = 64
N_HEADS = 16
N_KV_HEADS = 4
GROUP = N_HEADS // N_KV_HEADS
D_ATTN = N_HEADS * HEAD_DIM
D_KV = N_KV_HEADS * HEAD_DIM
WINDOW = 128
BLOCK = 128
ROT_DIM = HEAD_DIM // 4
ROPE_THETA = 500000.0
ATTN_SCALE = 1.0 / math.sqrt(HEAD_DIM)
LOG2E = math.log2(math.e)
NEG_INF = -1e30
D_FF = 2816
EPS = 1e-5

OFF_CB = 0
OFF_CC = OFF_CB + D_MODEL
OFF_CX = OFF_CC + D_MODEL
OFF_Q = OFF_CX + D_MODEL
OFF_K = OFF_Q + D_ATTN
OFF_V = OFF_K + D_KV
OFF_GC = OFF_V + D_KV
OFF_GA = OFF_GC + D_MODEL
N_IN = OFF_GA + D_MODEL

LANES = 128
SUBLANES = 8
BF16_SUBLANES = 16
SEQ_TILE = 512
FFN_SEQ_TILE = 1024
STAGE_COLS = 512
STAGE_SLOTS = 4
FF_CHUNKS = ((0, 1024), (1024, 1024), (2048, 768))
VMEM_LIMIT_BYTES = 58 * 1024 * 1024

F32 = jnp.float32
BF16 = jnp.bfloat16

assert WINDOW == BLOCK and 2 * HEAD_DIM == LANES and GROUP == 4


def _rms_norm(x, g):
    ms = jnp.mean(x * x, axis=-1, keepdims=True)
    return x * lax.rsqrt(ms + EPS) * g


def _sigmoid(x):
    return 1.0 / (1.0 + jnp.exp(-x))


def _rope(t, cos, sin_next, sin_prev):
    nxt = pltpu.roll(t, LANES - ROT_DIM // 2, 1)
    prv = pltpu.roll(t, ROT_DIM // 2, 1)
    return t * cos + nxt * sin_next + prv * sin_prev


def _split_head_pair(t, lane_lo):
    sw = pltpu.roll(t, HEAD_DIM, 1)
    zero = jnp.zeros_like(t)
    even = (jnp.where(lane_lo, t, zero), jnp.where(lane_lo, zero, sw))
    odd = (jnp.where(lane_lo, sw, zero), jnp.where(lane_lo, zero, t))
    return even, odd


def _stage_weights(pairs, stage_ref, sem_ref):
    slots = stage_ref.shape[0]
    jobs = [(src, dst, c * STAGE_COLS)
            for src, dst in pairs for c in range(src.shape[1] // STAGE_COLS)]

    def slab_copy(n):
        src, _, col = jobs[n]
        return pltpu.make_async_copy(
            src.at[:, pl.ds(col, STAGE_COLS)], stage_ref.at[n % slots], sem_ref.at[n % slots])

    for n in range(min(slots - 1, len(jobs))):
        slab_copy(n).start()
    for n, (_, dst, col) in enumerate(jobs):
        if n + slots - 1 < len(jobs):
            slab_copy(n + slots - 1).start()
        slab_copy(n).wait()
        dst[:, col:col + STAGE_COLS] = stage_ref[n % slots].astype(BF16)


def _mixer_kernel(sinks_ref, x_ref, g_ref, rope_ref, convw_ref, w_in_hbm,
                  w_co_hbm, w_ao_hbm, w_o_hbm, w_gu_f32_ref, w_d_f32_ref,
                  out_ref, w_gu_bf16_ref, w_d_bf16_ref,
                  w_in_ref, w_co_ref, w_ao_ref, w_o_ref, stage_ref, stage_sem,
                  u_scr, q_scr, klo_scr, khi_scr, vlo_scr, vhi_scr, attn_scr):
    ts = x_ref.shape[0]
    n_blk = ts // BLOCK
    s_idx = pl.program_id(1)
    kv_scrs = (klo_scr, khi_scr, vlo_scr, vhi_scr)

    @pl.when((pl.program_id(0) == 0) & (s_idx == 0))
    def _():
        _stage_weights(((w_in_hbm, w_in_ref), (w_co_hbm, w_co_ref),
                        (w_ao_hbm, w_ao_ref), (w_o_hbm, w_o_ref)), stage_ref, stage_sem)

    @pl.when(s_idx == 0)
    def _():
        u_scr[0:SUBLANES, :] = jnp.zeros((SUBLANES, D_MODEL), F32)
        for scr in kv_scrs:
            scr[0:BLOCK, :] = jnp.zeros((BLOCK, N_KV_HEADS * LANES), BF16)

    w_gu_bf16_ref[...] = w_gu_f32_ref[...].astype(BF16)
    w_d_bf16_ref[...] = w_d_f32_ref[...].astype(BF16)

    x = x_ref[...]
    h = _rms_norm(x, g_ref[...]).astype(BF16)

    def proj(off, width):
        return jnp.dot(h, w_in_ref[:, off:off + width], preferred_element_type=F32)

    u = proj(OFF_CC, D_MODEL) * proj(OFF_CX, D_MODEL)
    u_scr[SUBLANES:SUBLANES + ts, :] = u
    conv = (convw_ref[0:1, :] * u_scr[SUBLANES - 2:SUBLANES - 2 + ts, :]
            + convw_ref[1:2, :] * u_scr[SUBLANES - 1:SUBLANES - 1 + ts, :]
            + convw_ref[2:3, :] * u)
    u_scr[0:SUBLANES, :] = u_scr[ts:ts + SUBLANES, :]
    conv_y = (proj(OFF_CB, D_MODEL) * conv).astype(BF16)
    conv_out = jnp.dot(conv_y, w_co_ref[...], preferred_element_type=F32)

    q_tabs = [rope_ref[i] * (ATTN_SCALE * LOG2E) for i in range(3)]
    q = proj(OFF_Q, D_ATTN)
    for t in range(D_ATTN // LANES):
        sl = slice(t * LANES, (t + 1) * LANES)
        q_scr[:, sl] = _rope(q[:, sl], *q_tabs).astype(BF16)
    lane_lo_t = lax.broadcasted_iota(jnp.int32, (ts, LANES), 1) < HEAD_DIM
    k = proj(OFF_K, D_KV)
    v = proj(OFF_V, D_KV)
    for t in range(D_KV // LANES):
        sl = slice(t * LANES, (t + 1) * LANES)
        k_t = _rope(k[:, sl], rope_ref[0], rope_ref[1], rope_ref[2])
        for src, lo_scr, hi_scr in ((k_t, klo_scr, khi_scr), (v[:, sl], vlo_scr, vhi_scr)):
            for hd, (lo, hi) in zip((2 * t, 2 * t + 1), _split_head_pair(src, lane_lo_t)):
                hsl = slice(hd * LANES, (hd + 1) * LANES)
                lo_scr[BLOCK:BLOCK + ts, hsl] = lo.astype(BF16)
                hi_scr[BLOCK:BLOCK + ts, hsl] = hi.astype(BF16)
    merged = _sigmoid(proj(OFF_GC, D_MODEL)) * conv_out

    lane = lax.broadcasted_iota(jnp.int32, (BLOCK, LANES), 1)
    rowi = lax.broadcasted_iota(jnp.int32, (BLOCK, LANES), 0)
    from_cur = lane <= rowi
    lane_lo = lane < HEAD_DIM
    lane2 = lax.broadcasted_iota(jnp.int32, (2 * BLOCK, LANES), 1)
    ones_lo = jnp.where(lane2 < HEAD_DIM, 1.0, 0.0).astype(BF16)
    ones_hi = jnp.where(lane2 < HEAD_DIM, 0.0, 1.0).astype(BF16)
    nt_dims = (((1,), (1,)), ((), ()))

    units = [(j, kv) for j in range(n_blk) for kv in range(N_KV_HEADS)]

    def q_slices(kv):
        return [slice((2 * kv + pr) * LANES, (2 * kv + pr + 1) * LANES) for pr in range(2)]

    def scores(j, kv):
        r0 = j * BLOCK
        kv_sl = slice(kv * LANES, (kv + 1) * LANES)
        q2 = jnp.concatenate([q_scr[r0:r0 + BLOCK, sl] for sl in q_slices(kv)], axis=0)
        k_rhs = jnp.concatenate([klo_scr[r0:r0 + 2 * BLOCK, kv_sl],
                                 khi_scr[r0:r0 + 2 * BLOCK, kv_sl]], axis=0)
        return lax.dot_general(q2, k_rhs, nt_dims, preferred_element_type=F32)

    s_next = scores(*units[0])
    for n, (j, kv) in enumerate(units):
        r0 = j * BLOCK
        kv_sl = slice(kv * LANES, (kv + 1) * LANES)
        q_sl = q_slices(kv)
        s = s_next
        if n + 1 < len(units):
            s_next = scores(*units[n + 1])
        v_rhs = jnp.concatenate(
            [jnp.concatenate([vlo_scr[r0:r0 + 2 * BLOCK, kv_sl], ones_lo], axis=1),
             jnp.concatenate([vhi_scr[r0:r0 + 2 * BLOCK, kv_sl], ones_hi], axis=1)], axis=0)
        p_rows, sink_terms = [], []
        for pr in range(2):
            rows = slice(pr * BLOCK, (pr + 1) * BLOCK)
            p_tiles, e_sink = [], []
            for half in range(2):
                hd = kv * GROUP + 2 * pr + half
                s_prev = s[rows, (2 * half) * BLOCK:(2 * half + 1) * BLOCK]
                s_cur = s[rows, (2 * half + 1) * BLOCK:(2 * half + 2) * BLOCK]
                if j == 0:
                    s_prev = jnp.where(s_idx > 0, s_prev, NEG_INF)
                t = jnp.where(from_cur, s_cur, s_prev)
                m = jnp.max(t, axis=-1, keepdims=True)
                p = jnp.exp2(t - m).astype(BF16)
                zero = jnp.zeros_like(p)
                p_tiles += [jnp.where(from_cur, zero, p), jnp.where(from_cur, p, zero)]
                e_sink.append(jnp.exp2(sinks_ref[hd] * LOG2E - m))
            p_rows.append(jnp.concatenate(p_tiles, axis=1))
            sink_terms.append(jnp.where(lane_lo, e_sink[0], e_sink[1]))
        o = jnp.dot(jnp.concatenate(p_rows, axis=0), v_rhs, preferred_element_type=F32)
        for pr in range(2):
            rows = slice(pr * BLOCK, (pr + 1) * BLOCK)
            den = o[rows, LANES:2 * LANES] + sink_terms[pr]
            attn_scr[r0:r0 + BLOCK, q_sl[pr]] = (o[rows, 0:LANES] / den).astype(BF16)

    for scr in kv_scrs:
        scr[0:BLOCK, :] = scr[ts:ts + BLOCK, :]

    gate_a = _sigmoid(proj(OFF_GA, D_MODEL))
    attn_out = jnp.dot(attn_scr[...], w_ao_ref[...], preferred_element_type=F32)
    merged = merged + gate_a * attn_out
    out_ref[...] = x + jnp.dot(merged.astype(BF16), w_o_ref[...], preferred_element_type=F32)


def _ffn_kernel(x_ref, g_ref, gf_ref, w_gu_ref, w_d_ref, out_ref, act_scr, *, final_norm):
    x = x_ref[...]
    h = _rms_norm(x, g_ref[...]).astype(BF16)
    for off, width in FF_CHUNKS:
        gate = jnp.dot(h, w_gu_ref[:, off:off + width], preferred_element_type=F32)
        up = jnp.dot(h, w_gu_ref[:, D_FF + off:D_FF + off + width], preferred_element_type=F32)
        act_scr[:, off:off + width] = (gate * _sigmoid(gate) * up).astype(BF16)
    y = x + jnp.dot(act_scr[...], w_d_ref[...], preferred_element_type=F32)
    if final_norm:
        y = _rms_norm(y, gf_ref[...])
    out_ref[...] = y


def _resident(shape):
    return pl.BlockSpec(shape, lambda b, s: (0,) * len(shape), pipeline_mode=pl.Buffered(1))


def _rope_tables(seq):
    half = ROT_DIM // 2
    d = jnp.arange(LANES) % HEAD_DIM
    inv_freq = ROPE_THETA ** (-(2 * (d % half)).astype(F32) / ROT_DIM)
    ang = jnp.arange(seq, dtype=F32)[:, None] * inv_freq[None, :]
    cos, sin = jnp.cos(ang), jnp.sin(ang)
    c = jnp.where(d < ROT_DIM, cos, 1.0)
    s_next = jnp.where(d < half, -sin, 0.0)
    s_prev = jnp.where((d >= half) & (d < ROT_DIM), sin, 0.0)
    kind = jnp.arange(3)[:, None, None]
    return jnp.where(kind == 0, c, jnp.where(kind == 1, s_next, s_prev))


def _mixer(x, g, rope, conv_w, sinks, layer, w_in, w_co, w_ao, w_o, w_gu, w_d):
    b, s, d = x.shape
    ts = SEQ_TILE
    n_s = s // ts
    gu_rows = w_gu.shape[0] // (b * n_s)
    d_rows = 2 * w_d.shape[0] // (b * n_s)
    assert gu_rows % BF16_SUBLANES == 0 and d_rows % BF16_SUBLANES == 0
    gu_slab = pl.BlockSpec((gu_rows, w_gu.shape[1]), lambda i, j: (i * n_s + j, 0))
    d_slab = pl.BlockSpec((d_rows, w_d.shape[1]), lambda i, j: ((i * n_s + j) // 2, 0))
    tile = pl.BlockSpec((None, ts, d), lambda i, j: (i, j, 0))
    kv_scratch = pltpu.VMEM((BLOCK + ts, N_KV_HEADS * LANES), BF16)
    return pl.pallas_call(
        _mixer_kernel,
        grid=(b, n_s),
        in_specs=[
            pl.BlockSpec(memory_space=pltpu.SMEM),
            tile,
            _resident((1, d)),
            pl.BlockSpec((3, ts, LANES), lambda i, j: (0, j, 0)),
            pl.BlockSpec((None, CONV_K, d), lambda i, j: (layer, 0, 0),
                         pipeline_mode=pl.Buffered(1)),
            pl.BlockSpec(memory_space=pl.ANY),
            pl.BlockSpec(memory_space=pl.ANY),
            pl.BlockSpec(memory_space=pl.ANY),
            pl.BlockSpec(memory_space=pl.ANY),
            gu_slab,
            d_slab,
        ],
        out_specs=[tile, gu_slab, d_slab],
        out_shape=[jax.ShapeDtypeStruct(x.shape, x.dtype),
                   jax.ShapeDtypeStruct(w_gu.shape, BF16),
                   jax.ShapeDtypeStruct(w_d.shape, BF16)],
        scratch_shapes=[
            pltpu.VMEM(w_in.shape, BF16),
            pltpu.VMEM(w_co.shape, BF16),
            pltpu.VMEM(w_ao.shape, BF16),
            pltpu.VMEM(w_o.shape, BF16),
            pltpu.VMEM((STAGE_SLOTS, d, STAGE_COLS), F32),
            pltpu.SemaphoreType.DMA((STAGE_SLOTS,)),
            pltpu.VMEM((SUBLANES + ts, d), F32),
            pltpu.VMEM((ts, D_ATTN), BF16),
            kv_scratch, kv_scratch, kv_scratch, kv_scratch,
            pltpu.VMEM((ts, D_ATTN), BF16),
        ],
        compiler_params=pltpu.CompilerParams(
            dimension_semantics=("arbitrary", "arbitrary"),
            vmem_limit_bytes=VMEM_LIMIT_BYTES),
        name="mixer",
    )(sinks, x, g, rope, conv_w, w_in, w_co, w_ao, w_o, w_gu, w_d)


def _ffn(x, g, g_final, w_gu, w_d, final_norm):
    b, s, d = x.shape
    ts = FFN_SEQ_TILE
    tile = pl.BlockSpec((None, ts, d), lambda i, j: (i, j, 0))
    return pl.pallas_call(
        functools.partial(_ffn_kernel, final_norm=final_norm),
        grid=(b, s // ts),
        in_specs=[tile, _resident((1, d)), _resident((1, d)),
                  _resident(w_gu.shape), _resident(w_d.shape)],
        out_specs=tile,
        out_shape=jax.ShapeDtypeStruct(x.shape, x.dtype),
        scratch_shapes=[pltpu.VMEM((ts, D_FF), BF16)],
        compiler_params=pltpu.CompilerParams(
            dimension_semantics=("arbitrary", "arbitrary"),
            vmem_limit_bytes=VMEM_LIMIT_BYTES),
        name="ffn",
    )(x, g, g_final, w_gu, w_d)


def kernel(x, g_mix, w_in, conv_w, attn_sinks, w_conv_out, w_attn_out, w_o,
           g_ffn, w_gate_up, w_down, g_final):
    b, s, d = x.shape
    depth = w_in.shape[0]
    assert d == D_MODEL and s % SEQ_TILE == 0 and s % FFN_SEQ_TILE == 0
    assert w_in.shape[-1] == N_IN and w_gate_up.shape[-1] == 2 * D_FF
    assert N_IN % STAGE_COLS == 0 and d % STAGE_COLS == 0
    rope = _rope_tables(s)
    g_fin = g_final.reshape(1, d)
    for l in range(depth):
        x, w_gu, w_d = _mixer(
            x, g_mix[l].reshape(1, d), rope, conv_w, attn_sinks[l], l,
            w_in[l], w_conv_out[l], w_attn_out[l], w_o[l],
            w_gate_up[l], w_down[l])
        x = _ffn(x, g_ffn[l].reshape(1, d), g_fin, w_gu, w_d, final_norm=(l == depth - 1))
    return x
```

```python
import functools
import math

import jax
import jax.numpy as jnp
from jax import lax
from jax.experimental import pallas as pl
from jax.experimental.pallas import tpu as pltpu

D_MODEL = 1024
CONV_K = 3
HEAD_DIM = 64
N_HEADS = 16
N_KV_HEADS = 4
GROUP = N_HEADS // N_KV_HEADS
D_ATTN = N_HEADS * HEAD_DIM
D_KV = N_KV_HEADS * HEAD_DIM
WINDOW = 128
BLOCK = 128
ROT_DIM = HEAD_DIM // 4
ROPE_THETA = 500000.0
ATTN_SCALE = 1.0 / math.sqrt(HEAD_DIM)
LOG2E = math.log2(math.e)
NEG_INF = -1e30
D_FF = 2816
EPS = 1e-5

OFF_CB = 0
OFF_CC = OFF_CB + D_MODEL
OFF_CX = OFF_CC + D_MODEL
OFF_Q = OFF_CX + D_MODEL
OFF_K = OFF_Q + D_ATTN
OFF_V = OFF_K + D_KV
OFF_GC = OFF_V + D_KV
OFF_GA = OFF_GC + D_MODEL
N_IN = OFF_GA + D_MODEL

LANES = 128
SUBLANES = 8
BF16_SUBLANES = 16
SEQ_TILE = 512
FFN_SEQ_TILE = 1024
STAGE_COLS = 512
STAGE_SLOTS = 4
FF_CHUNKS = ((0, 1024), (1024, 1024), (2048, 768))
VMEM_LIMIT_BYTES = 58 * 1024 * 1024

F32 = jnp.float32
BF16 = jnp.bfloat16

assert WINDOW == BLOCK and 2 * HEAD_DIM == LANES and GROUP == 4


def _rms_norm(x, g):
    ms = jnp.mean(x * x, axis=-1, keepdims=True)
    return x * lax.rsqrt(ms + EPS) * g


def _sigmoid(x):
    return 1.0 / (1.0 + jnp.exp(-x))


def _rope(t, cos, sin_next, sin_prev):
    nxt = pltpu.roll(t, LANES - ROT_DIM // 2, 1)
    prv = pltpu.roll(t, ROT_DIM // 2, 1)
    return t * cos + nxt * sin_next + prv * sin_prev


def _split_head_pair(t, lane_lo):
    sw = pltpu.roll(t, HEAD_DIM, 1)
    zero = jnp.zeros_like(t)
    even = (jnp.where(lane_lo, t, zero), jnp.where(lane_lo, zero, sw))
    odd = (jnp.where(lane_lo, sw, zero), jnp.where(lane_lo, zero, t))
    return even, odd


def _stage_weights(pairs, stage_ref, sem_ref):
    slots = stage_ref.shape[0]
    jobs = [(src, dst, c * STAGE_COLS)
            for src, dst in pairs for c in range(src.shape[1] // STAGE_COLS)]

    def slab_copy(n):
        src, _, col = jobs[n]
        return pltpu.make_async_copy(
            src.at[:, pl.ds(col, STAGE_COLS)], stage_ref.at[n % slots], sem_ref.at[n % slots])

    for n in range(min(slots - 1, len(jobs))):
        slab_copy(n).start()
    for n, (_, dst, col) in enumerate(jobs):
        if n + slots - 1 < len(jobs):
            slab_copy(n + slots - 1).start()
        slab_copy(n).wait()
        dst[:, col:col + STAGE_COLS] = stage_ref[n % slots].astype(BF16)


def _mixer_kernel(sinks_ref, x_ref, g_ref, rope_ref, convw_ref, w_in_hbm,
                  w_co_hbm, w_ao_hbm, w_o_hbm, w_gu_f32_ref, w_d_f32_ref,
                  out_ref, w_gu_bf16_ref, w_d_bf16_ref,
                  w_in_ref, w_co_ref, w_ao_ref, w_o_ref, stage_ref, stage_sem,
                  u_scr, q_scr, klo_scr, khi_scr, vlo_scr, vhi_scr, attn_scr):
    ts = x_ref.shape[0]
    n_blk = ts // BLOCK
    s_idx = pl.program_id(1)
    kv_scrs = (klo_scr, khi_scr, vlo_scr, vhi_scr)

    @pl.when((pl.program_id(0) == 0) & (s_idx == 0))
    def _():
        _stage_weights(((w_in_hbm, w_in_ref), (w_co_hbm, w_co_ref),
                        (w_ao_hbm, w_ao_ref), (w_o_hbm, w_o_ref)), stage_ref, stage_sem)

    @pl.when(s_idx == 0)
    def _():
        u_scr[0:SUBLANES, :] = jnp.zeros((SUBLANES, D_MODEL), F32)
        for scr in kv_scrs:
            scr[0:BLOCK, :] = jnp.zeros((BLOCK, N_KV_HEADS * LANES), BF16)

    w_gu_bf16_ref[...] = w_gu_f32_ref[...].astype(BF16)
    w_d_bf16_ref[...] = w_d_f32_ref[...].astype(BF16)

    x = x_ref[...]
    h = _rms_norm(x, g_ref[...]).astype(BF16)

    def proj(off, width):
        return jnp.dot(h, w_in_ref[:, off:off + width], preferred_element_type=F32)

    q_tabs = [rope_ref[i] * (ATTN_SCALE * LOG2E) for i in range(3)]
    q = proj(OFF_Q, D_ATTN)
    for t in range(D_ATTN // LANES):
        sl = slice(t * LANES, (t + 1) * LANES)
        q_scr[:, sl] = _rope(q[:, sl], *q_tabs).astype(BF16)
    lane_lo_t = lax.broadcasted_iota(jnp.int32, (ts, LANES), 1) < HEAD_DIM
    k = proj(OFF_K, D_KV)
    v = proj(OFF_V, D_KV)
    for t in range(D_KV // LANES):
        sl = slice(t * LANES, (t + 1) * LANES)
        k_t = _rope(k[:, sl], rope_ref[0], rope_ref[1], rope_ref[2])
        for src, lo_scr, hi_scr in ((k_t, klo_scr, khi_scr), (v[:, sl], vlo_scr, vhi_scr)):
            for hd, (lo, hi) in zip((2 * t, 2 * t + 1), _split_head_pair(src, lane_lo_t)):
                hsl = slice(hd * LANES, (hd + 1) * LANES)
                lo_scr[BLOCK:BLOCK + ts, hsl] = lo.astype(BF16)
                hi_scr[BLOCK:BLOCK + ts, hsl] = hi.astype(BF16)

    u = proj(OFF_CC, D_MODEL) * proj(OFF_CX, D_MODEL)
    u_scr[SUBLANES:SUBLANES + ts, :] = u
    conv = (convw_ref[0:1, :] * u_scr[SUBLANES - 2:SUBLANES - 2 + ts, :]
            + convw_ref[1:2, :] * u_scr[SUBLANES - 1:SUBLANES - 1 + ts, :]
            + convw_ref[2:3, :] * u)
    u_scr[0:SUBLANES, :] = u_scr[ts:ts + SUBLANES, :]
    conv_y = (proj(OFF_CB, D_MODEL) * conv).astype(BF16)
    gate_c = _sigmoid(proj(OFF_GC, D_MODEL))
    merged = gate_c * jnp.dot(conv_y, w_co_ref[...], preferred_element_type=F32)

    lane = lax.broadcasted_iota(jnp.int32, (BLOCK, LANES), 1)
    rowi = lax.broadcasted_iota(jnp.int32, (BLOCK, LANES), 0)
    from_cur = lane <= rowi
    lane_lo = lane < HEAD_DIM
    lane2 = lax.broadcasted_iota(jnp.int32, (2 * BLOCK, LANES), 1)
    ones_lo = jnp.where(lane2 < HEAD_DIM, 1.0, 0.0).astype(BF16)
    ones_hi = jnp.where(lane2 < HEAD_DIM, 0.0, 1.0).astype(BF16)
    nt_dims = (((1,), (1,)), ((), ()))

    units = [(j, kv) for j in range(n_blk) for kv in range(N_KV_HEADS)]

    def q_slices(kv):
        return [slice((2 * kv + pr) * LANES, (2 * kv + pr + 1) * LANES) for pr in range(2)]

    def scores(j, kv):
        r0 = j * BLOCK
        kv_sl = slice(kv * LANES, (kv + 1) * LANES)
        q2 = jnp.concatenate([q_scr[r0:r0 + BLOCK, sl] for sl in q_slices(kv)], axis=0)
        k_rhs = jnp.concatenate([klo_scr[r0:r0 + 2 * BLOCK, kv_sl],
                                 khi_scr[r0:r0 + 2 * BLOCK, kv_sl]], axis=0)
        return lax.dot_general(q2, k_rhs, nt_dims, preferred_element_type=F32)

    s_next = scores(*units[0])
    for n, (j, kv) in enumerate(units):
        r0 = j * BLOCK
        kv_sl = slice(kv * LANES, (kv + 1) * LANES)
        q_sl = q_slices(kv)
        s = s_next
        if n + 1 < len(units):
            s_next = scores(*units[n + 1])
        else:
            gate_a_halves = [_sigmoid(proj(OFF_GA, D_MODEL // 2))]
        v_rhs = jnp.concatenate(
            [jnp.concatenate([vlo_scr[r0:r0 + 2 * BLOCK, kv_sl], ones_lo], axis=1),
             jnp.concatenate([vhi_scr[r0:r0 + 2 * BLOCK, kv_sl], ones_hi], axis=1)], axis=0)
        p_rows, sink_terms = [], []
        for pr in range(2):
            rows = slice(pr * BLOCK, (pr + 1) * BLOCK)
            p_tiles, e_sink = [], []
            for half in range(2):
                hd = kv * GROUP + 2 * pr + half
                s_prev = s[rows, (2 * half) * BLOCK:(2 * half + 1) * BLOCK]
                s_cur = s[rows, (2 * half + 1) * BLOCK:(2 * half + 2) * BLOCK]
                if j == 0:
                    s_prev = jnp.where(s_idx > 0, s_prev, NEG_INF)
                t = jnp.where(from_cur, s_cur, s_prev)
                m = jnp.max(t, axis=-1, keepdims=True)
                p = jnp.exp2(t - m).astype(BF16)
                zero = jnp.zeros_like(p)
                p_tiles += [jnp.where(from_cur, zero, p), jnp.where(from_cur, p, zero)]
                e_sink.append(jnp.exp2(sinks_ref[hd] * LOG2E - m))
            p_rows.append(jnp.concatenate(p_tiles, axis=1))
            sink_terms.append(jnp.where(lane_lo, e_sink[0], e_sink[1]))
        o = jnp.dot(jnp.concatenate(p_rows, axis=0), v_rhs, preferred_element_type=F32)
        if n + 1 == len(units):
            gate_a_halves.append(_sigmoid(proj(OFF_GA + D_MODEL // 2, D_MODEL // 2)))
        for pr in range(2):
            rows = slice(pr * BLOCK, (pr + 1) * BLOCK)
            den = o[rows, LANES:2 * LANES] + sink_terms[pr]
            attn_scr[r0:r0 + BLOCK, q_sl[pr]] = (o[rows, 0:LANES] / den).astype(BF16)

    for scr in kv_scrs:
        scr[0:BLOCK, :] = scr[ts:ts + BLOCK, :]

    attn_out = jnp.dot(attn_scr[...], w_ao_ref[...], preferred_element_type=F32)
    merged = merged + jnp.concatenate(gate_a_halves, axis=1) * attn_out
    out_ref[...] = x + jnp.dot(merged.astype(BF16), w_o_ref[...], preferred_element_type=F32)


def _ffn_kernel(x_ref, g_ref, gf_ref, w_gu_ref, w_d_ref, out_ref, act_scr, *, final_norm):
    x = x_ref[...]
    h = _rms_norm(x, g_ref[...]).astype(BF16)
    for off, width in FF_CHUNKS:
        gate = jnp.dot(h, w_gu_ref[:, off:off + width], preferred_element_type=F32)
        up = jnp.dot(h, w_gu_ref[:, D_FF + off:D_FF + off + width], preferred_element_type=F32)
        act_scr[:, off:off + width] = (gate * _sigmoid(gate) * up).astype(BF16)
    y = x + jnp.dot(act_scr[...], w_d_ref[...], preferred_element_type=F32)
    if final_norm:
        y = _rms_norm(y, gf_ref[...])
    out_ref[...] = y


def _resident(shape):
    return pl.BlockSpec(shape, lambda b, s: (0,) * len(shape), pipeline_mode=pl.Buffered(1))


def _rope_tables(seq):
    half = ROT_DIM // 2
    d = jnp.arange(LANES) % HEAD_DIM
    inv_freq = ROPE_THETA ** (-(2 * (d % half)).astype(F32) / ROT_DIM)
    ang = jnp.arange(seq, dtype=F32)[:, None] * inv_freq[None, :]
    cos, sin = jnp.cos(ang), jnp.sin(ang)
    c = jnp.where(d < ROT_DIM, cos, 1.0)
    s_next = jnp.where(d < half, -sin, 0.0)
    s_prev = jnp.where((d >= half) & (d < ROT_DIM), sin, 0.0)
    kind = jnp.arange(3)[:, None, None]
    return jnp.where(kind == 0, c, jnp.where(kind == 1, s_next, s_prev))


def _mixer(x, g, rope, conv_w, sinks, layer, w_in, w_co, w_ao, w_o, w_gu, w_d):
    b, s, d = x.shape
    ts = SEQ_TILE
    n_s = s // ts
    gu_rows = w_gu.shape[0] // (b * n_s)
    d_rows = 2 * w_d.shape[0] // (b * n_s)
    assert gu_rows % BF16_SUBLANES == 0 and d_rows % BF16_SUBLANES == 0
    gu_slab = pl.BlockSpec((gu_rows, w_gu.shape[1]), lambda i, j: (i * n_s + j, 0))
    d_slab = pl.BlockSpec((d_rows, w_d.shape[1]), lambda i, j: ((i * n_s + j) // 2, 0))
    tile = pl.BlockSpec((None, ts, d), lambda i, j: (i, j, 0))
    kv_scratch = pltpu.VMEM((BLOCK + ts, N_KV_HEADS * LANES), BF16)
    return pl.pallas_call(
        _mixer_kernel,
        grid=(b, n_s),
        in_specs=[
            pl.BlockSpec(memory_space=pltpu.SMEM),
            tile,
            _resident((1, d)),
            pl.BlockSpec((3, ts, LANES), lambda i, j: (0, j, 0)),
            pl.BlockSpec((None, CONV_K, d), lambda i, j: (layer, 0, 0),
                         pipeline_mode=pl.Buffered(1)),
            pl.BlockSpec(memory_space=pl.ANY),
            pl.BlockSpec(memory_space=pl.ANY),
            pl.BlockSpec(memory_space=pl.ANY),
            pl.BlockSpec(memory_space=pl.ANY),
            gu_slab,
            d_slab,
        ],
        out_specs=[tile, gu_slab, d_slab],
        out_shape=[jax.ShapeDtypeStruct(x.shape, x.dtype),
                   jax.ShapeDtypeStruct(w_gu.shape, BF16),
                   jax.ShapeDtypeStruct(w_d.shape, BF16)],
        scratch_shapes=[
            pltpu.VMEM(w_in.shape, BF16),
            pltpu.VMEM(w_co.shape, BF16),
            pltpu.VMEM(w_ao.shape, BF16),
            pltpu.VMEM(w_o.shape, BF16),
            pltpu.VMEM((STAGE_SLOTS, d, STAGE_COLS), F32),
            pltpu.SemaphoreType.DMA((STAGE_SLOTS,)),
            pltpu.VMEM((SUBLANES + ts, d), F32),
            pltpu.VMEM((ts, D_ATTN), BF16),
            kv_scratch, kv_scratch, kv_scratch, kv_scratch,
            pltpu.VMEM((ts, D_ATTN), BF16),
        ],
        compiler_params=pltpu.CompilerParams(
            dimension_semantics=("arbitrary", "arbitrary"),
            vmem_limit_bytes=VMEM_LIMIT_BYTES),
        name="mixer",
    )(sinks, x, g, rope, conv_w, w_in, w_co, w_ao, w_o, w_gu, w_d)


def _ffn(x, g, g_final, w_gu, w_d, final_norm):
    b, s, d = x.shape
    ts = FFN_SEQ_TILE
    tile = pl.BlockSpec((None, ts, d), lambda i, j: (i, j, 0))
    return pl.pallas_call(
        functools.partial(_ffn_kernel, final_norm=final_norm),
        grid=(b, s // ts),
        in_specs=[tile, _resident((1, d)), _resident((1, d)),
                  _resident(w_gu.shape), _resident(w_d.shape)],
        out_specs=tile,
        out_shape=jax.ShapeDtypeStruct(x.shape, x.dtype),
        scratch_shapes=[pltpu.VMEM((ts, D_FF), BF16)],
        compiler_params=pltpu.CompilerParams(
            dimension_semantics=("arbitrary", "arbitrary"),
            vmem_limit_bytes=VMEM_LIMIT_BYTES),
        name="ffn",
    )(x, g, g_final, w_gu, w_d)


def kernel(x, g_mix, w_in, conv_w, attn_sinks, w_conv_out, w_attn_out, w_o,
           g_ffn, w_gate_up, w_down, g_final):
    b, s, d = x.shape
    depth = w_in.shape[0]
    assert d == D_MODEL and s % SEQ_TILE == 0 and s % FFN_SEQ_TILE == 0
    assert w_in.shape[-1] == N_IN and w_gate_up.shape[-1] == 2 * D_FF
    assert N_IN % STAGE_COLS == 0 and d % STAGE_COLS == 0
    rope = _rope_tables(s)
    g_fin = g_final.reshape(1, d)
    for l in range(depth):
        x, w_gu, w_d = _mixer(
            x, g_mix[l].reshape(1, d), rope, conv_w, attn_sinks[l], l,
            w_in[l], w_conv_out[l], w_attn_out[l], w_o[l],
            w_gate_up[l], w_down[l])
        x = _ffn(x, g_ffn[l].reshape(1, d), g_fin, w_gu, w_d, final_norm=(l == depth - 1))
    return x
```

```python
import functools
import math

import jax
import jax.numpy as jnp
from jax import lax
from jax.experimental import pallas as pl
from jax.experimental.pallas import tpu as pltpu

D_MODEL = 1024
CONV_K = 3
HEAD_DIM = 64
N_HEADS = 16
N_KV_HEADS = 4
GROUP = N_HEADS // N_KV_HEADS
D_ATTN = N_HEADS * HEAD_DIM
D_KV = N_KV_HEADS * HEAD_DIM
WINDOW = 128
BLOCK = 128
ROT_DIM = HEAD_DIM // 4
ROPE_THETA = 500000.0
ATTN_SCALE = 1.0 / math.sqrt(HEAD_DIM)
LOG2E = math.log2(math.e)
NEG_INF = -1e30
D_FF = 2816
EPS = 1e-5

OFF_CB = 0
OFF_CC = OFF_CB + D_MODEL
OFF_CX = OFF_CC + D_MODEL
OFF_Q = OFF_CX + D_MODEL
OFF_K = OFF_Q + D_ATTN
OFF_V = OFF_K + D_KV
OFF_GC = OFF_V + D_KV
OFF_GA = OFF_GC + D_MODEL
N_IN = OFF_GA + D_MODEL

LANES = 128
SUBLANES = 8
BF16_SUBLANES = 16
SEQ_TILE = 512
FFN_SEQ_TILE = 1024
STAGE_COLS = 512
STAGE_SLOTS = 4
FF_CHUNKS = ((0, 1024), (1024, 1024), (2048, 768))
VMEM_LIMIT_BYTES = 58 * 1024 * 1024

F32 = jnp.float32
BF16 = jnp.bfloat16

assert WINDOW == BLOCK and 2 * HEAD_DIM == LANES and GROUP == 4


def _rms_norm(x, g):
    ms = jnp.mean(x * x, axis=-1, keepdims=True)
    return x * lax.rsqrt(ms + EPS) * g


def _sigmoid(x):
    return 1.0 / (1.0 + jnp.exp(-x))


def _rope(t, cos, sin_next, sin_prev):
    nxt = pltpu.roll(t, LANES - ROT_DIM // 2, 1)
    prv = pltpu.roll(t, ROT_DIM // 2, 1)
    return t * cos + nxt * sin_next + prv * sin_prev


def _split_head_pair(t, lane_lo):
    sw = pltpu.roll(t, HEAD_DIM, 1)
    zero = jnp.zeros_like(t)
    even = (jnp.where(lane_lo, t, zero), jnp.where(lane_lo, zero, sw))
    odd = (jnp.where(lane_lo, sw, zero), jnp.where(lane_lo, zero, t))
    return even, odd


def _stage_weights(pairs, stage_ref, sem_ref):
    slots = stage_ref.shape[0]
    jobs = [(src, dst, c * STAGE_COLS)
            for src, dst in pairs for c in range(src.shape[1] // STAGE_COLS)]

    def slab_copy(n):
        src, _, col = jobs[n]
        return pltpu.make_async_copy(
            src.at[:, pl.ds(col, STAGE_COLS)], stage_ref.at[n % slots], sem_ref.at[n % slots])

    for n in range(min(slots - 1, len(jobs))):
        slab_copy(n).start()
    for n, (_, dst, col) in enumerate(jobs):
        if n + slots - 1 < len(jobs):
            slab_copy(n + slots - 1).start()
        slab_copy(n).wait()
        dst[:, col:col + STAGE_COLS] = stage_ref[n % slots].astype(BF16)


def _mixer_kernel(sinks_ref, x_ref, x_next_ref, g_ref, rope_ref, convw_ref, w_in_hbm,
                  w_co_hbm, w_ao_hbm, w_o_hbm, w_gu_f32_ref, w_d_f32_ref,
                  out_ref, w_gu_bf16_ref, w_d_bf16_ref,
                  w_in_ref, w_co_ref, w_ao_ref, w_o_ref, stage_ref, stage_sem, h_scr,
                  u_scr, q_scr, klo_scr, khi_scr, vlo_scr, vhi_scr, attn_scr):
    ts = x_ref.shape[0]
    n_blk = ts // BLOCK
    s_idx = pl.program_id(1)
    kv_scrs = (klo_scr, khi_scr, vlo_scr, vhi_scr)

    @pl.when((pl.program_id(0) == 0) & (s_idx == 0))
    def _():
        _stage_weights(((w_in_hbm, w_in_ref), (w_co_hbm, w_co_ref),
                        (w_ao_hbm, w_ao_ref), (w_o_hbm, w_o_ref)), stage_ref, stage_sem)
        h_scr[...] = _rms_norm(x_ref[...], g_ref[...]).astype(BF16)

    @pl.when(s_idx == 0)
    def _():
        u_scr[0:SUBLANES, :] = jnp.zeros((SUBLANES, D_MODEL), F32)
        for scr in kv_scrs:
            scr[0:BLOCK, :] = jnp.zeros((BLOCK, N_KV_HEADS * LANES), BF16)

    w_gu_bf16_ref[...] = w_gu_f32_ref[...].astype(BF16)
    w_d_bf16_ref[...] = w_d_f32_ref[...].astype(BF16)

    def proj(off, width):
        return jnp.dot(h_scr[...], w_in_ref[:, off:off + width], preferred_element_type=F32)

    q_tabs = [rope_ref[i] * (ATTN_SCALE * LOG2E) for i in range(3)]
    q = proj(OFF_Q, D_ATTN)
    for t in range(D_ATTN // LANES):
        sl = slice(t * LANES, (t + 1) * LANES)
        q_scr[:, sl] = _rope(q[:, sl], *q_tabs).astype(BF16)
    lane_lo_t = lax.broadcasted_iota(jnp.int32, (ts, LANES), 1) < HEAD_DIM
    k = proj(OFF_K, D_KV)
    v = proj(OFF_V, D_KV)
    for t in range(D_KV // LANES):
        sl = slice(t * LANES, (t + 1) * LANES)
        k_t = _rope(k[:, sl], rope_ref[0], rope_ref[1], rope_ref[2])
        for src, lo_scr, hi_scr in ((k_t, klo_scr, khi_scr), (v[:, sl], vlo_scr, vhi_scr)):
            for hd, (lo, hi) in zip((2 * t, 2 * t + 1), _split_head_pair(src, lane_lo_t)):
                hsl = slice(hd * LANES, (hd + 1) * LANES)
                lo_scr[BLOCK:BLOCK + ts, hsl] = lo.astype(BF16)
                hi_scr[BLOCK:BLOCK + ts, hsl] = hi.astype(BF16)

    u = proj(OFF_CC, D_MODEL) * proj(OFF_CX, D_MODEL)
    u_scr[SUBLANES:SUBLANES + ts, :] = u
    conv = (convw_ref[0:1, :] * u_scr[SUBLANES - 2:SUBLANES - 2 + ts, :]
            + convw_ref[1:2, :] * u_scr[SUBLANES - 1:SUBLANES - 1 + ts, :]
            + convw_ref[2:3, :] * u)
    u_scr[0:SUBLANES, :] = u_scr[ts:ts + SUBLANES, :]
    conv_y = (proj(OFF_CB, D_MODEL) * conv).astype(BF16)
    gate_c = _sigmoid(proj(OFF_GC, D_MODEL))
    merged = gate_c * jnp.dot(conv_y, w_co_ref[...], preferred_element_type=F32)

    lane = lax.broadcasted_iota(jnp.int32, (BLOCK, LANES), 1)
    rowi = lax.broadcasted_iota(jnp.int32, (BLOCK, LANES), 0)
    from_cur = lane <= rowi
    lane_lo = lane < HEAD_DIM
    lane2 = lax.broadcasted_iota(jnp.int32, (2 * BLOCK, LANES), 1)
    ones_lo = jnp.where(lane2 < HEAD_DIM, 1.0, 0.0).astype(BF16)
    ones_hi = jnp.where(lane2 < HEAD_DIM, 0.0, 1.0).astype(BF16)
    nt_dims = (((1,), (1,)), ((), ()))

    units = [(j, kv) for j in range(n_blk) for kv in range(N_KV_HEADS)]

    def q_slices(kv):
        return [slice((2 * kv + pr) * LANES, (2 * kv + pr + 1) * LANES) for pr in range(2)]

    def scores(j, kv):
        r0 = j * BLOCK
        kv_sl = slice(kv * LANES, (kv + 1) * LANES)
        q2 = jnp.concatenate([q_scr[r0:r0 + BLOCK, sl] for sl in q_slices(kv)], axis=0)
        k_rhs = jnp.concatenate([klo_scr[r0:r0 + 2 * BLOCK, kv_sl],
                                 khi_scr[r0:r0 + 2 * BLOCK, kv_sl]], axis=0)
        return lax.dot_general(q2, k_rhs, nt_dims, preferred_element_type=F32)

    s_next = scores(*units[0])
    for n, (j, kv) in enumerate(units):
        r0 = j * BLOCK
        kv_sl = slice(kv * LANES, (kv + 1) * LANES)
        q_sl = q_slices(kv)
        s = s_next
        if n + 1 < len(units):
            s_next = scores(*units[n + 1])
        else:
            gate_a_halves = [_sigmoid(proj(OFF_GA, D_MODEL // 2))]
        v_rhs = jnp.concatenate(
            [jnp.concatenate([vlo_scr[r0:r0 + 2 * BLOCK, kv_sl], ones_lo], axis=1),
             jnp.concatenate([vhi_scr[r0:r0 + 2 * BLOCK, kv_sl], ones_hi], axis=1)], axis=0)
        p_rows, sink_terms = [], []
        for pr in range(2):
            rows = slice(pr * BLOCK, (pr + 1) * BLOCK)
            p_tiles, e_sink = [], []
            for half in range(2):
                hd = kv * GROUP + 2 * pr + half
                s_prev = s[rows, (2 * half) * BLOCK:(2 * half + 1) * BLOCK]
                s_cur = s[rows, (2 * half + 1) * BLOCK:(2 * half + 2) * BLOCK]
                if j == 0:
                    s_prev = jnp.where(s_idx > 0, s_prev, NEG_INF)
                t = jnp.where(from_cur, s_cur, s_prev)
                m = jnp.max(t, axis=-1, keepdims=True)
                p = jnp.exp2(t - m).astype(BF16)
                zero = jnp.zeros_like(p)
                p_tiles += [jnp.where(from_cur, zero, p), jnp.where(from_cur, p, zero)]
                e_sink.append(jnp.exp2(sinks_ref[hd] * LOG2E - m))
            p_rows.append(jnp.concatenate(p_tiles, axis=1))
            sink_terms.append(jnp.where(lane_lo, e_sink[0], e_sink[1]))
        o = jnp.dot(jnp.concatenate(p_rows, axis=0), v_rhs, preferred_element_type=F32)
        if n + 1 == len(units):
            gate_a_halves.append(_sigmoid(proj(OFF_GA + D_MODEL // 2, D_MODEL // 2)))
        for pr in range(2):
            rows = slice(pr * BLOCK, (pr + 1) * BLOCK)
            den = o[rows, LANES:2 * LANES] + sink_terms[pr]
            attn_scr[r0:r0 + BLOCK, q_sl[pr]] = (o[rows, 0:LANES] / den).astype(BF16)

    for scr in kv_scrs:
        scr[0:BLOCK, :] = scr[ts:ts + BLOCK, :]

    h_scr[...] = _rms_norm(x_next_ref[...], g_ref[...]).astype(BF16)
    attn_out = jnp.dot(attn_scr[...], w_ao_ref[...], preferred_element_type=F32)
    merged = merged + jnp.concatenate(gate_a_halves, axis=1) * attn_out
    out_ref[...] = x_ref[...] + jnp.dot(
        merged.astype(BF16), w_o_ref[...], preferred_element_type=F32)


def _ffn_kernel(x_ref, x_next_ref, g_ref, gf_ref, w_gu_ref, w_d_ref, out_ref,
                h_scr, act_scr, *, final_norm):
    @pl.when((pl.program_id(0) == 0) & (pl.program_id(1) == 0))
    def _():
        h_scr[...] = _rms_norm(x_ref[...], g_ref[...]).astype(BF16)

    for off, width in FF_CHUNKS:
        gate = jnp.dot(h_scr[...], w_gu_ref[:, off:off + width], preferred_element_type=F32)
        up = jnp.dot(h_scr[...], w_gu_ref[:, D_FF + off:D_FF + off + width],
                     preferred_element_type=F32)
        act_scr[:, off:off + width] = (gate * _sigmoid(gate) * up).astype(BF16)
    h_scr[...] = _rms_norm(x_next_ref[...], g_ref[...]).astype(BF16)
    y = x_ref[...] + jnp.dot(act_scr[...], w_d_ref[...], preferred_element_type=F32)
    if final_norm:
        y = _rms_norm(y, gf_ref[...])
    out_ref[...] = y


def _resident(shape):
    return pl.BlockSpec(shape, lambda b, s: (0,) * len(shape), pipeline_mode=pl.Buffered(1))


def _next_tile_spec(n_batch, n_seq_tiles, ts, d):
    def index_map(i, j):
        nxt = jnp.minimum(i * n_seq_tiles + j + 1, n_batch * n_seq_tiles - 1)
        return nxt // n_seq_tiles, nxt % n_seq_tiles, 0
    return pl.BlockSpec((None, ts, d), index_map)


def _rope_tables(seq):
    half = ROT_DIM // 2
    d = jnp.arange(LANES) % HEAD_DIM
    inv_freq = ROPE_THETA ** (-(2 * (d % half)).astype(F32) / ROT_DIM)
    ang = jnp.arange(seq, dtype=F32)[:, None] * inv_freq[None, :]
    cos, sin = jnp.cos(ang), jnp.sin(ang)
    c = jnp.where(d < ROT_DIM, cos, 1.0)
    s_next = jnp.where(d < half, -sin, 0.0)
    s_prev = jnp.where((d >= half) & (d < ROT_DIM), sin, 0.0)
    kind = jnp.arange(3)[:, None, None]
    return jnp.where(kind == 0, c, jnp.where(kind == 1, s_next, s_prev))


def _mixer(x, g, rope, conv_w, sinks, layer, w_in, w_co, w_ao, w_o, w_gu, w_d):
    b, s, d = x.shape
    ts = SEQ_TILE
    n_s = s // ts
    gu_rows = w_gu.shape[0] // (b * n_s)
    d_rows = 2 * w_d.shape[0] // (b * n_s)
    assert gu_rows % BF16_SUBLANES == 0 and d_rows % BF16_SUBLANES == 0
    gu_slab = pl.BlockSpec((gu_rows, w_gu.shape[1]), lambda i, j: (i * n_s + j, 0))
    d_slab = pl.BlockSpec((d_rows, w_d.shape[1]), lambda i, j: ((i * n_s + j) // 2, 0))
    tile = pl.BlockSpec((None, ts, d), lambda i, j: (i, j, 0))
    next_tile = _next_tile_spec(b, n_s, ts, d)
    kv_scratch = pltpu.VMEM((BLOCK + ts, N_KV_HEADS * LANES), BF16)
    return pl.pallas_call(
        _mixer_kernel,
        grid=(b, n_s),
        in_specs=[
            pl.BlockSpec(memory_space=pltpu.SMEM),
            tile,
            next_tile,
            _resident((1, d)),
            pl.BlockSpec((3, ts, LANES), lambda i, j: (0, j, 0)),
            pl.BlockSpec((None, CONV_K, d), lambda i, j: (layer, 0, 0),
                         pipeline_mode=pl.Buffered(1)),
            pl.BlockSpec(memory_space=pl.ANY),
            pl.BlockSpec(memory_space=pl.ANY),
            pl.BlockSpec(memory_space=pl.ANY),
            pl.BlockSpec(memory_space=pl.ANY),
            gu_slab,
            d_slab,
        ],
        out_specs=[tile, gu_slab, d_slab],
        out_shape=[jax.ShapeDtypeStruct(x.shape, x.dtype),
                   jax.ShapeDtypeStruct(w_gu.shape, BF16),
                   jax.ShapeDtypeStruct(w_d.shape, BF16)],
        scratch_shapes=[
            pltpu.VMEM(w_in.shape, BF16),
            pltpu.VMEM(w_co.shape, BF16),
            pltpu.VMEM(w_ao.shape, BF16),
            pltpu.VMEM(w_o.shape, BF16),
            pltpu.VMEM((STAGE_SLOTS, d, STAGE_COLS), F32),
            pltpu.SemaphoreType.DMA((STAGE_SLOTS,)),
            pltpu.VMEM((ts, d), BF16),
            pltpu.VMEM((SUBLANES + ts, d), F32),
            pltpu.VMEM((ts, D_ATTN), BF16),
            kv_scratch, kv_scratch, kv_scratch, kv_scratch,
            pltpu.VMEM((ts, D_ATTN), BF16),
        ],
        compiler_params=pltpu.CompilerParams(
            dimension_semantics=("arbitrary", "arbitrary"),
            vmem_limit_bytes=VMEM_LIMIT_BYTES),
        name="mixer",
    )(sinks, x, x, g, rope, conv_w, w_in, w_co, w_ao, w_o, w_gu, w_d)


def _ffn(x, g, g_final, w_gu, w_d, final_norm):
    b, s, d = x.shape
    ts = FFN_SEQ_TILE
    tile = pl.BlockSpec((None, ts, d), lambda i, j: (i, j, 0))
    return pl.pallas_call(
        functools.partial(_ffn_kernel, final_norm=final_norm),
        grid=(b, s // ts),
        in_specs=[tile, _next_tile_spec(b, s // ts, ts, d), _resident((1, d)), _resident((1, d)),
                  _resident(w_gu.shape), _resident(w_d.shape)],
        out_specs=tile,
        out_shape=jax.ShapeDtypeStruct(x.shape, x.dtype),
        scratch_shapes=[pltpu.VMEM((ts, d), BF16), pltpu.VMEM((ts, D_FF), BF16)],
        compiler_params=pltpu.CompilerParams(
            dimension_semantics=("arbitrary", "arbitrary"),
            vmem_limit_bytes=VMEM_LIMIT_BYTES),
        name="ffn",
    )(x, x, g, g_final, w_gu, w_d)


def kernel(x, g_mix, w_in, conv_w, attn_sinks, w_conv_out, w_attn_out, w_o,
           g_ffn, w_gate_up, w_down, g_final):
    b, s, d = x.shape
    depth = w_in.shape[0]
    assert d == D_MODEL and s % SEQ_TILE == 0 and s % FFN_SEQ_TILE == 0
    assert w_in.shape[-1] == N_IN and w_gate_up.shape[-1] == 2 * D_FF
    assert N_IN % STAGE_COLS == 0 and d % STAGE_COLS == 0
    rope = _rope_tables(s)
    g_fin = g_final.reshape(1, d)
    for l in range(depth):
        x, w_gu, w_d = _mixer(
            x, g_mix[l].reshape(1, d), rope, conv_w, attn_sinks[l], l,
            w_in[l], w_conv_out[l], w_attn_out[l], w_o[l],
            w_gate_up[l], w_down[l])
        x = _ffn(x, g_ffn[l].reshape(1, d), g_fin, w_gu, w_d, final_norm=(l == depth - 1))
    return x
```

```python
import functools
import math

import jax
import jax.numpy as jnp
from jax import lax
from jax.experimental import pallas as pl
from jax.experimental.pallas import tpu as pltpu

D_MODEL = 1024
CONV_K = 3
HEAD_DIM = 64
N_HEADS = 16
N_KV_HEADS = 4
GROUP = N_HEADS // N_KV_HEADS
D_ATTN = N_HEADS * HEAD_DIM
D_KV = N_KV_HEADS * HEAD_DIM
WINDOW = 128
BLOCK = 128
ROT_DIM = HEAD_DIM // 4
ROPE_THETA = 500000.0
ATTN_SCALE = 1.0 / math.sqrt(HEAD_DIM)
LOG2E = math.log2(math.e)
NEG_INF = -1e30
D_FF = 2816
EPS = 1e-5

OFF_CB = 0
OFF_CC = OFF_CB + D_MODEL
OFF_CX = OFF_CC + D_MODEL
OFF_Q = OFF_CX + D_MODEL
OFF_K = OFF_Q + D_ATTN
OFF_V = OFF_K + D_KV
OFF_GC = OFF_V + D_KV
OFF_GA = OFF_GC + D_MODEL
N_IN = OFF_GA + D_MODEL

LANES = 128
SUBLANES = 8
BF16_SUBLANES = 16
MXU_TILE = 256
SEQ_TILE = 512
FFN_SEQ_TILE = 1024
LOOKAHEAD = 2
STAGE_COLS = 512
STAGE_SLOTS = 4
FF_CHUNKS = ((0, 1024), (1024, 1024), (2048, 768))
VMEM_LIMIT_BYTES = 58 * 1024 * 1024

F32 = jnp.float32
BF16 = jnp.bfloat16

assert WINDOW == BLOCK and 2 * HEAD_DIM == LANES and GROUP == 4


def _rms_norm(x, g):
    ms = jnp.mean(x * x, axis=-1, keepdims=True)
    return x * lax.rsqrt(ms + EPS) * g


def _sigmoid(x):
    return 1.0 / (1.0 + jnp.exp(-x))


def _rope(t, cos, sin_next, sin_prev):
    nxt = pltpu.roll(t, LANES - ROT_DIM // 2, 1)
    prv = pltpu.roll(t, ROT_DIM // 2, 1)
    return t * cos + nxt * sin_next + prv * sin_prev


def _split_head_pair(t, lane_lo):
    sw = pltpu.roll(t, HEAD_DIM, 1)
    zero = jnp.zeros_like(t)
    even = (jnp.where(lane_lo, t, zero), jnp.where(lane_lo, zero, sw))
    odd = (jnp.where(lane_lo, sw, zero), jnp.where(lane_lo, zero, t))
    return even, odd


def _stage_weights(pairs, stage_ref, sem_ref):
    slots = stage_ref.shape[0]
    jobs = [(src, dst, c * STAGE_COLS)
            for src, dst in pairs for c in range(src.shape[1] // STAGE_COLS)]

    def slab_copy(n):
        src, _, col = jobs[n]
        return pltpu.make_async_copy(
            src.at[:, pl.ds(col, STAGE_COLS)], stage_ref.at[n % slots], sem_ref.at[n % slots])

    for n in range(min(slots - 1, len(jobs))):
        slab_copy(n).start()
    for n, (_, dst, col) in enumerate(jobs):
        if n + slots - 1 < len(jobs):
            slab_copy(n + slots - 1).start()
        slab_copy(n).wait()
        dst[:, col:col + STAGE_COLS] = stage_ref[n % slots].astype(BF16)


def _mixer_kernel(sinks_ref, x_ref, g_ref, rope_ref, convw_ref, w_in_hbm,
                  w_co_hbm, w_ao_hbm, w_o_hbm, w_gu_f32_ref, w_d_f32_ref,
                  out_ref, w_gu_bf16_ref, w_d_bf16_ref,
                  w_in_ref, w_co_ref, w_ao_ref, w_o_ref, stage_ref, stage_sem,
                  u_scr, q_scr, klo_scr, khi_scr, vlo_scr, vhi_scr, attn_scr):
    ts = x_ref.shape[0]
    n_blk = ts // BLOCK
    s_idx = pl.program_id(1)
    kv_scrs = (klo_scr, khi_scr, vlo_scr, vhi_scr)

    @pl.when((pl.program_id(0) == 0) & (s_idx == 0))
    def _():
        _stage_weights(((w_in_hbm, w_in_ref), (w_co_hbm, w_co_ref),
                        (w_ao_hbm, w_ao_ref), (w_o_hbm, w_o_ref)), stage_ref, stage_sem)

    @pl.when(s_idx == 0)
    def _():
        u_scr[0:SUBLANES, :] = jnp.zeros((SUBLANES, D_MODEL), F32)
        for scr in kv_scrs:
            scr[0:BLOCK, :] = jnp.zeros((BLOCK, N_KV_HEADS * LANES), BF16)

    w_gu_bf16_ref[...] = w_gu_f32_ref[...].astype(BF16)
    w_d_bf16_ref[...] = w_d_f32_ref[...].astype(BF16)

    x = x_ref[...]
    h = _rms_norm(x, g_ref[...]).astype(BF16)

    def proj(off, width):
        return jnp.dot(h, w_in_ref[:, off:off + width], preferred_element_type=F32)

    q_tabs = [rope_ref[i] * (ATTN_SCALE * LOG2E) for i in range(3)]
    q = proj(OFF_Q, D_ATTN)
    for t in range(D_ATTN // LANES):
        sl = slice(t * LANES, (t + 1) * LANES)
        q_scr[:, sl] = _rope(q[:, sl], *q_tabs).astype(BF16)
    lane_lo_t = lax.broadcasted_iota(jnp.int32, (ts, LANES), 1) < HEAD_DIM
    k = proj(OFF_K, D_KV)
    v = proj(OFF_V, D_KV)
    for t in range(D_KV // LANES):
        sl = slice(t * LANES, (t + 1) * LANES)
        k_t = _rope(k[:, sl], rope_ref[0], rope_ref[1], rope_ref[2])
        for src, lo_scr, hi_scr in ((k_t, klo_scr, khi_scr), (v[:, sl], vlo_scr, vhi_scr)):
            for hd, (lo, hi) in zip((2 * t, 2 * t + 1), _split_head_pair(src, lane_lo_t)):
                hsl = slice(hd * LANES, (hd + 1) * LANES)
                lo_scr[BLOCK:BLOCK + ts, hsl] = lo.astype(BF16)
                hi_scr[BLOCK:BLOCK + ts, hsl] = hi.astype(BF16)

    u = proj(OFF_CC, D_MODEL) * proj(OFF_CX, D_MODEL)
    u_scr[SUBLANES:SUBLANES + ts, :] = u
    conv = (convw_ref[0:1, :] * u_scr[SUBLANES - 2:SUBLANES - 2 + ts, :]
            + convw_ref[1:2, :] * u_scr[SUBLANES - 1:SUBLANES - 1 + ts, :]
            + convw_ref[2:3, :] * u)
    u_scr[0:SUBLANES, :] = u_scr[ts:ts + SUBLANES, :]

    n_chunks = D_MODEL // MXU_TILE
    conv_y_chunks, gate_c_chunks, merged_chunks, gate_a_chunks = [], [], [], []

    def chunk(c):
        return slice(c * MXU_TILE, (c + 1) * MXU_TILE)

    def conv_gate_chunk(c):
        conv_y_chunks.append(
            (proj(OFF_CB + c * MXU_TILE, MXU_TILE) * conv[:, chunk(c)]).astype(BF16))

    def merge_gate_chunk(c):
        gate_c_chunks.append(_sigmoid(proj(OFF_GC + c * MXU_TILE, MXU_TILE)))

    def conv_out_chunk(c):
        conv_y = jnp.concatenate(conv_y_chunks, axis=1)
        merged_chunks.append(gate_c_chunks[c] * jnp.dot(
            conv_y, w_co_ref[:, chunk(c)], preferred_element_type=F32))

    def attn_gate_chunk(c):
        gate_a_chunks.append(_sigmoid(proj(OFF_GA + c * MXU_TILE, MXU_TILE)))

    fillers = [functools.partial(f, c)
               for f in (conv_gate_chunk, merge_gate_chunk, conv_out_chunk, attn_gate_chunk)
               for c in range(n_chunks)]

    lane = lax.broadcasted_iota(jnp.int32, (BLOCK, LANES), 1)
    rowi = lax.broadcasted_iota(jnp.int32, (BLOCK, LANES), 0)
    from_cur = lane <= rowi
    lane_lo = lane < HEAD_DIM
    lane2 = lax.broadcasted_iota(jnp.int32, (2 * BLOCK, LANES), 1)
    ones_lo = jnp.where(lane2 < HEAD_DIM, 1.0, 0.0).astype(BF16)
    ones_hi = jnp.where(lane2 < HEAD_DIM, 0.0, 1.0).astype(BF16)
    nt_dims = (((1,), (1,)), ((), ()))

    units = [(j, kv) for j in range(n_blk) for kv in range(N_KV_HEADS)]

    def q_slices(kv):
        return [slice((2 * kv + pr) * LANES, (2 * kv + pr + 1) * LANES) for pr in range(2)]

    def scores(j, kv):
        r0 = j * BLOCK
        kv_sl = slice(kv * LANES, (kv + 1) * LANES)
        q2 = jnp.concatenate([q_scr[r0:r0 + BLOCK, sl] for sl in q_slices(kv)], axis=0)
        k_rhs = jnp.concatenate([klo_scr[r0:r0 + 2 * BLOCK, kv_sl],
                                 khi_scr[r0:r0 + 2 * BLOCK, kv_sl]], axis=0)
        return lax.dot_general(q2, k_rhs, nt_dims, preferred_element_type=F32)

    s_queue = [scores(*u) for u in units[:LOOKAHEAD]]
    for n, (j, kv) in enumerate(units):
        r0 = j * BLOCK
        kv_sl = slice(kv * LANES, (kv + 1) * LANES)
        q_sl = q_slices(kv)
        s = s_queue.pop(0)
        if n + LOOKAHEAD < len(units):
            s_queue.append(scores(*units[n + LOOKAHEAD]))
        if fillers:
            fillers.pop(0)()
        v_rhs = jnp.concatenate(
            [jnp.concatenate([vlo_scr[r0:r0 + 2 * BLOCK, kv_sl], ones_lo], axis=1),
             jnp.concatenate([vhi_scr[r0:r0 + 2 * BLOCK, kv_sl], ones_hi], axis=1)], axis=0)
        p_rows, sink_terms = [], []
        for pr in range(2):
            rows = slice(pr * BLOCK, (pr + 1) * BLOCK)
            p_tiles, e_sink = [], []
            for half in range(2):
                hd = kv * GROUP + 2 * pr + half
                s_prev = s[rows, (2 * half) * BLOCK:(2 * half + 1) * BLOCK]
                s_cur = s[rows, (2 * half + 1) * BLOCK:(2 * half + 2) * BLOCK]
                if j == 0:
                    s_prev = jnp.where(s_idx > 0, s_prev, NEG_INF)
                t = jnp.where(from_cur, s_cur, s_prev)
                m = jnp.max(t, axis=-1, keepdims=True)
                p = jnp.exp2(t - m).astype(BF16)
                zero = jnp.zeros_like(p)
                p_tiles += [jnp.where(from_cur, zero, p), jnp.where(from_cur, p, zero)]
                e_sink.append(jnp.exp2(sinks_ref[hd] * LOG2E - m))
            p_rows.append(jnp.concatenate(p_tiles, axis=1))
            sink_terms.append(jnp.where(lane_lo, e_sink[0], e_sink[1]))
        o = jnp.dot(jnp.concatenate(p_rows, axis=0), v_rhs, preferred_element_type=F32)
        for pr in range(2):
            rows = slice(pr * BLOCK, (pr + 1) * BLOCK)
            den = o[rows, LANES:2 * LANES] + sink_terms[pr]
            attn_scr[r0:r0 + BLOCK, q_sl[pr]] = (o[rows, 0:LANES] / den).astype(BF16)

    for scr in kv_scrs:
        scr[0:BLOCK, :] = scr[ts:ts + BLOCK, :]

    for f in fillers:
        f()
    attn_out = jnp.dot(attn_scr[...], w_ao_ref[...], preferred_element_type=F32)
    merged = (jnp.concatenate(merged_chunks, axis=1)
              + jnp.concatenate(gate_a_chunks, axis=1) * attn_out)
    out_ref[...] = x + jnp.dot(merged.astype(BF16), w_o_ref[...], preferred_element_type=F32)


def _ffn_kernel(x_ref, g_ref, gf_ref, w_gu_ref, w_d_ref, out_ref, act_scr, *, final_norm):
    x = x_ref[...]
    h = _rms_norm(x, g_ref[...]).astype(BF16)
    for off, width in FF_CHUNKS:
        gate = jnp.dot(h, w_gu_ref[:, off:off + width], preferred_element_type=F32)
        up = jnp.dot(h, w_gu_ref[:, D_FF + off:D_FF + off + width], preferred_element_type=F32)
        act_scr[:, off:off + width] = (gate * _sigmoid(gate) * up).astype(BF16)
    y = x + jnp.dot(act_scr[...], w_d_ref[...], preferred_element_type=F32)
    if final_norm:
        y = _rms_norm(y, gf_ref[...])
    out_ref[...] = y


def _resident(shape):
    return pl.BlockSpec(shape, lambda b, s: (0,) * len(shape), pipeline_mode=pl.Buffered(1))


def _rope_tables(seq):
    half = ROT_DIM // 2
    d = jnp.arange(LANES) % HEAD_DIM
    inv_freq = ROPE_THETA ** (-(2 * (d % half)).astype(F32) / ROT_DIM)
    ang = jnp.arange(seq, dtype=F32)[:, None] * inv_freq[None, :]
    cos, sin = jnp.cos(ang), jnp.sin(ang)
    c = jnp.where(d < ROT_DIM, cos, 1.0)
    s_next = jnp.where(d < half, -sin, 0.0)
    s_prev = jnp.where((d >= half) & (d < ROT_DIM), sin, 0.0)
    kind = jnp.arange(3)[:, None, None]
    return jnp.where(kind == 0, c, jnp.where(kind == 1, s_next, s_prev))


def _mixer(x, g, rope, conv_w, sinks, layer, w_in, w_co, w_ao, w_o, w_gu, w_d):
    b, s, d = x.shape
    ts = SEQ_TILE
    n_s = s // ts
    gu_rows = w_gu.shape[0] // (b * n_s)
    d_rows = 2 * w_d.shape[0] // (b * n_s)
    assert gu_rows % BF16_SUBLANES == 0 and d_rows % BF16_SUBLANES == 0
    gu_slab = pl.BlockSpec((gu_rows, w_gu.shape[1]), lambda i, j: (i * n_s + j, 0))
    d_slab = pl.BlockSpec((d_rows, w_d.shape[1]), lambda i, j: ((i * n_s + j) // 2, 0))
    tile = pl.BlockSpec((None, ts, d), lambda i, j: (i, j, 0))
    kv_scratch = pltpu.VMEM((BLOCK + ts, N_KV_HEADS * LANES), BF16)
    return pl.pallas_call(
        _mixer_kernel,
        grid=(b, n_s),
        in_specs=[
            pl.BlockSpec(memory_space=pltpu.SMEM),
            tile,
            _resident((1, d)),
            pl.BlockSpec((3, ts, LANES), lambda i, j: (0, j, 0)),
            pl.BlockSpec((None, CONV_K, d), lambda i, j: (layer, 0, 0),
                         pipeline_mode=pl.Buffered(1)),
            pl.BlockSpec(memory_space=pl.ANY),
            pl.BlockSpec(memory_space=pl.ANY),
            pl.BlockSpec(memory_space=pl.ANY),
            pl.BlockSpec(memory_space=pl.ANY),
            gu_slab,
            d_slab,
        ],
        out_specs=[tile, gu_slab, d_slab],
        out_shape=[jax.ShapeDtypeStruct(x.shape, x.dtype),
                   jax.ShapeDtypeStruct(w_gu.shape, BF16),
                   jax.ShapeDtypeStruct(w_d.shape, BF16)],
        scratch_shapes=[
            pltpu.VMEM(w_in.shape, BF16),
            pltpu.VMEM(w_co.shape, BF16),
            pltpu.VMEM(w_ao.shape, BF16),
            pltpu.VMEM(w_o.shape, BF16),
            pltpu.VMEM((STAGE_SLOTS, d, STAGE_COLS), F32),
            pltpu.SemaphoreType.DMA((STAGE_SLOTS,)),
            pltpu.VMEM((SUBLANES + ts, d), F32),
            pltpu.VMEM((ts, D_ATTN), BF16),
            kv_scratch, kv_scratch, kv_scratch, kv_scratch,
            pltpu.VMEM((ts, D_ATTN), BF16),
        ],
        compiler_params=pltpu.CompilerParams(
            dimension_semantics=("arbitrary", "arbitrary"),
            vmem_limit_bytes=VMEM_LIMIT_BYTES),
        name="mixer",
    )(sinks, x, g, rope, conv_w, w_in, w_co, w_ao, w_o, w_gu, w_d)


def _ffn(x, g, g_final, w_gu, w_d, final_norm):
    b, s, d = x.shape
    ts = FFN_SEQ_TILE
    tile = pl.BlockSpec((None, ts, d), lambda i, j: (i, j, 0))
    return pl.pallas_call(
        functools.partial(_ffn_kernel, final_norm=final_norm),
        grid=(b, s // ts),
        in_specs=[tile, _resident((1, d)), _resident((1, d)),
                  _resident(w_gu.shape), _resident(w_d.shape)],
        out_specs=tile,
        out_shape=jax.ShapeDtypeStruct(x.shape, x.dtype),
        scratch_shapes=[pltpu.VMEM((ts, D_FF), BF16)],
        compiler_params=pltpu.CompilerParams(
            dimension_semantics=("arbitrary", "arbitrary"),
            vmem_limit_bytes=VMEM_LIMIT_BYTES),
        name="ffn",
    )(x, g, g_final, w_gu, w_d)


def kernel(x, g_mix, w_in, conv_w, attn_sinks, w_conv_out, w_attn_out, w_o,
           g_ffn, w_gate_up, w_down, g_final):
    b, s, d = x.shape
    depth = w_in.shape[0]
    assert d == D_MODEL and s % SEQ_TILE == 0 and s % FFN_SEQ_TILE == 0
    assert w_in.shape[-1] == N_IN and w_gate_up.shape[-1] == 2 * D_FF
    assert N_IN % STAGE_COLS == 0 and d % STAGE_COLS == 0
    rope = _rope_tables(s)
    g_fin = g_final.reshape(1, d)
    for l in range(depth):
        x, w_gu, w_d = _mixer(
            x, g_mix[l].reshape(1, d), rope, conv_w, attn_sinks[l], l,
            w_in[l], w_conv_out[l], w_attn_out[l], w_o[l],
            w_gate_up[l], w_down[l])
        x = _ffn(x, g_ffn[l].reshape(1, d), g_fin, w_gu, w_d, final_norm=(l == depth - 1))
    return x
```

```python
import functools
import math

import jax
import jax.numpy as jnp
from jax import lax
from jax.experimental import pallas as pl
from jax.experimental.pallas import tpu as pltpu

D_MODEL = 1024
CONV_K = 3
HEAD_DIM = 64
N_HEADS = 16
N_KV_HEADS = 4
GROUP = N_HEADS // N_KV_HEADS
D_ATTN = N_HEADS * HEAD_DIM
D_KV = N_KV_HEADS * HEAD_DIM
WINDOW = 128
BLOCK = 128
ROT_DIM = HEAD_DIM // 4
ROPE_THETA = 500000.0
ATTN_SCALE = 1.0 / math.sqrt(HEAD_DIM)
LOG2E = math.log2(math.e)
NEG_INF = -1e30
D_FF = 2816
EPS = 1e-5

OFF_CB = 0
OFF_CC = OFF_CB + D_MODEL
OFF_CX = OFF_CC + D_MODEL
OFF_Q = OFF_CX + D_MODEL
OFF_K = OFF_Q + D_ATTN
OFF_V = OFF_K + D_KV
OFF_GC = OFF_V + D_KV
OFF_GA = OFF_GC + D_MODEL
N_IN = OFF_GA + D_MODEL

LANES = 128
SUBLANES = 8
BF16_SUBLANES = 16
SEQ_TILE = 512
FFN_SEQ_TILE = 1024
LOOKAHEAD = 2
STAGE_COLS = 512
STAGE_SLOTS = 4
FF_CHUNKS = ((0, 1024), (1024, 1024), (2048, 768))
VMEM_LIMIT_BYTES = 58 * 1024 * 1024

F32 = jnp.float32
BF16 = jnp.bfloat16

assert WINDOW == BLOCK and 2 * HEAD_DIM == LANES and GROUP == 4


def _rms_norm(x, g):
    ms = jnp.mean(x * x, axis=-1, keepdims=True)
    return x * lax.rsqrt(ms + EPS) * g


def _sigmoid(x):
    return 1.0 / (1.0 + jnp.exp(-x))


def _rope(t, cos, sin_next, sin_prev):
    nxt = pltpu.roll(t, LANES - ROT_DIM // 2, 1)
    prv = pltpu.roll(t, ROT_DIM // 2, 1)
    return t * cos + nxt * sin_next + prv * sin_prev


def _split_head_pair(t, lane_lo):
    sw = pltpu.roll(t, HEAD_DIM, 1)
    zero = jnp.zeros_like(t)
    even = (jnp.where(lane_lo, t, zero), jnp.where(lane_lo, zero, sw))
    odd = (jnp.where(lane_lo, sw, zero), jnp.where(lane_lo, zero, t))
    return even, odd


def _stage_weights(pairs, stage_ref, sem_ref):
    slots = stage_ref.shape[0]
    jobs = [(src, dst, c * STAGE_COLS)
            for src, dst in pairs for c in range(src.shape[1] // STAGE_COLS)]

    def slab_copy(n):
        src, _, col = jobs[n]
        return pltpu.make_async_copy(
            src.at[:, pl.ds(col, STAGE_COLS)], stage_ref.at[n % slots], sem_ref.at[n % slots])

    for n in range(min(slots - 1, len(jobs))):
        slab_copy(n).start()
    for n, (_, dst, col) in enumerate(jobs):
        if n + slots - 1 < len(jobs):
            slab_copy(n + slots - 1).start()
        slab_copy(n).wait()
        dst[:, col:col + STAGE_COLS] = stage_ref[n % slots].astype(BF16)


def _mixer_kernel(sinks_ref, x_ref, g_ref, rope_ref, convw_ref, w_in_hbm,
                  w_co_hbm, w_ao_hbm, w_o_hbm, w_gu_f32_ref, w_d_f32_ref,
                  out_ref, w_gu_bf16_ref, w_d_bf16_ref,
                  w_in_ref, w_co_ref, w_ao_ref, w_o_ref, stage_ref, stage_sem,
                  u_scr, q_scr, klo_scr, khi_scr, vlo_scr, vhi_scr, attn_scr):
    ts = x_ref.shape[0]
    n_blk = ts // BLOCK
    s_idx = pl.program_id(1)
    kv_scrs = (klo_scr, khi_scr, vlo_scr, vhi_scr)

    @pl.when((pl.program_id(0) == 0) & (s_idx == 0))
    def _():
        _stage_weights(((w_in_hbm, w_in_ref), (w_co_hbm, w_co_ref),
                        (w_ao_hbm, w_ao_ref), (w_o_hbm, w_o_ref)), stage_ref, stage_sem)

    @pl.when(s_idx == 0)
    def _():
        u_scr[0:SUBLANES, :] = jnp.zeros((SUBLANES, D_MODEL), F32)
        for scr in kv_scrs:
            scr[0:BLOCK, :] = jnp.zeros((BLOCK, N_KV_HEADS * LANES), BF16)

    w_gu_bf16_ref[...] = w_gu_f32_ref[...].astype(BF16)
    w_d_bf16_ref[...] = w_d_f32_ref[...].astype(BF16)

    x = x_ref[...]
    h = _rms_norm(x, g_ref[...]).astype(BF16)

    def proj(off, width):
        return jnp.dot(h, w_in_ref[:, off:off + width], preferred_element_type=F32)

    q_tabs = [rope_ref[i] * (ATTN_SCALE * LOG2E) for i in range(3)]
    q = proj(OFF_Q, D_ATTN)
    for t in range(D_ATTN // LANES):
        sl = slice(t * LANES, (t + 1) * LANES)
        q_scr[:, sl] = _rope(q[:, sl], *q_tabs).astype(BF16)
    lane_lo_t = lax.broadcasted_iota(jnp.int32, (ts, LANES), 1) < HEAD_DIM
    k = proj(OFF_K, D_KV)
    v = proj(OFF_V, D_KV)
    for t in range(D_KV // LANES):
        sl = slice(t * LANES, (t + 1) * LANES)
        k_t = _rope(k[:, sl], rope_ref[0], rope_ref[1], rope_ref[2])
        for src, lo_scr, hi_scr in ((k_t, klo_scr, khi_scr), (v[:, sl], vlo_scr, vhi_scr)):
            for hd, (lo, hi) in zip((2 * t, 2 * t + 1), _split_head_pair(src, lane_lo_t)):
                hsl = slice(hd * LANES, (hd + 1) * LANES)
                lo_scr[BLOCK:BLOCK + ts, hsl] = lo.astype(BF16)
                hi_scr[BLOCK:BLOCK + ts, hsl] = hi.astype(BF16)

    u = proj(OFF_CC, D_MODEL) * proj(OFF_CX, D_MODEL)
    u_scr[SUBLANES:SUBLANES + ts, :] = u
    conv = (convw_ref[0:1, :] * u_scr[SUBLANES - 2:SUBLANES - 2 + ts, :]
            + convw_ref[1:2, :] * u_scr[SUBLANES - 1:SUBLANES - 1 + ts, :]
            + convw_ref[2:3, :] * u)
    u_scr[0:SUBLANES, :] = u_scr[ts:ts + SUBLANES, :]
    conv_y = (proj(OFF_CB, D_MODEL) * conv).astype(BF16)
    gate_c = _sigmoid(proj(OFF_GC, D_MODEL))
    merged = gate_c * jnp.dot(conv_y, w_co_ref[...], preferred_element_type=F32)

    lane = lax.broadcasted_iota(jnp.int32, (BLOCK, LANES), 1)
    rowi = lax.broadcasted_iota(jnp.int32, (BLOCK, LANES), 0)
    from_cur = lane <= rowi
    lane_lo = lane < HEAD_DIM
    lane2 = lax.broadcasted_iota(jnp.int32, (2 * BLOCK, LANES), 1)
    ones_lo = jnp.where(lane2 < HEAD_DIM, 1.0, 0.0).astype(BF16)
    ones_hi = jnp.where(lane2 < HEAD_DIM, 0.0, 1.0).astype(BF16)
    nt_dims = (((1,), (1,)), ((), ()))

    units = [(j, kv) for j in range(n_blk) for kv in range(N_KV_HEADS)]

    def q_slices(kv):
        return [slice((2 * kv + pr) * LANES, (2 * kv + pr + 1) * LANES) for pr in range(2)]

    def scores(j, kv):
        r0 = j * BLOCK
        kv_sl = slice(kv * LANES, (kv + 1) * LANES)
        q2 = jnp.concatenate([q_scr[r0:r0 + BLOCK, sl] for sl in q_slices(kv)], axis=0)
        k_rhs = jnp.concatenate([klo_scr[r0:r0 + 2 * BLOCK, kv_sl],
                                 khi_scr[r0:r0 + 2 * BLOCK, kv_sl]], axis=0)
        return lax.dot_general(q2, k_rhs, nt_dims, preferred_element_type=F32)

    s_queue = [scores(*u) for u in units[:LOOKAHEAD]]
    for n, (j, kv) in enumerate(units):
        r0 = j * BLOCK
        kv_sl = slice(kv * LANES, (kv + 1) * LANES)
        q_sl = q_slices(kv)
        s = s_queue.pop(0)
        if n + LOOKAHEAD < len(units):
            s_queue.append(scores(*units[n + LOOKAHEAD]))
        if n + 1 == len(units):
            gate_a_halves = [_sigmoid(proj(OFF_GA, D_MODEL // 2))]
        v_rhs = jnp.concatenate(
            [jnp.concatenate([vlo_scr[r0:r0 + 2 * BLOCK, kv_sl], ones_lo], axis=1),
             jnp.concatenate([vhi_scr[r0:r0 + 2 * BLOCK, kv_sl], ones_hi], axis=1)], axis=0)
        p_rows, sink_terms = [], []
        for pr in range(2):
            rows = slice(pr * BLOCK, (pr + 1) * BLOCK)
            p_tiles, e_sink = [], []
            for half in range(2):
                hd = kv * GROUP + 2 * pr + half
                s_prev = s[rows, (2 * half) * BLOCK:(2 * half + 1) * BLOCK]
                s_cur = s[rows, (2 * half + 1) * BLOCK:(2 * half + 2) * BLOCK]
                if j == 0:
                    s_prev = jnp.where(s_idx > 0, s_prev, NEG_INF)
                t = jnp.where(from_cur, s_cur, s_prev)
                m = jnp.max(t, axis=-1, keepdims=True)
                p = jnp.exp2(t - m).astype(BF16)
                zero = jnp.zeros_like(p)
                p_tiles += [jnp.where(from_cur, zero, p), jnp.where(from_cur, p, zero)]
                e_sink.append(jnp.exp2(sinks_ref[hd] * LOG2E - m))
            p_rows.append(jnp.concatenate(p_tiles, axis=1))
            sink_terms.append(jnp.where(lane_lo, e_sink[0], e_sink[1]))
        o = jnp.dot(jnp.concatenate(p_rows, axis=0), v_rhs, preferred_element_type=F32)
        if n + 1 == len(units):
            gate_a_halves.append(_sigmoid(proj(OFF_GA + D_MODEL // 2, D_MODEL // 2)))
        for pr in range(2):
            rows = slice(pr * BLOCK, (pr + 1) * BLOCK)
            den = o[rows, LANES:2 * LANES] + sink_terms[pr]
            attn_scr[r0:r0 + BLOCK, q_sl[pr]] = (o[rows, 0:LANES] / den).astype(BF16)

    for scr in kv_scrs:
        scr[0:BLOCK, :] = scr[ts:ts + BLOCK, :]

    attn_out = jnp.dot(attn_scr[...], w_ao_ref[...], preferred_element_type=F32)
    merged = merged + jnp.concatenate(gate_a_halves, axis=1) * attn_out
    out_ref[...] = x + jnp.dot(merged.astype(BF16), w_o_ref[...], preferred_element_type=F32)


def _ffn_kernel(x_ref, g_ref, gf_ref, w_gu_ref, w_d_ref, out_ref, act_scr, *, final_norm):
    x = x_ref[...]
    h = _rms_norm(x, g_ref[...]).astype(BF16)
    for off, width in FF_CHUNKS:
        gate = jnp.dot(h, w_gu_ref[:, off:off + width], preferred_element_type=F32)
        up = jnp.dot(h, w_gu_ref[:, D_FF + off:D_FF + off + width], preferred_element_type=F32)
        act_scr[:, off:off + width] = (gate * _sigmoid(gate) * up).astype(BF16)
    y = x + jnp.dot(act_scr[...], w_d_ref[...], preferred_element_type=F32)
    if final_norm:
        y = _rms_norm(y, gf_ref[...])
    out_ref[...] = y


def _resident(shape):
    return pl.BlockSpec(shape, lambda b, s: (0,) * len(shape), pipeline_mode=pl.Buffered(1))


def _rope_tables(seq):
    half = ROT_DIM // 2
    d = jnp.arange(LANES) % HEAD_DIM
    inv_freq = ROPE_THETA ** (-(2 * (d % half)).astype(F32) / ROT_DIM)
    ang = jnp.arange(seq, dtype=F32)[:, None] * inv_freq[None, :]
    cos, sin = jnp.cos(ang), jnp.sin(ang)
    c = jnp.where(d < ROT_DIM, cos, 1.0)
    s_next = jnp.where(d < half, -sin, 0.0)
    s_prev = jnp.where((d >= half) & (d < ROT_DIM), sin, 0.0)
    kind = jnp.arange(3)[:, None, None]
    return jnp.where(kind == 0, c, jnp.where(kind == 1, s_next, s_prev))


def _mixer(x, g, rope, conv_w, sinks, layer, w_in, w_co, w_ao, w_o, w_gu, w_d):
    b, s, d = x.shape
    ts = SEQ_TILE
    n_s = s // ts
    gu_rows = w_gu.shape[0] // (b * n_s)
    d_rows = 2 * w_d.shape[0] // (b * n_s)
    assert gu_rows % BF16_SUBLANES == 0 and d_rows % BF16_SUBLANES == 0
    gu_slab = pl.BlockSpec((gu_rows, w_gu.shape[1]), lambda i, j: (i * n_s + j, 0))
    d_slab = pl.BlockSpec((d_rows, w_d.shape[1]), lambda i, j: ((i * n_s + j) // 2, 0))
    tile = pl.BlockSpec((None, ts, d), lambda i, j: (i, j, 0))
    kv_scratch = pltpu.VMEM((BLOCK + ts, N_KV_HEADS * LANES), BF16)
    return pl.pallas_call(
        _mixer_kernel,
        grid=(b, n_s),
        in_specs=[
            pl.BlockSpec(memory_space=pltpu.SMEM),
            tile,
            _resident((1, d)),
            pl.BlockSpec((3, ts, LANES), lambda i, j: (0, j, 0)),
            pl.BlockSpec((None, CONV_K, d), lambda i, j: (layer, 0, 0),
                         pipeline_mode=pl.Buffered(1)),
            pl.BlockSpec(memory_space=pl.ANY),
            pl.BlockSpec(memory_space=pl.ANY),
            pl.BlockSpec(memory_space=pl.ANY),
            pl.BlockSpec(memory_space=pl.ANY),
            gu_slab,
            d_slab,
        ],
        out_specs=[tile, gu_slab, d_slab],
        out_shape=[jax.ShapeDtypeStruct(x.shape, x.dtype),
                   jax.ShapeDtypeStruct(w_gu.shape, BF16),
                   jax.ShapeDtypeStruct(w_d.shape, BF16)],
        scratch_shapes=[
            pltpu.VMEM(w_in.shape, BF16),
            pltpu.VMEM(w_co.shape, BF16),
            pltpu.VMEM(w_ao.shape, BF16),
            pltpu.VMEM(w_o.shape, BF16),
            pltpu.VMEM((STAGE_SLOTS, d, STAGE_COLS), F32),
            pltpu.SemaphoreType.DMA((STAGE_SLOTS,)),
            pltpu.VMEM((SUBLANES + ts, d), F32),
            pltpu.VMEM((ts, D_ATTN), BF16),
            kv_scratch, kv_scratch, kv_scratch, kv_scratch,
            pltpu.VMEM((ts, D_ATTN), BF16),
        ],
        compiler_params=pltpu.CompilerParams(
            dimension_semantics=("arbitrary", "arbitrary"),
            vmem_limit_bytes=VMEM_LIMIT_BYTES),
        name="mixer",
    )(sinks, x, g, rope, conv_w, w_in, w_co, w_ao, w_o, w_gu, w_d)


def _ffn(x, g, g_final, w_gu, w_d, final_norm):
    b, s, d = x.shape
    ts = FFN_SEQ_TILE
    tile = pl.BlockSpec((None, ts, d), lambda i, j: (i, j, 0))
    return pl.pallas_call(
        functools.partial(_ffn_kernel, final_norm=final_norm),
        grid=(b, s // ts),
        in_specs=[tile, _resident((1, d)), _resident((1, d)),
                  _resident(w_gu.shape), _resident(w_d.shape)],
        out_specs=tile,
        out_shape=jax.ShapeDtypeStruct(x.shape, x.dtype),
        scratch_shapes=[pltpu.VMEM((ts, D_FF), BF16)],
        compiler_params=pltpu.CompilerParams(
            dimension_semantics=("arbitrary", "arbitrary"),
            vmem_limit_bytes=VMEM_LIMIT_BYTES),
        name="ffn",
    )(x, g, g_final, w_gu, w_d)


def kernel(x, g_mix, w_in, conv_w, attn_sinks, w_conv_out, w_attn_out, w_o,
           g_ffn, w_gate_up, w_down, g_final):
    b, s, d = x.shape
    depth = w_in.shape[0]
    assert d == D_MODEL and s % SEQ_TILE == 0 and s % FFN_SEQ_TILE == 0
    assert w_in.shape[-1] == N_IN and w_gate_up.shape[-1] == 2 * D_FF
    assert N_IN % STAGE_COLS == 0 and d % STAGE_COLS == 0
    rope = _rope_tables(s)
    g_fin = g_final.reshape(1, d)
    for l in range(depth):
        x, w_gu, w_d = _mixer(
            x, g_mix[l].reshape(1, d), rope, conv_w, attn_sinks[l], l,
            w_in[l], w_conv_out[l], w_attn_out[l], w_o[l],
            w_gate_up[l], w_down[l])
        x = _ffn(x, g_ffn[l].reshape(1, d), g_fin, w_gu, w_d, final_norm=(l == depth - 1))
    return x
```

```python
import functools
import math

import jax
import jax.numpy as jnp
from jax import lax
from jax.experimental import pallas as pl
from jax.experimental.pallas import tpu as pltpu

D_MODEL = 1024
CONV_K = 3
HEAD_DIM = 64
N_HEADS = 16
N_KV_HEADS = 4
GROUP = N_HEADS // N_KV_HEADS
D_ATTN = N_HEADS * HEAD_DIM
D_KV = N_KV_HEADS * HEAD_DIM
WINDOW = 128
BLOCK = 128
ROT_DIM = HEAD_DIM // 4
ROPE_THETA = 500000.0
ATTN_SCALE = 1.0 / math.sqrt(HEAD_DIM)
LOG2E = math.log2(math.e)
NEG_INF = -1e30
D_FF = 2816
EPS = 1e-5

OFF_CB = 0
OFF_CC = OFF_CB + D_MODEL
OFF_CX = OFF_CC + D_MODEL
OFF_Q = OFF_CX + D_MODEL
OFF_K = OFF_Q + D_ATTN
OFF_V = OFF_K + D_KV
OFF_GC = OFF_V + D_KV
OFF_GA = OFF_GC + D_MODEL
N_IN = OFF_GA + D_MODEL

LANES = 128
SUBLANES = 8
BF16_SUBLANES = 16
SEQ_TILE = 512
FFN_SEQ_TILE = 1024
LOOKAHEAD = 2
STAGE_COLS = 512
STAGE_SLOTS = 4
FF_CHUNKS = ((0, 1024), (1024, 1024), (2048, 768))
VMEM_LIMIT_BYTES = 58 * 1024 * 1024

F32 = jnp.float32
BF16 = jnp.bfloat16

assert WINDOW == BLOCK and 2 * HEAD_DIM == LANES and GROUP == 4


def _rms_norm(x, g):
    ms = jnp.mean(x * x, axis=-1, keepdims=True)
    return x * lax.rsqrt(ms + EPS) * g


def _sigmoid(x):
    return 1.0 / (1.0 + jnp.exp(-x))


def _rope(t, cos, sin_next, sin_prev):
    nxt = pltpu.roll(t, LANES - ROT_DIM // 2, 1)
    prv = pltpu.roll(t, ROT_DIM // 2, 1)
    return t * cos + nxt * sin_next + prv * sin_prev


def _split_head_pair(t, lane_lo):
    sw = pltpu.roll(t, HEAD_DIM, 1)
    zero = jnp.zeros_like(t)
    even = (jnp.where(lane_lo, t, zero), jnp.where(lane_lo, zero, sw))
    odd = (jnp.where(lane_lo, sw, zero), jnp.where(lane_lo, zero, t))
    return even, odd


def _stage_weights(pairs, stage_ref, sem_ref):
    slots = stage_ref.shape[0]
    jobs = [(src, dst, c * STAGE_COLS)
            for src, dst in pairs for c in range(src.shape[1] // STAGE_COLS)]

    def slab_copy(n):
        src, _, col = jobs[n]
        return pltpu.make_async_copy(
            src.at[:, pl.ds(col, STAGE_COLS)], stage_ref.at[n % slots], sem_ref.at[n % slots])

    for n in range(min(slots - 1, len(jobs))):
        slab_copy(n).start()
    for n, (_, dst, col) in enumerate(jobs):
        if n + slots - 1 < len(jobs):
            slab_copy(n + slots - 1).start()
        slab_copy(n).wait()
        dst[:, col:col + STAGE_COLS] = stage_ref[n % slots].astype(BF16)


def _mixer_kernel(sinks_ref, x_ref, x_next_ref, g_ref, rope_ref, convw_ref, w_in_hbm,
                  w_co_hbm, w_ao_hbm, w_o_hbm, w_gu_f32_ref, w_d_f32_ref,
                  out_ref, w_gu_bf16_ref, w_d_bf16_ref,
                  w_in_ref, w_co_ref, w_ao_ref, w_o_ref, stage_ref, stage_sem, h_scr,
                  u_scr, q_scr, klo_scr, khi_scr, vlo_scr, vhi_scr, attn_scr):
    ts = x_ref.shape[0]
    n_blk = ts // BLOCK
    s_idx = pl.program_id(1)
    kv_scrs = (klo_scr, khi_scr, vlo_scr, vhi_scr)

    @pl.when((pl.program_id(0) == 0) & (s_idx == 0))
    def _():
        _stage_weights(((w_in_hbm, w_in_ref), (w_co_hbm, w_co_ref),
                        (w_ao_hbm, w_ao_ref), (w_o_hbm, w_o_ref)), stage_ref, stage_sem)
        h_scr[...] = _rms_norm(x_ref[...], g_ref[...]).astype(BF16)

    @pl.when(s_idx == 0)
    def _():
        u_scr[0:SUBLANES, :] = jnp.zeros((SUBLANES, D_MODEL), F32)
        for scr in kv_scrs:
            scr[0:BLOCK, :] = jnp.zeros((BLOCK, N_KV_HEADS * LANES), BF16)

    w_gu_bf16_ref[...] = w_gu_f32_ref[...].astype(BF16)
    w_d_bf16_ref[...] = w_d_f32_ref[...].astype(BF16)

    h = h_scr[...]

    def proj(off, width):
        return jnp.dot(h, w_in_ref[:, off:off + width], preferred_element_type=F32)

    q_tabs = [rope_ref[i] * (ATTN_SCALE * LOG2E) for i in range(3)]
    q = proj(OFF_Q, D_ATTN)
    for t in range(D_ATTN // LANES):
        sl = slice(t * LANES, (t + 1) * LANES)
        q_scr[:, sl] = _rope(q[:, sl], *q_tabs).astype(BF16)
    lane_lo_t = lax.broadcasted_iota(jnp.int32, (ts, LANES), 1) < HEAD_DIM
    k = proj(OFF_K, D_KV)
    v = proj(OFF_V, D_KV)
    for t in range(D_KV // LANES):
        sl = slice(t * LANES, (t + 1) * LANES)
        k_t = _rope(k[:, sl], rope_ref[0], rope_ref[1], rope_ref[2])
        for src, lo_scr, hi_scr in ((k_t, klo_scr, khi_scr), (v[:, sl], vlo_scr, vhi_scr)):
            for hd, (lo, hi) in zip((2 * t, 2 * t + 1), _split_head_pair(src, lane_lo_t)):
                hsl = slice(hd * LANES, (hd + 1) * LANES)
                lo_scr[BLOCK:BLOCK + ts, hsl] = lo.astype(BF16)
                hi_scr[BLOCK:BLOCK + ts, hsl] = hi.astype(BF16)

    u = proj(OFF_CC, D_MODEL) * proj(OFF_CX, D_MODEL)
    u_scr[SUBLANES:SUBLANES + ts, :] = u
    conv = (convw_ref[0:1, :] * u_scr[SUBLANES - 2:SUBLANES - 2 + ts, :]
            + convw_ref[1:2, :] * u_scr[SUBLANES - 1:SUBLANES - 1 + ts, :]
            + convw_ref[2:3, :] * u)
    u_scr[0:SUBLANES, :] = u_scr[ts:ts + SUBLANES, :]
    conv_y = (proj(OFF_CB, D_MODEL) * conv).astype(BF16)
    gate_c = _sigmoid(proj(OFF_GC, D_MODEL))
    merged = gate_c * jnp.dot(conv_y, w_co_ref[...], preferred_element_type=F32)

    lane = lax.broadcasted_iota(jnp.int32, (BLOCK, LANES), 1)
    rowi = lax.broadcasted_iota(jnp.int32, (BLOCK, LANES), 0)
    from_cur = lane <= rowi
    lane_lo = lane < HEAD_DIM
    lane2 = lax.broadcasted_iota(jnp.int32, (2 * BLOCK, LANES), 1)
    ones_lo = jnp.where(lane2 < HEAD_DIM, 1.0, 0.0).astype(BF16)
    ones_hi = jnp.where(lane2 < HEAD_DIM, 0.0, 1.0).astype(BF16)
    nt_dims = (((1,), (1,)), ((), ()))

    units = [(j, kv) for j in range(n_blk) for kv in range(N_KV_HEADS)]

    def q_slices(kv):
        return [slice((2 * kv + pr) * LANES, (2 * kv + pr + 1) * LANES) for pr in range(2)]

    def scores(j, kv):
        r0 = j * BLOCK
        kv_sl = slice(kv * LANES, (kv + 1) * LANES)
        q2 = jnp.concatenate([q_scr[r0:r0 + BLOCK, sl] for sl in q_slices(kv)], axis=0)
        k_rhs = jnp.concatenate([klo_scr[r0:r0 + 2 * BLOCK, kv_sl],
                                 khi_scr[r0:r0 + 2 * BLOCK, kv_sl]], axis=0)
        return lax.dot_general(q2, k_rhs, nt_dims, preferred_element_type=F32)

    s_queue = [scores(*u) for u in units[:LOOKAHEAD]]
    for n, (j, kv) in enumerate(units):
        r0 = j * BLOCK
        kv_sl = slice(kv * LANES, (kv + 1) * LANES)
        q_sl = q_slices(kv)
        s = s_queue.pop(0)
        if n + LOOKAHEAD < len(units):
            s_queue.append(scores(*units[n + LOOKAHEAD]))
        if n + 1 == len(units):
            gate_a_halves = [_sigmoid(proj(OFF_GA, D_MODEL // 2))]
        v_rhs = jnp.concatenate(
            [jnp.concatenate([vlo_scr[r0:r0 + 2 * BLOCK, kv_sl], ones_lo], axis=1),
             jnp.concatenate([vhi_scr[r0:r0 + 2 * BLOCK, kv_sl], ones_hi], axis=1)], axis=0)
        p_rows, sink_terms = [], []
        for pr in range(2):
            rows = slice(pr * BLOCK, (pr + 1) * BLOCK)
            p_tiles, e_sink = [], []
            for half in range(2):
                hd = kv * GROUP + 2 * pr + half
                s_prev = s[rows, (2 * half) * BLOCK:(2 * half + 1) * BLOCK]
                s_cur = s[rows, (2 * half + 1) * BLOCK:(2 * half + 2) * BLOCK]
                if j == 0:
                    s_prev = jnp.where(s_idx > 0, s_prev, NEG_INF)
                t = jnp.where(from_cur, s_cur, s_prev)
                m = jnp.max(t, axis=-1, keepdims=True)
                p = jnp.exp2(t - m).astype(BF16)
                zero = jnp.zeros_like(p)
                p_tiles += [jnp.where(from_cur, zero, p), jnp.where(from_cur, p, zero)]
                e_sink.append(jnp.exp2(sinks_ref[hd] * LOG2E - m))
            p_rows.append(jnp.concatenate(p_tiles, axis=1))
            sink_terms.append(jnp.where(lane_lo, e_sink[0], e_sink[1]))
        o = jnp.dot(jnp.concatenate(p_rows, axis=0), v_rhs, preferred_element_type=F32)
        if n + 1 == len(units):
            gate_a_halves.append(_sigmoid(proj(OFF_GA + D_MODEL // 2, D_MODEL // 2)))
        for pr in range(2):
            rows = slice(pr * BLOCK, (pr + 1) * BLOCK)
            den = o[rows, LANES:2 * LANES] + sink_terms[pr]
            attn_scr[r0:r0 + BLOCK, q_sl[pr]] = (o[rows, 0:LANES] / den).astype(BF16)

    for scr in kv_scrs:
        scr[0:BLOCK, :] = scr[ts:ts + BLOCK, :]

    h_scr[...] = _rms_norm(x_next_ref[...], g_ref[...]).astype(BF16)
    attn_out = jnp.dot(attn_scr[...], w_ao_ref[...], preferred_element_type=F32)
    merged = merged + jnp.concatenate(gate_a_halves, axis=1) * attn_out
    out_ref[...] = x_ref[...] + jnp.dot(
        merged.astype(BF16), w_o_ref[...], preferred_element_type=F32)


def _ffn_kernel(x_ref, x_next_ref, g_ref, gf_ref, w_gu_ref, w_d_ref, out_ref,
                h_scr, act_scr, *, final_norm):
    @pl.when((pl.program_id(0) == 0) & (pl.program_id(1) == 0))
    def _():
        h_scr[...] = _rms_norm(x_ref[...], g_ref[...]).astype(BF16)

    h = h_scr[...]
    for off, width in FF_CHUNKS:
        gate = jnp.dot(h, w_gu_ref[:, off:off + width], preferred_element_type=F32)
        up = jnp.dot(h, w_gu_ref[:, D_FF + off:D_FF + off + width], preferred_element_type=F32)
        act_scr[:, off:off + width] = (gate * _sigmoid(gate) * up).astype(BF16)
    h_scr[...] = _rms_norm(x_next_ref[...], g_ref[...]).astype(BF16)
    y = x_ref[...] + jnp.dot(act_scr[...], w_d_ref[...], preferred_element_type=F32)
    if final_norm:
        y = _rms_norm(y, gf_ref[...])
    out_ref[...] = y


def _resident(shape):
    return pl.BlockSpec(shape, lambda b, s: (0,) * len(shape), pipeline_mode=pl.Buffered(1))


def _next_tile_spec(n_batch, n_seq_tiles, ts, d):
    def index_map(i, j):
        nxt = jnp.minimum(i * n_seq_tiles + j + 1, n_batch * n_seq_tiles - 1)
        return nxt // n_seq_tiles, nxt % n_seq_tiles, 0
    return pl.BlockSpec((None, ts, d), index_map)


def _rope_tables(seq):
    half = ROT_DIM // 2
    d = jnp.arange(LANES) % HEAD_DIM
    inv_freq = ROPE_THETA ** (-(2 * (d % half)).astype(F32) / ROT_DIM)
    ang = jnp.arange(seq, dtype=F32)[:, None] * inv_freq[None, :]
    cos, sin = jnp.cos(ang), jnp.sin(ang)
    c = jnp.where(d < ROT_DIM, cos, 1.0)
    s_next = jnp.where(d < half, -sin, 0.0)
    s_prev = jnp.where((d >= half) & (d < ROT_DIM), sin, 0.0)
    kind = jnp.arange(3)[:, None, None]
    return jnp.where(kind == 0, c, jnp.where(kind == 1, s_next, s_prev))


def _mixer(x, g, rope, conv_w, sinks, layer, w_in, w_co, w_ao, w_o, w_gu, w_d):
    b, s, d = x.shape
    ts = SEQ_TILE
    n_s = s // ts
    gu_rows = w_gu.shape[0] // (b * n_s)
    d_rows = 2 * w_d.shape[0] // (b * n_s)
    assert gu_rows % BF16_SUBLANES == 0 and d_rows % BF16_SUBLANES == 0
    gu_slab = pl.BlockSpec((gu_rows, w_gu.shape[1]), lambda i, j: (i * n_s + j, 0))
    d_slab = pl.BlockSpec((d_rows, w_d.shape[1]), lambda i, j: ((i * n_s + j) // 2, 0))
    tile = pl.BlockSpec((None, ts, d), lambda i, j: (i, j, 0))
    kv_scratch = pltpu.VMEM((BLOCK + ts, N_KV_HEADS * LANES), BF16)
    return pl.pallas_call(
        _mixer_kernel,
        grid=(b, n_s),
        in_specs=[
            pl.BlockSpec(memory_space=pltpu.SMEM),
            tile,
            _next_tile_spec(b, n_s, ts, d),
            _resident((1, d)),
            pl.BlockSpec((3, ts, LANES), lambda i, j: (0, j, 0)),
            pl.BlockSpec((None, CONV_K, d), lambda i, j: (layer, 0, 0),
                         pipeline_mode=pl.Buffered(1)),
            pl.BlockSpec(memory_space=pl.ANY),
            pl.BlockSpec(memory_space=pl.ANY),
            pl.BlockSpec(memory_space=pl.ANY),
            pl.BlockSpec(memory_space=pl.ANY),
            gu_slab,
            d_slab,
        ],
        out_specs=[tile, gu_slab, d_slab],
        out_shape=[jax.ShapeDtypeStruct(x.shape, x.dtype),
                   jax.ShapeDtypeStruct(w_gu.shape, BF16),
                   jax.ShapeDtypeStruct(w_d.shape, BF16)],
        scratch_shapes=[
            pltpu.VMEM(w_in.shape, BF16),
            pltpu.VMEM(w_co.shape, BF16),
            pltpu.VMEM(w_ao.shape, BF16),
            pltpu.VMEM(w_o.shape, BF16),
            pltpu.VMEM((STAGE_SLOTS, d, STAGE_COLS), F32),
            pltpu.SemaphoreType.DMA((STAGE_SLOTS,)),
            pltpu.VMEM((ts, d), BF16),
            pltpu.VMEM((SUBLANES + ts, d), F32),
            pltpu.VMEM((ts, D_ATTN), BF16),
            kv_scratch, kv_scratch, kv_scratch, kv_scratch,
            pltpu.VMEM((ts, D_ATTN), BF16),
        ],
        compiler_params=pltpu.CompilerParams(
            dimension_semantics=("arbitrary", "arbitrary"),
            vmem_limit_bytes=VMEM_LIMIT_BYTES),
        name="mixer",
    )(sinks, x, x, g, rope, conv_w, w_in, w_co, w_ao, w_o, w_gu, w_d)


def _ffn(x, g, g_final, w_gu, w_d, final_norm):
    b, s, d = x.shape
    ts = FFN_SEQ_TILE
    tile = pl.BlockSpec((None, ts, d), lambda i, j: (i, j, 0))
    return pl.pallas_call(
        functools.partial(_ffn_kernel, final_norm=final_norm),
        grid=(b, s // ts),
        in_specs=[tile, _next_tile_spec(b, s // ts, ts, d), _resident((1, d)), _resident((1, d)),
                  _resident(w_gu.shape), _resident(w_d.shape)],
        out_specs=tile,
        out_shape=jax.ShapeDtypeStruct(x.shape, x.dtype),
        scratch_shapes=[pltpu.VMEM((ts, d), BF16), pltpu.VMEM((ts, D_FF), BF16)],
        compiler_params=pltpu.CompilerParams(
            dimension_semantics=("arbitrary", "arbitrary"),
            vmem_limit_bytes=VMEM_LIMIT_BYTES),
        name="ffn",
    )(x, x, g, g_final, w_gu, w_d)


def kernel(x, g_mix, w_in, conv_w, attn_sinks, w_conv_out, w_attn_out, w_o,
           g_ffn, w_gate_up, w_down, g_final):
    b, s, d = x.shape
    depth = w_in.shape[0]
    assert d == D_MODEL and s % SEQ_TILE == 0 and s % FFN_SEQ_TILE == 0
    assert w_in.shape[-1] == N_IN and w_gate_up.shape[-1] == 2 * D_FF
    assert N_IN % STAGE_COLS == 0 and d % STAGE_COLS == 0
    rope = _rope_tables(s)
    g_fin = g_final.reshape(1, d)
    for l in range(depth):
        x, w_gu, w_d = _mixer(
            x, g_mix[l].reshape(1, d), rope, conv_w, attn_sinks[l], l,
            w_in[l], w_conv_out[l], w_attn_out[l], w_o[l],
            w_gate_up[l], w_down[l])
        x = _ffn(x, g_ffn[l].reshape(1, d), g_fin, w_gu, w_d, final_norm=(l == depth - 1))
    return x
```

```python
import functools
import math

import jax
import jax.numpy as jnp
from jax import lax
from jax.experimental import pallas as pl
from jax.experimental.pallas import tpu as pltpu

D_MODEL = 1024
CONV_K = 3
HEAD_DIM = 64
N_HEADS = 16
N_KV_HEADS = 4
GROUP = N_HEADS // N_KV_HEADS
D_ATTN = N_HEADS * HEAD_DIM
D_KV = N_KV_HEADS * HEAD_DIM
WINDOW = 128
BLOCK = 128
ROT_DIM = HEAD_DIM // 4
ROPE_THETA = 500000.0
ATTN_SCALE = 1.0 / math.sqrt(HEAD_DIM)
LOG2E = math.log2(math.e)
NEG_INF = -1e30
D_FF = 2816
EPS = 1e-5

OFF_CB = 0
OFF_CC = OFF_CB + D_MODEL
OFF_CX = OFF_CC + D_MODEL
OFF_Q = OFF_CX + D_MODEL
OFF_K = OFF_Q + D_ATTN
OFF_V = OFF_K + D_KV
OFF_GC = OFF_V + D_KV
OFF_GA = OFF_GC + D_MODEL
N_IN = OFF_GA + D_MODEL

LANES = 128
SUBLANES = 8
BF16_SUBLANES = 16
SEQ_TILE = 512
FFN_SEQ_TILE = 1024
GATE_A_PARTS = 4
STAGE_COLS = 512
STAGE_SLOTS = 4
FF_CHUNKS = ((0, 1024), (1024, 1024), (2048, 768))
VMEM_LIMIT_BYTES = 58 * 1024 * 1024

F32 = jnp.float32
BF16 = jnp.bfloat16

assert WINDOW == BLOCK and 2 * HEAD_DIM == LANES and GROUP == 4


def _rms_norm(x, g):
    ms = jnp.mean(x * x, axis=-1, keepdims=True)
    return x * lax.rsqrt(ms + EPS) * g


def _sigmoid(x):
    return 1.0 / (1.0 + jnp.exp(-x))


def _rope(t, cos, sin_next, sin_prev):
    nxt = pltpu.roll(t, LANES - ROT_DIM // 2, 1)
    prv = pltpu.roll(t, ROT_DIM // 2, 1)
    return t * cos + nxt * sin_next + prv * sin_prev


def _split_head_pair(t, lane_lo):
    sw = pltpu.roll(t, HEAD_DIM, 1)
    zero = jnp.zeros_like(t)
    even = (jnp.where(lane_lo, t, zero), jnp.where(lane_lo, zero, sw))
    odd = (jnp.where(lane_lo, sw, zero), jnp.where(lane_lo, zero, t))
    return even, odd


def _stage_weights(pairs, stage_ref, sem_ref):
    slots = stage_ref.shape[0]
    jobs = [(src, dst, c * STAGE_COLS)
            for src, dst in pairs for c in range(src.shape[1] // STAGE_COLS)]

    def slab_copy(n):
        src, _, col = jobs[n]
        return pltpu.make_async_copy(
            src.at[:, pl.ds(col, STAGE_COLS)], stage_ref.at[n % slots], sem_ref.at[n % slots])

    for n in range(min(slots - 1, len(jobs))):
        slab_copy(n).start()
    for n, (_, dst, col) in enumerate(jobs):
        if n + slots - 1 < len(jobs):
            slab_copy(n + slots - 1).start()
        slab_copy(n).wait()
        dst[:, col:col + STAGE_COLS] = stage_ref[n % slots].astype(BF16)


def _mixer_kernel(sinks_ref, x_ref, g_ref, rope_ref, convw_ref, w_in_hbm,
                  w_co_hbm, w_ao_hbm, w_o_hbm, w_gu_f32_ref, w_d_f32_ref,
                  out_ref, w_gu_bf16_ref, w_d_bf16_ref,
                  w_in_ref, w_co_ref, w_ao_ref, w_o_ref, stage_ref, stage_sem,
                  u_scr, q_scr, klo_scr, khi_scr, vlo_scr, vhi_scr, attn_scr):
    ts = x_ref.shape[0]
    n_blk = ts // BLOCK
    s_idx = pl.program_id(1)
    kv_scrs = (klo_scr, khi_scr, vlo_scr, vhi_scr)

    @pl.when((pl.program_id(0) == 0) & (s_idx == 0))
    def _():
        _stage_weights(((w_in_hbm, w_in_ref), (w_co_hbm, w_co_ref),
                        (w_ao_hbm, w_ao_ref), (w_o_hbm, w_o_ref)), stage_ref, stage_sem)

    @pl.when(s_idx == 0)
    def _():
        u_scr[0:SUBLANES, :] = jnp.zeros((SUBLANES, D_MODEL), F32)
        for scr in kv_scrs:
            scr[0:BLOCK, :] = jnp.zeros((BLOCK, N_KV_HEADS * LANES), BF16)

    w_gu_bf16_ref[...] = w_gu_f32_ref[...].astype(BF16)
    w_d_bf16_ref[...] = w_d_f32_ref[...].astype(BF16)

    x = x_ref[...]
    h = _rms_norm(x, g_ref[...]).astype(BF16)

    def proj(off, width):
        return jnp.dot(h, w_in_ref[:, off:off + width], preferred_element_type=F32)

    q_tabs = [rope_ref[i] * (ATTN_SCALE * LOG2E) for i in range(3)]
    q = proj(OFF_Q, D_ATTN)
    for t in range(D_ATTN // LANES):
        sl = slice(t * LANES, (t + 1) * LANES)
        q_scr[:, sl] = _rope(q[:, sl], *q_tabs).astype(BF16)
    lane_lo_t = lax.broadcasted_iota(jnp.int32, (ts, LANES), 1) < HEAD_DIM
    k = proj(OFF_K, D_KV)
    v = proj(OFF_V, D_KV)
    for t in range(D_KV // LANES):
        sl = slice(t * LANES, (t + 1) * LANES)
        k_t = _rope(k[:, sl], rope_ref[0], rope_ref[1], rope_ref[2])
        for src, lo_scr, hi_scr in ((k_t, klo_scr, khi_scr), (v[:, sl], vlo_scr, vhi_scr)):
            for hd, (lo, hi) in zip((2 * t, 2 * t + 1), _split_head_pair(src, lane_lo_t)):
                hsl = slice(hd * LANES, (hd + 1) * LANES)
                lo_scr[BLOCK:BLOCK + ts, hsl] = lo.astype(BF16)
                hi_scr[BLOCK:BLOCK + ts, hsl] = hi.astype(BF16)

    u = proj(OFF_CC, D_MODEL) * proj(OFF_CX, D_MODEL)
    u_scr[SUBLANES:SUBLANES + ts, :] = u
    conv = (convw_ref[0:1, :] * u_scr[SUBLANES - 2:SUBLANES - 2 + ts, :]
            + convw_ref[1:2, :] * u_scr[SUBLANES - 1:SUBLANES - 1 + ts, :]
            + convw_ref[2:3, :] * u)
    u_scr[0:SUBLANES, :] = u_scr[ts:ts + SUBLANES, :]
    conv_y = (proj(OFF_CB, D_MODEL) * conv).astype(BF16)
    gate_c = _sigmoid(proj(OFF_GC, D_MODEL))
    merged = gate_c * jnp.dot(conv_y, w_co_ref[...], preferred_element_type=F32)

    lane = lax.broadcasted_iota(jnp.int32, (BLOCK, LANES), 1)
    rowi = lax.broadcasted_iota(jnp.int32, (BLOCK, LANES), 0)
    from_cur = lane <= rowi
    lane_lo = lane < HEAD_DIM
    lane2 = lax.broadcasted_iota(jnp.int32, (2 * BLOCK, LANES), 1)
    ones_lo = jnp.where(lane2 < HEAD_DIM, 1.0, 0.0).astype(BF16)
    ones_hi = jnp.where(lane2 < HEAD_DIM, 0.0, 1.0).astype(BF16)
    nt_dims = (((1,), (1,)), ((), ()))

    units = [(j, kv) for j in range(n_blk) for kv in range(N_KV_HEADS)]

    def q_slices(kv):
        return [slice((2 * kv + pr) * LANES, (2 * kv + pr + 1) * LANES) for pr in range(2)]

    def scores(j, kv):
        r0 = j * BLOCK
        kv_sl = slice(kv * LANES, (kv + 1) * LANES)
        q2 = jnp.concatenate([q_scr[r0:r0 + BLOCK, sl] for sl in q_slices(kv)], axis=0)
        k_rhs = jnp.concatenate([klo_scr[r0:r0 + 2 * BLOCK, kv_sl],
                                 khi_scr[r0:r0 + 2 * BLOCK, kv_sl]], axis=0)
        return lax.dot_general(q2, k_rhs, nt_dims, preferred_element_type=F32)

    s_next = scores(*units[0])
    gate_a_parts = []

    def gate_a_part(c):
        width = D_MODEL // GATE_A_PARTS
        return _sigmoid(proj(OFF_GA + c * width, width))

    for n, (j, kv) in enumerate(units):
        r0 = j * BLOCK
        kv_sl = slice(kv * LANES, (kv + 1) * LANES)
        q_sl = q_slices(kv)
        s = s_next
        if n + 1 < len(units):
            s_next = scores(*units[n + 1])
        if n > len(units) - GATE_A_PARTS:
            gate_a_parts.append(gate_a_part(len(gate_a_parts)))
        v_rhs = jnp.concatenate(
            [jnp.concatenate([vlo_scr[r0:r0 + 2 * BLOCK, kv_sl], ones_lo], axis=1),
             jnp.concatenate([vhi_scr[r0:r0 + 2 * BLOCK, kv_sl], ones_hi], axis=1)], axis=0)
        p_rows, sink_terms = [], []
        for pr in range(2):
            rows = slice(pr * BLOCK, (pr + 1) * BLOCK)
            p_tiles, e_sink = [], []
            for half in range(2):
                hd = kv * GROUP + 2 * pr + half
                s_prev = s[rows, (2 * half) * BLOCK:(2 * half + 1) * BLOCK]
                s_cur = s[rows, (2 * half + 1) * BLOCK:(2 * half + 2) * BLOCK]
                if j == 0:
                    s_prev = jnp.where(s_idx > 0, s_prev, NEG_INF)
                t = jnp.where(from_cur, s_cur, s_prev)
                m = jnp.max(t, axis=-1, keepdims=True)
                p = jnp.exp2(t - m).astype(BF16)
                zero = jnp.zeros_like(p)
                p_tiles += [jnp.where(from_cur, zero, p), jnp.where(from_cur, p, zero)]
                e_sink.append(jnp.exp2(sinks_ref[hd] * LOG2E - m))
            p_rows.append(jnp.concatenate(p_tiles, axis=1))
            sink_terms.append(jnp.where(lane_lo, e_sink[0], e_sink[1]))
        o = jnp.dot(jnp.concatenate(p_rows, axis=0), v_rhs, preferred_element_type=F32)
        if n + 1 == len(units):
            gate_a_parts.append(gate_a_part(len(gate_a_parts)))
        for pr in range(2):
            rows = slice(pr * BLOCK, (pr + 1) * BLOCK)
            den = o[rows, LANES:2 * LANES] + sink_terms[pr]
            attn_scr[r0:r0 + BLOCK, q_sl[pr]] = (o[rows, 0:LANES] / den).astype(BF16)

    for scr in kv_scrs:
        scr[0:BLOCK, :] = scr[ts:ts + BLOCK, :]

    attn_out = jnp.dot(attn_scr[...], w_ao_ref[...], preferred_element_type=F32)
    merged = merged + jnp.concatenate(gate_a_parts, axis=1) * attn_out
    out_ref[...] = x + jnp.dot(merged.astype(BF16), w_o_ref[...], preferred_element_type=F32)


def _ffn_kernel(x_ref, g_ref, gf_ref, w_gu_ref, w_d_ref, out_ref, act_scr, *, final_norm):
    x = x_ref[...]
    h = _rms_norm(x, g_ref[...]).astype(BF16)
    for off, width in FF_CHUNKS:
        gate = jnp.dot(h, w_gu_ref[:, off:off + width], preferred_element_type=F32)
        up = jnp.dot(h, w_gu_ref[:, D_FF + off:D_FF + off + width], preferred_element_type=F32)
        act_scr[:, off:off + width] = (gate * _sigmoid(gate) * up).astype(BF16)
    y = x + jnp.dot(act_scr[...], w_d_ref[...], preferred_element_type=F32)
    if final_norm:
        y = _rms_norm(y, gf_ref[...])
    out_ref[...] = y


def _resident(shape):
    return pl.BlockSpec(shape, lambda b, s: (0,) * len(shape), pipeline_mode=pl.Buffered(1))


def _rope_tables(seq):
    half = ROT_DIM // 2
    d = jnp.arange(LANES) % HEAD_DIM
    inv_freq = ROPE_THETA ** (-(2 * (d % half)).astype(F32) / ROT_DIM)
    ang = jnp.arange(seq, dtype=F32)[:, None] * inv_freq[None, :]
    cos, sin = jnp.cos(ang), jnp.sin(ang)
    c = jnp.where(d < ROT_DIM, cos, 1.0)
    s_next = jnp.where(d < half, -sin, 0.0)
    s_prev = jnp.where((d >= half) & (d < ROT_DIM), sin, 0.0)
    kind = jnp.arange(3)[:, None, None]
    return jnp.where(kind == 0, c, jnp.where(kind == 1, s_next, s_prev))


def _mixer(x, g, rope, conv_w, sinks, layer, w_in, w_co, w_ao, w_o, w_gu, w_d):
    b, s, d = x.shape
    ts = SEQ_TILE
    n_s = s // ts
    gu_rows = w_gu.shape[0] // (b * n_s)
    d_rows = 2 * w_d.shape[0] // (b * n_s)
    assert gu_rows % BF16_SUBLANES == 0 and d_rows % BF16_SUBLANES == 0
    gu_slab = pl.BlockSpec((gu_rows, w_gu.shape[1]), lambda i, j: (i * n_s + j, 0))
    d_slab = pl.BlockSpec((d_rows, w_d.shape[1]), lambda i, j: ((i * n_s + j) // 2, 0))
    tile = pl.BlockSpec((None, ts, d), lambda i, j: (i, j, 0))
    kv_scratch = pltpu.VMEM((BLOCK + ts, N_KV_HEADS * LANES), BF16)
    return pl.pallas_call(
        _mixer_kernel,
        grid=(b, n_s),
        in_specs=[
            pl.BlockSpec(memory_space=pltpu.SMEM),
            tile,
            _resident((1, d)),
            pl.BlockSpec((3, ts, LANES), lambda i, j: (0, j, 0)),
            pl.BlockSpec((None, CONV_K, d), lambda i, j: (layer, 0, 0),
                         pipeline_mode=pl.Buffered(1)),
            pl.BlockSpec(memory_space=pl.ANY),
            pl.BlockSpec(memory_space=pl.ANY),
            pl.BlockSpec(memory_space=pl.ANY),
            pl.BlockSpec(memory_space=pl.ANY),
            gu_slab,
            d_slab,
        ],
        out_specs=[tile, gu_slab, d_slab],
        out_shape=[jax.ShapeDtypeStruct(x.shape, x.dtype),
                   jax.ShapeDtypeStruct(w_gu.shape, BF16),
                   jax.ShapeDtypeStruct(w_d.shape, BF16)],
        scratch_shapes=[
            pltpu.VMEM(w_in.shape, BF16),
            pltpu.VMEM(w_co.shape, BF16),
            pltpu.VMEM(w_ao.shape, BF16),
            pltpu.VMEM(w_o.shape, BF16),
            pltpu.VMEM((STAGE_SLOTS, d, STAGE_COLS), F32),
            pltpu.SemaphoreType.DMA((STAGE_SLOTS,)),
            pltpu.VMEM((SUBLANES + ts, d), F32),
            pltpu.VMEM((ts, D_ATTN), BF16),
            kv_scratch, kv_scratch, kv_scratch, kv_scratch,
            pltpu.VMEM((ts, D_ATTN), BF16),
        ],
        compiler_params=pltpu.CompilerParams(
            dimension_semantics=("arbitrary", "arbitrary"),
            vmem_limit_bytes=VMEM_LIMIT_BYTES),
        name="mixer",
    )(sinks, x, g, rope, conv_w, w_in, w_co, w_ao, w_o, w_gu, w_d)


def _ffn(x, g, g_final, w_gu, w_d, final_norm):
    b, s, d = x.shape
    ts = FFN_SEQ_TILE
    tile = pl.BlockSpec((None, ts, d), lambda i, j: (i, j, 0))
    return pl.pallas_call(
        functools.partial(_ffn_kernel, final_norm=final_norm),
        grid=(b, s // ts),
        in_specs=[tile, _resident((1, d)), _resident((1, d)),
                  _resident(w_gu.shape), _resident(w_d.shape)],
        out_specs=tile,
        out_shape=jax.ShapeDtypeStruct(x.shape, x.dtype),
        scratch_shapes=[pltpu.VMEM((ts, D_FF), BF16)],
        compiler_params=pltpu.CompilerParams(
            dimension_semantics=("arbitrary", "arbitrary"),
            vmem_limit_bytes=VMEM_LIMIT_BYTES),
        name="ffn",
    )(x, g, g_final, w_gu, w_d)


def kernel(x, g_mix, w_in, conv_w, attn_sinks, w_conv_out, w_attn_out, w_o,
           g_ffn, w_gate_up, w_down, g_final):
    b, s, d = x.shape
    depth = w_in.shape[0]
    assert d == D_MODEL and s % SEQ_TILE == 0 and s % FFN_SEQ_TILE == 0
    assert w_in.shape[-1] == N_IN and w_gate_up.shape[-1] == 2 * D_FF
    assert N_IN % STAGE_COLS == 0 and d % STAGE_COLS == 0
    rope = _rope_tables(s)
    g_fin = g_final.reshape(1, d)
    for l in range(depth):
        x, w_gu, w_d = _mixer(
            x, g_mix[l].reshape(1, d), rope, conv_w, attn_sinks[l], l,
            w_in[l], w_conv_out[l], w_attn_out[l], w_o[l],
            w_gate_up[l], w_down[l])
        x = _ffn(x, g_ffn[l].reshape(1, d), g_fin, w_gu, w_d, final_norm=(l == depth - 1))
    return x
```

```python
import functools
import math

import jax
import jax.numpy as jnp
from jax import lax
from jax.experimental import pallas as pl
from jax.experimental.pallas import tpu as pltpu

D_MODEL = 1024
CONV_K = 3
HEAD_DIM = 64
N_HEADS = 16
N_KV_HEADS = 4
GROUP = N_HEADS // N_KV_HEADS
D_ATTN = N_HEADS * HEAD_DIM
D_KV = N_KV_HEADS * HEAD_DIM
WINDOW = 128
BLOCK = 128
ROT_DIM = HEAD_DIM // 4
ROPE_THETA = 500000.0
ATTN_SCALE = 1.0 / math.sqrt(HEAD_DIM)
LOG2E = math.log2(math.e)
NEG_INF = -1e30
D_FF = 2816
EPS = 1e-5

OFF_CB = 0
OFF_CC = OFF_CB + D_MODEL
OFF_CX = OFF_CC + D_MODEL
OFF_Q = OFF_CX + D_MODEL
OFF_K = OFF_Q + D_ATTN
OFF_V = OFF_K + D_KV
OFF_GC = OFF_V + D_KV
OFF_GA = OFF_GC + D_MODEL
N_IN = OFF_GA + D_MODEL

LANES = 128
SUBLANES = 8
BF16_SUBLANES = 16
SEQ_TILE = 512
FFN_SEQ_TILE = 1024
FILL_COLS = 512
LOOKAHEAD = 1
STAGE_COLS = 512
STAGE_SLOTS = 4
FF_CHUNKS = ((0, 1024), (1024, 1024), (2048, 768))
VMEM_LIMIT_BYTES = 58 * 1024 * 1024

F32 = jnp.float32
BF16 = jnp.bfloat16

assert WINDOW == BLOCK and 2 * HEAD_DIM == LANES and GROUP == 4


def _rms_norm(x, g):
    ms = jnp.mean(x * x, axis=-1, keepdims=True)
    return x * lax.rsqrt(ms + EPS) * g


def _sigmoid(x):
    return 1.0 / (1.0 + jnp.exp(-x))


def _rope(t, cos, sin_next, sin_prev):
    nxt = pltpu.roll(t, LANES - ROT_DIM // 2, 1)
    prv = pltpu.roll(t, ROT_DIM // 2, 1)
    return t * cos + nxt * sin_next + prv * sin_prev


def _split_head_pair(t, lane_lo):
    sw = pltpu.roll(t, HEAD_DIM, 1)
    zero = jnp.zeros_like(t)
    even = (jnp.where(lane_lo, t, zero), jnp.where(lane_lo, zero, sw))
    odd = (jnp.where(lane_lo, sw, zero), jnp.where(lane_lo, zero, t))
    return even, odd


def _stage_weights(pairs, stage_ref, sem_ref):
    slots = stage_ref.shape[0]
    jobs = [(src, dst, c * STAGE_COLS)
            for src, dst in pairs for c in range(src.shape[1] // STAGE_COLS)]

    def slab_copy(n):
        src, _, col = jobs[n]
        return pltpu.make_async_copy(
            src.at[:, pl.ds(col, STAGE_COLS)], stage_ref.at[n % slots], sem_ref.at[n % slots])

    for n in range(min(slots - 1, len(jobs))):
        slab_copy(n).start()
    for n, (_, dst, col) in enumerate(jobs):
        if n + slots - 1 < len(jobs):
            slab_copy(n + slots - 1).start()
        slab_copy(n).wait()
        dst[:, col:col + STAGE_COLS] = stage_ref[n % slots].astype(BF16)


def _mixer_kernel(sinks_ref, x_ref, g_ref, rope_ref, convw_ref, w_in_hbm,
                  w_co_hbm, w_ao_hbm, w_o_hbm, w_gu_f32_ref, w_d_f32_ref,
                  out_ref, w_gu_bf16_ref, w_d_bf16_ref,
                  w_in_ref, w_co_ref, w_ao_ref, w_o_ref, stage_ref, stage_sem,
                  u_scr, q_scr, klo_scr, khi_scr, vlo_scr, vhi_scr, attn_scr):
    ts = x_ref.shape[0]
    n_blk = ts // BLOCK
    s_idx = pl.program_id(1)
    kv_scrs = (klo_scr, khi_scr, vlo_scr, vhi_scr)

    @pl.when((pl.program_id(0) == 0) & (s_idx == 0))
    def _():
        _stage_weights(((w_in_hbm, w_in_ref), (w_co_hbm, w_co_ref),
                        (w_ao_hbm, w_ao_ref), (w_o_hbm, w_o_ref)), stage_ref, stage_sem)

    @pl.when(s_idx == 0)
    def _():
        u_scr[0:SUBLANES, :] = jnp.zeros((SUBLANES, D_MODEL), F32)
        for scr in kv_scrs:
            scr[0:BLOCK, :] = jnp.zeros((BLOCK, N_KV_HEADS * LANES), BF16)

    w_gu_bf16_ref[...] = w_gu_f32_ref[...].astype(BF16)
    w_d_bf16_ref[...] = w_d_f32_ref[...].astype(BF16)

    x = x_ref[...]
    h = _rms_norm(x, g_ref[...]).astype(BF16)

    def proj(off, width):
        return jnp.dot(h, w_in_ref[:, off:off + width], preferred_element_type=F32)

    q_tabs = [rope_ref[i] * (ATTN_SCALE * LOG2E) for i in range(3)]
    q = proj(OFF_Q, D_ATTN)
    for t in range(D_ATTN // LANES):
        sl = slice(t * LANES, (t + 1) * LANES)
        q_scr[:, sl] = _rope(q[:, sl], *q_tabs).astype(BF16)
    lane_lo_t = lax.broadcasted_iota(jnp.int32, (ts, LANES), 1) < HEAD_DIM
    k = proj(OFF_K, D_KV)
    v = proj(OFF_V, D_KV)
    for t in range(D_KV // LANES):
        sl = slice(t * LANES, (t + 1) * LANES)
        k_t = _rope(k[:, sl], rope_ref[0], rope_ref[1], rope_ref[2])
        for src, lo_scr, hi_scr in ((k_t, klo_scr, khi_scr), (v[:, sl], vlo_scr, vhi_scr)):
            for hd, (lo, hi) in zip((2 * t, 2 * t + 1), _split_head_pair(src, lane_lo_t)):
                hsl = slice(hd * LANES, (hd + 1) * LANES)
                lo_scr[BLOCK:BLOCK + ts, hsl] = lo.astype(BF16)
                hi_scr[BLOCK:BLOCK + ts, hsl] = hi.astype(BF16)

    u = proj(OFF_CC, D_MODEL) * proj(OFF_CX, D_MODEL)
    u_scr[SUBLANES:SUBLANES + ts, :] = u
    conv = (convw_ref[0:1, :] * u_scr[SUBLANES - 2:SUBLANES - 2 + ts, :]
            + convw_ref[1:2, :] * u_scr[SUBLANES - 1:SUBLANES - 1 + ts, :]
            + convw_ref[2:3, :] * u)
    u_scr[0:SUBLANES, :] = u_scr[ts:ts + SUBLANES, :]

    n_chunks = D_MODEL // FILL_COLS
    conv_y_chunks, gate_c_chunks, merged_chunks, gate_a_chunks = [], [], [], []

    def chunk(c):
        return slice(c * FILL_COLS, (c + 1) * FILL_COLS)

    def conv_gate_chunk(c):
        conv_y_chunks.append(
            (proj(OFF_CB + c * FILL_COLS, FILL_COLS) * conv[:, chunk(c)]).astype(BF16))

    def merge_gate_chunk(c):
        gate_c_chunks.append(_sigmoid(proj(OFF_GC + c * FILL_COLS, FILL_COLS)))

    def conv_out_chunk(c):
        conv_y = jnp.concatenate(conv_y_chunks, axis=1)
        merged_chunks.append(gate_c_chunks[c] * jnp.dot(
            conv_y, w_co_ref[:, chunk(c)], preferred_element_type=F32))

    def attn_gate_chunk(c):
        gate_a_chunks.append(_sigmoid(proj(OFF_GA + c * FILL_COLS, FILL_COLS)))

    fillers = [functools.partial(f, c)
               for f in (conv_gate_chunk, merge_gate_chunk, conv_out_chunk, attn_gate_chunk)
               for c in range(n_chunks)]

    lane = lax.broadcasted_iota(jnp.int32, (BLOCK, LANES), 1)
    rowi = lax.broadcasted_iota(jnp.int32, (BLOCK, LANES), 0)
    from_cur = lane <= rowi
    lane_lo = lane < HEAD_DIM
    lane2 = lax.broadcasted_iota(jnp.int32, (2 * BLOCK, LANES), 1)
    ones_lo = jnp.where(lane2 < HEAD_DIM, 1.0, 0.0).astype(BF16)
    ones_hi = jnp.where(lane2 < HEAD_DIM, 0.0, 1.0).astype(BF16)
    nt_dims = (((1,), (1,)), ((), ()))

    units = [(j, kv) for j in range(n_blk) for kv in range(N_KV_HEADS)]

    def q_slices(kv):
        return [slice((2 * kv + pr) * LANES, (2 * kv + pr + 1) * LANES) for pr in range(2)]

    def scores(j, kv):
        r0 = j * BLOCK
        kv_sl = slice(kv * LANES, (kv + 1) * LANES)
        q2 = jnp.concatenate([q_scr[r0:r0 + BLOCK, sl] for sl in q_slices(kv)], axis=0)
        k_rhs = jnp.concatenate([klo_scr[r0:r0 + 2 * BLOCK, kv_sl],
                                 khi_scr[r0:r0 + 2 * BLOCK, kv_sl]], axis=0)
        return lax.dot_general(q2, k_rhs, nt_dims, preferred_element_type=F32)

    s_queue = [scores(*u) for u in units[:LOOKAHEAD]]
    for n, (j, kv) in enumerate(units):
        r0 = j * BLOCK
        kv_sl = slice(kv * LANES, (kv + 1) * LANES)
        q_sl = q_slices(kv)
        s = s_queue.pop(0)
        if n + LOOKAHEAD < len(units):
            s_queue.append(scores(*units[n + LOOKAHEAD]))
        last = n + 1 == len(units)
        if fillers and (n % 2 == 1 or last) and (len(fillers) > 2 or last):
            fillers.pop(0)()
        v_rhs = jnp.concatenate(
            [jnp.concatenate([vlo_scr[r0:r0 + 2 * BLOCK, kv_sl], ones_lo], axis=1),
             jnp.concatenate([vhi_scr[r0:r0 + 2 * BLOCK, kv_sl], ones_hi], axis=1)], axis=0)
        p_rows, sink_terms = [], []
        for pr in range(2):
            rows = slice(pr * BLOCK, (pr + 1) * BLOCK)
            p_tiles, e_sink = [], []
            for half in range(2):
                hd = kv * GROUP + 2 * pr + half
                s_prev = s[rows, (2 * half) * BLOCK:(2 * half + 1) * BLOCK]
                s_cur = s[rows, (2 * half + 1) * BLOCK:(2 * half + 2) * BLOCK]
                if j == 0:
                    s_prev = jnp.where(s_idx > 0, s_prev, NEG_INF)
                t = jnp.where(from_cur, s_cur, s_prev)
                m = jnp.max(t, axis=-1, keepdims=True)
                p = jnp.exp2(t - m).astype(BF16)
                zero = jnp.zeros_like(p)
                p_tiles += [jnp.where(from_cur, zero, p), jnp.where(from_cur, p, zero)]
                e_sink.append(jnp.exp2(sinks_ref[hd] * LOG2E - m))
            p_rows.append(jnp.concatenate(p_tiles, axis=1))
            sink_terms.append(jnp.where(lane_lo, e_sink[0], e_sink[1]))
        o = jnp.dot(jnp.concatenate(p_rows, axis=0), v_rhs, preferred_element_type=F32)
        if last and fillers:
            fillers.pop(0)()
        for pr in range(2):
            rows = slice(pr * BLOCK, (pr + 1) * BLOCK)
            den = o[rows, LANES:2 * LANES] + sink_terms[pr]
            attn_scr[r0:r0 + BLOCK, q_sl[pr]] = (o[rows, 0:LANES] / den).astype(BF16)

    for scr in kv_scrs:
        scr[0:BLOCK, :] = scr[ts:ts + BLOCK, :]

    for f in fillers:
        f()
    attn_out = jnp.dot(attn_scr[...], w_ao_ref[...], preferred_element_type=F32)
    merged = (jnp.concatenate(merged_chunks, axis=1)
              + jnp.concatenate(gate_a_chunks, axis=1) * attn_out)
    out_ref[...] = x + jnp.dot(merged.astype(BF16), w_o_ref[...], preferred_element_type=F32)


def _ffn_kernel(x_ref, g_ref, gf_ref, w_gu_ref, w_d_ref, out_ref, act_scr, *, final_norm):
    x = x_ref[...]
    h = _rms_norm(x, g_ref[...]).astype(BF16)
    for off, width in FF_CHUNKS:
        gate = jnp.dot(h, w_gu_ref[:, off:off + width], preferred_element_type=F32)
        up = jnp.dot(h, w_gu_ref[:, D_FF + off:D_FF + off + width], preferred_element_type=F32)
        act_scr[:, off:off + width] = (gate * _sigmoid(gate) * up).astype(BF16)
    y = x + jnp.dot(act_scr[...], w_d_ref[...], preferred_element_type=F32)
    if final_norm:
        y = _rms_norm(y, gf_ref[...])
    out_ref[...] = y


def _resident(shape):
    return pl.BlockSpec(shape, lambda b, s: (0,) * len(shape), pipeline_mode=pl.Buffered(1))


def _rope_tables(seq):
    half = ROT_DIM // 2
    d = jnp.arange(LANES) % HEAD_DIM
    inv_freq = ROPE_THETA ** (-(2 * (d % half)).astype(F32) / ROT_DIM)
    ang = jnp.arange(seq, dtype=F32)[:, None] * inv_freq[None, :]
    cos, sin = jnp.cos(ang), jnp.sin(ang)
    c = jnp.where(d < ROT_DIM, cos, 1.0)
    s_next = jnp.where(d < half, -sin, 0.0)
    s_prev = jnp.where((d >= half) & (d < ROT_DIM), sin, 0.0)
    kind = jnp.arange(3)[:, None, None]
    return jnp.where(kind == 0, c, jnp.where(kind == 1, s_next, s_prev))


def _mixer(x, g, rope, conv_w, sinks, layer, w_in, w_co, w_ao, w_o, w_gu, w_d):
    b, s, d = x.shape
    ts = SEQ_TILE
    n_s = s // ts
    gu_rows = w_gu.shape[0] // (b * n_s)
    d_rows = 2 * w_d.shape[0] // (b * n_s)
    assert gu_rows % BF16_SUBLANES == 0 and d_rows % BF16_SUBLANES == 0
    gu_slab = pl.BlockSpec((gu_rows, w_gu.shape[1]), lambda i, j: (i * n_s + j, 0))
    d_slab = pl.BlockSpec((d_rows, w_d.shape[1]), lambda i, j: ((i * n_s + j) // 2, 0))
    tile = pl.BlockSpec((None, ts, d), lambda i, j: (i, j, 0))
    kv_scratch = pltpu.VMEM((BLOCK + ts, N_KV_HEADS * LANES), BF16)
    return pl.pallas_call(
        _mixer_kernel,
        grid=(b, n_s),
        in_specs=[
            pl.BlockSpec(memory_space=pltpu.SMEM),
            tile,
            _resident((1, d)),
            pl.BlockSpec((3, ts, LANES), lambda i, j: (0, j, 0)),
            pl.BlockSpec((None, CONV_K, d), lambda i, j: (layer, 0, 0),
                         pipeline_mode=pl.Buffered(1)),
            pl.BlockSpec(memory_space=pl.ANY),
            pl.BlockSpec(memory_space=pl.ANY),
            pl.BlockSpec(memory_space=pl.ANY),
            pl.BlockSpec(memory_space=pl.ANY),
            gu_slab,
            d_slab,
        ],
        out_specs=[tile, gu_slab, d_slab],
        out_shape=[jax.ShapeDtypeStruct(x.shape, x.dtype),
                   jax.ShapeDtypeStruct(w_gu.shape, BF16),
                   jax.ShapeDtypeStruct(w_d.shape, BF16)],
        scratch_shapes=[
            pltpu.VMEM(w_in.shape, BF16),
            pltpu.VMEM(w_co.shape, BF16),
            pltpu.VMEM(w_ao.shape, BF16),
            pltpu.VMEM(w_o.shape, BF16),
            pltpu.VMEM((STAGE_SLOTS, d, STAGE_COLS), F32),
            pltpu.SemaphoreType.DMA((STAGE_SLOTS,)),
            pltpu.VMEM((SUBLANES + ts, d), F32),
            pltpu.VMEM((ts, D_ATTN), BF16),
            kv_scratch, kv_scratch, kv_scratch, kv_scratch,
            pltpu.VMEM((ts, D_ATTN), BF16),
        ],
        compiler_params=pltpu.CompilerParams(
            dimension_semantics=("arbitrary", "arbitrary"),
            vmem_limit_bytes=VMEM_LIMIT_BYTES),
        name="mixer",
    )(sinks, x, g, rope, conv_w, w_in, w_co, w_ao, w_o, w_gu, w_d)


def _ffn(x, g, g_final, w_gu, w_d, final_norm):
    b, s, d = x.shape
    ts = FFN_SEQ_TILE
    tile = pl.BlockSpec((None, ts, d), lambda i, j: (i, j, 0))
    return pl.pallas_call(
        functools.partial(_ffn_kernel, final_norm=final_norm),
        grid=(b, s // ts),
        in_specs=[tile, _resident((1, d)), _resident((1, d)),
                  _resident(w_gu.shape), _resident(w_d.shape)],
        out_specs=tile,
        out_shape=jax.ShapeDtypeStruct(x.shape, x.dtype),
        scratch_shapes=[pltpu.VMEM((ts, D_FF), BF16)],
        compiler_params=pltpu.CompilerParams(
            dimension_semantics=("arbitrary", "arbitrary"),
            vmem_limit_bytes=VMEM_LIMIT_BYTES),
        name="ffn",
    )(x, g, g_final, w_gu, w_d)


def kernel(x, g_mix, w_in, conv_w, attn_sinks, w_conv_out, w_attn_out, w_o,
           g_ffn, w_gate_up, w_down, g_final):
    b, s, d = x.shape
    depth = w_in.shape[0]
    assert d == D_MODEL and s % SEQ_TILE == 0 and s % FFN_SEQ_TILE == 0
    assert w_in.shape[-1] == N_IN and w_gate_up.shape[-1] == 2 * D_FF
    assert N_IN % STAGE_COLS == 0 and d % STAGE_COLS == 0
    rope = _rope_tables(s)
    g_fin = g_final.reshape(1, d)
    for l in range(depth):
        x, w_gu, w_d = _mixer(
            x, g_mix[l].reshape(1, d), rope, conv_w, attn_sinks[l], l,
            w_in[l], w_conv_out[l], w_attn_out[l], w_o[l],
            w_gate_up[l], w_down[l])
        x = _ffn(x, g_ffn[l].reshape(1, d), g_fin, w_gu, w_d, final_norm=(l == depth - 1))
    return x
```

```python
import functools
import math

import jax
import jax.numpy as jnp
from jax import lax
from jax.experimental import pallas as pl
from jax.experimental.pallas import tpu as pltpu

D_MODEL = 1024
CONV_K = 3
HEAD_DIM = 64
N_HEADS = 16
N_KV_HEADS = 4
GROUP = N_HEADS // N_KV_HEADS
D_ATTN = N_HEADS * HEAD_DIM
D_KV = N_KV_HEADS * HEAD_DIM
WINDOW = 128
BLOCK = 128
ROT_DIM = HEAD_DIM // 4
ROPE_THETA = 500000.0
ATTN_SCALE = 1.0 / math.sqrt(HEAD_DIM)
LOG2E = math.log2(math.e)
NEG_INF = -1e30
D_FF = 2816
EPS = 1e-5

OFF_CB = 0
OFF_CC = OFF_CB + D_MODEL
OFF_CX = OFF_CC + D_MODEL
OFF_Q = OFF_CX + D_MODEL
OFF_K = OFF_Q + D_ATTN
OFF_V = OFF_K + D_KV
OFF_GC = OFF_V + D_KV
OFF_GA = OFF_GC + D_MODEL
N_IN = OFF_GA + D_MODEL

LANES = 128
SUBLANES = 8
BF16_SUBLANES = 16
SEQ_TILE = 512
FFN_SEQ_TILE = 1024
STAGE_COLS = 512
STAGE_SLOTS = 4
FF_CHUNKS = ((0, 1024), (1024, 1024), (2048, 768))
VMEM_LIMIT_BYTES = 58 * 1024 * 1024

F32 = jnp.float32
BF16 = jnp.bfloat16

assert WINDOW == BLOCK and 2 * HEAD_DIM == LANES and GROUP == 4


def _rms_norm(x, g):
    ms = jnp.mean(x * x, axis=-1, keepdims=True)
    return x * lax.rsqrt(ms + EPS) * g


def _inv_rms(x):
    return lax.rsqrt(jnp.mean(x * x, axis=-1, keepdims=True) + EPS)


def _sigmoid(x):
    return 1.0 / (1.0 + jnp.exp(-x))


def _rope(t, cos, sin_next, sin_prev):
    nxt = pltpu.roll(t, LANES - ROT_DIM // 2, 1)
    prv = pltpu.roll(t, ROT_DIM // 2, 1)
    return t * cos + nxt * sin_next + prv * sin_prev


def _split_head_pair(t, lane_lo):
    sw = pltpu.roll(t, HEAD_DIM, 1)
    zero = jnp.zeros_like(t)
    even = (jnp.where(lane_lo, t, zero), jnp.where(lane_lo, zero, sw))
    odd = (jnp.where(lane_lo, sw, zero), jnp.where(lane_lo, zero, t))
    return even, odd


def _stage_weights(pairs, stage_ref, sem_ref):
    slots = stage_ref.shape[0]
    jobs = [(src, dst, c * STAGE_COLS, gain)
            for src, dst, gain in pairs for c in range(src.shape[1] // STAGE_COLS)]

    def slab_copy(n):
        src, _, col, _ = jobs[n]
        return pltpu.make_async_copy(
            src.at[:, pl.ds(col, STAGE_COLS)], stage_ref.at[n % slots], sem_ref.at[n % slots])

    for n in range(min(slots - 1, len(jobs))):
        slab_copy(n).start()
    for n, (_, dst, col, gain) in enumerate(jobs):
        if n + slots - 1 < len(jobs):
            slab_copy(n + slots - 1).start()
        slab_copy(n).wait()
        slab = stage_ref[n % slots]
        if gain is not None:
            slab = slab * gain
        dst[:, col:col + STAGE_COLS] = slab.astype(BF16)


def _mixer_kernel(sinks_ref, x_ref, g_ref, g_ffn_ref, rope_ref, convw_ref, w_in_hbm,
                  w_co_hbm, w_ao_hbm, w_o_hbm, w_gu_f32_ref, w_d_f32_ref,
                  out_ref, w_gu_bf16_ref, w_d_bf16_ref,
                  w_in_ref, w_co_ref, w_ao_ref, w_o_ref, stage_ref, stage_sem,
                  u_scr, q_scr, klo_scr, khi_scr, vlo_scr, vhi_scr, attn_scr):
    ts = x_ref.shape[0]
    n_blk = ts // BLOCK
    s_idx = pl.program_id(1)
    kv_scrs = (klo_scr, khi_scr, vlo_scr, vhi_scr)

    @pl.when((pl.program_id(0) == 0) & (s_idx == 0))
    def _():
        _stage_weights(((w_in_hbm, w_in_ref, g_ref[...]), (w_co_hbm, w_co_ref, None),
                        (w_ao_hbm, w_ao_ref, None), (w_o_hbm, w_o_ref, None)),
                       stage_ref, stage_sem)

    @pl.when(s_idx == 0)
    def _():
        u_scr[0:SUBLANES, :] = jnp.zeros((SUBLANES, D_MODEL), F32)
        for scr in kv_scrs:
            scr[0:BLOCK, :] = jnp.zeros((BLOCK, N_KV_HEADS * LANES), BF16)

    w_gu_bf16_ref[...] = (w_gu_f32_ref[...] * g_ffn_ref[...]).astype(BF16)
    w_d_bf16_ref[...] = w_d_f32_ref[...].astype(BF16)

    x = x_ref[...]
    h = x.astype(BF16)
    inv_rms = _inv_rms(x)

    def proj(off, width):
        return inv_rms * jnp.dot(h, w_in_ref[:, off:off + width], preferred_element_type=F32)

    q_tabs = [rope_ref[i] * (ATTN_SCALE * LOG2E) for i in range(3)]
    q = proj(OFF_Q, D_ATTN)
    for t in range(D_ATTN // LANES):
        sl = slice(t * LANES, (t + 1) * LANES)
        q_scr[:, sl] = _rope(q[:, sl], *q_tabs).astype(BF16)
    lane_lo_t = lax.broadcasted_iota(jnp.int32, (ts, LANES), 1) < HEAD_DIM
    k = proj(OFF_K, D_KV)
    v = proj(OFF_V, D_KV)
    for t in range(D_KV // LANES):
        sl = slice(t * LANES, (t + 1) * LANES)
        k_t = _rope(k[:, sl], rope_ref[0], rope_ref[1], rope_ref[2])
        for src, lo_scr, hi_scr in ((k_t, klo_scr, khi_scr), (v[:, sl], vlo_scr, vhi_scr)):
            for hd, (lo, hi) in zip((2 * t, 2 * t + 1), _split_head_pair(src, lane_lo_t)):
                hsl = slice(hd * LANES, (hd + 1) * LANES)
                lo_scr[BLOCK:BLOCK + ts, hsl] = lo.astype(BF16)
                hi_scr[BLOCK:BLOCK + ts, hsl] = hi.astype(BF16)

    u = proj(OFF_CC, D_MODEL) * proj(OFF_CX, D_MODEL)
    u_scr[SUBLANES:SUBLANES + ts, :] = u
    conv = (convw_ref[0:1, :] * u_scr[SUBLANES - 2:SUBLANES - 2 + ts, :]
            + convw_ref[1:2, :] * u_scr[SUBLANES - 1:SUBLANES - 1 + ts, :]
            + convw_ref[2:3, :] * u)
    u_scr[0:SUBLANES, :] = u_scr[ts:ts + SUBLANES, :]
    conv_y = (proj(OFF_CB, D_MODEL) * conv).astype(BF16)
    gate_c = _sigmoid(proj(OFF_GC, D_MODEL))
    merged = gate_c * jnp.dot(conv_y, w_co_ref[...], preferred_element_type=F32)

    lane = lax.broadcasted_iota(jnp.int32, (BLOCK, LANES), 1)
    rowi = lax.broadcasted_iota(jnp.int32, (BLOCK, LANES), 0)
    from_cur = lane <= rowi
    lane_lo = lane < HEAD_DIM
    lane2 = lax.broadcasted_iota(jnp.int32, (2 * BLOCK, LANES), 1)
    ones_lo = jnp.where(lane2 < HEAD_DIM, 1.0, 0.0).astype(BF16)
    ones_hi = jnp.where(lane2 < HEAD_DIM, 0.0, 1.0).astype(BF16)
    nt_dims = (((1,), (1,)), ((), ()))

    units = [(j, kv) for j in range(n_blk) for kv in range(N_KV_HEADS)]

    def q_slices(kv):
        return [slice((2 * kv + pr) * LANES, (2 * kv + pr + 1) * LANES) for pr in range(2)]

    def scores(j, kv):
        r0 = j * BLOCK
        kv_sl = slice(kv * LANES, (kv + 1) * LANES)
        q2 = jnp.concatenate([q_scr[r0:r0 + BLOCK, sl] for sl in q_slices(kv)], axis=0)
        k_rhs = jnp.concatenate([klo_scr[r0:r0 + 2 * BLOCK, kv_sl],
                                 khi_scr[r0:r0 + 2 * BLOCK, kv_sl]], axis=0)
        return lax.dot_general(q2, k_rhs, nt_dims, preferred_element_type=F32)

    s_next = scores(*units[0])
    for n, (j, kv) in enumerate(units):
        r0 = j * BLOCK
        kv_sl = slice(kv * LANES, (kv + 1) * LANES)
        q_sl = q_slices(kv)
        s = s_next
        if n + 1 < len(units):
            s_next = scores(*units[n + 1])
        else:
            gate_a_halves = [_sigmoid(proj(OFF_GA, D_MODEL // 2))]
        v_rhs = jnp.concatenate(
            [jnp.concatenate([vlo_scr[r0:r0 + 2 * BLOCK, kv_sl], ones_lo], axis=1),
             jnp.concatenate([vhi_scr[r0:r0 + 2 * BLOCK, kv_sl], ones_hi], axis=1)], axis=0)
        p_rows, sink_terms = [], []
        for pr in range(2):
            rows = slice(pr * BLOCK, (pr + 1) * BLOCK)
            p_tiles, e_sink = [], []
            for half in range(2):
                hd = kv * GROUP + 2 * pr + half
                s_prev = s[rows, (2 * half) * BLOCK:(2 * half + 1) * BLOCK]
                s_cur = s[rows, (2 * half + 1) * BLOCK:(2 * half + 2) * BLOCK]
                if j == 0:
                    s_prev = jnp.where(s_idx > 0, s_prev, NEG_INF)
                t = jnp.where(from_cur, s_cur, s_prev)
                m = jnp.max(t, axis=-1, keepdims=True)
                p = jnp.exp2(t - m).astype(BF16)
                zero = jnp.zeros_like(p)
                p_tiles += [jnp.where(from_cur, zero, p), jnp.where(from_cur, p, zero)]
                e_sink.append(jnp.exp2(sinks_ref[hd] * LOG2E - m))
            p_rows.append(jnp.concatenate(p_tiles, axis=1))
            sink_terms.append(jnp.where(lane_lo, e_sink[0], e_sink[1]))
        o = jnp.dot(jnp.concatenate(p_rows, axis=0), v_rhs, preferred_element_type=F32)
        if n + 1 == len(units):
            gate_a_halves.append(_sigmoid(proj(OFF_GA + D_MODEL // 2, D_MODEL // 2)))
        for pr in range(2):
            rows = slice(pr * BLOCK, (pr + 1) * BLOCK)
            den = o[rows, LANES:2 * LANES] + sink_terms[pr]
            attn_scr[r0:r0 + BLOCK, q_sl[pr]] = (o[rows, 0:LANES] / den).astype(BF16)

    for scr in kv_scrs:
        scr[0:BLOCK, :] = scr[ts:ts + BLOCK, :]

    attn_out = jnp.dot(attn_scr[...], w_ao_ref[...], preferred_element_type=F32)
    merged = merged + jnp.concatenate(gate_a_halves, axis=1) * attn_out
    out_ref[...] = x + jnp.dot(merged.astype(BF16), w_o_ref[...], preferred_element_type=F32)


def _ffn_kernel(x_ref, gf_ref, w_gu_ref, w_d_ref, out_ref, act_scr, *, final_norm):
    x = x_ref[...]
    h = x.astype(BF16)
    inv_rms = _inv_rms(x)
    for off, width in FF_CHUNKS:
        gate = inv_rms * jnp.dot(h, w_gu_ref[:, off:off + width], preferred_element_type=F32)
        up = inv_rms * jnp.dot(h, w_gu_ref[:, D_FF + off:D_FF + off + width],
                               preferred_element_type=F32)
        act_scr[:, off:off + width] = (gate * _sigmoid(gate) * up).astype(BF16)
    y = x + jnp.dot(act_scr[...], w_d_ref[...], preferred_element_type=F32)
    if final_norm:
        y = _rms_norm(y, gf_ref[...])
    out_ref[...] = y


def _resident(shape):
    return pl.BlockSpec(shape, lambda b, s: (0,) * len(shape), pipeline_mode=pl.Buffered(1))


def _rope_tables(seq):
    half = ROT_DIM // 2
    d = jnp.arange(LANES) % HEAD_DIM
    inv_freq = ROPE_THETA ** (-(2 * (d % half)).astype(F32) / ROT_DIM)
    ang = jnp.arange(seq, dtype=F32)[:, None] * inv_freq[None, :]
    cos, sin = jnp.cos(ang), jnp.sin(ang)
    c = jnp.where(d < ROT_DIM, cos, 1.0)
    s_next = jnp.where(d < half, -sin, 0.0)
    s_prev = jnp.where((d >= half) & (d < ROT_DIM), sin, 0.0)
    kind = jnp.arange(3)[:, None, None]
    return jnp.where(kind == 0, c, jnp.where(kind == 1, s_next, s_prev))


def _mixer(x, g, g_ffn, rope, conv_w, sinks, layer, w_in, w_co, w_ao, w_o, w_gu, w_d):
    b, s, d = x.shape
    ts = SEQ_TILE
    n_s = s // ts
    gu_rows = w_gu.shape[0] // (b * n_s)
    d_rows = 2 * w_d.shape[0] // (b * n_s)
    assert gu_rows % BF16_SUBLANES == 0 and d_rows % BF16_SUBLANES == 0
    gu_slab = pl.BlockSpec((gu_rows, w_gu.shape[1]), lambda i, j: (i * n_s + j, 0))
    d_slab = pl.BlockSpec((d_rows, w_d.shape[1]), lambda i, j: ((i * n_s + j) // 2, 0))
    tile = pl.BlockSpec((None, ts, d), lambda i, j: (i, j, 0))
    kv_scratch = pltpu.VMEM((BLOCK + ts, N_KV_HEADS * LANES), BF16)
    return pl.pallas_call(
        _mixer_kernel,
        grid=(b, n_s),
        in_specs=[
            pl.BlockSpec(memory_space=pltpu.SMEM),
            tile,
            _resident((d, 1)),
            pl.BlockSpec((gu_rows, 1), lambda i, j: (i * n_s + j, 0)),
            pl.BlockSpec((3, ts, LANES), lambda i, j: (0, j, 0)),
            pl.BlockSpec((None, CONV_K, d), lambda i, j: (layer, 0, 0),
                         pipeline_mode=pl.Buffered(1)),
            pl.BlockSpec(memory_space=pl.ANY),
            pl.BlockSpec(memory_space=pl.ANY),
            pl.BlockSpec(memory_space=pl.ANY),
            pl.BlockSpec(memory_space=pl.ANY),
            gu_slab,
            d_slab,
        ],
        out_specs=[tile, gu_slab, d_slab],
        out_shape=[jax.ShapeDtypeStruct(x.shape, x.dtype),
                   jax.ShapeDtypeStruct(w_gu.shape, BF16),
                   jax.ShapeDtypeStruct(w_d.shape, BF16)],
        scratch_shapes=[
            pltpu.VMEM(w_in.shape, BF16),
            pltpu.VMEM(w_co.shape, BF16),
            pltpu.VMEM(w_ao.shape, BF16),
            pltpu.VMEM(w_o.shape, BF16),
            pltpu.VMEM((STAGE_SLOTS, d, STAGE_COLS), F32),
            pltpu.SemaphoreType.DMA((STAGE_SLOTS,)),
            pltpu.VMEM((SUBLANES + ts, d), F32),
            pltpu.VMEM((ts, D_ATTN), BF16),
            kv_scratch, kv_scratch, kv_scratch, kv_scratch,
            pltpu.VMEM((ts, D_ATTN), BF16),
        ],
        compiler_params=pltpu.CompilerParams(
            dimension_semantics=("arbitrary", "arbitrary"),
            vmem_limit_bytes=VMEM_LIMIT_BYTES),
        name="mixer",
    )(sinks, x, g, g_ffn, rope, conv_w, w_in, w_co, w_ao, w_o, w_gu, w_d)


def _ffn(x, g_final, w_gu, w_d, final_norm):
    b, s, d = x.shape
    ts = FFN_SEQ_TILE
    tile = pl.BlockSpec((None, ts, d), lambda i, j: (i, j, 0))
    return pl.pallas_call(
        functools.partial(_ffn_kernel, final_norm=final_norm),
        grid=(b, s // ts),
        in_specs=[tile, _resident((1, d)), _resident(w_gu.shape), _resident(w_d.shape)],
        out_specs=tile,
        out_shape=jax.ShapeDtypeStruct(x.shape, x.dtype),
        scratch_shapes=[pltpu.VMEM((ts, D_FF), BF16)],
        compiler_params=pltpu.CompilerParams(
            dimension_semantics=("arbitrary", "arbitrary"),
            vmem_limit_bytes=VMEM_LIMIT_BYTES),
        name="ffn",
    )(x, g_final, w_gu, w_d)


def kernel(x, g_mix, w_in, conv_w, attn_sinks, w_conv_out, w_attn_out, w_o,
           g_ffn, w_gate_up, w_down, g_final):
    b, s, d = x.shape
    depth = w_in.shape[0]
    assert d == D_MODEL and s % SEQ_TILE == 0 and s % FFN_SEQ_TILE == 0
    assert w_in.shape[-1] == N_IN and w_gate_up.shape[-1] == 2 * D_FF
    assert N_IN % STAGE_COLS == 0 and d % STAGE_COLS == 0
    rope = _rope_tables(s)
    g_fin = g_final.reshape(1, d)
    for l in range(depth):
        x, w_gu, w_d = _mixer(
            x, g_mix[l].reshape(d, 1), g_ffn[l].reshape(d, 1), rope, conv_w, attn_sinks[l], l,
            w_in[l], w_conv_out[l], w_attn_out[l], w_o[l],
            w_gate_up[l], w_down[l])
        x = _ffn(x, g_fin, w_gu, w_d, final_norm=(l == depth - 1))
    return x
```

```python
import functools
import math

import jax
import jax.numpy as jnp
from jax import lax
from jax.experimental import pallas as pl
from jax.experimental.pallas import tpu as pltpu

D_MODEL = 1024
CONV_K = 3
HEAD_DIM = 64
N_HEADS = 16
N_KV_HEADS = 4
GROUP = N_HEADS // N_KV_HEADS
D_ATTN = N_HEADS * HEAD_DIM
D_KV = N_KV_HEADS * HEAD_DIM
WINDOW = 128
BLOCK = 128
ROT_DIM = HEAD_DIM // 4
ROPE_THETA = 500000.0
ATTN_SCALE = 1.0 / math.sqrt(HEAD_DIM)
LOG2E = math.log2(math.e)
NEG_INF = -1e30
D_FF = 2816
EPS = 1e-5

OFF_CB = 0
OFF_CC = OFF_CB + D_MODEL
OFF_CX = OFF_CC + D_MODEL
OFF_Q = OFF_CX + D_MODEL
OFF_K = OFF_Q + D_ATTN
OFF_V = OFF_K + D_KV
OFF_GC = OFF_V + D_KV
OFF_GA = OFF_GC + D_MODEL
N_IN = OFF_GA + D_MODEL

LANES = 128
SUBLANES = 8
BF16_SUBLANES = 16
SEQ_TILE = 512
FFN_SEQ_TILE = 1024
STAGE_COLS = 512
STAGE_SLOTS = 4
FF_CHUNKS = ((0, 1024), (1024, 1024), (2048, 768))
VMEM_LIMIT_BYTES = 62 * 1024 * 1024

F32 = jnp.float32
BF16 = jnp.bfloat16

assert WINDOW == BLOCK and 2 * HEAD_DIM == LANES and GROUP == 4


def _rms_norm(x, g):
    ms = jnp.mean(x * x, axis=-1, keepdims=True)
    return x * lax.rsqrt(ms + EPS) * g


def _sigmoid(x):
    return 1.0 / (1.0 + jnp.exp(-x))


def _rope(t, cos, sin_next, sin_prev):
    nxt = pltpu.roll(t, LANES - ROT_DIM // 2, 1)
    prv = pltpu.roll(t, ROT_DIM // 2, 1)
    return t * cos + nxt * sin_next + prv * sin_prev


def _split_head_pair(t, lane_lo, fill):
    sw = pltpu.roll(t, HEAD_DIM, 1)
    other = jnp.full_like(t, fill)
    even = (jnp.where(lane_lo, t, other), jnp.where(lane_lo, other, sw))
    odd = (jnp.where(lane_lo, sw, other), jnp.where(lane_lo, other, t))
    return even, odd


def _stage_weights(pairs, stage_ref, sem_ref):
    slots = stage_ref.shape[0]
    jobs = [(src, dst, c * STAGE_COLS)
            for src, dst in pairs for c in range(src.shape[1] // STAGE_COLS)]

    def slab_copy(n):
        src, _, col = jobs[n]
        return pltpu.make_async_copy(
            src.at[:, pl.ds(col, STAGE_COLS)], stage_ref.at[n % slots], sem_ref.at[n % slots])

    for n in range(min(slots - 1, len(jobs))):
        slab_copy(n).start()
    for n, (_, dst, col) in enumerate(jobs):
        if n + slots - 1 < len(jobs):
            slab_copy(n + slots - 1).start()
        slab_copy(n).wait()
        dst[:, col:col + STAGE_COLS] = stage_ref[n % slots].astype(BF16)


def _mixer_kernel(sinks_ref, x_ref, g_ref, rope_ref, convw_ref, w_in_hbm,
                  w_co_hbm, w_ao_hbm, w_o_hbm, w_gu_f32_ref, w_d_f32_ref,
                  out_ref, w_gu_bf16_ref, w_d_bf16_ref,
                  w_in_ref, w_co_ref, w_ao_ref, w_o_ref, stage_ref, stage_sem,
                  u_scr, q_scr, klo_scr, khi_scr, vlo_scr, vhi_scr, attn_scr):
    ts = x_ref.shape[0]
    n_blk = ts // BLOCK
    s_idx = pl.program_id(1)
    kv_scrs = (klo_scr, khi_scr, vlo_scr, vhi_scr)

    @pl.when((pl.program_id(0) == 0) & (s_idx == 0))
    def _():
        _stage_weights(((w_in_hbm, w_in_ref), (w_co_hbm, w_co_ref),
                        (w_ao_hbm, w_ao_ref), (w_o_hbm, w_o_ref)), stage_ref, stage_sem)

    @pl.when(s_idx == 0)
    def _():
        u_scr[0:SUBLANES, :] = jnp.zeros((SUBLANES, D_MODEL), F32)
        for scr in kv_scrs:
            scr[0:BLOCK, :] = jnp.zeros((BLOCK, N_KV_HEADS * LANES), BF16)

    w_gu_bf16_ref[...] = w_gu_f32_ref[...].astype(BF16)
    w_d_bf16_ref[...] = w_d_f32_ref[...].astype(BF16)

    x = x_ref[...]
    h = _rms_norm(x, g_ref[...]).astype(BF16)

    def proj(off, width):
        return jnp.dot(h, w_in_ref[:, off:off + width], preferred_element_type=F32)

    q_tabs = [rope_ref[i] * (ATTN_SCALE * LOG2E) for i in range(3)]
    q = proj(OFF_Q, D_ATTN)
    for t in range(D_ATTN // LANES):
        sl = slice(t * LANES, (t + 1) * LANES)
        q_scr[:, sl] = _rope(q[:, sl], *q_tabs).astype(BF16)
    lane_lo_t = lax.broadcasted_iota(jnp.int32, (ts, LANES), 1) < HEAD_DIM
    k = proj(OFF_K, D_KV)
    v = proj(OFF_V, D_KV)
    for t in range(D_KV // LANES):
        sl = slice(t * LANES, (t + 1) * LANES)
        k_t = _rope(k[:, sl], rope_ref[0], rope_ref[1], rope_ref[2])
        for src, fill, lo_scr, hi_scr in ((k_t, 0.0, klo_scr, khi_scr),
                                          (v[:, sl], 1.0, vlo_scr, vhi_scr)):
            for hd, (lo, hi) in zip((2 * t, 2 * t + 1), _split_head_pair(src, lane_lo_t, fill)):
                hsl = slice(hd * LANES, (hd + 1) * LANES)
                lo_scr[BLOCK:BLOCK + ts, hsl] = lo.astype(BF16)
                hi_scr[BLOCK:BLOCK + ts, hsl] = hi.astype(BF16)

    u = proj(OFF_CC, D_MODEL) * proj(OFF_CX, D_MODEL)
    u_scr[SUBLANES:SUBLANES + ts, :] = u
    conv = (convw_ref[0:1, :] * u_scr[SUBLANES - 2:SUBLANES - 2 + ts, :]
            + convw_ref[1:2, :] * u_scr[SUBLANES - 1:SUBLANES - 1 + ts, :]
            + convw_ref[2:3, :] * u)
    u_scr[0:SUBLANES, :] = u_scr[ts:ts + SUBLANES, :]
    conv_y = (proj(OFF_CB, D_MODEL) * conv).astype(BF16)
    gate_c = _sigmoid(proj(OFF_GC, D_MODEL))
    merged = gate_c * jnp.dot(conv_y, w_co_ref[...], preferred_element_type=F32)

    lane = lax.broadcasted_iota(jnp.int32, (BLOCK, LANES), 1)
    rowi = lax.broadcasted_iota(jnp.int32, (BLOCK, LANES), 0)
    from_cur = lane <= rowi
    lane_lo = lane < HEAD_DIM
    nt_dims = (((1,), (1,)), ((), ()))

    units = [(j, kv) for j in range(n_blk) for kv in range(N_KV_HEADS)]

    def q_slices(kv):
        return [slice((2 * kv + pr) * LANES, (2 * kv + pr + 1) * LANES) for pr in range(2)]

    def scores(j, kv):
        r0 = j * BLOCK
        kv_sl = slice(kv * LANES, (kv + 1) * LANES)
        q2 = jnp.concatenate([q_scr[r0:r0 + BLOCK, sl] for sl in q_slices(kv)], axis=0)
        k_rhs = jnp.concatenate([klo_scr[r0:r0 + 2 * BLOCK, kv_sl],
                                 khi_scr[r0:r0 + 2 * BLOCK, kv_sl]], axis=0)
        return lax.dot_general(q2, k_rhs, nt_dims, preferred_element_type=F32)

    s_next = scores(*units[0])
    for n, (j, kv) in enumerate(units):
        r0 = j * BLOCK
        kv_sl = slice(kv * LANES, (kv + 1) * LANES)
        q_sl = q_slices(kv)
        s = s_next
        if n + 1 < len(units):
            s_next = scores(*units[n + 1])
        else:
            gate_a_halves = [_sigmoid(proj(OFF_GA, D_MODEL // 2))]
        p_heads, sink_terms = [[], []], []
        for pr in range(2):
            rows = slice(pr * BLOCK, (pr + 1) * BLOCK)
            e_sink = []
            for half in range(2):
                hd = kv * GROUP + 2 * pr + half
                s_prev = s[rows, (2 * half) * BLOCK:(2 * half + 1) * BLOCK]
                s_cur = s[rows, (2 * half + 1) * BLOCK:(2 * half + 2) * BLOCK]
                if j == 0:
                    s_prev = jnp.where(s_idx > 0, s_prev, NEG_INF)
                t = jnp.where(from_cur, s_cur, s_prev)
                m = jnp.max(t, axis=-1, keepdims=True)
                p = jnp.exp2(t - m).astype(BF16)
                zero = jnp.zeros_like(p)
                p_heads[half].append(jnp.concatenate(
                    [jnp.where(from_cur, zero, p), jnp.where(from_cur, p, zero)], axis=1))
                e_sink.append(jnp.exp2(sinks_ref[hd] * LOG2E - m))
            sink_terms.append(jnp.where(lane_lo, e_sink[0], e_sink[1]))
        keys = slice(r0, r0 + 2 * BLOCK)
        o_even = jnp.dot(jnp.concatenate(p_heads[0], axis=0), vlo_scr[keys, kv_sl],
                         preferred_element_type=F32)
        o_odd = jnp.dot(jnp.concatenate(p_heads[1], axis=0), vhi_scr[keys, kv_sl],
                        preferred_element_type=F32)
        if n + 1 == len(units):
            gate_a_halves.append(_sigmoid(proj(OFF_GA + D_MODEL // 2, D_MODEL // 2)))
        for pr in range(2):
            rows = slice(pr * BLOCK, (pr + 1) * BLOCK)
            num = jnp.where(lane_lo, o_even[rows], o_odd[rows])
            den = pltpu.roll(jnp.where(lane_lo, o_odd[rows], o_even[rows]), HEAD_DIM, 1)
            attn_scr[r0:r0 + BLOCK, q_sl[pr]] = (num / (den + sink_terms[pr])).astype(BF16)

    for scr in kv_scrs:
        scr[0:BLOCK, :] = scr[ts:ts + BLOCK, :]

    attn_out = jnp.dot(attn_scr[...], w_ao_ref[...], preferred_element_type=F32)
    merged = merged + jnp.concatenate(gate_a_halves, axis=1) * attn_out
    out_ref[...] = x + jnp.dot(merged.astype(BF16), w_o_ref[...], preferred_element_type=F32)


def _ffn_kernel(x_ref, g_ref, gf_ref, w_gu_ref, w_d_ref, out_ref, act_scr, *, final_norm):
    x = x_ref[...]
    h = _rms_norm(x, g_ref[...]).astype(BF16)
    for off, width in FF_CHUNKS:
        gate = jnp.dot(h, w_gu_ref[:, off:off + width], preferred_element_type=F32)
        up = jnp.dot(h, w_gu_ref[:, D_FF + off:D_FF + off + width], preferred_element_type=F32)
        act_scr[:, off:off + width] = (gate * _sigmoid(gate) * up).astype(BF16)
    y = x + jnp.dot(act_scr[...], w_d_ref[...], preferred_element_type=F32)
    if final_norm:
        y = _rms_norm(y, gf_ref[...])
    out_ref[...] = y


def _resident(shape):
    return pl.BlockSpec(shape, lambda b, s: (0,) * len(shape), pipeline_mode=pl.Buffered(1))


def _rope_tables(seq):
    half = ROT_DIM // 2
    d = jnp.arange(LANES) % HEAD_DIM
    inv_freq = ROPE_THETA ** (-(2 * (d % half)).astype(F32) / ROT_DIM)
    ang = jnp.arange(seq, dtype=F32)[:, None] * inv_freq[None, :]
    cos, sin = jnp.cos(ang), jnp.sin(ang)
    c = jnp.where(d < ROT_DIM, cos, 1.0)
    s_next = jnp.where(d < half, -sin, 0.0)
    s_prev = jnp.where((d >= half) & (d < ROT_DIM), sin, 0.0)
    kind = jnp.arange(3)[:, None, None]
    return jnp.where(kind == 0, c, jnp.where(kind == 1, s_next, s_prev))


def _mixer(x, g, rope, conv_w, sinks, layer, w_in, w_co, w_ao, w_o, w_gu, w_d):
    b, s, d = x.shape
    ts = SEQ_TILE
    n_s = s // ts
    gu_rows = w_gu.shape[0] // (b * n_s)
    d_rows = 2 * w_d.shape[0] // (b * n_s)
    assert gu_rows % BF16_SUBLANES == 0 and d_rows % BF16_SUBLANES == 0
    gu_slab = pl.BlockSpec((gu_rows, w_gu.shape[1]), lambda i, j: (i * n_s + j, 0))
    d_slab = pl.BlockSpec((d_rows, w_d.shape[1]), lambda i, j: ((i * n_s + j) // 2, 0))
    tile = pl.BlockSpec((None, ts, d), lambda i, j: (i, j, 0))
    kv_scratch = pltpu.VMEM((BLOCK + ts, N_KV_HEADS * LANES), BF16)
    return pl.pallas_call(
        _mixer_kernel,
        grid=(b, n_s),
        in_specs=[
            pl.BlockSpec(memory_space=pltpu.SMEM),
            tile,
            _resident((1, d)),
            pl.BlockSpec((3, ts, LANES), lambda i, j: (0, j, 0)),
            pl.BlockSpec((None, CONV_K, d), lambda i, j: (layer, 0, 0),
                         pipeline_mode=pl.Buffered(1)),
            pl.BlockSpec(memory_space=pl.ANY),
            pl.BlockSpec(memory_space=pl.ANY),
            pl.BlockSpec(memory_space=pl.ANY),
            pl.BlockSpec(memory_space=pl.ANY),
            gu_slab,
            d_slab,
        ],
        out_specs=[tile, gu_slab, d_slab],
        out_shape=[jax.ShapeDtypeStruct(x.shape, x.dtype),
                   jax.ShapeDtypeStruct(w_gu.shape, BF16),
                   jax.ShapeDtypeStruct(w_d.shape, BF16)],
        scratch_shapes=[
            pltpu.VMEM(w_in.shape, BF16),
            pltpu.VMEM(w_co.shape, BF16),
            pltpu.VMEM(w_ao.shape, BF16),
            pltpu.VMEM(w_o.shape, BF16),
            pltpu.VMEM((STAGE_SLOTS, d, STAGE_COLS), F32),
            pltpu.SemaphoreType.DMA((STAGE_SLOTS,)),
            pltpu.VMEM((SUBLANES + ts, d), F32),
            pltpu.VMEM((ts, D_ATTN), BF16),
            kv_scratch, kv_scratch, kv_scratch, kv_scratch,
            pltpu.VMEM((ts, D_ATTN), BF16),
        ],
        compiler_params=pltpu.CompilerParams(
            dimension_semantics=("arbitrary", "arbitrary"),
            vmem_limit_bytes=VMEM_LIMIT_BYTES),
        name="mixer",
    )(sinks, x, g, rope, conv_w, w_in, w_co, w_ao, w_o, w_gu, w_d)


def _ffn(x, g, g_final, w_gu, w_d, final_norm):
    b, s, d = x.shape
    ts = FFN_SEQ_TILE
    tile = pl.BlockSpec((None, ts, d), lambda i, j: (i, j, 0))
    return pl.pallas_call(
        functools.partial(_ffn_kernel, final_norm=final_norm),
        grid=(b, s // ts),
        in_specs=[tile, _resident((1, d)), _resident((1, d)),
                  _resident(w_gu.shape), _resident(w_d.shape)],
        out_specs=tile,
        out_shape=jax.ShapeDtypeStruct(x.shape, x.dtype),
        scratch_shapes=[pltpu.VMEM((ts, D_FF), BF16)],
        compiler_params=pltpu.CompilerParams(
            dimension_semantics=("arbitrary", "arbitrary"),
            vmem_limit_bytes=VMEM_LIMIT_BYTES),
        name="ffn",
    )(x, g, g_final, w_gu, w_d)


def kernel(x, g_mix, w_in, conv_w, attn_sinks, w_conv_out, w_attn_out, w_o,
           g_ffn, w_gate_up, w_down, g_final):
    b, s, d = x.shape
    depth = w_in.shape[0]
    assert d == D_MODEL and s % SEQ_TILE == 0 and s % FFN_SEQ_TILE == 0
    assert w_in.shape[-1] == N_IN and w_gate_up.shape[-1] == 2 * D_FF
    assert N_IN % STAGE_COLS == 0 and d % STAGE_COLS == 0
    rope = _rope_tables(s)
    g_fin = g_final.reshape(1, d)
    for l in range(depth):
        x, w_gu, w_d = _mixer(
            x, g_mix[l].reshape(1, d), rope, conv_w, attn_sinks[l], l,
            w_in[l], w_conv_out[l], w_attn_out[l], w_o[l],
            w_gate_up[l], w_down[l])
        x = _ffn(x, g_ffn[l].reshape(1, d), g_fin, w_gu, w_d, final_norm=(l == depth - 1))
    return x
```

```python
import functools
import math

import jax
import jax.numpy as jnp
from jax import lax
from jax.experimental import pallas as pl
from jax.experimental.pallas import tpu as pltpu

D_MODEL = 1024
CONV_K = 3
HEAD_DIM = 64
N_HEADS = 16
N_KV_HEADS = 4
GROUP = N_HEADS // N_KV_HEADS
D_ATTN = N_HEADS * HEAD_DIM
D_KV = N_KV_HEADS * HEAD_DIM
WINDOW = 128
BLOCK = 128
ROT_DIM = HEAD_DIM // 4
ROPE_THETA = 500000.0
ATTN_SCALE = 1.0 / math.sqrt(HEAD_DIM)
LOG2E = math.log2(math.e)
NEG_INF = -1e30
D_FF = 2816
EPS = 1e-5

OFF_CB = 0
OFF_CC = OFF_CB + D_MODEL
OFF_CX = OFF_CC + D_MODEL
OFF_Q = OFF_CX + D_MODEL
OFF_K = OFF_Q + D_ATTN
OFF_V = OFF_K + D_KV
OFF_GC = OFF_V + D_KV
OFF_GA = OFF_GC + D_MODEL
N_IN = OFF_GA + D_MODEL

LANES = 128
SUBLANES = 8
BF16_SUBLANES = 16
SEQ_TILE = 512
FFN_SEQ_TILE = 1024
STAGE_COLS = 512
STAGE_SLOTS = 4
FF_CHUNKS = ((0, 1024), (1024, 1024), (2048, 768))
VMEM_LIMIT_BYTES = 62 * 1024 * 1024

F32 = jnp.float32
BF16 = jnp.bfloat16

assert WINDOW == BLOCK and 2 * HEAD_DIM == LANES and GROUP == 4


def _rms_norm(x, g):
    ms = jnp.mean(x * x, axis=-1, keepdims=True)
    return x * lax.rsqrt(ms + EPS) * g


def _sigmoid(x):
    return 1.0 / (1.0 + jnp.exp(-x))


def _rope(t, cos, sin_next, sin_prev):
    nxt = pltpu.roll(t, LANES - ROT_DIM // 2, 1)
    prv = pltpu.roll(t, ROT_DIM // 2, 1)
    return t * cos + nxt * sin_next + prv * sin_prev


def _split_head_pair(t, lane_lo):
    sw = pltpu.roll(t, HEAD_DIM, 1)
    zero = jnp.zeros_like(t)
    even = (jnp.where(lane_lo, t, zero), jnp.where(lane_lo, zero, sw))
    odd = (jnp.where(lane_lo, sw, zero), jnp.where(lane_lo, zero, t))
    return even, odd


def _stage_weights(pairs, stage_ref, sem_ref):
    slots = stage_ref.shape[0]
    jobs = [(src, dst, c * STAGE_COLS)
            for src, dst in pairs for c in range(src.shape[1] // STAGE_COLS)]

    def slab_copy(n):
        src, _, col = jobs[n]
        return pltpu.make_async_copy(
            src.at[:, pl.ds(col, STAGE_COLS)], stage_ref.at[n % slots], sem_ref.at[n % slots])

    for n in range(min(slots - 1, len(jobs))):
        slab_copy(n).start()
    for n, (_, dst, col) in enumerate(jobs):
        if n + slots - 1 < len(jobs):
            slab_copy(n + slots - 1).start()
        slab_copy(n).wait()
        dst[:, col:col + STAGE_COLS] = stage_ref[n % slots].astype(BF16)


def _mixer_kernel(sinks_ref, x_ref, g_ref, rope_ref, convw_ref, w_in_hbm,
                  w_co_hbm, w_ao_hbm, w_o_hbm, w_gu_f32_ref, w_d_f32_ref,
                  out_ref, w_gu_bf16_ref, w_d_bf16_ref,
                  w_in_ref, w_co_ref, w_ao_ref, w_o_ref, stage_ref, stage_sem,
                  u_scr, q_scr, klo_scr, khi_scr, vlo_scr, vhi_scr, attn_scr):
    ts = x_ref.shape[0]
    n_blk = ts // BLOCK
    s_idx = pl.program_id(1)
    kv_scrs = (klo_scr, khi_scr, vlo_scr, vhi_scr)

    @pl.when((pl.program_id(0) == 0) & (s_idx == 0))
    def _():
        _stage_weights(((w_in_hbm, w_in_ref), (w_co_hbm, w_co_ref),
                        (w_ao_hbm, w_ao_ref), (w_o_hbm, w_o_ref)), stage_ref, stage_sem)

    @pl.when(s_idx == 0)
    def _():
        u_scr[0:SUBLANES, :] = jnp.zeros((SUBLANES, D_MODEL), F32)
        for scr in kv_scrs:
            scr[0:BLOCK, :] = jnp.zeros((BLOCK, N_KV_HEADS * LANES), BF16)

    w_gu_bf16_ref[...] = w_gu_f32_ref[...].astype(BF16)
    w_d_bf16_ref[...] = w_d_f32_ref[...].astype(BF16)

    x = x_ref[...]
    h = _rms_norm(x, g_ref[...]).astype(BF16)

    def proj(off, width):
        return jnp.dot(h, w_in_ref[:, off:off + width], preferred_element_type=F32)

    q_tabs = [rope_ref[i] * (ATTN_SCALE * LOG2E) for i in range(3)]
    q = proj(OFF_Q, D_ATTN)
    for t in range(D_ATTN // LANES):
        sl = slice(t * LANES, (t + 1) * LANES)
        q_scr[:, sl] = _rope(q[:, sl], *q_tabs).astype(BF16)
    lane_lo_t = lax.broadcasted_iota(jnp.int32, (ts, LANES), 1) < HEAD_DIM
    k = proj(OFF_K, D_KV)
    v = proj(OFF_V, D_KV)
    for t in range(D_KV // LANES):
        sl = slice(t * LANES, (t + 1) * LANES)
        k_t = _rope(k[:, sl], rope_ref[0], rope_ref[1], rope_ref[2])
        for src, lo_scr, hi_scr in ((k_t, klo_scr, khi_scr), (v[:, sl], vlo_scr, vhi_scr)):
            for hd, (lo, hi) in zip((2 * t, 2 * t + 1), _split_head_pair(src, lane_lo_t)):
                hsl = slice(hd * LANES, (hd + 1) * LANES)
                lo_scr[BLOCK:BLOCK + ts, hsl] = lo.astype(BF16)
                hi_scr[BLOCK:BLOCK + ts, hsl] = hi.astype(BF16)

    u = proj(OFF_CC, D_MODEL) * proj(OFF_CX, D_MODEL)
    u_scr[SUBLANES:SUBLANES + ts, :] = u
    conv = (convw_ref[0:1, :] * u_scr[SUBLANES - 2:SUBLANES - 2 + ts, :]
            + convw_ref[1:2, :] * u_scr[SUBLANES - 1:SUBLANES - 1 + ts, :]
            + convw_ref[2:3, :] * u)
    u_scr[0:SUBLANES, :] = u_scr[ts:ts + SUBLANES, :]
    conv_y = (proj(OFF_CB, D_MODEL) * conv).astype(BF16)
    gate_c = _sigmoid(proj(OFF_GC, D_MODEL))
    merged = gate_c * jnp.dot(conv_y, w_co_ref[...], preferred_element_type=F32)

    lane = lax.broadcasted_iota(jnp.int32, (BLOCK, LANES), 1)
    rowi = lax.broadcasted_iota(jnp.int32, (BLOCK, LANES), 0)
    from_cur = lane <= rowi
    lane_lo = lane < HEAD_DIM
    ones_tile = jnp.ones((2 * BLOCK, LANES), BF16)
    nt_dims = (((1,), (1,)), ((), ()))

    units = [(j, kv) for j in range(n_blk) for kv in range(N_KV_HEADS)]

    def q_slices(kv):
        return [slice((2 * kv + pr) * LANES, (2 * kv + pr + 1) * LANES) for pr in range(2)]

    def scores(j, kv):
        r0 = j * BLOCK
        kv_sl = slice(kv * LANES, (kv + 1) * LANES)
        q2 = jnp.concatenate([q_scr[r0:r0 + BLOCK, sl] for sl in q_slices(kv)], axis=0)
        k_rhs = jnp.concatenate([klo_scr[r0:r0 + 2 * BLOCK, kv_sl],
                                 khi_scr[r0:r0 + 2 * BLOCK, kv_sl]], axis=0)
        return lax.dot_general(q2, k_rhs, nt_dims, preferred_element_type=F32)

    s_next = scores(*units[0])
    for n, (j, kv) in enumerate(units):
        r0 = j * BLOCK
        kv_sl = slice(kv * LANES, (kv + 1) * LANES)
        q_sl = q_slices(kv)
        s = s_next
        if n + 1 < len(units):
            s_next = scores(*units[n + 1])
        else:
            gate_a_halves = [_sigmoid(proj(OFF_GA, D_MODEL // 2))]
        v_rhs = jnp.concatenate([vlo_scr[r0:r0 + 2 * BLOCK, kv_sl], ones_tile], axis=1)
        p_heads, sink_terms = [], []
        for pr in range(2):
            rows = slice(pr * BLOCK, (pr + 1) * BLOCK)
            e_sink = []
            for half in range(2):
                hd = kv * GROUP + 2 * pr + half
                s_prev = s[rows, (2 * half) * BLOCK:(2 * half + 1) * BLOCK]
                s_cur = s[rows, (2 * half + 1) * BLOCK:(2 * half + 2) * BLOCK]
                if j == 0:
                    s_prev = jnp.where(s_idx > 0, s_prev, NEG_INF)
                t = jnp.where(from_cur, s_cur, s_prev)
                m = jnp.max(t, axis=-1, keepdims=True)
                p = jnp.exp2(t - m).astype(BF16)
                zero = jnp.zeros_like(p)
                p_heads.append(jnp.concatenate(
                    [jnp.where(from_cur, zero, p), jnp.where(from_cur, p, zero)], axis=1))
                e_sink.append(jnp.exp2(sinks_ref[hd] * LOG2E - m))
            sink_terms.append(jnp.where(lane_lo, e_sink[0], e_sink[1]))
        o = jnp.dot(jnp.concatenate(p_heads, axis=0), v_rhs, preferred_element_type=F32)
        if n + 1 == len(units):
            gate_a_halves.append(_sigmoid(proj(OFF_GA + D_MODEL // 2, D_MODEL // 2)))
        for pr in range(2):
            even = slice(2 * pr * BLOCK, (2 * pr + 1) * BLOCK)
            odd = slice((2 * pr + 1) * BLOCK, (2 * pr + 2) * BLOCK)
            num = jnp.where(lane_lo, o[even, 0:LANES], pltpu.roll(o[odd, 0:LANES], HEAD_DIM, 1))
            den = jnp.where(lane_lo, o[even, LANES:2 * LANES], o[odd, LANES:2 * LANES])
            attn_scr[r0:r0 + BLOCK, q_sl[pr]] = (num / (den + sink_terms[pr])).astype(BF16)

    for scr in kv_scrs:
        scr[0:BLOCK, :] = scr[ts:ts + BLOCK, :]

    attn_out = jnp.dot(attn_scr[...], w_ao_ref[...], preferred_element_type=F32)
    merged = merged + jnp.concatenate(gate_a_halves, axis=1) * attn_out
    out_ref[...] = x + jnp.dot(merged.astype(BF16), w_o_ref[...], preferred_element_type=F32)


def _ffn_kernel(x_ref, g_ref, gf_ref, w_gu_ref, w_d_ref, out_ref, act_scr, *, final_norm):
    x = x_ref[...]
    h = _rms_norm(x, g_ref[...]).astype(BF16)
    for off, width in FF_CHUNKS:
        gate = jnp.dot(h, w_gu_ref[:, off:off + width], preferred_element_type=F32)
        up = jnp.dot(h, w_gu_ref[:, D_FF + off:D_FF + off + width], preferred_element_type=F32)
        act_scr[:, off:off + width] = (gate * _sigmoid(gate) * up).astype(BF16)
    y = x + jnp.dot(act_scr[...], w_d_ref[...], preferred_element_type=F32)
    if final_norm:
        y = _rms_norm(y, gf_ref[...])
    out_ref[...] = y


def _resident(shape):
    return pl.BlockSpec(shape, lambda b, s: (0,) * len(shape), pipeline_mode=pl.Buffered(1))


def _rope_tables(seq):
    half = ROT_DIM // 2
    d = jnp.arange(LANES) % HEAD_DIM
    inv_freq = ROPE_THETA ** (-(2 * (d % half)).astype(F32) / ROT_DIM)
    ang = jnp.arange(seq, dtype=F32)[:, None] * inv_freq[None, :]
    cos, sin = jnp.cos(ang), jnp.sin(ang)
    c = jnp.where(d < ROT_DIM, cos, 1.0)
    s_next = jnp.where(d < half, -sin, 0.0)
    s_prev = jnp.where((d >= half) & (d < ROT_DIM), sin, 0.0)
    kind = jnp.arange(3)[:, None, None]
    return jnp.where(kind == 0, c, jnp.where(kind == 1, s_next, s_prev))


def _mixer(x, g, rope, conv_w, sinks, layer, w_in, w_co, w_ao, w_o, w_gu, w_d):
    b, s, d = x.shape
    ts = SEQ_TILE
    n_s = s // ts
    gu_rows = w_gu.shape[0] // (b * n_s)
    d_rows = 2 * w_d.shape[0] // (b * n_s)
    assert gu_rows % BF16_SUBLANES == 0 and d_rows % BF16_SUBLANES == 0
    gu_slab = pl.BlockSpec((gu_rows, w_gu.shape[1]), lambda i, j: (i * n_s + j, 0))
    d_slab = pl.BlockSpec((d_rows, w_d.shape[1]), lambda i, j: ((i * n_s + j) // 2, 0))
    tile = pl.BlockSpec((None, ts, d), lambda i, j: (i, j, 0))
    kv_scratch = pltpu.VMEM((BLOCK + ts, N_KV_HEADS * LANES), BF16)
    return pl.pallas_call(
        _mixer_kernel,
        grid=(b, n_s),
        in_specs=[
            pl.BlockSpec(memory_space=pltpu.SMEM),
            tile,
            _resident((1, d)),
            pl.BlockSpec((3, ts, LANES), lambda i, j: (0, j, 0)),
            pl.BlockSpec((None, CONV_K, d), lambda i, j: (layer, 0, 0),
                         pipeline_mode=pl.Buffered(1)),
            pl.BlockSpec(memory_space=pl.ANY),
            pl.BlockSpec(memory_space=pl.ANY),
            pl.BlockSpec(memory_space=pl.ANY),
            pl.BlockSpec(memory_space=pl.ANY),
            gu_slab,
            d_slab,
        ],
        out_specs=[tile, gu_slab, d_slab],
        out_shape=[jax.ShapeDtypeStruct(x.shape, x.dtype),
                   jax.ShapeDtypeStruct(w_gu.shape, BF16),
                   jax.ShapeDtypeStruct(w_d.shape, BF16)],
        scratch_shapes=[
            pltpu.VMEM(w_in.shape, BF16),
            pltpu.VMEM(w_co.shape, BF16),
            pltpu.VMEM(w_ao.shape, BF16),
            pltpu.VMEM(w_o.shape, BF16),
            pltpu.VMEM((STAGE_SLOTS, d, STAGE_COLS), F32),
            pltpu.SemaphoreType.DMA((STAGE_SLOTS,)),
            pltpu.VMEM((SUBLANES + ts, d), F32),
            pltpu.VMEM((ts, D_ATTN), BF16),
            kv_scratch, kv_scratch, kv_scratch, kv_scratch,
            pltpu.VMEM((ts, D_ATTN), BF16),
        ],
        compiler_params=pltpu.CompilerParams(
            dimension_semantics=("arbitrary", "arbitrary"),
            vmem_limit_bytes=VMEM_LIMIT_BYTES),
        name="mixer",
    )(sinks, x, g, rope, conv_w, w_in, w_co, w_ao, w_o, w_gu, w_d)


def _ffn(x, g, g_final, w_gu, w_d, final_norm):
    b, s, d = x.shape
    ts = FFN_SEQ_TILE
    tile = pl.BlockSpec((None, ts, d), lambda i, j: (i, j, 0))
    return pl.pallas_call(
        functools.partial(_ffn_kernel, final_norm=final_norm),
        grid=(b, s // ts),
        in_specs=[tile, _resident((1, d)), _resident((1, d)),
                  _resident(w_gu.shape), _resident(w_d.shape)],
        out_specs=tile,
        out_shape=jax.ShapeDtypeStruct(x.shape, x.dtype),
        scratch_shapes=[pltpu.VMEM((ts, D_FF), BF16)],
        compiler_params=pltpu.CompilerParams(
            dimension_semantics=("arbitrary", "arbitrary"),
            vmem_limit_bytes=VMEM_LIMIT_BYTES),
        name="ffn",
    )(x, g, g_final, w_gu, w_d)


def kernel(x, g_mix, w_in, conv_w, attn_sinks, w_conv_out, w_attn_out, w_o,
           g_ffn, w_gate_up, w_down, g_final):
    b, s, d = x.shape
    depth = w_in.shape[0]
    assert d == D_MODEL and s % SEQ_TILE == 0 and s % FFN_SEQ_TILE == 0
    assert w_in.shape[-1] == N_IN and w_gate_up.shape[-1] == 2 * D_FF
    assert N_IN % STAGE_COLS == 0 and d % STAGE_COLS == 0
    rope = _rope_tables(s)
    g_fin = g_final.reshape(1, d)
    for l in range(depth):
        x, w_gu, w_d = _mixer(
            x, g_mix[l].reshape(1, d), rope, conv_w, attn_sinks[l], l,
            w_in[l], w_conv_out[l], w_attn_out[l], w_o[l],
            w_gate_up[l], w_down[l])
        x = _ffn(x, g_ffn[l].reshape(1, d), g_fin, w_gu, w_d, final_norm=(l == depth - 1))
    return x
```

```python
import functools
import math

import jax
import jax.numpy as jnp
from jax import lax
from jax.experimental import pallas as pl
from jax.experimental.pallas import tpu as pltpu

D_MODEL = 1024
CONV_K = 3
HEAD_DIM = 64
N_HEADS = 16
N_KV_HEADS = 4
GROUP = N_HEADS // N_KV_HEADS
D_ATTN = N_HEADS * HEAD_DIM
D_KV = N_KV_HEADS * HEAD_DIM
WINDOW = 128
BLOCK = 128
ROT_DIM = HEAD_DIM // 4
ROPE_THETA = 500000.0
ATTN_SCALE = 1.0 / math.sqrt(HEAD_DIM)
LOG2E = math.log2(math.e)
NEG_INF = -1e30
D_FF = 2816
EPS = 1e-5

OFF_CB = 0
OFF_CC = OFF_CB + D_MODEL
OFF_CX = OFF_CC + D_MODEL
OFF_Q = OFF_CX + D_MODEL
OFF_K = OFF_Q + D_ATTN
OFF_V = OFF_K + D_KV
OFF_GC = OFF_V + D_KV
OFF_GA = OFF_GC + D_MODEL
N_IN = OFF_GA + D_MODEL

LANES = 128
SUBLANES = 8
BF16_SUBLANES = 16
SEQ_TILE = 512
FFN_SEQ_TILE = 1024
STAGE_COLS = 512
STAGE_SLOTS = 4
FF_CHUNKS = ((0, 1024), (1024, 1024), (2048, 768))
VMEM_LIMIT_BYTES = 58 * 1024 * 1024

F32 = jnp.float32
BF16 = jnp.bfloat16

assert WINDOW == BLOCK and 2 * HEAD_DIM == LANES and GROUP == 4


def _rms_norm(x, g):
    ms = jnp.mean(x * x, axis=-1, keepdims=True)
    return x * lax.rsqrt(ms + EPS) * g


def _sigmoid(x):
    return 1.0 / (1.0 + jnp.exp(-x))


def _rope(t, cos, sin_next, sin_prev):
    nxt = pltpu.roll(t, LANES - ROT_DIM // 2, 1)
    prv = pltpu.roll(t, ROT_DIM // 2, 1)
    return t * cos + nxt * sin_next + prv * sin_prev


def _split_head_pair(t, lane_lo):
    sw = pltpu.roll(t, HEAD_DIM, 1)
    zero = jnp.zeros_like(t)
    even = (jnp.where(lane_lo, t, zero), jnp.where(lane_lo, zero, sw))
    odd = (jnp.where(lane_lo, sw, zero), jnp.where(lane_lo, zero, t))
    return even, odd


def _stage_weights(pairs, stage_ref, sem_ref):
    slots = stage_ref.shape[0]
    jobs = [(src, dst, c * STAGE_COLS)
            for src, dst in pairs for c in range(src.shape[1] // STAGE_COLS)]

    def slab_copy(n):
        src, _, col = jobs[n]
        return pltpu.make_async_copy(
            src.at[:, pl.ds(col, STAGE_COLS)], stage_ref.at[n % slots], sem_ref.at[n % slots])

    for n in range(min(slots - 1, len(jobs))):
        slab_copy(n).start()
    for n, (_, dst, col) in enumerate(jobs):
        if n + slots - 1 < len(jobs):
            slab_copy(n + slots - 1).start()
        slab_copy(n).wait()
        dst[:, col:col + STAGE_COLS] = stage_ref[n % slots].astype(BF16)


def _mixer_kernel(sinks_ref, x_ref, g_ref, rope_ref, convw_ref, w_in_hbm,
                  w_co_hbm, w_ao_hbm, w_o_hbm, w_gu_f32_ref, w_d_f32_ref,
                  out_ref, w_gu_bf16_ref, w_d_bf16_ref,
                  w_in_ref, w_co_ref, w_ao_ref, w_o_ref, stage_ref, stage_sem,
                  u_scr, q_scr, klo_scr, khi_scr, vlo_scr, vhi_scr, attn_scr):
    ts = x_ref.shape[0]
    n_blk = ts // BLOCK
    s_idx = pl.program_id(1)
    kv_scrs = (klo_scr, khi_scr, vlo_scr, vhi_scr)

    @pl.when((pl.program_id(0) == 0) & (s_idx == 0))
    def _():
        _stage_weights(((w_in_hbm, w_in_ref), (w_co_hbm, w_co_ref),
                        (w_ao_hbm, w_ao_ref), (w_o_hbm, w_o_ref)), stage_ref, stage_sem)

    @pl.when(s_idx == 0)
    def _():
        u_scr[0:SUBLANES, :] = jnp.zeros((SUBLANES, D_MODEL), F32)
        for scr in kv_scrs:
            scr[0:BLOCK, :] = jnp.zeros((BLOCK, N_KV_HEADS * LANES), BF16)

    w_gu_bf16_ref[...] = w_gu_f32_ref[...].astype(BF16)
    w_d_bf16_ref[...] = w_d_f32_ref[...].astype(BF16)

    x = x_ref[...]
    h = _rms_norm(x, g_ref[...]).astype(BF16)

    def proj(off, width):
        return jnp.dot(h, w_in_ref[:, off:off + width], preferred_element_type=F32)

    q_tabs = [rope_ref[i] * (ATTN_SCALE * LOG2E) for i in range(3)]
    q = proj(OFF_Q, D_ATTN)
    for t in range(D_ATTN // LANES):
        sl = slice(t * LANES, (t + 1) * LANES)
        q_scr[:, sl] = _rope(q[:, sl], *q_tabs).astype(BF16)
    lane_lo_t = lax.broadcasted_iota(jnp.int32, (ts, LANES), 1) < HEAD_DIM
    k = proj(OFF_K, D_KV)
    v = proj(OFF_V, D_KV)
    for t in range(D_KV // LANES):
        sl = slice(t * LANES, (t + 1) * LANES)
        k_t = _rope(k[:, sl], rope_ref[0], rope_ref[1], rope_ref[2])
        for src, lo_scr, hi_scr in ((k_t, klo_scr, khi_scr), (v[:, sl], vlo_scr, vhi_scr)):
            for hd, (lo, hi) in zip((2 * t, 2 * t + 1), _split_head_pair(src, lane_lo_t)):
                hsl = slice(hd * LANES, (hd + 1) * LANES)
                lo_scr[BLOCK:BLOCK + ts, hsl] = lo.astype(BF16)
                hi_scr[BLOCK:BLOCK + ts, hsl] = hi.astype(BF16)

    u = proj(OFF_CC, D_MODEL) * proj(OFF_CX, D_MODEL)
    u_scr[SUBLANES:SUBLANES + ts, :] = u
    conv = (convw_ref[0:1, :] * u_scr[SUBLANES - 2:SUBLANES - 2 + ts, :]
            + convw_ref[1:2, :] * u_scr[SUBLANES - 1:SUBLANES - 1 + ts, :]
            + convw_ref[2:3, :] * u)
    u_scr[0:SUBLANES, :] = u_scr[ts:ts + SUBLANES, :]
    conv_y = (proj(OFF_CB, D_MODEL) * conv).astype(BF16)
    gate_c = _sigmoid(proj(OFF_GC, D_MODEL))

    lane = lax.broadcasted_iota(jnp.int32, (BLOCK, LANES), 1)
    rowi = lax.broadcasted_iota(jnp.int32, (BLOCK, LANES), 0)
    from_cur = lane <= rowi
    lane_lo = lane < HEAD_DIM
    lane2 = lax.broadcasted_iota(jnp.int32, (2 * BLOCK, LANES), 1)
    ones_lo = jnp.where(lane2 < HEAD_DIM, 1.0, 0.0).astype(BF16)
    ones_hi = jnp.where(lane2 < HEAD_DIM, 0.0, 1.0).astype(BF16)
    nt_dims = (((1,), (1,)), ((), ()))

    units = [(j, kv) for j in range(n_blk) for kv in range(N_KV_HEADS)]

    def q_slices(kv):
        return [slice((2 * kv + pr) * LANES, (2 * kv + pr + 1) * LANES) for pr in range(2)]

    def scores(j, kv):
        r0 = j * BLOCK
        kv_sl = slice(kv * LANES, (kv + 1) * LANES)
        q2 = jnp.concatenate([q_scr[r0:r0 + BLOCK, sl] for sl in q_slices(kv)], axis=0)
        k_rhs = jnp.concatenate([klo_scr[r0:r0 + 2 * BLOCK, kv_sl],
                                 khi_scr[r0:r0 + 2 * BLOCK, kv_sl]], axis=0)
        return lax.dot_general(q2, k_rhs, nt_dims, preferred_element_type=F32)

    s_next = scores(*units[0])
    for n, (j, kv) in enumerate(units):
        r0 = j * BLOCK
        kv_sl = slice(kv * LANES, (kv + 1) * LANES)
        q_sl = q_slices(kv)
        s = s_next
        if n + 1 < len(units):
            s_next = scores(*units[n + 1])
        else:
            gate_a_halves = [_sigmoid(proj(OFF_GA, D_MODEL // 2))]
        v_rhs = jnp.concatenate(
            [jnp.concatenate([vlo_scr[r0:r0 + 2 * BLOCK, kv_sl], ones_lo], axis=1),
             jnp.concatenate([vhi_scr[r0:r0 + 2 * BLOCK, kv_sl], ones_hi], axis=1)], axis=0)
        p_rows, sink_terms = [], []
        for pr in range(2):
            rows = slice(pr * BLOCK, (pr + 1) * BLOCK)
            p_tiles, e_sink = [], []
            for half in range(2):
                hd = kv * GROUP + 2 * pr + half
                s_prev = s[rows, (2 * half) * BLOCK:(2 * half + 1) * BLOCK]
                s_cur = s[rows, (2 * half + 1) * BLOCK:(2 * half + 2) * BLOCK]
                if j == 0:
                    s_prev = jnp.where(s_idx > 0, s_prev, NEG_INF)
                t = jnp.where(from_cur, s_cur, s_prev)
                m = jnp.max(t, axis=-1, keepdims=True)
                p = jnp.exp2(t - m).astype(BF16)
                zero = jnp.zeros_like(p)
                p_tiles += [jnp.where(from_cur, zero, p), jnp.where(from_cur, p, zero)]
                e_sink.append(jnp.exp2(sinks_ref[hd] * LOG2E - m))
            p_rows.append(jnp.concatenate(p_tiles, axis=1))
            sink_terms.append(jnp.where(lane_lo, e_sink[0], e_sink[1]))
        o = jnp.dot(jnp.concatenate(p_rows, axis=0), v_rhs, preferred_element_type=F32)
        if n + 1 == len(units):
            gate_a_halves.append(_sigmoid(proj(OFF_GA + D_MODEL // 2, D_MODEL // 2)))
        for pr in range(2):
            rows = slice(pr * BLOCK, (pr + 1) * BLOCK)
            den = o[rows, LANES:2 * LANES] + sink_terms[pr]
            attn_scr[r0:r0 + BLOCK, q_sl[pr]] = (o[rows, 0:LANES] / den).astype(BF16)

    for scr in kv_scrs:
        scr[0:BLOCK, :] = scr[ts:ts + BLOCK, :]

    merged = gate_c * jnp.dot(conv_y, w_co_ref[...], preferred_element_type=F32)
    attn_out = jnp.dot(attn_scr[...], w_ao_ref[...], preferred_element_type=F32)
    merged = merged + jnp.concatenate(gate_a_halves, axis=1) * attn_out
    out_ref[...] = x + jnp.dot(merged.astype(BF16), w_o_ref[...], preferred_element_type=F32)


def _ffn_kernel(x_ref, g_ref, gf_ref, w_gu_ref, w_d_ref, out_ref, act_scr, *, final_norm):
    x = x_ref[...]
    h = _rms_norm(x, g_ref[...]).astype(BF16)
    for off, width in FF_CHUNKS:
        gate = jnp.dot(h, w_gu_ref[:, off:off + width], preferred_element_type=F32)
        up = jnp.dot(h, w_gu_ref[:, D_FF + off:D_FF + off + width], preferred_element_type=F32)
        act_scr[:, off:off + width] = (gate * _sigmoid(gate) * up).astype(BF16)
    y = x + jnp.dot(act_scr[...], w_d_ref[...], preferred_element_type=F32)
    if final_norm:
        y = _rms_norm(y, gf_ref[...])
    out_ref[...] = y


def _resident(shape):
    return pl.BlockSpec(shape, lambda b, s: (0,) * len(shape), pipeline_mode=pl.Buffered(1))


def _rope_tables(seq):
    half = ROT_DIM // 2
    d = jnp.arange(LANES) % HEAD_DIM
    inv_freq = ROPE_THETA ** (-(2 * (d % half)).astype(F32) / ROT_DIM)
    ang = jnp.arange(seq, dtype=F32)[:, None] * inv_freq[None, :]
    cos, sin = jnp.cos(ang), jnp.sin(ang)
    c = jnp.where(d < ROT_DIM, cos, 1.0)
    s_next = jnp.where(d < half, -sin, 0.0)
    s_prev = jnp.where((d >= half) & (d < ROT_DIM), sin, 0.0)
    kind = jnp.arange(3)[:, None, None]
    return jnp.where(kind == 0, c, jnp.where(kind == 1, s_next, s_prev))


def _mixer(x, g, rope, conv_w, sinks, layer, w_in, w_co, w_ao, w_o, w_gu, w_d):
    b, s, d = x.shape
    ts = SEQ_TILE
    n_s = s // ts
    gu_rows = w_gu.shape[0] // (b * n_s)
    d_rows = 2 * w_d.shape[0] // (b * n_s)
    assert gu_rows % BF16_SUBLANES == 0 and d_rows % BF16_SUBLANES == 0
    gu_slab = pl.BlockSpec((gu_rows, w_gu.shape[1]), lambda i, j: (i * n_s + j, 0))
    d_slab = pl.BlockSpec((d_rows, w_d.shape[1]), lambda i, j: ((i * n_s + j) // 2, 0))
    tile = pl.BlockSpec((None, ts, d), lambda i, j: (i, j, 0))
    kv_scratch = pltpu.VMEM((BLOCK + ts, N_KV_HEADS * LANES), BF16)
    return pl.pallas_call(
        _mixer_kernel,
        grid=(b, n_s),
        in_specs=[
            pl.BlockSpec(memory_space=pltpu.SMEM),
            tile,
            _resident((1, d)),
            pl.BlockSpec((3, ts, LANES), lambda i, j: (0, j, 0)),
            pl.BlockSpec((None, CONV_K, d), lambda i, j: (layer, 0, 0),
                         pipeline_mode=pl.Buffered(1)),
            pl.BlockSpec(memory_space=pl.ANY),
            pl.BlockSpec(memory_space=pl.ANY),
            pl.BlockSpec(memory_space=pl.ANY),
            pl.BlockSpec(memory_space=pl.ANY),
            gu_slab,
            d_slab,
        ],
        out_specs=[tile, gu_slab, d_slab],
        out_shape=[jax.ShapeDtypeStruct(x.shape, x.dtype),
                   jax.ShapeDtypeStruct(w_gu.shape, BF16),
                   jax.ShapeDtypeStruct(w_d.shape, BF16)],
        scratch_shapes=[
            pltpu.VMEM(w_in.shape, BF16),
            pltpu.VMEM(w_co.shape, BF16),
            pltpu.VMEM(w_ao.shape, BF16),
            pltpu.VMEM(w_o.shape, BF16),
            pltpu.VMEM((STAGE_SLOTS, d, STAGE_COLS), F32),
            pltpu.SemaphoreType.DMA((STAGE_SLOTS,)),
            pltpu.VMEM((SUBLANES + ts, d), F32),
            pltpu.VMEM((ts, D_ATTN), BF16),
            kv_scratch, kv_scratch, kv_scratch, kv_scratch,
            pltpu.VMEM((ts, D_ATTN), BF16),
        ],
        compiler_params=pltpu.CompilerParams(
            dimension_semantics=("arbitrary", "arbitrary"),
            vmem_limit_bytes=VMEM_LIMIT_BYTES),
        name="mixer",
    )(sinks, x, g, rope, conv_w, w_in, w_co, w_ao, w_o, w_gu, w_d)


def _ffn(x, g, g_final, w_gu, w_d, final_norm):
    b, s, d = x.shape
    ts = FFN_SEQ_TILE
    tile = pl.BlockSpec((None, ts, d), lambda i, j: (i, j, 0))
    return pl.pallas_call(
        functools.partial(_ffn_kernel, final_norm=final_norm),
        grid=(b, s // ts),
        in_specs=[tile, _resident((1, d)), _resident((1, d)),
                  _resident(w_gu.shape), _resident(w_d.shape)],
        out_specs=tile,
        out_shape=jax.ShapeDtypeStruct(x.shape, x.dtype),
        scratch_shapes=[pltpu.VMEM((ts, D_FF), BF16)],
        compiler_params=pltpu.CompilerParams(
            dimension_semantics=("arbitrary", "arbitrary"),
            vmem_limit_bytes=VMEM_LIMIT_BYTES),
        name="ffn",
    )(x, g, g_final, w_gu, w_d)


def kernel(x, g_mix, w_in, conv_w, attn_sinks, w_conv_out, w_attn_out, w_o,
           g_ffn, w_gate_up, w_down, g_final):
    b, s, d = x.shape
    depth = w_in.shape[0]
    assert d == D_MODEL and s % SEQ_TILE == 0 and s % FFN_SEQ_TILE == 0
    assert w_in.shape[-1] == N_IN and w_gate_up.shape[-1] == 2 * D_FF
    assert N_IN % STAGE_COLS == 0 and d % STAGE_COLS == 0
    rope = _rope_tables(s)
    g_fin = g_final.reshape(1, d)
    for l in range(depth):
        x, w_gu, w_d = _mixer(
            x, g_mix[l].reshape(1, d), rope, conv_w, attn_sinks[l], l,
            w_in[l], w_conv_out[l], w_attn_out[l], w_o[l],
            w_gate_up[l], w_down[l])
        x = _ffn(x, g_ffn[l].reshape(1, d), g_fin, w_gu, w_d, final_norm=(l == depth - 1))
    return x
```

```python
import functools
import math

import jax
import jax.numpy as jnp
from jax import lax
from jax.experimental import pallas as pl
from jax.experimental.pallas import tpu as pltpu

D_MODEL = 1024
CONV_K = 3
HEAD_DIM = 64
N_HEADS = 16
N_KV_HEADS = 4
GROUP = N_HEADS // N_KV_HEADS
D_ATTN = N_HEADS * HEAD_DIM
D_KV = N_KV_HEADS * HEAD_DIM
WINDOW = 128
BLOCK = 128
ROT_DIM = HEAD_DIM // 4
ROPE_THETA = 500000.0
ATTN_SCALE = 1.0 / math.sqrt(HEAD_DIM)
LOG2E = math.log2(math.e)
NEG_INF = -1e30
D_FF = 2816
EPS = 1e-5

OFF_CB = 0
OFF_CC = OFF_CB + D_MODEL
OFF_CX = OFF_CC + D_MODEL
OFF_Q = OFF_CX + D_MODEL
OFF_K = OFF_Q + D_ATTN
OFF_V = OFF_K + D_KV
OFF_GC = OFF_V + D_KV
OFF_GA = OFF_GC + D_MODEL
N_IN = OFF_GA + D_MODEL

LANES = 128
SUBLANES = 8
BF16_SUBLANES = 16
SEQ_TILE = 512
FFN_SEQ_TILE = 1024
STAGE_COLS = 512
STAGE_SLOTS = 4
FF_CHUNKS = ((0, 1024), (1024, 1024), (2048, 768))
VMEM_LIMIT_BYTES = 58 * 1024 * 1024

F32 = jnp.float32
BF16 = jnp.bfloat16

assert WINDOW == BLOCK and 2 * HEAD_DIM == LANES and GROUP == 4


def _rms_norm(x, g):
    ms = jnp.mean(x * x, axis=-1, keepdims=True)
    return x * lax.rsqrt(ms + EPS) * g


def _sigmoid(x):
    return 1.0 / (1.0 + jnp.exp(-x))


def _rope(t, cos, sin_next, sin_prev):
    nxt = pltpu.roll(t, LANES - ROT_DIM // 2, 1)
    prv = pltpu.roll(t, ROT_DIM // 2, 1)
    return t * cos + nxt * sin_next + prv * sin_prev


def _split_head_pair(t, lane_lo):
    sw = pltpu.roll(t, HEAD_DIM, 1)
    zero = jnp.zeros_like(t)
    even = (jnp.where(lane_lo, t, zero), jnp.where(lane_lo, zero, sw))
    odd = (jnp.where(lane_lo, sw, zero), jnp.where(lane_lo, zero, t))
    return even, odd


def _stage_weights(pairs, stage_ref, sem_ref):
    slots = stage_ref.shape[0]
    jobs = [(src, dst, c * STAGE_COLS)
            for src, dst in pairs for c in range(src.shape[1] // STAGE_COLS)]

    def slab_copy(n):
        src, _, col = jobs[n]
        return pltpu.make_async_copy(
            src.at[:, pl.ds(col, STAGE_COLS)], stage_ref.at[n % slots], sem_ref.at[n % slots])

    for n in range(min(slots - 1, len(jobs))):
        slab_copy(n).start()
    for n, (_, dst, col) in enumerate(jobs):
        if n + slots - 1 < len(jobs):
            slab_copy(n + slots - 1).start()
        slab_copy(n).wait()
        dst[:, col:col + STAGE_COLS] = stage_ref[n % slots].astype(BF16)


def _mixer_kernel(sinks_ref, x_ref, g_ref, rope_ref, convw_ref, w_in_hbm,
                  w_co_hbm, w_ao_hbm, w_o_hbm, w_gu_f32_ref, w_d_f32_ref,
                  out_ref, w_gu_bf16_ref, w_d_bf16_ref,
                  w_in_ref, w_co_ref, w_ao_ref, w_o_ref, stage_ref, stage_sem,
                  u_scr, q_scr, klo_scr, khi_scr, vlo_scr, vhi_scr, attn_scr):
    ts = x_ref.shape[0]
    n_blk = ts // BLOCK
    s_idx = pl.program_id(1)
    kv_scrs = (klo_scr, khi_scr, vlo_scr, vhi_scr)

    @pl.when((pl.program_id(0) == 0) & (s_idx == 0))
    def _():
        _stage_weights(((w_in_hbm, w_in_ref), (w_co_hbm, w_co_ref),
                        (w_ao_hbm, w_ao_ref), (w_o_hbm, w_o_ref)), stage_ref, stage_sem)

    @pl.when(s_idx == 0)
    def _():
        u_scr[0:SUBLANES, :] = jnp.zeros((SUBLANES, D_MODEL), F32)
        for scr in kv_scrs:
            scr[0:BLOCK, :] = jnp.zeros((BLOCK, N_KV_HEADS * LANES), BF16)

    x = x_ref[...]
    h = _rms_norm(x, g_ref[...]).astype(BF16)

    def proj(off, width):
        return jnp.dot(h, w_in_ref[:, off:off + width], preferred_element_type=F32)

    q_tabs = [rope_ref[i] * (ATTN_SCALE * LOG2E) for i in range(3)]
    q = proj(OFF_Q, D_ATTN)
    for t in range(D_ATTN // LANES):
        sl = slice(t * LANES, (t + 1) * LANES)
        q_scr[:, sl] = _rope(q[:, sl], *q_tabs).astype(BF16)
    lane_lo_t = lax.broadcasted_iota(jnp.int32, (ts, LANES), 1) < HEAD_DIM
    k = proj(OFF_K, D_KV)
    v = proj(OFF_V, D_KV)
    for t in range(D_KV // LANES):
        sl = slice(t * LANES, (t + 1) * LANES)
        k_t = _rope(k[:, sl], rope_ref[0], rope_ref[1], rope_ref[2])
        for src, lo_scr, hi_scr in ((k_t, klo_scr, khi_scr), (v[:, sl], vlo_scr, vhi_scr)):
            for hd, (lo, hi) in zip((2 * t, 2 * t + 1), _split_head_pair(src, lane_lo_t)):
                hsl = slice(hd * LANES, (hd + 1) * LANES)
                lo_scr[BLOCK:BLOCK + ts, hsl] = lo.astype(BF16)
                hi_scr[BLOCK:BLOCK + ts, hsl] = hi.astype(BF16)

    u = proj(OFF_CC, D_MODEL) * proj(OFF_CX, D_MODEL)
    u_scr[SUBLANES:SUBLANES + ts, :] = u
    conv = (convw_ref[0:1, :] * u_scr[SUBLANES - 2:SUBLANES - 2 + ts, :]
            + convw_ref[1:2, :] * u_scr[SUBLANES - 1:SUBLANES - 1 + ts, :]
            + convw_ref[2:3, :] * u)
    u_scr[0:SUBLANES, :] = u_scr[ts:ts + SUBLANES, :]
    conv_y = (proj(OFF_CB, D_MODEL) * conv).astype(BF16)
    gate_c = _sigmoid(proj(OFF_GC, D_MODEL))
    merged = gate_c * jnp.dot(conv_y, w_co_ref[...], preferred_element_type=F32)

    lane = lax.broadcasted_iota(jnp.int32, (BLOCK, LANES), 1)
    rowi = lax.broadcasted_iota(jnp.int32, (BLOCK, LANES), 0)
    from_cur = lane <= rowi
    lane_lo = lane < HEAD_DIM
    lane2 = lax.broadcasted_iota(jnp.int32, (2 * BLOCK, LANES), 1)
    ones_lo = jnp.where(lane2 < HEAD_DIM, 1.0, 0.0).astype(BF16)
    ones_hi = jnp.where(lane2 < HEAD_DIM, 0.0, 1.0).astype(BF16)
    nt_dims = (((1,), (1,)), ((), ()))

    units = [(j, kv) for j in range(n_blk) for kv in range(N_KV_HEADS)]

    def q_slices(kv):
        return [slice((2 * kv + pr) * LANES, (2 * kv + pr + 1) * LANES) for pr in range(2)]

    def scores(j, kv):
        r0 = j * BLOCK
        kv_sl = slice(kv * LANES, (kv + 1) * LANES)
        q2 = jnp.concatenate([q_scr[r0:r0 + BLOCK, sl] for sl in q_slices(kv)], axis=0)
        k_rhs = jnp.concatenate([klo_scr[r0:r0 + 2 * BLOCK, kv_sl],
                                 khi_scr[r0:r0 + 2 * BLOCK, kv_sl]], axis=0)
        return lax.dot_general(q2, k_rhs, nt_dims, preferred_element_type=F32)

    s_next = scores(*units[0])
    for n, (j, kv) in enumerate(units):
        r0 = j * BLOCK
        kv_sl = slice(kv * LANES, (kv + 1) * LANES)
        q_sl = q_slices(kv)
        s = s_next
        if n + 1 < len(units):
            s_next = scores(*units[n + 1])
        else:
            gate_a_halves = [_sigmoid(proj(OFF_GA, D_MODEL // 2))]
        v_rhs = jnp.concatenate(
            [jnp.concatenate([vlo_scr[r0:r0 + 2 * BLOCK, kv_sl], ones_lo], axis=1),
             jnp.concatenate([vhi_scr[r0:r0 + 2 * BLOCK, kv_sl], ones_hi], axis=1)], axis=0)
        p_rows, sink_terms = [], []
        for pr in range(2):
            rows = slice(pr * BLOCK, (pr + 1) * BLOCK)
            p_tiles, e_sink = [], []
            for half in range(2):
                hd = kv * GROUP + 2 * pr + half
                s_prev = s[rows, (2 * half) * BLOCK:(2 * half + 1) * BLOCK]
                s_cur = s[rows, (2 * half + 1) * BLOCK:(2 * half + 2) * BLOCK]
                if j == 0:
                    s_prev = jnp.where(s_idx > 0, s_prev, NEG_INF)
                t = jnp.where(from_cur, s_cur, s_prev)
                m = jnp.max(t, axis=-1, keepdims=True)
                p = jnp.exp2(t - m).astype(BF16)
                zero = jnp.zeros_like(p)
                p_tiles += [jnp.where(from_cur, zero, p), jnp.where(from_cur, p, zero)]
                e_sink.append(jnp.exp2(sinks_ref[hd] * LOG2E - m))
            p_rows.append(jnp.concatenate(p_tiles, axis=1))
            sink_terms.append(jnp.where(lane_lo, e_sink[0], e_sink[1]))
        o = jnp.dot(jnp.concatenate(p_rows, axis=0), v_rhs, preferred_element_type=F32)
        if n + 1 == len(units):
            gate_a_halves.append(_sigmoid(proj(OFF_GA + D_MODEL // 2, D_MODEL // 2)))
        for pr in range(2):
            rows = slice(pr * BLOCK, (pr + 1) * BLOCK)
            den = o[rows, LANES:2 * LANES] + sink_terms[pr]
            attn_scr[r0:r0 + BLOCK, q_sl[pr]] = (o[rows, 0:LANES] / den).astype(BF16)

    for scr in kv_scrs:
        scr[0:BLOCK, :] = scr[ts:ts + BLOCK, :]

    w_gu_bf16_ref[...] = w_gu_f32_ref[...].astype(BF16)
    w_d_bf16_ref[...] = w_d_f32_ref[...].astype(BF16)

    attn_out = jnp.dot(attn_scr[...], w_ao_ref[...], preferred_element_type=F32)
    merged = merged + jnp.concatenate(gate_a_halves, axis=1) * attn_out
    out_ref[...] = x + jnp.dot(merged.astype(BF16), w_o_ref[...], preferred_element_type=F32)


def _ffn_kernel(x_ref, g_ref, gf_ref, w_gu_ref, w_d_ref, out_ref, act_scr, *, final_norm):
    x = x_ref[...]
    h = _rms_norm(x, g_ref[...]).astype(BF16)
    for off, width in FF_CHUNKS:
        gate = jnp.dot(h, w_gu_ref[:, off:off + width], preferred_element_type=F32)
        up = jnp.dot(h, w_gu_ref[:, D_FF + off:D_FF + off + width], preferred_element_type=F32)
        act_scr[:, off:off + width] = (gate * _sigmoid(gate) * up).astype(BF16)
    y = x + jnp.dot(act_scr[...], w_d_ref[...], preferred_element_type=F32)
    if final_norm:
        y = _rms_norm(y, gf_ref[...])
    out_ref[...] = y


def _resident(shape):
    return pl.BlockSpec(shape, lambda b, s: (0,) * len(shape), pipeline_mode=pl.Buffered(1))


def _rope_tables(seq):
    half = ROT_DIM // 2
    d = jnp.arange(LANES) % HEAD_DIM
    inv_freq = ROPE_THETA ** (-(2 * (d % half)).astype(F32) / ROT_DIM)
    ang = jnp.arange(seq, dtype=F32)[:, None] * inv_freq[None, :]
    cos, sin = jnp.cos(ang), jnp.sin(ang)
    c = jnp.where(d < ROT_DIM, cos, 1.0)
    s_next = jnp.where(d < half, -sin, 0.0)
    s_prev = jnp.where((d >= half) & (d < ROT_DIM), sin, 0.0)
    kind = jnp.arange(3)[:, None, None]
    return jnp.where(kind == 0, c, jnp.where(kind == 1, s_next, s_prev))


def _mixer(x, g, rope, conv_w, sinks, layer, w_in, w_co, w_ao, w_o, w_gu, w_d):
    b, s, d = x.shape
    ts = SEQ_TILE
    n_s = s // ts
    gu_rows = w_gu.shape[0] // (b * n_s)
    d_rows = 2 * w_d.shape[0] // (b * n_s)
    assert gu_rows % BF16_SUBLANES == 0 and d_rows % BF16_SUBLANES == 0
    gu_slab = pl.BlockSpec((gu_rows, w_gu.shape[1]), lambda i, j: (i * n_s + j, 0))
    d_slab = pl.BlockSpec((d_rows, w_d.shape[1]), lambda i, j: ((i * n_s + j) // 2, 0))
    tile = pl.BlockSpec((None, ts, d), lambda i, j: (i, j, 0))
    kv_scratch = pltpu.VMEM((BLOCK + ts, N_KV_HEADS * LANES), BF16)
    return pl.pallas_call(
        _mixer_kernel,
        grid=(b, n_s),
        in_specs=[
            pl.BlockSpec(memory_space=pltpu.SMEM),
            tile,
            _resident((1, d)),
            pl.BlockSpec((3, ts, LANES), lambda i, j: (0, j, 0)),
            pl.BlockSpec((None, CONV_K, d), lambda i, j: (layer, 0, 0),
                         pipeline_mode=pl.Buffered(1)),
            pl.BlockSpec(memory_space=pl.ANY),
            pl.BlockSpec(memory_space=pl.ANY),
            pl.BlockSpec(memory_space=pl.ANY),
            pl.BlockSpec(memory_space=pl.ANY),
            gu_slab,
            d_slab,
        ],
        out_specs=[tile, gu_slab, d_slab],
        out_shape=[jax.ShapeDtypeStruct(x.shape, x.dtype),
                   jax.ShapeDtypeStruct(w_gu.shape, BF16),
                   jax.ShapeDtypeStruct(w_d.shape, BF16)],
        scratch_shapes=[
            pltpu.VMEM(w_in.shape, BF16),
            pltpu.VMEM(w_co.shape, BF16),
            pltpu.VMEM(w_ao.shape, BF16),
            pltpu.VMEM(w_o.shape, BF16),
            pltpu.VMEM((STAGE_SLOTS, d, STAGE_COLS), F32),
            pltpu.SemaphoreType.DMA((STAGE_SLOTS,)),
            pltpu.VMEM((SUBLANES + ts, d), F32),
            pltpu.VMEM((ts, D_ATTN), BF16),
            kv_scratch, kv_scratch, kv_scratch, kv_scratch,
            pltpu.VMEM((ts, D_ATTN), BF16),
        ],
        compiler_params=pltpu.CompilerParams(
            dimension_semantics=("arbitrary", "arbitrary"),
            vmem_limit_bytes=VMEM_LIMIT_BYTES),
        name="mixer",
    )(sinks, x, g, rope, conv_w, w_in, w_co, w_ao, w_o, w_gu, w_d)


def _ffn(x, g, g_final, w_gu, w_d, final_norm):
    b, s, d = x.shape
    ts = FFN_SEQ_TILE
    tile = pl.BlockSpec((None, ts, d), lambda i, j: (i, j, 0))
    return pl.pallas_call(
        functools.partial(_ffn_kernel, final_norm=final_norm),
        grid=(b, s // ts),
        in_specs=[tile, _resident((1, d)), _resident((1, d)),
                  _resident(w_gu.shape), _resident(w_d.shape)],
        out_specs=tile,
        out_shape=jax.ShapeDtypeStruct(x.shape, x.dtype),
        scratch_shapes=[pltpu.VMEM((ts, D_FF), BF16)],
        compiler_params=pltpu.CompilerParams(
            dimension_semantics=("arbitrary", "arbitrary"),
            vmem_limit_bytes=VMEM_LIMIT_BYTES),
        name="ffn",
    )(x, g, g_final, w_gu, w_d)


def kernel(x, g_mix, w_in, conv_w, attn_sinks, w_conv_out, w_attn_out, w_o,
           g_ffn, w_gate_up, w_down, g_final):
    b, s, d = x.shape
    depth = w_in.shape[0]
    assert d == D_MODEL and s % SEQ_TILE == 0 and s % FFN_SEQ_TILE == 0
    assert w_in.shape[-1] == N_IN and w_gate_up.shape[-1] == 2 * D_FF
    assert N_IN % STAGE_COLS == 0 and d % STAGE_COLS == 0
    rope = _rope_tables(s)
    g_fin = g_final.reshape(1, d)
    for l in range(depth):
        x, w_gu, w_d = _mixer(
            x, g_mix[l].reshape(1, d), rope, conv_w, attn_sinks[l], l,
            w_in[l], w_conv_out[l], w_attn_out[l], w_o[l],
            w_gate_up[l], w_down[l])
        x = _ffn(x, g_ffn[l].reshape(1, d), g_fin, w_gu, w_d, final_norm=(l == depth - 1))
    return x
```

```python
import functools
import math

import jax
import jax.numpy as jnp
from jax import lax
from jax.experimental import pallas as pl
from jax.experimental.pallas import tpu as pltpu

D_MODEL = 1024
CONV_K = 3
HEAD_DIM = 64
N_HEADS = 16
N_KV_HEADS = 4
GROUP = N_HEADS // N_KV_HEADS
D_ATTN = N_HEADS * HEAD_DIM
D_KV = N_KV_HEADS * HEAD_DIM
WINDOW = 128
BLOCK = 128
ROT_DIM = HEAD_DIM // 4
ROPE_THETA = 500000.0
ATTN_SCALE = 1.0 / math.sqrt(HEAD_DIM)
LOG2E = math.log2(math.e)
NEG_INF = -1e30
D_FF = 2816
EPS = 1e-5

OFF_CB = 0
OFF_CC = OFF_CB + D_MODEL
OFF_CX = OFF_CC + D_MODEL
OFF_Q = OFF_CX + D_MODEL
OFF_K = OFF_Q + D_ATTN
OFF_V = OFF_K + D_KV
OFF_GC = OFF_V + D_KV
OFF_GA = OFF_GC + D_MODEL
N_IN = OFF_GA + D_MODEL

LANES = 128
SUBLANES = 8
BF16_SUBLANES = 16
SEQ_TILE = 512
FFN_SEQ_TILE = 1024
STAGE_COLS = 512
STAGE_SLOTS = 4
FF_CHUNKS = ((0, 1024), (1024, 1024), (2048, 768))
VMEM_LIMIT_BYTES = 58 * 1024 * 1024

F32 = jnp.float32
BF16 = jnp.bfloat16

assert WINDOW == BLOCK and 2 * HEAD_DIM == LANES and GROUP == 4


def _rms_norm(x, g):
    ms = jnp.mean(x * x, axis=-1, keepdims=True)
    return x * lax.rsqrt(ms + EPS) * g


def _sigmoid(x):
    return 1.0 / (1.0 + jnp.exp(-x))


def _rope(t, cos, sin_next, sin_prev):
    nxt = pltpu.roll(t, LANES - ROT_DIM // 2, 1)
    prv = pltpu.roll(t, ROT_DIM // 2, 1)
    return t * cos + nxt * sin_next + prv * sin_prev


def _split_head_pair(t, lane_lo):
    sw = pltpu.roll(t, HEAD_DIM, 1)
    zero = jnp.zeros_like(t)
    even = (jnp.where(lane_lo, t, zero), jnp.where(lane_lo, zero, sw))
    odd = (jnp.where(lane_lo, sw, zero), jnp.where(lane_lo, zero, t))
    return even, odd


def _stage_weights(pairs, stage_ref, sem_ref):
    slots = stage_ref.shape[0]
    jobs = [(src, dst, c * STAGE_COLS)
            for src, dst in pairs for c in range(src.shape[1] // STAGE_COLS)]

    def slab_copy(n):
        src, _, col = jobs[n]
        return pltpu.make_async_copy(
            src.at[:, pl.ds(col, STAGE_COLS)], stage_ref.at[n % slots], sem_ref.at[n % slots])

    for n in range(min(slots - 1, len(jobs))):
        slab_copy(n).start()
    for n, (_, dst, col) in enumerate(jobs):
        if n + slots - 1 < len(jobs):
            slab_copy(n + slots - 1).start()
        slab_copy(n).wait()
        dst[:, col:col + STAGE_COLS] = stage_ref[n % slots].astype(BF16)


def _mixer_kernel(sinks_ref, x_ref, g_ref, rope_ref, convw_ref, w_in_hbm,
                  w_co_hbm, w_ao_hbm, w_o_hbm, w_gu_f32_ref, w_d_f32_ref,
                  out_ref, w_gu_bf16_ref, w_d_bf16_ref,
                  w_in_ref, w_co_ref, w_ao_ref, w_o_ref, stage_ref, stage_sem,
                  u_scr, q_scr, klo_scr, khi_scr, vlo_scr, vhi_scr, attn_scr):
    ts = x_ref.shape[0]
    n_blk = ts // BLOCK
    s_idx = pl.program_id(1)
    kv_scrs = (klo_scr, khi_scr, vlo_scr, vhi_scr)

    @pl.when((pl.program_id(0) == 0) & (s_idx == 0))
    def _():
        _stage_weights(((w_in_hbm, w_in_ref), (w_co_hbm, w_co_ref),
                        (w_ao_hbm, w_ao_ref), (w_o_hbm, w_o_ref)), stage_ref, stage_sem)

    @pl.when(s_idx == 0)
    def _():
        u_scr[0:SUBLANES, :] = jnp.zeros((SUBLANES, D_MODEL), F32)
        for scr in kv_scrs:
            scr[0:BLOCK, :] = jnp.zeros((BLOCK, N_KV_HEADS * LANES), BF16)

    w_gu_bf16_ref[...] = w_gu_f32_ref[...].astype(BF16)
    w_d_bf16_ref[...] = w_d_f32_ref[...].astype(BF16)

    x = x_ref[...]
    h = _rms_norm(x, g_ref[...]).astype(BF16)

    def proj(off, width):
        return jnp.dot(h, w_in_ref[:, off:off + width], preferred_element_type=F32)

    q_tabs = [rope_ref[i] * (ATTN_SCALE * LOG2E) for i in range(3)]
    q = proj(OFF_Q, D_ATTN)
    for t in range(D_ATTN // LANES):
        sl = slice(t * LANES, (t + 1) * LANES)
        q_scr[:, sl] = _rope(q[:, sl], *q_tabs).astype(BF16)
    lane_lo_t = lax.broadcasted_iota(jnp.int32, (ts, LANES), 1) < HEAD_DIM
    k = proj(OFF_K, D_KV)
    v = proj(OFF_V, D_KV)
    for t in range(D_KV // LANES):
        sl = slice(t * LANES, (t + 1) * LANES)
        k_t = _rope(k[:, sl], rope_ref[0], rope_ref[1], rope_ref[2])
        for src, lo_scr, hi_scr in ((k_t, klo_scr, khi_scr), (v[:, sl], vlo_scr, vhi_scr)):
            for hd, (lo, hi) in zip((2 * t, 2 * t + 1), _split_head_pair(src, lane_lo_t)):
                hsl = slice(hd * LANES, (hd + 1) * LANES)
                lo_scr[BLOCK:BLOCK + ts, hsl] = lo.astype(BF16)
                hi_scr[BLOCK:BLOCK + ts, hsl] = hi.astype(BF16)

    u = proj(OFF_CC, D_MODEL) * proj(OFF_CX, D_MODEL)
    u_scr[SUBLANES:SUBLANES + ts, :] = u
    conv = (convw_ref[0:1, :] * u_scr[SUBLANES - 2:SUBLANES - 2 + ts, :]
            + convw_ref[1:2, :] * u_scr[SUBLANES - 1:SUBLANES - 1 + ts, :]
            + convw_ref[2:3, :] * u)
    u_scr[0:SUBLANES, :] = u_scr[ts:ts + SUBLANES, :]
    conv_y = (proj(OFF_CB, D_MODEL) * conv).astype(BF16)
    gate_c = _sigmoid(proj(OFF_GC, D_MODEL))
    merged = gate_c * jnp.dot(conv_y, w_co_ref[...], preferred_element_type=F32)

    lane = lax.broadcasted_iota(jnp.int32, (BLOCK, LANES), 1)
    rowi = lax.broadcasted_iota(jnp.int32, (BLOCK, LANES), 0)
    from_cur = lane <= rowi
    lane_lo = lane < HEAD_DIM
    lane2 = lax.broadcasted_iota(jnp.int32, (2 * BLOCK, LANES), 1)
    ones_lo = jnp.where(lane2 < HEAD_DIM, 1.0, 0.0).astype(BF16)
    ones_hi = jnp.where(lane2 < HEAD_DIM, 0.0, 1.0).astype(BF16)
    nt_dims = (((1,), (1,)), ((), ()))

    units = [(j, kv) for j in range(n_blk) for kv in range(N_KV_HEADS)]

    def q_slices(kv):
        return [slice((2 * kv + pr) * LANES, (2 * kv + pr + 1) * LANES) for pr in range(2)]

    def scores(j, kv):
        r0 = j * BLOCK
        kv_sl = slice(kv * LANES, (kv + 1) * LANES)
        q2 = jnp.concatenate([q_scr[r0:r0 + BLOCK, sl] for sl in q_slices(kv)], axis=0)
        k_rhs = jnp.concatenate([klo_scr[r0:r0 + 2 * BLOCK, kv_sl],
                                 khi_scr[r0:r0 + 2 * BLOCK, kv_sl]], axis=0)
        return lax.dot_general(q2, k_rhs, nt_dims, preferred_element_type=F32)

    s_next = scores(*units[0])
    for n, (j, kv) in enumerate(units):
        r0 = j * BLOCK
        kv_sl = slice(kv * LANES, (kv + 1) * LANES)
        q_sl = q_slices(kv)
        s = s_next
        if n + 1 < len(units):
            s_next = scores(*units[n + 1])
        else:
            gate_a_halves = [_sigmoid(proj(OFF_GA, D_MODEL // 2))]
        v_rhs = jnp.concatenate(
            [jnp.concatenate([vlo_scr[r0:r0 + 2 * BLOCK, kv_sl], ones_lo], axis=1),
             jnp.concatenate([vhi_scr[r0:r0 + 2 * BLOCK, kv_sl], ones_hi], axis=1)], axis=0)
        p_rows, sink_terms = [], []
        for pr in range(2):
            rows = slice(pr * BLOCK, (pr + 1) * BLOCK)
            p_tiles, e_sink = [], []
            for half in range(2):
                hd = kv * GROUP + 2 * pr + half
                s_prev = s[rows, (2 * half) * BLOCK:(2 * half + 1) * BLOCK]
                s_cur = s[rows, (2 * half + 1) * BLOCK:(2 * half + 2) * BLOCK]
                if j == 0:
                    s_prev = jnp.where(s_idx > 0, s_prev, NEG_INF)
                t = jnp.where(from_cur, s_cur, s_prev)
                m = jnp.max(t, axis=-1, keepdims=True)
                p = jnp.exp2(t - m).astype(BF16)
                zero = jnp.zeros_like(p)
                p_tiles += [jnp.where(from_cur, zero, p), jnp.where(from_cur, p, zero)]
                e_sink.append(jnp.exp2(sinks_ref[hd] * LOG2E - m))
            p_rows.append(jnp.concatenate(p_tiles, axis=1))
            sink_terms.append(jnp.where(lane_lo, e_sink[0], e_sink[1]))
        o = jnp.dot(jnp.concatenate(p_rows, axis=0), v_rhs, preferred_element_type=F32)
        if n + 1 == len(units):
            gate_a_halves.append(_sigmoid(proj(OFF_GA + D_MODEL // 2, D_MODEL // 2)))
        for pr in range(2):
            rows = slice(pr * BLOCK, (pr + 1) * BLOCK)
            den = o[rows, LANES:2 * LANES] + sink_terms[pr]
            attn_scr[r0:r0 + BLOCK, q_sl[pr]] = (o[rows, 0:LANES] / den).astype(BF16)

    for scr in kv_scrs:
        scr[0:BLOCK, :] = scr[ts:ts + BLOCK, :]

    attn_out = jnp.dot(attn_scr[...], w_ao_ref[...], preferred_element_type=F32)
    merged = merged + jnp.concatenate(gate_a_halves, axis=1) * attn_out
    out_ref[...] = x + jnp.dot(merged.astype(BF16), w_o_ref[...], preferred_element_type=F32)


def _ffn_kernel(x_ref, g_ref, gf_ref, w_gu_ref, w_d_ref, out_ref, act_scr, *, final_norm):
    ts = x_ref.shape[0]
    halves = (slice(0, ts // 2), slice(ts // 2, ts))
    h_halves = [_rms_norm(x_ref[r, :], g_ref[...]).astype(BF16) for r in halves]

    def gated(h, rows, off, width):
        gate = jnp.dot(h, w_gu_ref[:, off:off + width], preferred_element_type=F32)
        up = jnp.dot(h, w_gu_ref[:, D_FF + off:D_FF + off + width], preferred_element_type=F32)
        act_scr[rows, off:off + width] = (gate * _sigmoid(gate) * up).astype(BF16)

    for rows, h in zip(halves, h_halves):
        gated(h, rows, *FF_CHUNKS[0])
    h_all = jnp.concatenate(h_halves, axis=0)
    for off, width in FF_CHUNKS[1:]:
        gated(h_all, slice(0, ts), off, width)
    for rows in halves:
        y = x_ref[rows, :] + jnp.dot(act_scr[rows, :], w_d_ref[...],
                                     preferred_element_type=F32)
        if final_norm:
            y = _rms_norm(y, gf_ref[...])
        out_ref[rows, :] = y


def _resident(shape):
    return pl.BlockSpec(shape, lambda b, s: (0,) * len(shape), pipeline_mode=pl.Buffered(1))


def _rope_tables(seq):
    half = ROT_DIM // 2
    d = jnp.arange(LANES) % HEAD_DIM
    inv_freq = ROPE_THETA ** (-(2 * (d % half)).astype(F32) / ROT_DIM)
    ang = jnp.arange(seq, dtype=F32)[:, None] * inv_freq[None, :]
    cos, sin = jnp.cos(ang), jnp.sin(ang)
    c = jnp.where(d < ROT_DIM, cos, 1.0)
    s_next = jnp.where(d < half, -sin, 0.0)
    s_prev = jnp.where((d >= half) & (d < ROT_DIM), sin, 0.0)
    kind = jnp.arange(3)[:, None, None]
    return jnp.where(kind == 0, c, jnp.where(kind == 1, s_next, s_prev))


def _mixer(x, g, rope, conv_w, sinks, layer, w_in, w_co, w_ao, w_o, w_gu, w_d):
    b, s, d = x.shape
    ts = SEQ_TILE
    n_s = s // ts
    gu_rows = w_gu.shape[0] // (b * n_s)
    d_rows = 2 * w_d.shape[0] // (b * n_s)
    assert gu_rows % BF16_SUBLANES == 0 and d_rows % BF16_SUBLANES == 0
    gu_slab = pl.BlockSpec((gu_rows, w_gu.shape[1]), lambda i, j: (i * n_s + j, 0))
    d_slab = pl.BlockSpec((d_rows, w_d.shape[1]), lambda i, j: ((i * n_s + j) // 2, 0))
    tile = pl.BlockSpec((None, ts, d), lambda i, j: (i, j, 0))
    kv_scratch = pltpu.VMEM((BLOCK + ts, N_KV_HEADS * LANES), BF16)
    return pl.pallas_call(
        _mixer_kernel,
        grid=(b, n_s),
        in_specs=[
            pl.BlockSpec(memory_space=pltpu.SMEM),
            tile,
            _resident((1, d)),
            pl.BlockSpec((3, ts, LANES), lambda i, j: (0, j, 0)),
            pl.BlockSpec((None, CONV_K, d), lambda i, j: (layer, 0, 0),
                         pipeline_mode=pl.Buffered(1)),
            pl.BlockSpec(memory_space=pl.ANY),
            pl.BlockSpec(memory_space=pl.ANY),
            pl.BlockSpec(memory_space=pl.ANY),
            pl.BlockSpec(memory_space=pl.ANY),
            gu_slab,
            d_slab,
        ],
        out_specs=[tile, gu_slab, d_slab],
        out_shape=[jax.ShapeDtypeStruct(x.shape, x.dtype),
                   jax.ShapeDtypeStruct(w_gu.shape, BF16),
                   jax.ShapeDtypeStruct(w_d.shape, BF16)],
        scratch_shapes=[
            pltpu.VMEM(w_in.shape, BF16),
            pltpu.VMEM(w_co.shape, BF16),
            pltpu.VMEM(w_ao.shape, BF16),
            pltpu.VMEM(w_o.shape, BF16),
            pltpu.VMEM((STAGE_SLOTS, d, STAGE_COLS), F32),
            pltpu.SemaphoreType.DMA((STAGE_SLOTS,)),
            pltpu.VMEM((SUBLANES + ts, d), F32),
            pltpu.VMEM((ts, D_ATTN), BF16),
            kv_scratch, kv_scratch, kv_scratch, kv_scratch,
            pltpu.VMEM((ts, D_ATTN), BF16),
        ],
        compiler_params=pltpu.CompilerParams(
            dimension_semantics=("arbitrary", "arbitrary"),
            vmem_limit_bytes=VMEM_LIMIT_BYTES),
        name="mixer",
    )(sinks, x, g, rope, conv_w, w_in, w_co, w_ao, w_o, w_gu, w_d)


def _ffn(x, g, g_final, w_gu, w_d, final_norm):
    b, s, d = x.shape
    ts = FFN_SEQ_TILE
    tile = pl.BlockSpec((None, ts, d), lambda i, j: (i, j, 0))
    return pl.pallas_call(
        functools.partial(_ffn_kernel, final_norm=final_norm),
        grid=(b, s // ts),
        in_specs=[tile, _resident((1, d)), _resident((1, d)),
                  _resident(w_gu.shape), _resident(w_d.shape)],
        out_specs=tile,
        out_shape=jax.ShapeDtypeStruct(x.shape, x.dtype),
        scratch_shapes=[pltpu.VMEM((ts, D_FF), BF16)],
        compiler_params=pltpu.CompilerParams(
            dimension_semantics=("arbitrary", "arbitrary"),
            vmem_limit_bytes=VMEM_LIMIT_BYTES),
        name="ffn",
    )(x, g, g_final, w_gu, w_d)


def kernel(x, g_mix, w_in, conv_w, attn_sinks, w_conv_out, w_attn_out, w_o,
           g_ffn, w_gate_up, w_down, g_final):
    b, s, d = x.shape
    depth = w_in.shape[0]
    assert d == D_MODEL and s % SEQ_TILE == 0 and s % FFN_SEQ_TILE == 0
    assert w_in.shape[-1] == N_IN and w_gate_up.shape[-1] == 2 * D_FF
    assert N_IN % STAGE_COLS == 0 and d % STAGE_COLS == 0
    rope = _rope_tables(s)
    g_fin = g_final.reshape(1, d)
    for l in range(depth):
        x, w_gu, w_d = _mixer(
            x, g_mix[l].reshape(1, d), rope, conv_w, attn_sinks[l], l,
            w_in[l], w_conv_out[l], w_attn_out[l], w_o[l],
            w_gate_up[l], w_down[l])
        x = _ffn(x, g_ffn[l].reshape(1, d), g_fin, w_gu, w_d, final_norm=(l == depth - 1))
    return x
```

```python
import functools
import math

import jax
import jax.numpy as jnp
from jax import lax
from jax.experimental import pallas as pl
from jax.experimental.pallas import tpu as pltpu

D_MODEL = 1024
CONV_K = 3
HEAD_DIM = 64
N_HEADS = 16
N_KV_HEADS = 4
GROUP = N_HEADS // N_KV_HEADS
D_ATTN = N_HEADS * HEAD_DIM
D_KV = N_KV_HEADS * HEAD_DIM
WINDOW = 128
BLOCK = 128
ROT_DIM = HEAD_DIM // 4
ROPE_THETA = 500000.0
ATTN_SCALE = 1.0 / math.sqrt(HEAD_DIM)
LOG2E = math.log2(math.e)
NEG_INF = -1e30
D_FF = 2816
EPS = 1e-5

OFF_CB = 0
OFF_CC = OFF_CB + D_MODEL
OFF_CX = OFF_CC + D_MODEL
OFF_Q = OFF_CX + D_MODEL
OFF_K = OFF_Q + D_ATTN
OFF_V = OFF_K + D_KV
OFF_GC = OFF_V + D_KV
OFF_GA = OFF_GC + D_MODEL
N_IN = OFF_GA + D_MODEL

LANES = 128
SUBLANES = 8
BF16_SUBLANES = 16
SEQ_TILE = 512
FFN_SEQ_TILE = 1024
STAGE_COLS = 512
STAGE_SLOTS = 4
FF_CHUNKS = ((0, 1024), (1024, 1024), (2048, 768))
VMEM_LIMIT_BYTES = 58 * 1024 * 1024

F32 = jnp.float32
BF16 = jnp.bfloat16

assert WINDOW == BLOCK and 2 * HEAD_DIM == LANES and GROUP == 4


def _rms_norm(x, g):
    ms = jnp.mean(x * x, axis=-1, keepdims=True)
    return x * lax.rsqrt(ms + EPS) * g


def _sigmoid(x):
    return 1.0 / (1.0 + jnp.exp(-x))


def _rope(t, cos, sin_next, sin_prev):
    nxt = pltpu.roll(t, LANES - ROT_DIM // 2, 1)
    prv = pltpu.roll(t, ROT_DIM // 2, 1)
    return t * cos + nxt * sin_next + prv * sin_prev


def _split_head_pair(t, lane_lo):
    sw = pltpu.roll(t, HEAD_DIM, 1)
    zero = jnp.zeros_like(t)
    even = (jnp.where(lane_lo, t, zero), jnp.where(lane_lo, zero, sw))
    odd = (jnp.where(lane_lo, sw, zero), jnp.where(lane_lo, zero, t))
    return even, odd


def _stage_weights(pairs, stage_ref, sem_ref):
    slots = stage_ref.shape[0]
    jobs = [(src, dst, c * STAGE_COLS)
            for src, dst in pairs for c in range(src.shape[1] // STAGE_COLS)]

    def slab_copy(n):
        src, _, col = jobs[n]
        return pltpu.make_async_copy(
            src.at[:, pl.ds(col, STAGE_COLS)], stage_ref.at[n % slots], sem_ref.at[n % slots])

    for n in range(min(slots - 1, len(jobs))):
        slab_copy(n).start()
    for n, (_, dst, col) in enumerate(jobs):
        if n + slots - 1 < len(jobs):
            slab_copy(n + slots - 1).start()
        slab_copy(n).wait()
        dst[:, col:col + STAGE_COLS] = stage_ref[n % slots].astype(BF16)


def _mixer_kernel(sinks_ref, x_ref, g_ref, rope_ref, convw_ref, w_in_hbm,
                  w_co_hbm, w_ao_hbm, w_o_hbm, w_gu_f32_ref, w_d_f32_ref,
                  out_ref, w_gu_bf16_ref, w_d_bf16_ref,
                  w_in_ref, w_co_ref, w_ao_ref, w_o_ref, stage_ref, stage_sem,
                  u_scr, q_scr, klo_scr, khi_scr, vlo_scr, vhi_scr, attn_scr):
    ts = x_ref.shape[0]
    n_blk = ts // BLOCK
    s_idx = pl.program_id(1)
    kv_scrs = (klo_scr, khi_scr, vlo_scr, vhi_scr)

    @pl.when((pl.program_id(0) == 0) & (s_idx == 0))
    def _():
        _stage_weights(((w_in_hbm, w_in_ref), (w_co_hbm, w_co_ref),
                        (w_ao_hbm, w_ao_ref), (w_o_hbm, w_o_ref)), stage_ref, stage_sem)

    @pl.when(s_idx == 0)
    def _():
        u_scr[0:SUBLANES, :] = jnp.zeros((SUBLANES, D_MODEL), F32)
        for scr in kv_scrs:
            scr[0:BLOCK, :] = jnp.zeros((BLOCK, N_KV_HEADS * LANES), BF16)

    w_gu_bf16_ref[...] = w_gu_f32_ref[...].astype(BF16)
    w_d_bf16_ref[...] = w_d_f32_ref[...].astype(BF16)

    x = x_ref[...]
    h = _rms_norm(x, g_ref[...]).astype(BF16)

    def proj(off, width):
        return jnp.dot(h, w_in_ref[:, off:off + width], preferred_element_type=F32)

    q_tabs = [rope_ref[i] * (ATTN_SCALE * LOG2E) for i in range(3)]
    q = proj(OFF_Q, D_ATTN)
    for t in range(D_ATTN // LANES):
        sl = slice(t * LANES, (t + 1) * LANES)
        q_scr[:, sl] = _rope(q[:, sl], *q_tabs).astype(BF16)
    lane_lo_t = lax.broadcasted_iota(jnp.int32, (ts, LANES), 1) < HEAD_DIM
    k = proj(OFF_K, D_KV)
    v = proj(OFF_V, D_KV)
    for t in range(D_KV // LANES):
        sl = slice(t * LANES, (t + 1) * LANES)
        k_t = _rope(k[:, sl], rope_ref[0], rope_ref[1], rope_ref[2])
        for src, lo_scr, hi_scr in ((k_t, klo_scr, khi_scr), (v[:, sl], vlo_scr, vhi_scr)):
            for hd, (lo, hi) in zip((2 * t, 2 * t + 1), _split_head_pair(src, lane_lo_t)):
                hsl = slice(hd * LANES, (hd + 1) * LANES)
                lo_scr[BLOCK:BLOCK + ts, hsl] = lo.astype(BF16)
                hi_scr[BLOCK:BLOCK + ts, hsl] = hi.astype(BF16)

    u = proj(OFF_CC, D_MODEL) * proj(OFF_CX, D_MODEL)
    u_scr[SUBLANES:SUBLANES + ts, :] = u
    conv = (convw_ref[0:1, :] * u_scr[SUBLANES - 2:SUBLANES - 2 + ts, :]
            + convw_ref[1:2, :] * u_scr[SUBLANES - 1:SUBLANES - 1 + ts, :]
            + convw_ref[2:3, :] * u)
    u_scr[0:SUBLANES, :] = u_scr[ts:ts + SUBLANES, :]
    conv_y = (proj(OFF_CB, D_MODEL) * conv).astype(BF16)
    gate_c = _sigmoid(proj(OFF_GC, D_MODEL))
    merged = gate_c * jnp.dot(conv_y, w_co_ref[...], preferred_element_type=F32)

    lane = lax.broadcasted_iota(jnp.int32, (BLOCK, LANES), 1)
    rowi = lax.broadcasted_iota(jnp.int32, (BLOCK, LANES), 0)
    from_cur = lane <= rowi
    lane_lo = lane < HEAD_DIM
    lane2 = lax.broadcasted_iota(jnp.int32, (2 * BLOCK, LANES), 1)
    ones_lo = jnp.where(lane2 < HEAD_DIM, 1.0, 0.0).astype(BF16)
    ones_hi = jnp.where(lane2 < HEAD_DIM, 0.0, 1.0).astype(BF16)
    nt_dims = (((1,), (1,)), ((), ()))

    units = [(j, kv) for j in range(n_blk) for kv in range(N_KV_HEADS)]

    def q_slices(kv):
        return [slice((2 * kv + pr) * LANES, (2 * kv + pr + 1) * LANES) for pr in range(2)]

    def scores(j, kv):
        r0 = j * BLOCK
        kv_sl = slice(kv * LANES, (kv + 1) * LANES)
        q2 = jnp.concatenate([q_scr[r0:r0 + BLOCK, sl] for sl in q_slices(kv)], axis=0)
        k_rhs = jnp.concatenate([klo_scr[r0:r0 + 2 * BLOCK, kv_sl],
                                 khi_scr[r0:r0 + 2 * BLOCK, kv_sl]], axis=0)
        return lax.dot_general(q2, k_rhs, nt_dims, preferred_element_type=F32)

    s_next = scores(*units[0])
    for n, (j, kv) in enumerate(units):
        r0 = j * BLOCK
        kv_sl = slice(kv * LANES, (kv + 1) * LANES)
        q_sl = q_slices(kv)
        s = s_next
        if n + 1 < len(units):
            s_next = scores(*units[n + 1])
        else:
            gate_a_halves = [_sigmoid(proj(OFF_GA, D_MODEL // 2))]
        v_rhs = jnp.concatenate(
            [jnp.concatenate([vlo_scr[r0:r0 + 2 * BLOCK, kv_sl], ones_lo], axis=1),
             jnp.concatenate([vhi_scr[r0:r0 + 2 * BLOCK, kv_sl], ones_hi], axis=1)], axis=0)
        p_rows, sink_terms = [], []
        for pr in range(2):
            rows = slice(pr * BLOCK, (pr + 1) * BLOCK)
            p_tiles, e_sink = [], []
            for half in range(2):
                hd = kv * GROUP + 2 * pr + half
                s_prev = s[rows, (2 * half) * BLOCK:(2 * half + 1) * BLOCK]
                s_cur = s[rows, (2 * half + 1) * BLOCK:(2 * half + 2) * BLOCK]
                if j == 0:
                    s_prev = jnp.where(s_idx > 0, s_prev, NEG_INF)
                t = jnp.where(from_cur, s_cur, s_prev)
                m = jnp.max(t, axis=-1, keepdims=True)
                p = jnp.exp2(t - m).astype(BF16)
                zero = jnp.zeros_like(p)
                p_tiles += [jnp.where(from_cur, zero, p), jnp.where(from_cur, p, zero)]
                e_sink.append(jnp.exp2(sinks_ref[hd] * LOG2E - m))
            p_rows.append(jnp.concatenate(p_tiles, axis=1))
            sink_terms.append(jnp.where(lane_lo, e_sink[0], e_sink[1]))
        o = jnp.dot(jnp.concatenate(p_rows, axis=0), v_rhs, preferred_element_type=F32)
        if n + 1 == len(units):
            gate_a_halves.append(_sigmoid(proj(OFF_GA + D_MODEL // 2, D_MODEL // 2)))
        for pr in range(2):
            rows = slice(pr * BLOCK, (pr + 1) * BLOCK)
            den = o[rows, LANES:2 * LANES] + sink_terms[pr]
            attn_scr[r0:r0 + BLOCK, q_sl[pr]] = (o[rows, 0:LANES] / den).astype(BF16)

    for scr in kv_scrs:
        scr[0:BLOCK, :] = scr[ts:ts + BLOCK, :]

    attn_out = jnp.dot(attn_scr[...], w_ao_ref[...], preferred_element_type=F32)
    merged = merged + jnp.concatenate(gate_a_halves, axis=1) * attn_out
    out_ref[...] = x + jnp.dot(merged.astype(BF16), w_o_ref[...], preferred_element_type=F32)


def _ffn_kernel(x_ref, g_ref, gf_ref, w_gu_ref, w_d_ref, out_ref, act_scr, *, final_norm):
    x = x_ref[...]
    h = _rms_norm(x, g_ref[...]).astype(BF16)
    for off, width in FF_CHUNKS:
        gate = jnp.dot(h, w_gu_ref[:, off:off + width], preferred_element_type=F32)
        up = jnp.dot(h, w_gu_ref[:, D_FF + off:D_FF + off + width], preferred_element_type=F32)
        act_scr[:, off:off + width] = (gate * _sigmoid(gate) * up).astype(BF16)
    d = x.shape[1]
    col_halves = (slice(0, d // 2), slice(d // 2, d))
    ys = [x[:, c] + jnp.dot(act_scr[...], w_d_ref[:, c], preferred_element_type=F32)
          for c in col_halves]
    if final_norm:
        sq = sum(jnp.sum(y * y, axis=-1, keepdims=True) for y in ys)
        inv_rms = lax.rsqrt(sq / d + EPS)
        ys = [y * inv_rms * gf_ref[:, c] for y, c in zip(ys, col_halves)]
    for y, c in zip(ys, col_halves):
        out_ref[:, c] = y


def _resident(shape):
    return pl.BlockSpec(shape, lambda b, s: (0,) * len(shape), pipeline_mode=pl.Buffered(1))


def _rope_tables(seq):
    half = ROT_DIM // 2
    d = jnp.arange(LANES) % HEAD_DIM
    inv_freq = ROPE_THETA ** (-(2 * (d % half)).astype(F32) / ROT_DIM)
    ang = jnp.arange(seq, dtype=F32)[:, None] * inv_freq[None, :]
    cos, sin = jnp.cos(ang), jnp.sin(ang)
    c = jnp.where(d < ROT_DIM, cos, 1.0)
    s_next = jnp.where(d < half, -sin, 0.0)
    s_prev = jnp.where((d >= half) & (d < ROT_DIM), sin, 0.0)
    kind = jnp.arange(3)[:, None, None]
    return jnp.where(kind == 0, c, jnp.where(kind == 1, s_next, s_prev))


def _mixer(x, g, rope, conv_w, sinks, layer, w_in, w_co, w_ao, w_o, w_gu, w_d):
    b, s, d = x.shape
    ts = SEQ_TILE
    n_s = s // ts
    gu_rows = w_gu.shape[0] // (b * n_s)
    d_rows = 2 * w_d.shape[0] // (b * n_s)
    assert gu_rows % BF16_SUBLANES == 0 and d_rows % BF16_SUBLANES == 0
    gu_slab = pl.BlockSpec((gu_rows, w_gu.shape[1]), lambda i, j: (i * n_s + j, 0))
    d_slab = pl.BlockSpec((d_rows, w_d.shape[1]), lambda i, j: ((i * n_s + j) // 2, 0))
    tile = pl.BlockSpec((None, ts, d), lambda i, j: (i, j, 0))
    kv_scratch = pltpu.VMEM((BLOCK + ts, N_KV_HEADS * LANES), BF16)
    return pl.pallas_call(
        _mixer_kernel,
        grid=(b, n_s),
        in_specs=[
            pl.BlockSpec(memory_space=pltpu.SMEM),
            tile,
            _resident((1, d)),
            pl.BlockSpec((3, ts, LANES), lambda i, j: (0, j, 0)),
            pl.BlockSpec((None, CONV_K, d), lambda i, j: (layer, 0, 0),
                         pipeline_mode=pl.Buffered(1)),
            pl.BlockSpec(memory_space=pl.ANY),
            pl.BlockSpec(memory_space=pl.ANY),
            pl.BlockSpec(memory_space=pl.ANY),
            pl.BlockSpec(memory_space=pl.ANY),
            gu_slab,
            d_slab,
        ],
        out_specs=[tile, gu_slab, d_slab],
        out_shape=[jax.ShapeDtypeStruct(x.shape, x.dtype),
                   jax.ShapeDtypeStruct(w_gu.shape, BF16),
                   jax.ShapeDtypeStruct(w_d.shape, BF16)],
        scratch_shapes=[
            pltpu.VMEM(w_in.shape, BF16),
            pltpu.VMEM(w_co.shape, BF16),
            pltpu.VMEM(w_ao.shape, BF16),
            pltpu.VMEM(w_o.shape, BF16),
            pltpu.VMEM((STAGE_SLOTS, d, STAGE_COLS), F32),
            pltpu.SemaphoreType.DMA((STAGE_SLOTS,)),
            pltpu.VMEM((SUBLANES + ts, d), F32),
            pltpu.VMEM((ts, D_ATTN), BF16),
            kv_scratch, kv_scratch, kv_scratch, kv_scratch,
            pltpu.VMEM((ts, D_ATTN), BF16),
        ],
        compiler_params=pltpu.CompilerParams(
            dimension_semantics=("arbitrary", "arbitrary"),
            vmem_limit_bytes=VMEM_LIMIT_BYTES),
        name="mixer",
    )(sinks, x, g, rope, conv_w, w_in, w_co, w_ao, w_o, w_gu, w_d)


def _ffn(x, g, g_final, w_gu, w_d, final_norm):
    b, s, d = x.shape
    ts = FFN_SEQ_TILE
    tile = pl.BlockSpec((None, ts, d), lambda i, j: (i, j, 0))
    return pl.pallas_call(
        functools.partial(_ffn_kernel, final_norm=final_norm),
        grid=(b, s // ts),
        in_specs=[tile, _resident((1, d)), _resident((1, d)),
                  _resident(w_gu.shape), _resident(w_d.shape)],
        out_specs=tile,
        out_shape=jax.ShapeDtypeStruct(x.shape, x.dtype),
        scratch_shapes=[pltpu.VMEM((ts, D_FF), BF16)],
        compiler_params=pltpu.CompilerParams(
            dimension_semantics=("arbitrary", "arbitrary"),
            vmem_limit_bytes=VMEM_LIMIT_BYTES),
        name="ffn",
    )(x, g, g_final, w_gu, w_d)


def kernel(x, g_mix, w_in, conv_w, attn_sinks, w_conv_out, w_attn_out, w_o,
           g_ffn, w_gate_up, w_down, g_final):
    b, s, d = x.shape
    depth = w_in.shape[0]
    assert d == D_MODEL and s % SEQ_TILE == 0 and s % FFN_SEQ_TILE == 0
    assert w_in.shape[-1] == N_IN and w_gate_up.shape[-1] == 2 * D_FF
    assert N_IN % STAGE_COLS == 0 and d % STAGE_COLS == 0
    rope = _rope_tables(s)
    g_fin = g_final.reshape(1, d)
    for l in range(depth):
        x, w_gu, w_d = _mixer(
            x, g_mix[l].reshape(1, d), rope, conv_w, attn_sinks[l], l,
            w_in[l], w_conv_out[l], w_attn_out[l], w_o[l],
            w_gate_up[l], w_down[l])
        x = _ffn(x, g_ffn[l].reshape(1, d), g_fin, w_gu, w_d, final_norm=(l == depth - 1))
    return x
```

```python
import functools
import math

import jax
import jax.numpy as jnp
from jax import lax
from jax.experimental import pallas as pl
from jax.experimental.pallas import tpu as pltpu

D_MODEL = 1024
CONV_K = 3
HEAD_DIM = 64
N_HEADS = 16
N_KV_HEADS = 4
GROUP = N_HEADS // N_KV_HEADS
D_ATTN = N_HEADS * HEAD_DIM
D_KV = N_KV_HEADS * HEAD_DIM
WINDOW = 128
BLOCK = 128
ROT_DIM = HEAD_DIM // 4
ROPE_THETA = 500000.0
ATTN_SCALE = 1.0 / math.sqrt(HEAD_DIM)
LOG2E = math.log2(math.e)
NEG_INF = -1e30
D_FF = 2816
EPS = 1e-5

OFF_CB = 0
OFF_CC = OFF_CB + D_MODEL
OFF_CX = OFF_CC + D_MODEL
OFF_Q = OFF_CX + D_MODEL
OFF_K = OFF_Q + D_ATTN
OFF_V = OFF_K + D_KV
OFF_GC = OFF_V + D_KV
OFF_GA = OFF_GC + D_MODEL
N_IN = OFF_GA + D_MODEL

LANES = 128
SUBLANES = 8
BF16_SUBLANES = 16
SEQ_TILE = 512
FFN_SEQ_TILE = 1024
STAGE_COLS = 512
STAGE_SLOTS = 4
FF_CHUNKS = ((0, 1024), (1024, 1024), (2048, 768))
VMEM_LIMIT_BYTES = 58 * 1024 * 1024

F32 = jnp.float32
BF16 = jnp.bfloat16

assert WINDOW == BLOCK and 2 * HEAD_DIM == LANES and GROUP == 4


def _rms_norm(x, g):
    ms = jnp.mean(x * x, axis=-1, keepdims=True)
    return x * lax.rsqrt(ms + EPS) * g


def _sigmoid(x):
    return 1.0 / (1.0 + jnp.exp(-x))


def _rope(t, cos, sin_next, sin_prev):
    nxt = pltpu.roll(t, LANES - ROT_DIM // 2, 1)
    prv = pltpu.roll(t, ROT_DIM // 2, 1)
    return t * cos + nxt * sin_next + prv * sin_prev


def _split_head_pair(t, lane_lo):
    sw = pltpu.roll(t, HEAD_DIM, 1)
    zero = jnp.zeros_like(t)
    even = (jnp.where(lane_lo, t, zero), jnp.where(lane_lo, zero, sw))
    odd = (jnp.where(lane_lo, sw, zero), jnp.where(lane_lo, zero, t))
    return even, odd


def _stage_weights(pairs, stage_ref, sem_ref):
    slots = stage_ref.shape[0]
    jobs = [(src, dst, c * STAGE_COLS)
            for src, dst in pairs for c in range(src.shape[1] // STAGE_COLS)]

    def slab_copy(n):
        src, _, col = jobs[n]
        return pltpu.make_async_copy(
            src.at[:, pl.ds(col, STAGE_COLS)], stage_ref.at[n % slots], sem_ref.at[n % slots])

    for n in range(min(slots - 1, len(jobs))):
        slab_copy(n).start()
    for n, (_, dst, col) in enumerate(jobs):
        if n + slots - 1 < len(jobs):
            slab_copy(n + slots - 1).start()
        slab_copy(n).wait()
        dst[:, col:col + STAGE_COLS] = stage_ref[n % slots].astype(BF16)


def _mixer_kernel(sinks_ref, x_ref, g_ref, rope_ref, convw_ref, w_in_hbm,
                  w_co_hbm, w_ao_hbm, w_o_hbm, w_gu_f32_ref, w_d_f32_ref,
                  out_ref, w_gu_bf16_ref, w_d_bf16_ref,
                  w_in_ref, w_co_ref, w_ao_ref, w_o_ref, stage_ref, stage_sem,
                  u_scr, q_scr, klo_scr, khi_scr, vlo_scr, vhi_scr, attn_scr):
    ts = x_ref.shape[0]
    n_blk = ts // BLOCK
    s_idx = pl.program_id(1)
    kv_scrs = (klo_scr, khi_scr, vlo_scr, vhi_scr)

    @pl.when((pl.program_id(0) == 0) & (s_idx == 0))
    def _():
        _stage_weights(((w_in_hbm, w_in_ref), (w_co_hbm, w_co_ref),
                        (w_ao_hbm, w_ao_ref), (w_o_hbm, w_o_ref)), stage_ref, stage_sem)

    @pl.when(s_idx == 0)
    def _():
        u_scr[0:SUBLANES, :] = jnp.zeros((SUBLANES, D_MODEL), F32)
        for scr in kv_scrs:
            scr[0:BLOCK, :] = jnp.zeros((BLOCK, N_KV_HEADS * LANES), BF16)

    w_gu_bf16_ref[...] = w_gu_f32_ref[...].astype(BF16)
    w_d_bf16_ref[...] = w_d_f32_ref[...].astype(BF16)

    x = x_ref[...]
    h = _rms_norm(x, g_ref[...]).astype(BF16)

    def proj(off, width):
        return jnp.dot(h, w_in_ref[:, off:off + width], preferred_element_type=F32)

    u = proj(OFF_CC, D_MODEL) * proj(OFF_CX, D_MODEL)
    u_scr[SUBLANES:SUBLANES + ts, :] = u
    conv = (convw_ref[0:1, :] * u_scr[SUBLANES - 2:SUBLANES - 2 + ts, :]
            + convw_ref[1:2, :] * u_scr[SUBLANES - 1:SUBLANES - 1 + ts, :]
            + convw_ref[2:3, :] * u)
    u_scr[0:SUBLANES, :] = u_scr[ts:ts + SUBLANES, :]

    q_tabs = [rope_ref[i] * (ATTN_SCALE * LOG2E) for i in range(3)]
    q = proj(OFF_Q, D_ATTN)
    for t in range(D_ATTN // LANES):
        sl = slice(t * LANES, (t + 1) * LANES)
        q_scr[:, sl] = _rope(q[:, sl], *q_tabs).astype(BF16)
    lane_lo_t = lax.broadcasted_iota(jnp.int32, (ts, LANES), 1) < HEAD_DIM
    k = proj(OFF_K, D_KV)
    v = proj(OFF_V, D_KV)
    for t in range(D_KV // LANES):
        sl = slice(t * LANES, (t + 1) * LANES)
        k_t = _rope(k[:, sl], rope_ref[0], rope_ref[1], rope_ref[2])
        for src, lo_scr, hi_scr in ((k_t, klo_scr, khi_scr), (v[:, sl], vlo_scr, vhi_scr)):
            for hd, (lo, hi) in zip((2 * t, 2 * t + 1), _split_head_pair(src, lane_lo_t)):
                hsl = slice(hd * LANES, (hd + 1) * LANES)
                lo_scr[BLOCK:BLOCK + ts, hsl] = lo.astype(BF16)
                hi_scr[BLOCK:BLOCK + ts, hsl] = hi.astype(BF16)

    conv_y = (proj(OFF_CB, D_MODEL) * conv).astype(BF16)
    gate_c = _sigmoid(proj(OFF_GC, D_MODEL))
    merged = gate_c * jnp.dot(conv_y, w_co_ref[...], preferred_element_type=F32)

    lane = lax.broadcasted_iota(jnp.int32, (BLOCK, LANES), 1)
    rowi = lax.broadcasted_iota(jnp.int32, (BLOCK, LANES), 0)
    from_cur = lane <= rowi
    lane_lo = lane < HEAD_DIM
    lane2 = lax.broadcasted_iota(jnp.int32, (2 * BLOCK, LANES), 1)
    ones_lo = jnp.where(lane2 < HEAD_DIM, 1.0, 0.0).astype(BF16)
    ones_hi = jnp.where(lane2 < HEAD_DIM, 0.0, 1.0).astype(BF16)
    nt_dims = (((1,), (1,)), ((), ()))

    units = [(j, kv) for j in range(n_blk) for kv in range(N_KV_HEADS)]

    def q_slices(kv):
        return [slice((2 * kv + pr) * LANES, (2 * kv + pr + 1) * LANES) for pr in range(2)]

    def scores(j, kv):
        r0 = j * BLOCK
        kv_sl = slice(kv * LANES, (kv + 1) * LANES)
        q2 = jnp.concatenate([q_scr[r0:r0 + BLOCK, sl] for sl in q_slices(kv)], axis=0)
        k_rhs = jnp.concatenate([klo_scr[r0:r0 + 2 * BLOCK, kv_sl],
                                 khi_scr[r0:r0 + 2 * BLOCK, kv_sl]], axis=0)
        return lax.dot_general(q2, k_rhs, nt_dims, preferred_element_type=F32)

    s_next = scores(*units[0])
    for n, (j, kv) in enumerate(units):
        r0 = j * BLOCK
        kv_sl = slice(kv * LANES, (kv + 1) * LANES)
        q_sl = q_slices(kv)
        s = s_next
        if n + 1 < len(units):
            s_next = scores(*units[n + 1])
        else:
            gate_a_halves = [_sigmoid(proj(OFF_GA, D_MODEL // 2))]
        v_rhs = jnp.concatenate(
            [jnp.concatenate([vlo_scr[r0:r0 + 2 * BLOCK, kv_sl], ones_lo], axis=1),
             jnp.concatenate([vhi_scr[r0:r0 + 2 * BLOCK, kv_sl], ones_hi], axis=1)], axis=0)
        p_rows, sink_terms = [], []
        for pr in range(2):
            rows = slice(pr * BLOCK, (pr + 1) * BLOCK)
            p_tiles, e_sink = [], []
            for half in range(2):
                hd = kv * GROUP + 2 * pr + half
                s_prev = s[rows, (2 * half) * BLOCK:(2 * half + 1) * BLOCK]
                s_cur = s[rows, (2 * half + 1) * BLOCK:(2 * half + 2) * BLOCK]
                if j == 0:
                    s_prev = jnp.where(s_idx > 0, s_prev, NEG_INF)
                t = jnp.where(from_cur, s_cur, s_prev)
                m = jnp.max(t, axis=-1, keepdims=True)
                p = jnp.exp2(t - m).astype(BF16)
                zero = jnp.zeros_like(p)
                p_tiles += [jnp.where(from_cur, zero, p), jnp.where(from_cur, p, zero)]
                e_sink.append(jnp.exp2(sinks_ref[hd] * LOG2E - m))
            p_rows.append(jnp.concatenate(p_tiles, axis=1))
            sink_terms.append(jnp.where(lane_lo, e_sink[0], e_sink[1]))
        o = jnp.dot(jnp.concatenate(p_rows, axis=0), v_rhs, preferred_element_type=F32)
        if n + 1 == len(units):
            gate_a_halves.append(_sigmoid(proj(OFF_GA + D_MODEL // 2, D_MODEL // 2)))
        for pr in range(2):
            rows = slice(pr * BLOCK, (pr + 1) * BLOCK)
            den = o[rows, LANES:2 * LANES] + sink_terms[pr]
            attn_scr[r0:r0 + BLOCK, q_sl[pr]] = (o[rows, 0:LANES] / den).astype(BF16)

    for scr in kv_scrs:
        scr[0:BLOCK, :] = scr[ts:ts + BLOCK, :]

    attn_out = jnp.dot(attn_scr[...], w_ao_ref[...], preferred_element_type=F32)
    merged = merged + jnp.concatenate(gate_a_halves, axis=1) * attn_out
    out_ref[...] = x + jnp.dot(merged.astype(BF16), w_o_ref[...], preferred_element_type=F32)


def _ffn_kernel(x_ref, g_ref, gf_ref, w_gu_ref, w_d_ref, out_ref, act_scr, *, final_norm):
    x = x_ref[...]
    h = _rms_norm(x, g_ref[...]).astype(BF16)
    for off, width in FF_CHUNKS:
        gate = jnp.dot(h, w_gu_ref[:, off:off + width], preferred_element_type=F32)
        up = jnp.dot(h, w_gu_ref[:, D_FF + off:D_FF + off + width], preferred_element_type=F32)
        act_scr[:, off:off + width] = (gate * _sigmoid(gate) * up).astype(BF16)
    y = x + jnp.dot(act_scr[...], w_d_ref[...], preferred_element_type=F32)
    if final_norm:
        y = _rms_norm(y, gf_ref[...])
    out_ref[...] = y


def _resident(shape):
    return pl.BlockSpec(shape, lambda b, s: (0,) * len(shape), pipeline_mode=pl.Buffered(1))


def _rope_tables(seq):
    half = ROT_DIM // 2
    d = jnp.arange(LANES) % HEAD_DIM
    inv_freq = ROPE_THETA ** (-(2 * (d % half)).astype(F32) / ROT_DIM)
    ang = jnp.arange(seq, dtype=F32)[:, None] * inv_freq[None, :]
    cos, sin = jnp.cos(ang), jnp.sin(ang)
    c = jnp.where(d < ROT_DIM, cos, 1.0)
    s_next = jnp.where(d < half, -sin, 0.0)
    s_prev = jnp.where((d >= half) & (d < ROT_DIM), sin, 0.0)
    kind = jnp.arange(3)[:, None, None]
    return jnp.where(kind == 0, c, jnp.where(kind == 1, s_next, s_prev))


def _mixer(x, g, rope, conv_w, sinks, layer, w_in, w_co, w_ao, w_o, w_gu, w_d):
    b, s, d = x.shape
    ts = SEQ_TILE
    n_s = s // ts
    gu_rows = w_gu.shape[0] // (b * n_s)
    d_rows = 2 * w_d.shape[0] // (b * n_s)
    assert gu_rows % BF16_SUBLANES == 0 and d_rows % BF16_SUBLANES == 0
    gu_slab = pl.BlockSpec((gu_rows, w_gu.shape[1]), lambda i, j: (i * n_s + j, 0))
    d_slab = pl.BlockSpec((d_rows, w_d.shape[1]), lambda i, j: ((i * n_s + j) // 2, 0))
    tile = pl.BlockSpec((None, ts, d), lambda i, j: (i, j, 0))
    kv_scratch = pltpu.VMEM((BLOCK + ts, N_KV_HEADS * LANES), BF16)
    return pl.pallas_call(
        _mixer_kernel,
        grid=(b, n_s),
        in_specs=[
            pl.BlockSpec(memory_space=pltpu.SMEM),
            tile,
            _resident((1, d)),
            pl.BlockSpec((3, ts, LANES), lambda i, j: (0, j, 0)),
            pl.BlockSpec((None, CONV_K, d), lambda i, j: (layer, 0, 0),
                         pipeline_mode=pl.Buffered(1)),
            pl.BlockSpec(memory_space=pl.ANY),
            pl.BlockSpec(memory_space=pl.ANY),
            pl.BlockSpec(memory_space=pl.ANY),
            pl.BlockSpec(memory_space=pl.ANY),
            gu_slab,
            d_slab,
        ],
        out_specs=[tile, gu_slab, d_slab],
        out_shape=[jax.ShapeDtypeStruct(x.shape, x.dtype),
                   jax.ShapeDtypeStruct(w_gu.shape, BF16),
                   jax.ShapeDtypeStruct(w_d.shape, BF16)],
        scratch_shapes=[
            pltpu.VMEM(w_in.shape, BF16),
            pltpu.VMEM(w_co.shape, BF16),
            pltpu.VMEM(w_ao.shape, BF16),
            pltpu.VMEM(w_o.shape, BF16),
            pltpu.VMEM((STAGE_SLOTS, d, STAGE_COLS), F32),
            pltpu.SemaphoreType.DMA((STAGE_SLOTS,)),
            pltpu.VMEM((SUBLANES + ts, d), F32),
            pltpu.VMEM((ts, D_ATTN), BF16),
            kv_scratch, kv_scratch, kv_scratch, kv_scratch,
            pltpu.VMEM((ts, D_ATTN), BF16),
        ],
        compiler_params=pltpu.CompilerParams(
            dimension_semantics=("arbitrary", "arbitrary"),
            vmem_limit_bytes=VMEM_LIMIT_BYTES),
        name="mixer",
    )(sinks, x, g, rope, conv_w, w_in, w_co, w_ao, w_o, w_gu, w_d)


def _ffn(x, g, g_final, w_gu, w_d, final_norm):
    b, s, d = x.shape
    ts = FFN_SEQ_TILE
    tile = pl.BlockSpec((None, ts, d), lambda i, j: (i, j, 0))
    return pl.pallas_call(
        functools.partial(_ffn_kernel, final_norm=final_norm),
        grid=(b, s // ts),
        in_specs=[tile, _resident((1, d)), _resident((1, d)),
                  _resident(w_gu.shape), _resident(w_d.shape)],
        out_specs=tile,
        out_shape=jax.ShapeDtypeStruct(x.shape, x.dtype),
        scratch_shapes=[pltpu.VMEM((ts, D_FF), BF16)],
        compiler_params=pltpu.CompilerParams(
            dimension_semantics=("arbitrary", "arbitrary"),
            vmem_limit_bytes=VMEM_LIMIT_BYTES),
        name="ffn",
    )(x, g, g_final, w_gu, w_d)


def kernel(x, g_mix, w_in, conv_w, attn_sinks, w_conv_out, w_attn_out, w_o,
           g_ffn, w_gate_up, w_down, g_final):
    b, s, d = x.shape
    depth = w_in.shape[0]
    assert d == D_MODEL and s % SEQ_TILE == 0 and s % FFN_SEQ_TILE == 0
    assert w_in.shape[-1] == N_IN and w_gate_up.shape[-1] == 2 * D_FF
    assert N_IN % STAGE_COLS == 0 and d % STAGE_COLS == 0
    rope = _rope_tables(s)
    g_fin = g_final.reshape(1, d)
    for l in range(depth):
        x, w_gu, w_d = _mixer(
            x, g_mix[l].reshape(1, d), rope, conv_w, attn_sinks[l], l,
            w_in[l], w_conv_out[l], w_attn_out[l], w_o[l],
            w_gate_up[l], w_down[l])
        x = _ffn(x, g_ffn[l].reshape(1, d), g_fin, w_gu, w_d, final_norm=(l == depth - 1))
    return x
```

```python
import functools
import math

import jax
import jax.numpy as jnp
from jax import lax
from jax.experimental import pallas as pl
from jax.experimental.pallas import tpu as pltpu

D_MODEL = 1024
CONV_K = 3
HEAD_DIM = 64
N_HEADS = 16
N_KV_HEADS = 4
GROUP = N_HEADS // N_KV_HEADS
D_ATTN = N_HEADS * HEAD_DIM
D_KV = N_KV_HEADS * HEAD_DIM
WINDOW = 128
BLOCK = 128
ROT_DIM = HEAD_DIM // 4
ROPE_THETA = 500000.0
ATTN_SCALE = 1.0 / math.sqrt(HEAD_DIM)
LOG2E = math.log2(math.e)
NEG_INF = -1e30
D_FF = 2816
EPS = 1e-5

OFF_CB = 0
OFF_CC = OFF_CB + D_MODEL
OFF_CX = OFF_CC + D_MODEL
OFF_Q = OFF_CX + D_MODEL
OFF_K = OFF_Q + D_ATTN
OFF_V = OFF_K + D_KV
OFF_GC = OFF_V + D_KV
OFF_GA = OFF_GC + D_MODEL
N_IN = OFF_GA + D_MODEL

LANES = 128
SUBLANES = 8
BF16_SUBLANES = 16
SEQ_TILE = 512
FFN_SEQ_TILE = 1024
STAGE_COLS = 512
STAGE_SLOTS = 4
FF_CHUNKS = ((0, 1024), (1024, 1024), (2048, 768))
VMEM_LIMIT_BYTES = 58 * 1024 * 1024

F32 = jnp.float32
BF16 = jnp.bfloat16

assert WINDOW == BLOCK and 2 * HEAD_DIM == LANES and GROUP == 4


def _rms_norm(x, g):
    ms = jnp.mean(x * x, axis=-1, keepdims=True)
    return x * lax.rsqrt(ms + EPS) * g


def _sigmoid(x):
    return 1.0 / (1.0 + jnp.exp(-x))


def _rope(t, cos, sin_next, sin_prev):
    nxt = pltpu.roll(t, LANES - ROT_DIM // 2, 1)
    prv = pltpu.roll(t, ROT_DIM // 2, 1)
    return t * cos + nxt * sin_next + prv * sin_prev


def _split_head_pair(t, lane_lo):
    sw = pltpu.roll(t, HEAD_DIM, 1)
    zero = jnp.zeros_like(t)
    even = (jnp.where(lane_lo, t, zero), jnp.where(lane_lo, zero, sw))
    odd = (jnp.where(lane_lo, sw, zero), jnp.where(lane_lo, zero, t))
    return even, odd


class _WeightStager:
    def __init__(self, jobs, stage_ref, sem_ref):
        self.jobs = jobs
        self.stage_ref, self.sem_ref = stage_ref, sem_ref
        self.slots = stage_ref.shape[0]
        self.started = self.done = 0
        for _ in range(min(self.slots - 1, len(jobs))):
            self._start_next()

    def _copy(self, n):
        _, src, _, col = self.jobs[n]
        slot = n % self.slots
        return pltpu.make_async_copy(
            src.at[:, pl.ds(col, STAGE_COLS)], self.stage_ref.at[slot], self.sem_ref.at[slot])

    def _start_next(self):
        self._copy(self.started).start()
        self.started += 1

    def _finish_next(self):
        n = self.done
        if self.started < len(self.jobs):
            self._start_next()
        self._copy(n).wait()
        _, _, dst, col = self.jobs[n]
        dst[:, col:col + STAGE_COLS] = self.stage_ref[n % self.slots].astype(BF16)
        self.done += 1

    def need(self, key, lo, hi):
        last = max(i for i, (k, _, _, col) in enumerate(self.jobs)
                   if k == key and lo < col + STAGE_COLS and col < hi)
        while self.done <= last:
            self._finish_next()

    def finish(self):
        while self.done < len(self.jobs):
            self._finish_next()


def _stage_jobs(w_in, w_co, w_ao, w_o):
    d = D_MODEL
    plan = (("in", OFF_Q, D_ATTN), ("in", OFF_K, 2 * D_KV), ("in", OFF_CC, d), ("in", OFF_CX, d),
            ("in", OFF_CB, d), ("in", OFF_GC, d), ("co", 0, d), ("in", OFF_GA, d),
            ("ao", 0, d), ("o", 0, d))
    refs = {"in": w_in, "co": w_co, "ao": w_ao, "o": w_o}
    return [(key, refs[key][0], refs[key][1], col)
            for key, off, width in plan for col in range(off, off + width, STAGE_COLS)]


def _mixer_kernel(sinks_ref, x_ref, g_ref, rope_ref, convw_ref, w_in_hbm,
                  w_co_hbm, w_ao_hbm, w_o_hbm, w_gu_f32_ref, w_d_f32_ref,
                  out_ref, w_gu_bf16_ref, w_d_bf16_ref,
                  w_in_ref, w_co_ref, w_ao_ref, w_o_ref, stage_ref, stage_sem,
                  u_scr, q_scr, klo_scr, khi_scr, vlo_scr, vhi_scr, attn_scr):
    s_idx = pl.program_id(1)
    kv_scrs = (klo_scr, khi_scr, vlo_scr, vhi_scr)
    first_step = (pl.program_id(0) == 0) & (s_idx == 0)

    @pl.when(s_idx == 0)
    def _():
        u_scr[0:SUBLANES, :] = jnp.zeros((SUBLANES, D_MODEL), F32)
        for scr in kv_scrs:
            scr[0:BLOCK, :] = jnp.zeros((BLOCK, N_KV_HEADS * LANES), BF16)

    w_gu_bf16_ref[...] = w_gu_f32_ref[...].astype(BF16)
    w_d_bf16_ref[...] = w_d_f32_ref[...].astype(BF16)

    tile = functools.partial(
        _mixer_tile, s_idx, sinks_ref, x_ref, g_ref, rope_ref, convw_ref,
        w_in_ref, w_co_ref, w_ao_ref, w_o_ref, out_ref, u_scr, q_scr, kv_scrs, attn_scr)

    @pl.when(first_step)
    def _():
        stager = _WeightStager(
            _stage_jobs((w_in_hbm, w_in_ref), (w_co_hbm, w_co_ref),
                        (w_ao_hbm, w_ao_ref), (w_o_hbm, w_o_ref)), stage_ref, stage_sem)
        tile(stager)
        stager.finish()

    @pl.when(jnp.logical_not(first_step))
    def _():
        tile(None)


def _mixer_tile(s_idx, sinks_ref, x_ref, g_ref, rope_ref, convw_ref,
                w_in_ref, w_co_ref, w_ao_ref, w_o_ref, out_ref,
                u_scr, q_scr, kv_scrs, attn_scr, stager):
    ts = x_ref.shape[0]
    n_blk = ts // BLOCK
    klo_scr, khi_scr, vlo_scr, vhi_scr = kv_scrs

    def ready(key, lo, hi):
        if stager is not None:
            stager.need(key, lo, hi)

    x = x_ref[...]
    h = _rms_norm(x, g_ref[...]).astype(BF16)

    def proj(off, width):
        ready("in", off, off + width)
        return jnp.dot(h, w_in_ref[:, off:off + width], preferred_element_type=F32)

    q_tabs = [rope_ref[i] * (ATTN_SCALE * LOG2E) for i in range(3)]
    q = proj(OFF_Q, D_ATTN)
    for t in range(D_ATTN // LANES):
        sl = slice(t * LANES, (t + 1) * LANES)
        q_scr[:, sl] = _rope(q[:, sl], *q_tabs).astype(BF16)
    lane_lo_t = lax.broadcasted_iota(jnp.int32, (ts, LANES), 1) < HEAD_DIM
    k = proj(OFF_K, D_KV)
    v = proj(OFF_V, D_KV)
    for t in range(D_KV // LANES):
        sl = slice(t * LANES, (t + 1) * LANES)
        k_t = _rope(k[:, sl], rope_ref[0], rope_ref[1], rope_ref[2])
        for src, lo_scr, hi_scr in ((k_t, klo_scr, khi_scr), (v[:, sl], vlo_scr, vhi_scr)):
            for hd, (lo, hi) in zip((2 * t, 2 * t + 1), _split_head_pair(src, lane_lo_t)):
                hsl = slice(hd * LANES, (hd + 1) * LANES)
                lo_scr[BLOCK:BLOCK + ts, hsl] = lo.astype(BF16)
                hi_scr[BLOCK:BLOCK + ts, hsl] = hi.astype(BF16)

    u = proj(OFF_CC, D_MODEL) * proj(OFF_CX, D_MODEL)
    u_scr[SUBLANES:SUBLANES + ts, :] = u
    conv = (convw_ref[0:1, :] * u_scr[SUBLANES - 2:SUBLANES - 2 + ts, :]
            + convw_ref[1:2, :] * u_scr[SUBLANES - 1:SUBLANES - 1 + ts, :]
            + convw_ref[2:3, :] * u)
    u_scr[0:SUBLANES, :] = u_scr[ts:ts + SUBLANES, :]
    conv_y = (proj(OFF_CB, D_MODEL) * conv).astype(BF16)
    gate_c = _sigmoid(proj(OFF_GC, D_MODEL))
    ready("co", 0, D_MODEL)
    merged = gate_c * jnp.dot(conv_y, w_co_ref[...], preferred_element_type=F32)

    lane = lax.broadcasted_iota(jnp.int32, (BLOCK, LANES), 1)
    rowi = lax.broadcasted_iota(jnp.int32, (BLOCK, LANES), 0)
    from_cur = lane <= rowi
    lane_lo = lane < HEAD_DIM
    lane2 = lax.broadcasted_iota(jnp.int32, (2 * BLOCK, LANES), 1)
    ones_lo = jnp.where(lane2 < HEAD_DIM, 1.0, 0.0).astype(BF16)
    ones_hi = jnp.where(lane2 < HEAD_DIM, 0.0, 1.0).astype(BF16)
    nt_dims = (((1,), (1,)), ((), ()))

    units = [(j, kv) for j in range(n_blk) for kv in range(N_KV_HEADS)]

    def q_slices(kv):
        return [slice((2 * kv + pr) * LANES, (2 * kv + pr + 1) * LANES) for pr in range(2)]

    def scores(j, kv):
        r0 = j * BLOCK
        kv_sl = slice(kv * LANES, (kv + 1) * LANES)
        q2 = jnp.concatenate([q_scr[r0:r0 + BLOCK, sl] for sl in q_slices(kv)], axis=0)
        k_rhs = jnp.concatenate([klo_scr[r0:r0 + 2 * BLOCK, kv_sl],
                                 khi_scr[r0:r0 + 2 * BLOCK, kv_sl]], axis=0)
        return lax.dot_general(q2, k_rhs, nt_dims, preferred_element_type=F32)

    s_next = scores(*units[0])
    for n, (j, kv) in enumerate(units):
        r0 = j * BLOCK
        kv_sl = slice(kv * LANES, (kv + 1) * LANES)
        q_sl = q_slices(kv)
        s = s_next
        if n + 1 < len(units):
            s_next = scores(*units[n + 1])
        else:
            gate_a_halves = [_sigmoid(proj(OFF_GA, D_MODEL // 2))]
        v_rhs = jnp.concatenate(
            [jnp.concatenate([vlo_scr[r0:r0 + 2 * BLOCK, kv_sl], ones_lo], axis=1),
             jnp.concatenate([vhi_scr[r0:r0 + 2 * BLOCK, kv_sl], ones_hi], axis=1)], axis=0)
        p_rows, sink_terms = [], []
        for pr in range(2):
            rows = slice(pr * BLOCK, (pr + 1) * BLOCK)
            p_tiles, e_sink = [], []
            for half in range(2):
                hd = kv * GROUP + 2 * pr + half
                s_prev = s[rows, (2 * half) * BLOCK:(2 * half + 1) * BLOCK]
                s_cur = s[rows, (2 * half + 1) * BLOCK:(2 * half + 2) * BLOCK]
                if j == 0:
                    s_prev = jnp.where(s_idx > 0, s_prev, NEG_INF)
                t = jnp.where(from_cur, s_cur, s_prev)
                m = jnp.max(t, axis=-1, keepdims=True)
                p = jnp.exp2(t - m).astype(BF16)
                zero = jnp.zeros_like(p)
                p_tiles += [jnp.where(from_cur, zero, p), jnp.where(from_cur, p, zero)]
                e_sink.append(jnp.exp2(sinks_ref[hd] * LOG2E - m))
            p_rows.append(jnp.concatenate(p_tiles, axis=1))
            sink_terms.append(jnp.where(lane_lo, e_sink[0], e_sink[1]))
        o = jnp.dot(jnp.concatenate(p_rows, axis=0), v_rhs, preferred_element_type=F32)
        if n + 1 == len(units):
            gate_a_halves.append(_sigmoid(proj(OFF_GA + D_MODEL // 2, D_MODEL // 2)))
        for pr in range(2):
            rows = slice(pr * BLOCK, (pr + 1) * BLOCK)
            den = o[rows, LANES:2 * LANES] + sink_terms[pr]
            attn_scr[r0:r0 + BLOCK, q_sl[pr]] = (o[rows, 0:LANES] / den).astype(BF16)

    for scr in kv_scrs:
        scr[0:BLOCK, :] = scr[ts:ts + BLOCK, :]

    ready("ao", 0, D_MODEL)
    attn_out = jnp.dot(attn_scr[...], w_ao_ref[...], preferred_element_type=F32)
    merged = merged + jnp.concatenate(gate_a_halves, axis=1) * attn_out
    ready("o", 0, D_MODEL)
    out_ref[...] = x + jnp.dot(merged.astype(BF16), w_o_ref[...], preferred_element_type=F32)


def _ffn_kernel(x_ref, g_ref, gf_ref, w_gu_ref, w_d_ref, out_ref, act_scr, *, final_norm):
    x = x_ref[...]
    h = _rms_norm(x, g_ref[...]).astype(BF16)
    for off, width in FF_CHUNKS:
        gate = jnp.dot(h, w_gu_ref[:, off:off + width], preferred_element_type=F32)
        up = jnp.dot(h, w_gu_ref[:, D_FF + off:D_FF + off + width], preferred_element_type=F32)
        act_scr[:, off:off + width] = (gate * _sigmoid(gate) * up).astype(BF16)
    y = x + jnp.dot(act_scr[...], w_d_ref[...], preferred_element_type=F32)
    if final_norm:
        y = _rms_norm(y, gf_ref[...])
    out_ref[...] = y


def _resident(shape):
    return pl.BlockSpec(shape, lambda b, s: (0,) * len(shape), pipeline_mode=pl.Buffered(1))


def _rope_tables(seq):
    half = ROT_DIM // 2
    d = jnp.arange(LANES) % HEAD_DIM
    inv_freq = ROPE_THETA ** (-(2 * (d % half)).astype(F32) / ROT_DIM)
    ang = jnp.arange(seq, dtype=F32)[:, None] * inv_freq[None, :]
    cos, sin = jnp.cos(ang), jnp.sin(ang)
    c = jnp.where(d < ROT_DIM, cos, 1.0)
    s_next = jnp.where(d < half, -sin, 0.0)
    s_prev = jnp.where((d >= half) & (d < ROT_DIM), sin, 0.0)
    kind = jnp.arange(3)[:, None, None]
    return jnp.where(kind == 0, c, jnp.where(kind == 1, s_next, s_prev))


def _mixer(x, g, rope, conv_w, sinks, layer, w_in, w_co, w_ao, w_o, w_gu, w_d):
    b, s, d = x.shape
    ts = SEQ_TILE
    n_s = s // ts
    gu_rows = w_gu.shape[0] // (b * n_s)
    d_rows = 2 * w_d.shape[0] // (b * n_s)
    assert gu_rows % BF16_SUBLANES == 0 and d_rows % BF16_SUBLANES == 0
    gu_slab = pl.BlockSpec((gu_rows, w_gu.shape[1]), lambda i, j: (i * n_s + j, 0))
    d_slab = pl.BlockSpec((d_rows, w_d.shape[1]), lambda i, j: ((i * n_s + j) // 2, 0))
    tile = pl.BlockSpec((None, ts, d), lambda i, j: (i, j, 0))
    kv_scratch = pltpu.VMEM((BLOCK + ts, N_KV_HEADS * LANES), BF16)
    return pl.pallas_call(
        _mixer_kernel,
        grid=(b, n_s),
        in_specs=[
            pl.BlockSpec(memory_space=pltpu.SMEM),
            tile,
            _resident((1, d)),
            pl.BlockSpec((3, ts, LANES), lambda i, j: (0, j, 0)),
            pl.BlockSpec((None, CONV_K, d), lambda i, j: (layer, 0, 0),
                         pipeline_mode=pl.Buffered(1)),
            pl.BlockSpec(memory_space=pl.ANY),
            pl.BlockSpec(memory_space=pl.ANY),
            pl.BlockSpec(memory_space=pl.ANY),
            pl.BlockSpec(memory_space=pl.ANY),
            gu_slab,
            d_slab,
        ],
        out_specs=[tile, gu_slab, d_slab],
        out_shape=[jax.ShapeDtypeStruct(x.shape, x.dtype),
                   jax.ShapeDtypeStruct(w_gu.shape, BF16),
                   jax.ShapeDtypeStruct(w_d.shape, BF16)],
        scratch_shapes=[
            pltpu.VMEM(w_in.shape, BF16),
            pltpu.VMEM(w_co.shape, BF16),
            pltpu.VMEM(w_ao.shape, BF16),
            pltpu.VMEM(w_o.shape, BF16),
            pltpu.VMEM((STAGE_SLOTS, d, STAGE_COLS), F32),
            pltpu.SemaphoreType.DMA((STAGE_SLOTS,)),
            pltpu.VMEM((SUBLANES + ts, d), F32),
            pltpu.VMEM((ts, D_ATTN), BF16),
            kv_scratch, kv_scratch, kv_scratch, kv_scratch,
            pltpu.VMEM((ts, D_ATTN), BF16),
        ],
        compiler_params=pltpu.CompilerParams(
            dimension_semantics=("arbitrary", "arbitrary"),
            vmem_limit_bytes=VMEM_LIMIT_BYTES),
        name="mixer",
    )(sinks, x, g, rope, conv_w, w_in, w_co, w_ao, w_o, w_gu, w_d)


def _ffn(x, g, g_final, w_gu, w_d, final_norm):
    b, s, d = x.shape
    ts = FFN_SEQ_TILE
    tile = pl.BlockSpec((None, ts, d), lambda i, j: (i, j, 0))
    return pl.pallas_call(
        functools.partial(_ffn_kernel, final_norm=final_norm),
        grid=(b, s // ts),
        in_specs=[tile, _resident((1, d)), _resident((1, d)),
                  _resident(w_gu.shape), _resident(w_d.shape)],
        out_specs=tile,
        out_shape=jax.ShapeDtypeStruct(x.shape, x.dtype),
        scratch_shapes=[pltpu.VMEM((ts, D_FF), BF16)],
        compiler_params=pltpu.CompilerParams(
            dimension_semantics=("arbitrary", "arbitrary"),
            vmem_limit_bytes=VMEM_LIMIT_BYTES),
        name="ffn",
    )(x, g, g_final, w_gu, w_d)


def kernel(x, g_mix, w_in, conv_w, attn_sinks, w_conv_out, w_attn_out, w_o,
           g_ffn, w_gate_up, w_down, g_final):
    b, s, d = x.shape
    depth = w_in.shape[0]
    assert d == D_MODEL and s % SEQ_TILE == 0 and s % FFN_SEQ_TILE == 0
    assert w_in.shape[-1] == N_IN and w_gate_up.shape[-1] == 2 * D_FF
    assert N_IN % STAGE_COLS == 0 and d % STAGE_COLS == 0
    rope = _rope_tables(s)
    g_fin = g_final.reshape(1, d)
    for l in range(depth):
        x, w_gu, w_d = _mixer(
            x, g_mix[l].reshape(1, d), rope, conv_w, attn_sinks[l], l,
            w_in[l], w_conv_out[l], w_attn_out[l], w_o[l],
            w_gate_up[l], w_down[l])
        x = _ffn(x, g_ffn[l].reshape(1, d), g_fin, w_gu, w_d, final_norm=(l == depth - 1))
    return x
```

```python
import functools
import math

import jax
import jax.numpy as jnp
from jax import lax
from jax.experimental import pallas as pl
from jax.experimental.pallas import tpu as pltpu

D_MODEL = 1024
CONV_K = 3
HEAD_DIM = 64
N_HEADS = 16
N_KV_HEADS = 4
GROUP = N_HEADS // N_KV_HEADS
D_ATTN = N_HEADS * HEAD_DIM
D_KV = N_KV_HEADS * HEAD_DIM
WINDOW = 128
BLOCK = 128
ROT_DIM = HEAD_DIM // 4
ROPE_THETA = 500000.0
ATTN_SCALE = 1.0 / math.sqrt(HEAD_DIM)
LOG2E = math.log2(math.e)
NEG_INF = -1e30
D_FF = 2816
EPS = 1e-5

OFF_CB = 0
OFF_CC = OFF_CB + D_MODEL
OFF_CX = OFF_CC + D_MODEL
OFF_Q = OFF_CX + D_MODEL
OFF_K = OFF_Q + D_ATTN
OFF_V = OFF_K + D_KV
OFF_GC = OFF_V + D_KV
OFF_GA = OFF_GC + D_MODEL
N_IN = OFF_GA + D_MODEL

LANES = 128
SUBLANES = 8
BF16_SUBLANES = 16
SEQ_TILE = 512
FFN_SEQ_TILE = 1024
STAGE_COLS = 512
STAGE_SLOTS = 4
FF_CHUNKS = ((0, 1024), (1024, 1024), (2048, 768))
VMEM_LIMIT_BYTES = 58 * 1024 * 1024

F32 = jnp.float32
BF16 = jnp.bfloat16

assert WINDOW == BLOCK and 2 * HEAD_DIM == LANES and GROUP == 4


def _rms_norm(x, g):
    ms = jnp.mean(x * x, axis=-1, keepdims=True)
    return x * lax.rsqrt(ms + EPS) * g


def _sigmoid(x):
    return 1.0 / (1.0 + jnp.exp(-x))


def _rope(t, cos, sin_next, sin_prev):
    nxt = pltpu.roll(t, LANES - ROT_DIM // 2, 1)
    prv = pltpu.roll(t, ROT_DIM // 2, 1)
    return t * cos + nxt * sin_next + prv * sin_prev


def _split_head_pair(t, lane_lo):
    sw = pltpu.roll(t, HEAD_DIM, 1)
    zero = jnp.zeros_like(t)
    even = (jnp.where(lane_lo, t, zero), jnp.where(lane_lo, zero, sw))
    odd = (jnp.where(lane_lo, sw, zero), jnp.where(lane_lo, zero, t))
    return even, odd


def _stage_weights(pairs, stage_ref, sem_ref):
    slots = stage_ref.shape[0]
    jobs = [(src, dst, c * STAGE_COLS)
            for src, dst in pairs for c in range(src.shape[1] // STAGE_COLS)]

    def slab_copy(n):
        src, _, col = jobs[n]
        return pltpu.make_async_copy(
            src.at[:, pl.ds(col, STAGE_COLS)], stage_ref.at[n % slots], sem_ref.at[n % slots])

    for n in range(min(slots - 1, len(jobs))):
        slab_copy(n).start()
    for n, (_, dst, col) in enumerate(jobs):
        if n + slots - 1 < len(jobs):
            slab_copy(n + slots - 1).start()
        slab_copy(n).wait()
        dst[:, col:col + STAGE_COLS] = stage_ref[n % slots].astype(BF16)


def _mixer_kernel(sinks_ref, x_ref, g_ref, rope_ref, convw_ref, w_in_hbm,
                  w_co_hbm, w_ao_hbm, w_o_hbm, w_gu_f32_ref, w_d_f32_ref,
                  out_ref, w_gu_bf16_ref, w_d_bf16_ref,
                  w_in_ref, w_co_ref, w_ao_ref, w_o_ref, stage_ref, stage_sem,
                  u_scr, q_scr, klo_scr, khi_scr, vlo_scr, vhi_scr, attn_scr):
    ts = x_ref.shape[0]
    n_blk = ts // BLOCK
    s_idx = pl.program_id(1)
    kv_scrs = (klo_scr, khi_scr, vlo_scr, vhi_scr)

    @pl.when((pl.program_id(0) == 0) & (s_idx == 0))
    def _():
        _stage_weights(((w_in_hbm, w_in_ref), (w_co_hbm, w_co_ref),
                        (w_ao_hbm, w_ao_ref), (w_o_hbm, w_o_ref)), stage_ref, stage_sem)

    @pl.when(s_idx == 0)
    def _():
        u_scr[0:SUBLANES, :] = jnp.zeros((SUBLANES, D_MODEL), F32)
        for scr in kv_scrs:
            scr[0:BLOCK, :] = jnp.zeros((BLOCK, N_KV_HEADS * LANES), BF16)

    w_gu_bf16_ref[...] = w_gu_f32_ref[...].astype(BF16)
    w_d_bf16_ref[...] = w_d_f32_ref[...].astype(BF16)

    x = x_ref[...]
    h = _rms_norm(x, g_ref[...]).astype(BF16)

    def proj(off, width):
        return jnp.dot(h, w_in_ref[:, off:off + width], preferred_element_type=F32)

    q_tabs = [rope_ref[i] * (ATTN_SCALE * LOG2E) for i in range(3)]
    q = proj(OFF_Q, D_ATTN)
    for t in range(D_ATTN // LANES):
        sl = slice(t * LANES, (t + 1) * LANES)
        q_scr[:, sl] = _rope(q[:, sl], *q_tabs).astype(BF16)
    lane_lo_t = lax.broadcasted_iota(jnp.int32, (ts, LANES), 1) < HEAD_DIM
    k = proj(OFF_K, D_KV)
    v = proj(OFF_V, D_KV)
    for t in range(D_KV // LANES):
        sl = slice(t * LANES, (t + 1) * LANES)
        k_t = _rope(k[:, sl], rope_ref[0], rope_ref[1], rope_ref[2])
        for src, lo_scr, hi_scr in ((k_t, klo_scr, khi_scr), (v[:, sl], vlo_scr, vhi_scr)):
            for hd, (lo, hi) in zip((2 * t, 2 * t + 1), _split_head_pair(src, lane_lo_t)):
                hsl = slice(hd * LANES, (hd + 1) * LANES)
                lo_scr[BLOCK:BLOCK + ts, hsl] = lo.astype(BF16)
                hi_scr[BLOCK:BLOCK + ts, hsl] = hi.astype(BF16)

    u = proj(OFF_CC, D_MODEL) * proj(OFF_CX, D_MODEL)
    u_scr[SUBLANES:SUBLANES + ts, :] = u
    conv = (convw_ref[0:1, :] * u_scr[SUBLANES - 2:SUBLANES - 2 + ts, :]
            + convw_ref[1:2, :] * u_scr[SUBLANES - 1:SUBLANES - 1 + ts, :]
            + convw_ref[2:3, :] * u)
    u_scr[0:SUBLANES, :] = u_scr[ts:ts + SUBLANES, :]
    conv_y = (proj(OFF_CB, D_MODEL) * conv).astype(BF16)
    gate_c = _sigmoid(proj(OFF_GC, D_MODEL))
    merged = gate_c * jnp.dot(conv_y, w_co_ref[...], preferred_element_type=F32)

    lane = lax.broadcasted_iota(jnp.int32, (BLOCK, LANES), 1)
    rowi = lax.broadcasted_iota(jnp.int32, (BLOCK, LANES), 0)
    from_cur = lane <= rowi
    lane_lo = lane < HEAD_DIM
    lane2 = lax.broadcasted_iota(jnp.int32, (2 * BLOCK, LANES), 1)
    ones_lo = jnp.where(lane2 < HEAD_DIM, 1.0, 0.0).astype(BF16)
    ones_hi = jnp.where(lane2 < HEAD_DIM, 0.0, 1.0).astype(BF16)
    nt_dims = (((1,), (1,)), ((), ()))

    units = [(j, kv) for j in range(n_blk) for kv in range(N_KV_HEADS)]

    def q_slices(kv):
        return [slice((2 * kv + pr) * LANES, (2 * kv + pr + 1) * LANES) for pr in range(2)]

    def scores(j, kv):
        r0 = j * BLOCK
        kv_sl = slice(kv * LANES, (kv + 1) * LANES)
        q2 = jnp.concatenate([q_scr[r0:r0 + BLOCK, sl] for sl in q_slices(kv)], axis=0)
        k_rhs = jnp.concatenate([klo_scr[r0:r0 + 2 * BLOCK, kv_sl],
                                 khi_scr[r0:r0 + 2 * BLOCK, kv_sl]], axis=0)
        return lax.dot_general(q2, k_rhs, nt_dims, preferred_element_type=F32)

    s_next = scores(*units[0])
    for n, (j, kv) in enumerate(units):
        r0 = j * BLOCK
        kv_sl = slice(kv * LANES, (kv + 1) * LANES)
        q_sl = q_slices(kv)
        s = s_next
        if n + 1 < len(units):
            s_next = scores(*units[n + 1])
        else:
            gate_a_halves = [_sigmoid(proj(OFF_GA, D_MODEL // 2))]
        v_rhs = jnp.concatenate(
            [jnp.concatenate([vlo_scr[r0:r0 + 2 * BLOCK, kv_sl], ones_lo], axis=1),
             jnp.concatenate([vhi_scr[r0:r0 + 2 * BLOCK, kv_sl], ones_hi], axis=1)], axis=0)
        p_rows, sink_terms = [], []
        for pr in range(2):
            rows = slice(pr * BLOCK, (pr + 1) * BLOCK)
            p_tiles, e_sink = [], []
            for half in range(2):
                hd = kv * GROUP + 2 * pr + half
                s_prev = s[rows, (2 * half) * BLOCK:(2 * half + 1) * BLOCK]
                s_cur = s[rows, (2 * half + 1) * BLOCK:(2 * half + 2) * BLOCK]
                if j == 0:
                    s_prev = jnp.where(s_idx > 0, s_prev, NEG_INF)
                t = jnp.where(from_cur, s_cur, s_prev)
                m = jnp.max(t, axis=-1, keepdims=True)
                p = jnp.exp2(t - m).astype(BF16)
                zero = jnp.zeros_like(p)
                p_tiles += [jnp.where(from_cur, zero, p), jnp.where(from_cur, p, zero)]
                e_sink.append(jnp.exp2(sinks_ref[hd] * LOG2E - m))
            p_rows.append(jnp.concatenate(p_tiles, axis=1))
            sink_terms.append(jnp.where(lane_lo, e_sink[0], e_sink[1]))
        o = jnp.dot(jnp.concatenate(p_rows, axis=0), v_rhs, preferred_element_type=F32)
        if n + 1 == len(units):
            gate_a_halves.append(_sigmoid(proj(OFF_GA + D_MODEL // 2, D_MODEL // 2)))
        for pr in range(2):
            rows = slice(pr * BLOCK, (pr + 1) * BLOCK)
            den = o[rows, LANES:2 * LANES] + sink_terms[pr]
            attn_scr[r0:r0 + BLOCK, q_sl[pr]] = (o[rows, 0:LANES] / den).astype(BF16)

    for scr in kv_scrs:
        scr[0:BLOCK, :] = scr[ts:ts + BLOCK, :]

    attn_out = jnp.dot(attn_scr[...], w_ao_ref[...], preferred_element_type=F32)
    merged = merged + jnp.concatenate(gate_a_halves, axis=1) * attn_out
    out_ref[...] = x_ref[...] + jnp.dot(
        merged.astype(BF16), w_o_ref[...], preferred_element_type=F32)


def _ffn_kernel(x_ref, g_ref, gf_ref, w_gu_ref, w_d_ref, out_ref, act_scr, *, final_norm):
    x = x_ref[...]
    h = _rms_norm(x, g_ref[...]).astype(BF16)
    for off, width in FF_CHUNKS:
        gate = jnp.dot(h, w_gu_ref[:, off:off + width], preferred_element_type=F32)
        up = jnp.dot(h, w_gu_ref[:, D_FF + off:D_FF + off + width], preferred_element_type=F32)
        act_scr[:, off:off + width] = (gate * _sigmoid(gate) * up).astype(BF16)
    y = x_ref[...] + jnp.dot(act_scr[...], w_d_ref[...], preferred_element_type=F32)
    if final_norm:
        y = _rms_norm(y, gf_ref[...])
    out_ref[...] = y


def _resident(shape):
    return pl.BlockSpec(shape, lambda b, s: (0,) * len(shape), pipeline_mode=pl.Buffered(1))


def _rope_tables(seq):
    half = ROT_DIM // 2
    d = jnp.arange(LANES) % HEAD_DIM
    inv_freq = ROPE_THETA ** (-(2 * (d % half)).astype(F32) / ROT_DIM)
    ang = jnp.arange(seq, dtype=F32)[:, None] * inv_freq[None, :]
    cos, sin = jnp.cos(ang), jnp.sin(ang)
    c = jnp.where(d < ROT_DIM, cos, 1.0)
    s_next = jnp.where(d < half, -sin, 0.0)
    s_prev = jnp.where((d >= half) & (d < ROT_DIM), sin, 0.0)
    kind = jnp.arange(3)[:, None, None]
    return jnp.where(kind == 0, c, jnp.where(kind == 1, s_next, s_prev))


def _mixer(x, g, rope, conv_w, sinks, layer, w_in, w_co, w_ao, w_o, w_gu, w_d):
    b, s, d = x.shape
    ts = SEQ_TILE
    n_s = s // ts
    gu_rows = w_gu.shape[0] // (b * n_s)
    d_rows = 2 * w_d.shape[0] // (b * n_s)
    assert gu_rows % BF16_SUBLANES == 0 and d_rows % BF16_SUBLANES == 0
    gu_slab = pl.BlockSpec((gu_rows, w_gu.shape[1]), lambda i, j: (i * n_s + j, 0))
    d_slab = pl.BlockSpec((d_rows, w_d.shape[1]), lambda i, j: ((i * n_s + j) // 2, 0))
    tile = pl.BlockSpec((None, ts, d), lambda i, j: (i, j, 0))
    kv_scratch = pltpu.VMEM((BLOCK + ts, N_KV_HEADS * LANES), BF16)
    return pl.pallas_call(
        _mixer_kernel,
        grid=(b, n_s),
        in_specs=[
            pl.BlockSpec(memory_space=pltpu.SMEM),
            tile,
            _resident((1, d)),
            pl.BlockSpec((3, ts, LANES), lambda i, j: (0, j, 0)),
            pl.BlockSpec((None, CONV_K, d), lambda i, j: (layer, 0, 0),
                         pipeline_mode=pl.Buffered(1)),
            pl.BlockSpec(memory_space=pl.ANY),
            pl.BlockSpec(memory_space=pl.ANY),
            pl.BlockSpec(memory_space=pl.ANY),
            pl.BlockSpec(memory_space=pl.ANY),
            gu_slab,
            d_slab,
        ],
        out_specs=[tile, gu_slab, d_slab],
        out_shape=[jax.ShapeDtypeStruct(x.shape, x.dtype),
                   jax.ShapeDtypeStruct(w_gu.shape, BF16),
                   jax.ShapeDtypeStruct(w_d.shape, BF16)],
        scratch_shapes=[
            pltpu.VMEM(w_in.shape, BF16),
            pltpu.VMEM(w_co.shape, BF16),
            pltpu.VMEM(w_ao.shape, BF16),
            pltpu.VMEM(w_o.shape, BF16),
            pltpu.VMEM((STAGE_SLOTS, d, STAGE_COLS), F32),
            pltpu.SemaphoreType.DMA((STAGE_SLOTS,)),
            pltpu.VMEM((SUBLANES + ts, d), F32),
            pltpu.VMEM((ts, D_ATTN), BF16),
            kv_scratch, kv_scratch, kv_scratch, kv_scratch,
            pltpu.VMEM((ts, D_ATTN), BF16),
        ],
        compiler_params=pltpu.CompilerParams(
            dimension_semantics=("arbitrary", "arbitrary"),
            vmem_limit_bytes=VMEM_LIMIT_BYTES),
        name="mixer",
    )(sinks, x, g, rope, conv_w, w_in, w_co, w_ao, w_o, w_gu, w_d)


def _ffn(x, g, g_final, w_gu, w_d, final_norm):
    b, s, d = x.shape
    ts = FFN_SEQ_TILE
    tile = pl.BlockSpec((None, ts, d), lambda i, j: (i, j, 0))
    return pl.pallas_call(
        functools.partial(_ffn_kernel, final_norm=final_norm),
        grid=(b, s // ts),
        in_specs=[tile, _resident((1, d)), _resident((1, d)),
                  _resident(w_gu.shape), _resident(w_d.shape)],
        out_specs=tile,
        out_shape=jax.ShapeDtypeStruct(x.shape, x.dtype),
        scratch_shapes=[pltpu.VMEM((ts, D_FF), BF16)],
        compiler_params=pltpu.CompilerParams(
            dimension_semantics=("arbitrary", "arbitrary"),
            vmem_limit_bytes=VMEM_LIMIT_BYTES),
        name="ffn",
    )(x, g, g_final, w_gu, w_d)


def kernel(x, g_mix, w_in, conv_w, attn_sinks, w_conv_out, w_attn_out, w_o,
           g_ffn, w_gate_up, w_down, g_final):
    b, s, d = x.shape
    depth = w_in.shape[0]
    assert d == D_MODEL and s % SEQ_TILE == 0 and s % FFN_SEQ_TILE == 0
    assert w_in.shape[-1] == N_IN and w_gate_up.shape[-1] == 2 * D_FF
    assert N_IN % STAGE_COLS == 0 and d % STAGE_COLS == 0
    rope = _rope_tables(s)
    g_fin = g_final.reshape(1, d)
    for l in range(depth):
        x, w_gu, w_d = _mixer(
            x, g_mix[l].reshape(1, d), rope, conv_w, attn_sinks[l], l,
            w_in[l], w_conv_out[l], w_attn_out[l], w_o[l],
            w_gate_up[l], w_down[l])
        x = _ffn(x, g_ffn[l].reshape(1, d), g_fin, w_gu, w_d, final_norm=(l == depth - 1))
    return x
```

```python
import functools
import math

import jax
import jax.numpy as jnp
from jax import lax
from jax.experimental import pallas as pl
from jax.experimental.pallas import tpu as pltpu

D_MODEL = 1024
CONV_K = 3
HEAD_DIM = 64
N_HEADS = 16
N_KV_HEADS = 4
GROUP = N_HEADS // N_KV_HEADS
D_ATTN = N_HEADS * HEAD_DIM
D_KV = N_KV_HEADS * HEAD_DIM
WINDOW = 128
BLOCK = 128
ROT_DIM = HEAD_DIM // 4
ROPE_THETA = 500000.0
ATTN_SCALE = 1.0 / math.sqrt(HEAD_DIM)
LOG2E = math.log2(math.e)
NEG_INF = -1e30
D_FF = 2816
EPS = 1e-5

OFF_CB = 0
OFF_CC = OFF_CB + D_MODEL
OFF_CX = OFF_CC + D_MODEL
OFF_Q = OFF_CX + D_MODEL
OFF_K = OFF_Q + D_ATTN
OFF_V = OFF_K + D_KV
OFF_GC = OFF_V + D_KV
OFF_GA = OFF_GC + D_MODEL
N_IN = OFF_GA + D_MODEL

LANES = 128
SUBLANES = 8
BF16_SUBLANES = 16
SEQ_TILE = 512
FFN_SEQ_TILE = 1024
STAGE_COLS = 512
STAGE_SLOTS = 4
FF_CHUNKS = ((0, 1024), (1024, 1024), (2048, 768))
VMEM_LIMIT_BYTES = 58 * 1024 * 1024

F32 = jnp.float32
BF16 = jnp.bfloat16

assert WINDOW == BLOCK and 2 * HEAD_DIM == LANES and GROUP == 4


def _rms_norm(x, g):
    ms = jnp.mean(x * x, axis=-1, keepdims=True)
    return x * lax.rsqrt(ms + EPS) * g


def _sigmoid(x):
    return 1.0 / (1.0 + jnp.exp(-x))


def _rope(t, cos, sin_next, sin_prev):
    nxt = pltpu.roll(t, LANES - ROT_DIM // 2, 1)
    prv = pltpu.roll(t, ROT_DIM // 2, 1)
    return t * cos + nxt * sin_next + prv * sin_prev


def _split_head_pair(t, lane_lo):
    sw = pltpu.roll(t, HEAD_DIM, 1)
    zero = jnp.zeros_like(t)
    even = (jnp.where(lane_lo, t, zero), jnp.where(lane_lo, zero, sw))
    odd = (jnp.where(lane_lo, sw, zero), jnp.where(lane_lo, zero, t))
    return even, odd


def _stage_weights(pairs, stage_ref, sem_ref):
    slots = stage_ref.shape[0]
    jobs = [(src, dst, c * STAGE_COLS)
            for src, dst in pairs for c in range(src.shape[1] // STAGE_COLS)]

    def slab_copy(n):
        src, _, col = jobs[n]
        return pltpu.make_async_copy(
            src.at[:, pl.ds(col, STAGE_COLS)], stage_ref.at[n % slots], sem_ref.at[n % slots])

    for n in range(min(slots - 1, len(jobs))):
        slab_copy(n).start()
    for n, (_, dst, col) in enumerate(jobs):
        if n + slots - 1 < len(jobs):
            slab_copy(n + slots - 1).start()
        slab_copy(n).wait()
        dst[:, col:col + STAGE_COLS] = stage_ref[n % slots].astype(BF16)


def _mixer_kernel(sinks_ref, x_ref, g_ref, rope_ref, convw_ref, w_in_hbm,
                  w_co_hbm, w_ao_hbm, w_o_hbm, w_gu_f32_ref, w_d_f32_ref,
                  out_ref, w_gu_bf16_ref, w_d_bf16_ref,
                  w_in_ref, w_co_ref, w_ao_ref, w_o_ref, stage_ref, stage_sem,
                  u_scr, q_scr, klo_scr, khi_scr, vlo_scr, vhi_scr, attn_scr):
    ts = x_ref.shape[0]
    n_blk = ts // BLOCK
    s_idx = pl.program_id(1)
    kv_scrs = (klo_scr, khi_scr, vlo_scr, vhi_scr)

    @pl.when((pl.program_id(0) == 0) & (s_idx == 0))
    def _():
        _stage_weights(((w_in_hbm, w_in_ref), (w_co_hbm, w_co_ref),
                        (w_ao_hbm, w_ao_ref), (w_o_hbm, w_o_ref)), stage_ref, stage_sem)

    @pl.when(s_idx == 0)
    def _():
        u_scr[0:SUBLANES, :] = jnp.zeros((SUBLANES, D_MODEL), F32)
        for scr in kv_scrs:
            scr[0:BLOCK, :] = jnp.zeros((BLOCK, N_KV_HEADS * LANES), BF16)

    w_gu_bf16_ref[...] = w_gu_f32_ref[...].astype(BF16)
    w_d_bf16_ref[...] = w_d_f32_ref[...].astype(BF16)

    x = x_ref[...]
    h = _rms_norm(x, g_ref[...]).astype(BF16)

    def proj(off, width):
        return jnp.dot(h, w_in_ref[:, off:off + width], preferred_element_type=F32)

    q_tabs = [rope_ref[i] * (ATTN_SCALE * LOG2E) for i in range(3)]
    q = proj(OFF_Q, D_ATTN)
    for t in range(D_ATTN // LANES):
        sl = slice(t * LANES, (t + 1) * LANES)
        q_scr[:, sl] = _rope(q[:, sl], *q_tabs).astype(BF16)
    lane_lo_t = lax.broadcasted_iota(jnp.int32, (ts, LANES), 1) < HEAD_DIM
    k = proj(OFF_K, D_KV)
    v = proj(OFF_V, D_KV)
    for t in range(D_KV // LANES):
        sl = slice(t * LANES, (t + 1) * LANES)
        k_t = _rope(k[:, sl], rope_ref[0], rope_ref[1], rope_ref[2])
        for src, lo_scr, hi_scr in ((k_t, klo_scr, khi_scr), (v[:, sl], vlo_scr, vhi_scr)):
            for hd, (lo, hi) in zip((2 * t, 2 * t + 1), _split_head_pair(src, lane_lo_t)):
                hsl = slice(hd * LANES, (hd + 1) * LANES)
                lo_scr[BLOCK:BLOCK + ts, hsl] = lo.astype(BF16)
                hi_scr[BLOCK:BLOCK + ts, hsl] = hi.astype(BF16)

    u = proj(OFF_CC, D_MODEL) * proj(OFF_CX, D_MODEL)
    u_scr[SUBLANES:SUBLANES + ts, :] = u
    conv = (convw_ref[0:1, :] * u_scr[SUBLANES - 2:SUBLANES - 2 + ts, :]
            + convw_ref[1:2, :] * u_scr[SUBLANES - 1:SUBLANES - 1 + ts, :]
            + convw_ref[2:3, :] * u)
    u_scr[0:SUBLANES, :] = u_scr[ts:ts + SUBLANES, :]
    conv_y = (proj(OFF_CB, D_MODEL) * conv).astype(BF16)
    gate_c = _sigmoid(proj(OFF_GC, D_MODEL))
    merged = gate_c * jnp.dot(conv_y, w_co_ref[...], preferred_element_type=F32)

    lane = lax.broadcasted_iota(jnp.int32, (BLOCK, LANES), 1)
    rowi = lax.broadcasted_iota(jnp.int32, (BLOCK, LANES), 0)
    from_cur = lane <= rowi
    lane_lo = lane < HEAD_DIM
    lane2 = lax.broadcasted_iota(jnp.int32, (2 * BLOCK, LANES), 1)
    ones_lo = jnp.where(lane2 < HEAD_DIM, 1.0, 0.0).astype(BF16)
    ones_hi = jnp.where(lane2 < HEAD_DIM, 0.0, 1.0).astype(BF16)
    nt_dims = (((1,), (1,)), ((), ()))

    units = [(j, kv) for j in range(n_blk) for kv in range(N_KV_HEADS)]

    def q_slices(kv):
        return [slice((2 * kv + pr) * LANES, (2 * kv + pr + 1) * LANES) for pr in range(2)]

    def scores(j, kv):
        r0 = j * BLOCK
        kv_sl = slice(kv * LANES, (kv + 1) * LANES)
        q2 = jnp.concatenate([q_scr[r0:r0 + BLOCK, sl] for sl in q_slices(kv)], axis=0)
        k_rhs = jnp.concatenate([klo_scr[r0:r0 + 2 * BLOCK, kv_sl],
                                 khi_scr[r0:r0 + 2 * BLOCK, kv_sl]], axis=0)
        return lax.dot_general(q2, k_rhs, nt_dims, preferred_element_type=F32)

    s_next = scores(*units[0])
    for n, (j, kv) in enumerate(units):
        r0 = j * BLOCK
        kv_sl = slice(kv * LANES, (kv + 1) * LANES)
        q_sl = q_slices(kv)
        s = s_next
        if n + 1 < len(units):
            s_next = scores(*units[n + 1])
        else:
            gate_a_halves = [_sigmoid(proj(OFF_GA, D_MODEL // 2))]
        v_rhs = jnp.concatenate(
            [jnp.concatenate([vlo_scr[r0:r0 + 2 * BLOCK, kv_sl], ones_lo], axis=1),
             jnp.concatenate([vhi_scr[r0:r0 + 2 * BLOCK, kv_sl], ones_hi], axis=1)], axis=0)
        p_rows, sink_terms = [], []
        for pr in range(2):
            rows = slice(pr * BLOCK, (pr + 1) * BLOCK)
            p_tiles, e_sink = [], []
            for half in range(2):
                hd = kv * GROUP + 2 * pr + half
                s_prev = s[rows, (2 * half) * BLOCK:(2 * half + 1) * BLOCK]
                s_cur = s[rows, (2 * half + 1) * BLOCK:(2 * half + 2) * BLOCK]
                if j == 0:
                    s_prev = jnp.where(s_idx > 0, s_prev, NEG_INF)
                t = jnp.where(from_cur, s_cur, s_prev)
                m = jnp.max(t, axis=-1, keepdims=True)
                p = jnp.exp2(t - m).astype(BF16)
                zero = jnp.zeros_like(p)
                p_tiles += [jnp.where(from_cur, zero, p), jnp.where(from_cur, p, zero)]
                e_sink.append(jnp.exp2(sinks_ref[hd] * LOG2E - m))
            p_rows.append(jnp.concatenate(p_tiles, axis=1))
            sink_terms.append(jnp.where(lane_lo, e_sink[0], e_sink[1]))
        o_pairs = [jnp.dot(p, v_rhs, preferred_element_type=F32) for p in p_rows]
        if n + 1 == len(units):
            gate_a_halves.append(_sigmoid(proj(OFF_GA + D_MODEL // 2, D_MODEL // 2)))
        for pr, o in enumerate(o_pairs):
            den = o[:, LANES:2 * LANES] + sink_terms[pr]
            attn_scr[r0:r0 + BLOCK, q_sl[pr]] = (o[:, 0:LANES] / den).astype(BF16)

    for scr in kv_scrs:
        scr[0:BLOCK, :] = scr[ts:ts + BLOCK, :]

    attn_out = jnp.dot(attn_scr[...], w_ao_ref[...], preferred_element_type=F32)
    merged = merged + jnp.concatenate(gate_a_halves, axis=1) * attn_out
    out_ref[...] = x + jnp.dot(merged.astype(BF16), w_o_ref[...], preferred_element_type=F32)


def _ffn_kernel(x_ref, g_ref, gf_ref, w_gu_ref, w_d_ref, out_ref, act_scr, *, final_norm):
    x = x_ref[...]
    h = _rms_norm(x, g_ref[...]).astype(BF16)
    for off, width in FF_CHUNKS:
        gate = jnp.dot(h, w_gu_ref[:, off:off + width], preferred_element_type=F32)
        up = jnp.dot(h, w_gu_ref[:, D_FF + off:D_FF + off + width], preferred_element_type=F32)
        act_scr[:, off:off + width] = (gate * _sigmoid(gate) * up).astype(BF16)
    y = x + jnp.dot(act_scr[...], w_d_ref[...], preferred_element_type=F32)
    if final_norm:
        y = _rms_norm(y, gf_ref[...])
    out_ref[...] = y


def _resident(shape):
    return pl.BlockSpec(shape, lambda b, s: (0,) * len(shape), pipeline_mode=pl.Buffered(1))


def _rope_tables(seq):
    half = ROT_DIM // 2
    d = jnp.arange(LANES) % HEAD_DIM
    inv_freq = ROPE_THETA ** (-(2 * (d % half)).astype(F32) / ROT_DIM)
    ang = jnp.arange(seq, dtype=F32)[:, None] * inv_freq[None, :]
    cos, sin = jnp.cos(ang), jnp.sin(ang)
    c = jnp.where(d < ROT_DIM, cos, 1.0)
    s_next = jnp.where(d < half, -sin, 0.0)
    s_prev = jnp.where((d >= half) & (d < ROT_DIM), sin, 0.0)
    kind = jnp.arange(3)[:, None, None]
    return jnp.where(kind == 0, c, jnp.where(kind == 1, s_next, s_prev))


def _mixer(x, g, rope, conv_w, sinks, layer, w_in, w_co, w_ao, w_o, w_gu, w_d):
    b, s, d = x.shape
    ts = SEQ_TILE
    n_s = s // ts
    gu_rows = w_gu.shape[0] // (b * n_s)
    d_rows = 2 * w_d.shape[0] // (b * n_s)
    assert gu_rows % BF16_SUBLANES == 0 and d_rows % BF16_SUBLANES == 0
    gu_slab = pl.BlockSpec((gu_rows, w_gu.shape[1]), lambda i, j: (i * n_s + j, 0))
    d_slab = pl.BlockSpec((d_rows, w_d.shape[1]), lambda i, j: ((i * n_s + j) // 2, 0))
    tile = pl.BlockSpec((None, ts, d), lambda i, j: (i, j, 0))
    kv_scratch = pltpu.VMEM((BLOCK + ts, N_KV_HEADS * LANES), BF16)
    return pl.pallas_call(
        _mixer_kernel,
        grid=(b, n_s),
        in_specs=[
            pl.BlockSpec(memory_space=pltpu.SMEM),
            tile,
            _resident((1, d)),
            pl.BlockSpec((3, ts, LANES), lambda i, j: (0, j, 0)),
            pl.BlockSpec((None, CONV_K, d), lambda i, j: (layer, 0, 0),
                         pipeline_mode=pl.Buffered(1)),
            pl.BlockSpec(memory_space=pl.ANY),
            pl.BlockSpec(memory_space=pl.ANY),
            pl.BlockSpec(memory_space=pl.ANY),
            pl.BlockSpec(memory_space=pl.ANY),
            gu_slab,
            d_slab,
        ],
        out_specs=[tile, gu_slab, d_slab],
        out_shape=[jax.ShapeDtypeStruct(x.shape, x.dtype),
                   jax.ShapeDtypeStruct(w_gu.shape, BF16),
                   jax.ShapeDtypeStruct(w_d.shape, BF16)],
        scratch_shapes=[
            pltpu.VMEM(w_in.shape, BF16),
            pltpu.VMEM(w_co.shape, BF16),
            pltpu.VMEM(w_ao.shape, BF16),
            pltpu.VMEM(w_o.shape, BF16),
            pltpu.VMEM((STAGE_SLOTS, d, STAGE_COLS), F32),
            pltpu.SemaphoreType.DMA((STAGE_SLOTS,)),
            pltpu.VMEM((SUBLANES + ts, d), F32),
            pltpu.VMEM((ts, D_ATTN), BF16),
            kv_scratch, kv_scratch, kv_scratch, kv_scratch,
            pltpu.VMEM((ts, D_ATTN), BF16),
        ],
        compiler_params=pltpu.CompilerParams(
            dimension_semantics=("arbitrary", "arbitrary"),
            vmem_limit_bytes=VMEM_LIMIT_BYTES),
        name="mixer",
    )(sinks, x, g, rope, conv_w, w_in, w_co, w_ao, w_o, w_gu, w_d)


def _ffn(x, g, g_final, w_gu, w_d, final_norm):
    b, s, d = x.shape
    ts = FFN_SEQ_TILE
    tile = pl.BlockSpec((None, ts, d), lambda i, j: (i, j, 0))
    return pl.pallas_call(
        functools.partial(_ffn_kernel, final_norm=final_norm),
        grid=(b, s // ts),
        in_specs=[tile, _resident((1, d)), _resident((1, d)),
                  _resident(w_gu.shape), _resident(w_d.shape)],
        out_specs=tile,
        out_shape=jax.ShapeDtypeStruct(x.shape, x.dtype),
        scratch_shapes=[pltpu.VMEM((ts, D_FF), BF16)],
        compiler_params=pltpu.CompilerParams(
            dimension_semantics=("arbitrary", "arbitrary"),
            vmem_limit_bytes=VMEM_LIMIT_BYTES),
        name="ffn",
    )(x, g, g_final, w_gu, w_d)


def kernel(x, g_mix, w_in, conv_w, attn_sinks, w_conv_out, w_attn_out, w_o,
           g_ffn, w_gate_up, w_down, g_final):
    b, s, d = x.shape
    depth = w_in.shape[0]
    assert d == D_MODEL and s % SEQ_TILE == 0 and s % FFN_SEQ_TILE == 0
    assert w_in.shape[-1] == N_IN and w_gate_up.shape[-1] == 2 * D_FF
    assert N_IN % STAGE_COLS == 0 and d % STAGE_COLS == 0
    rope = _rope_tables(s)
    g_fin = g_final.reshape(1, d)
    for l in range(depth):
        x, w_gu, w_d = _mixer(
            x, g_mix[l].reshape(1, d), rope, conv_w, attn_sinks[l], l,
            w_in[l], w_conv_out[l], w_attn_out[l], w_o[l],
            w_gate_up[l], w_down[l])
        x = _ffn(x, g_ffn[l].reshape(1, d), g_fin, w_gu, w_d, final_norm=(l == depth - 1))
    return x
```

```python
import functools
import math

import jax
import jax.numpy as jnp
from jax import lax
from jax.experimental import pallas as pl
from jax.experimental.pallas import tpu as pltpu

D_MODEL = 1024
CONV_K = 3
HEAD_DIM = 64
N_HEADS = 16
N_KV_HEADS = 4
GROUP = N_HEADS // N_KV_HEADS
D_ATTN = N_HEADS * HEAD_DIM
D_KV = N_KV_HEADS * HEAD_DIM
WINDOW = 128
BLOCK = 128
ROT_DIM = HEAD_DIM // 4
ROPE_THETA = 500000.0
ATTN_SCALE = 1.0 / math.sqrt(HEAD_DIM)
LOG2E = math.log2(math.e)
NEG_INF = -1e30
D_FF = 2816
EPS = 1e-5

OFF_CB = 0
OFF_CC = OFF_CB + D_MODEL
OFF_CX = OFF_CC + D_MODEL
OFF_Q = OFF_CX + D_MODEL
OFF_K = OFF_Q + D_ATTN
OFF_V = OFF_K + D_KV
OFF_GC = OFF_V + D_KV
OFF_GA = OFF_GC + D_MODEL
N_IN = OFF_GA + D_MODEL

LANES = 128
SUBLANES = 8
BF16_SUBLANES = 16
SEQ_TILE = 512
FFN_SEQ_TILE = 1024
STAGE_COLS = 512
STAGE_SLOTS = 4
FF_CHUNKS = ((0, 1024), (1024, 1024), (2048, 768))
VMEM_LIMIT_BYTES = 58 * 1024 * 1024

F32 = jnp.float32
BF16 = jnp.bfloat16

assert WINDOW == BLOCK and 2 * HEAD_DIM == LANES and GROUP == 4


def _rms_norm(x, g):
    ms = jnp.mean(x * x, axis=-1, keepdims=True)
    return x * lax.rsqrt(ms + EPS) * g


def _sigmoid(x):
    return 1.0 / (1.0 + jnp.exp(-x))


def _rope(t, cos, sin_next, sin_prev):
    nxt = pltpu.roll(t, LANES - ROT_DIM // 2, 1)
    prv = pltpu.roll(t, ROT_DIM // 2, 1)
    return t * cos + nxt * sin_next + prv * sin_prev


def _split_head_pair(t, lane_lo):
    sw = pltpu.roll(t, HEAD_DIM, 1)
    zero = jnp.zeros_like(t)
    even = (jnp.where(lane_lo, t, zero), jnp.where(lane_lo, zero, sw))
    odd = (jnp.where(lane_lo, sw, zero), jnp.where(lane_lo, zero, t))
    return even, odd


def _stage_weights(pairs, stage_ref, sem_ref):
    slots = stage_ref.shape[0]
    jobs = [(src, dst, c * STAGE_COLS)
            for src, dst in pairs for c in range(src.shape[1] // STAGE_COLS)]

    def slab_copy(n):
        src, _, col = jobs[n]
        return pltpu.make_async_copy(
            src.at[:, pl.ds(col, STAGE_COLS)], stage_ref.at[n % slots], sem_ref.at[n % slots])

    for n in range(min(slots - 1, len(jobs))):
        slab_copy(n).start()
    for n, (_, dst, col) in enumerate(jobs):
        if n + slots - 1 < len(jobs):
            slab_copy(n + slots - 1).start()
        slab_copy(n).wait()
        dst[:, col:col + STAGE_COLS] = stage_ref[n % slots].astype(BF16)


def _mixer_kernel(sinks_ref, x_ref, g_ref, rope_ref, convw_ref, w_in_hbm,
                  w_co_hbm, w_ao_hbm, w_o_hbm, w_gu_f32_ref, w_d_f32_ref,
                  out_ref, w_gu_bf16_ref, w_d_bf16_ref,
                  w_in_ref, w_co_ref, w_ao_ref, w_o_ref, stage_ref, stage_sem,
                  u_scr, q_scr, klo_scr, khi_scr, vlo_scr, vhi_scr, attn_scr):
    ts = x_ref.shape[0]
    n_blk = ts // BLOCK
    s_idx = pl.program_id(1)
    kv_scrs = (klo_scr, khi_scr, vlo_scr, vhi_scr)

    @pl.when((pl.program_id(0) == 0) & (s_idx == 0))
    def _():
        _stage_weights(((w_in_hbm, w_in_ref), (w_co_hbm, w_co_ref),
                        (w_ao_hbm, w_ao_ref), (w_o_hbm, w_o_ref)), stage_ref, stage_sem)

    @pl.when(s_idx == 0)
    def _():
        u_scr[0:SUBLANES, :] = jnp.zeros((SUBLANES, D_MODEL), F32)
        for scr in kv_scrs:
            scr[0:BLOCK, :] = jnp.zeros((BLOCK, N_KV_HEADS * LANES), BF16)

    w_gu_bf16_ref[...] = w_gu_f32_ref[...].astype(BF16)
    w_d_bf16_ref[...] = w_d_f32_ref[...].astype(BF16)

    x = x_ref[...]
    h = _rms_norm(x, g_ref[...]).astype(BF16)

    def proj(off, width):
        return jnp.dot(h, w_in_ref[:, off:off + width], preferred_element_type=F32)

    q_tabs = [rope_ref[i] * (ATTN_SCALE * LOG2E) for i in range(3)]
    q = proj(OFF_Q, D_ATTN)
    for t in range(D_ATTN // LANES):
        sl = slice(t * LANES, (t + 1) * LANES)
        q_scr[:, sl] = _rope(q[:, sl], *q_tabs).astype(BF16)
    lane_lo_t = lax.broadcasted_iota(jnp.int32, (ts, LANES), 1) < HEAD_DIM
    k = proj(OFF_K, D_KV)
    v = proj(OFF_V, D_KV)
    for t in range(D_KV // LANES):
        sl = slice(t * LANES, (t + 1) * LANES)
        k_t = _rope(k[:, sl], rope_ref[0], rope_ref[1], rope_ref[2])
        for src, lo_scr, hi_scr in ((k_t, klo_scr, khi_scr), (v[:, sl], vlo_scr, vhi_scr)):
            for hd, (lo, hi) in zip((2 * t, 2 * t + 1), _split_head_pair(src, lane_lo_t)):
                hsl = slice(hd * LANES, (hd + 1) * LANES)
                lo_scr[BLOCK:BLOCK + ts, hsl] = lo.astype(BF16)
                hi_scr[BLOCK:BLOCK + ts, hsl] = hi.astype(BF16)

    u = proj(OFF_CC, D_MODEL) * proj(OFF_CX, D_MODEL)
    u_scr[SUBLANES:SUBLANES + ts, :] = u
    conv = (convw_ref[0:1, :] * u_scr[SUBLANES - 2:SUBLANES - 2 + ts, :]
            + convw_ref[1:2, :] * u_scr[SUBLANES - 1:SUBLANES - 1 + ts, :]
            + convw_ref[2:3, :] * u)
    u_scr[0:SUBLANES, :] = u_scr[ts:ts + SUBLANES, :]
    conv_y = (proj(OFF_CB, D_MODEL) * conv).astype(BF16)
    gate_c = _sigmoid(proj(OFF_GC, D_MODEL))
    merged = gate_c * jnp.dot(conv_y, w_co_ref[...], preferred_element_type=F32)

    lane = lax.broadcasted_iota(jnp.int32, (BLOCK, LANES), 1)
    rowi = lax.broadcasted_iota(jnp.int32, (BLOCK, LANES), 0)
    from_cur = lane <= rowi
    lane_lo = lane < HEAD_DIM
    lane2 = lax.broadcasted_iota(jnp.int32, (2 * BLOCK, LANES), 1)
    ones_lo = jnp.where(lane2 < HEAD_DIM, 1.0, 0.0).astype(BF16)
    ones_hi = jnp.where(lane2 < HEAD_DIM, 0.0, 1.0).astype(BF16)
    nt_dims = (((1,), (1,)), ((), ()))

    units = [(j, kv) for j in range(n_blk) for kv in range(N_KV_HEADS)]

    def q_slices(kv):
        return [slice((2 * kv + pr) * LANES, (2 * kv + pr + 1) * LANES) for pr in range(2)]

    def scores(j, kv):
        r0 = j * BLOCK
        kv_sl = slice(kv * LANES, (kv + 1) * LANES)
        q2 = jnp.concatenate([q_scr[r0:r0 + BLOCK, sl] for sl in q_slices(kv)], axis=0)
        k_rhs = jnp.concatenate([klo_scr[r0:r0 + 2 * BLOCK, kv_sl],
                                 khi_scr[r0:r0 + 2 * BLOCK, kv_sl]], axis=0)
        return lax.dot_general(q2, k_rhs, nt_dims, preferred_element_type=F32)

    s_next = scores(*units[0])
    for n, (j, kv) in enumerate(units):
        r0 = j * BLOCK
        kv_sl = slice(kv * LANES, (kv + 1) * LANES)
        q_sl = q_slices(kv)
        s = s_next
        if n + 1 < len(units):
            s_next = scores(*units[n + 1])
        else:
            gate_a_halves = [_sigmoid(proj(OFF_GA, D_MODEL // 2))]
        v_rhs = jnp.concatenate(
            [jnp.concatenate([vlo_scr[r0:r0 + 2 * BLOCK, kv_sl], ones_lo], axis=1),
             jnp.concatenate([vhi_scr[r0:r0 + 2 * BLOCK, kv_sl], ones_hi], axis=1)], axis=0)
        p_rows, sink_terms = [], []
        for pr in range(2):
            rows = slice(pr * BLOCK, (pr + 1) * BLOCK)
            p_tiles, e_sink = [], []
            for half in range(2):
                hd = kv * GROUP + 2 * pr + half
                s_prev = s[rows, (2 * half) * BLOCK:(2 * half + 1) * BLOCK]
                s_cur = s[rows, (2 * half + 1) * BLOCK:(2 * half + 2) * BLOCK]
                if j == 0:
                    s_prev = jnp.where(s_idx > 0, s_prev, NEG_INF)
                t = jnp.where(from_cur, s_cur, s_prev)
                m = jnp.max(t, axis=-1, keepdims=True)
                p = jnp.exp2(t - m).astype(BF16)
                zero = jnp.zeros_like(p)
                p_tiles += [jnp.where(from_cur, zero, p), jnp.where(from_cur, p, zero)]
                e_sink.append(jnp.exp2(sinks_ref[hd] * LOG2E - m))
            p_rows.append(jnp.concatenate(p_tiles, axis=1))
            sink_terms.append(jnp.where(lane_lo, e_sink[0], e_sink[1]))
        o = jnp.dot(jnp.concatenate(p_rows, axis=0), v_rhs, preferred_element_type=F32)
        if n + 1 == len(units):
            gate_a_halves.append(_sigmoid(proj(OFF_GA + D_MODEL // 2, D_MODEL // 2)))
        for pr in range(2):
            rows = slice(pr * BLOCK, (pr + 1) * BLOCK)
            den = o[rows, LANES:2 * LANES] + sink_terms[pr]
            attn_scr[r0:r0 + BLOCK, q_sl[pr]] = (o[rows, 0:LANES] / den).astype(BF16)

    for scr in kv_scrs:
        scr[0:BLOCK, :] = scr[ts:ts + BLOCK, :]

    attn_out = jnp.dot(attn_scr[...], w_ao_ref[...], preferred_element_type=F32)
    merged = merged + jnp.concatenate(gate_a_halves, axis=1) * attn_out
    out_ref[...] = x + jnp.dot(merged.astype(BF16), w_o_ref[...], preferred_element_type=F32)


def _ffn_kernel(x_ref, g_ref, gf_ref, w_gu_ref, w_d_ref, out_ref, act_scr, *, final_norm):
    x = x_ref[...]
    h = _rms_norm(x, g_ref[...]).astype(BF16)
    for off, width in FF_CHUNKS:
        gate = jnp.dot(h, w_gu_ref[:, off:off + width], preferred_element_type=F32)
        up = jnp.dot(h, w_gu_ref[:, D_FF + off:D_FF + off + width], preferred_element_type=F32)
        act_scr[:, off:off + width] = (gate * _sigmoid(gate) * up).astype(BF16)
    ts = x.shape[0]
    for rows in (slice(0, ts // 2), slice(ts // 2, ts)):
        y = x[rows] + jnp.dot(act_scr[rows, :], w_d_ref[...], preferred_element_type=F32)
        if final_norm:
            y = _rms_norm(y, gf_ref[...])
        out_ref[rows, :] = y


def _resident(shape):
    return pl.BlockSpec(shape, lambda b, s: (0,) * len(shape), pipeline_mode=pl.Buffered(1))


def _rope_tables(seq):
    half = ROT_DIM // 2
    d = jnp.arange(LANES) % HEAD_DIM
    inv_freq = ROPE_THETA ** (-(2 * (d % half)).astype(F32) / ROT_DIM)
    ang = jnp.arange(seq, dtype=F32)[:, None] * inv_freq[None, :]
    cos, sin = jnp.cos(ang), jnp.sin(ang)
    c = jnp.where(d < ROT_DIM, cos, 1.0)
    s_next = jnp.where(d < half, -sin, 0.0)
    s_prev = jnp.where((d >= half) & (d < ROT_DIM), sin, 0.0)
    kind = jnp.arange(3)[:, None, None]
    return jnp.where(kind == 0, c, jnp.where(kind == 1, s_next, s_prev))


def _mixer(x, g, rope, conv_w, sinks, layer, w_in, w_co, w_ao, w_o, w_gu, w_d):
    b, s, d = x.shape
    ts = SEQ_TILE
    n_s = s // ts
    gu_rows = w_gu.shape[0] // (b * n_s)
    d_rows = 2 * w_d.shape[0] // (b * n_s)
    assert gu_rows % BF16_SUBLANES == 0 and d_rows % BF16_SUBLANES == 0
    gu_slab = pl.BlockSpec((gu_rows, w_gu.shape[1]), lambda i, j: (i * n_s + j, 0))
    d_slab = pl.BlockSpec((d_rows, w_d.shape[1]), lambda i, j: ((i * n_s + j) // 2, 0))
    tile = pl.BlockSpec((None, ts, d), lambda i, j: (i, j, 0))
    kv_scratch = pltpu.VMEM((BLOCK + ts, N_KV_HEADS * LANES), BF16)
    return pl.pallas_call(
        _mixer_kernel,
        grid=(b, n_s),
        in_specs=[
            pl.BlockSpec(memory_space=pltpu.SMEM),
            tile,
            _resident((1, d)),
            pl.BlockSpec((3, ts, LANES), lambda i, j: (0, j, 0)),
            pl.BlockSpec((None, CONV_K, d), lambda i, j: (layer, 0, 0),
                         pipeline_mode=pl.Buffered(1)),
            pl.BlockSpec(memory_space=pl.ANY),
            pl.BlockSpec(memory_space=pl.ANY),
            pl.BlockSpec(memory_space=pl.ANY),
            pl.BlockSpec(memory_space=pl.ANY),
            gu_slab,
            d_slab,
        ],
        out_specs=[tile, gu_slab, d_slab],
        out_shape=[jax.ShapeDtypeStruct(x.shape, x.dtype),
                   jax.ShapeDtypeStruct(w_gu.shape, BF16),
                   jax.ShapeDtypeStruct(w_d.shape, BF16)],
        scratch_shapes=[
            pltpu.VMEM(w_in.shape, BF16),
            pltpu.VMEM(w_co.shape, BF16),
            pltpu.VMEM(w_ao.shape, BF16),
            pltpu.VMEM(w_o.shape, BF16),
            pltpu.VMEM((STAGE_SLOTS, d, STAGE_COLS), F32),
            pltpu.SemaphoreType.DMA((STAGE_SLOTS,)),
            pltpu.VMEM((SUBLANES + ts, d), F32),
            pltpu.VMEM((ts, D_ATTN), BF16),
            kv_scratch, kv_scratch, kv_scratch, kv_scratch,
            pltpu.VMEM((ts, D_ATTN), BF16),
        ],
        compiler_params=pltpu.CompilerParams(
            dimension_semantics=("arbitrary", "arbitrary"),
            vmem_limit_bytes=VMEM_LIMIT_BYTES),
        name="mixer",
    )(sinks, x, g, rope, conv_w, w_in, w_co, w_ao, w_o, w_gu, w_d)


def _ffn(x, g, g_final, w_gu, w_d, final_norm):
    b, s, d = x.shape
    ts = FFN_SEQ_TILE
    tile = pl.BlockSpec((None, ts, d), lambda i, j: (i, j, 0))
    return pl.pallas_call(
        functools.partial(_ffn_kernel, final_norm=final_norm),
        grid=(b, s // ts),
        in_specs=[tile, _resident((1, d)), _resident((1, d)),
                  _resident(w_gu.shape), _resident(w_d.shape)],
        out_specs=tile,
        out_shape=jax.ShapeDtypeStruct(x.shape, x.dtype),
        scratch_shapes=[pltpu.VMEM((ts, D_FF), BF16)],
        compiler_params=pltpu.CompilerParams(
            dimension_semantics=("arbitrary", "arbitrary"),
            vmem_limit_bytes=VMEM_LIMIT_BYTES),
        name="ffn",
    )(x, g, g_final, w_gu, w_d)


def kernel(x, g_mix, w_in, conv_w, attn_sinks, w_conv_out, w_attn_out, w_o,
           g_ffn, w_gate_up, w_down, g_final):
    b, s, d = x.shape
    depth = w_in.shape[0]
    assert d == D_MODEL and s % SEQ_TILE == 0 and s % FFN_SEQ_TILE == 0
    assert w_in.shape[-1] == N_IN and w_gate_up.shape[-1] == 2 * D_FF
    assert N_IN % STAGE_COLS == 0 and d % STAGE_COLS == 0
    rope = _rope_tables(s)
    g_fin = g_final.reshape(1, d)
    for l in range(depth):
        x, w_gu, w_d = _mixer(
            x, g_mix[l].reshape(1, d), rope, conv_w, attn_sinks[l], l,
            w_in[l], w_conv_out[l], w_attn_out[l], w_o[l],
            w_gate_up[l], w_down[l])
        x = _ffn(x, g_ffn[l].reshape(1, d), g_fin, w_gu, w_d, final_norm=(l == depth - 1))
    return x
```

```python
import functools
import math

import jax
import jax.numpy as jnp
from jax import lax
from jax.experimental import pallas as pl
from jax.experimental.pallas import tpu as pltpu

D_MODEL = 1024
CONV_K = 3
HEAD_DIM = 64
N_HEADS = 16
N_KV_HEADS = 4
GROUP = N_HEADS // N_KV_HEADS
D_ATTN = N_HEADS * HEAD_DIM
D_KV = N_KV_HEADS * HEAD_DIM
WINDOW = 128
BLOCK = 128
ROT_DIM = HEAD_DIM // 4
ROPE_THETA = 500000.0
ATTN_SCALE = 1.0 / math.sqrt(HEAD_DIM)
LOG2E = math.log2(math.e)
NEG_INF = -1e30
D_FF = 2816
EPS = 1e-5

OFF_CB = 0
OFF_CC = OFF_CB + D_MODEL
OFF_CX = OFF_CC + D_MODEL
OFF_Q = OFF_CX + D_MODEL
OFF_K = OFF_Q + D_ATTN
OFF_V = OFF_K + D_KV
OFF_GC = OFF_V + D_KV
OFF_GA = OFF_GC + D_MODEL
N_IN = OFF_GA + D_MODEL

LANES = 128
SUBLANES = 8
BF16_SUBLANES = 16
SEQ_TILE = 512
FFN_SEQ_TILE = 1024
STAGE_COLS = 512
STAGE_SLOTS = 4
FF_CHUNKS = ((0, 1024), (1024, 1024), (2048, 768))
VMEM_LIMIT_BYTES = 58 * 1024 * 1024

F32 = jnp.float32
BF16 = jnp.bfloat16

assert WINDOW == BLOCK and 2 * HEAD_DIM == LANES and GROUP == 4


def _rms_norm(x, g):
    ms = jnp.mean(x * x, axis=-1, keepdims=True)
    return x * lax.rsqrt(ms + EPS) * g


def _sigmoid(x):
    return 1.0 / (1.0 + jnp.exp(-x))


def _rope(t, cos, sin_next, sin_prev):
    nxt = pltpu.roll(t, LANES - ROT_DIM // 2, 1)
    prv = pltpu.roll(t, ROT_DIM // 2, 1)
    return t * cos + nxt * sin_next + prv * sin_prev


def _split_head_pair(t, lane_lo):
    sw = pltpu.roll(t, HEAD_DIM, 1)
    zero = jnp.zeros_like(t)
    even = (jnp.where(lane_lo, t, zero), jnp.where(lane_lo, zero, sw))
    odd = (jnp.where(lane_lo, sw, zero), jnp.where(lane_lo, zero, t))
    return even, odd


def _stage_weights(pairs, stage_ref, sem_ref):
    slots = stage_ref.shape[0]
    jobs = [(src, dst, c * STAGE_COLS)
            for src, dst in pairs for c in range(src.shape[1] // STAGE_COLS)]

    def slab_copy(n):
        src, _, col = jobs[n]
        return pltpu.make_async_copy(
            src.at[:, pl.ds(col, STAGE_COLS)], stage_ref.at[n % slots], sem_ref.at[n % slots])

    for n in range(min(slots - 1, len(jobs))):
        slab_copy(n).start()
    for n, (_, dst, col) in enumerate(jobs):
        if n + slots - 1 < len(jobs):
            slab_copy(n + slots - 1).start()
        slab_copy(n).wait()
        dst[:, col:col + STAGE_COLS] = stage_ref[n % slots].astype(BF16)


def _mixer_kernel(sinks_ref, x_ref, g_ref, rope_ref, convw_ref, w_in_hbm,
                  w_co_hbm, w_ao_hbm, w_o_hbm, w_gu_f32_ref, w_d_f32_ref,
                  out_ref, w_gu_bf16_ref, w_d_bf16_ref,
                  w_in_ref, w_co_ref, w_ao_ref, w_o_ref, stage_ref, stage_sem,
                  u_scr, q_scr, klo_scr, khi_scr, vlo_scr, vhi_scr, attn_scr):
    ts = x_ref.shape[0]
    n_blk = ts // BLOCK
    s_idx = pl.program_id(1)
    kv_scrs = (klo_scr, khi_scr, vlo_scr, vhi_scr)

    @pl.when((pl.program_id(0) == 0) & (s_idx == 0))
    def _():
        _stage_weights(((w_in_hbm, w_in_ref), (w_co_hbm, w_co_ref),
                        (w_ao_hbm, w_ao_ref), (w_o_hbm, w_o_ref)), stage_ref, stage_sem)

    @pl.when(s_idx == 0)
    def _():
        u_scr[0:SUBLANES, :] = jnp.zeros((SUBLANES, D_MODEL), F32)
        for scr in kv_scrs:
            scr[0:BLOCK, :] = jnp.zeros((BLOCK, N_KV_HEADS * LANES), BF16)

    w_gu_bf16_ref[...] = w_gu_f32_ref[...].astype(BF16)
    w_d_bf16_ref[...] = w_d_f32_ref[...].astype(BF16)

    x = x_ref[...]
    h = _rms_norm(x, g_ref[...]).astype(BF16)

    def proj(off, width):
        return jnp.dot(h, w_in_ref[:, off:off + width], preferred_element_type=F32)

    q_tabs = [rope_ref[i] * (ATTN_SCALE * LOG2E) for i in range(3)]
    q = proj(OFF_Q, D_ATTN)
    for t in range(D_ATTN // LANES):
        sl = slice(t * LANES, (t + 1) * LANES)
        q_scr[:, sl] = _rope(q[:, sl], *q_tabs).astype(BF16)
    lane_lo_t = lax.broadcasted_iota(jnp.int32, (ts, LANES), 1) < HEAD_DIM
    k = proj(OFF_K, D_KV)
    v = proj(OFF_V, D_KV)
    for t in range(D_KV // LANES):
        sl = slice(t * LANES, (t + 1) * LANES)
        k_t = _rope(k[:, sl], rope_ref[0], rope_ref[1], rope_ref[2])
        for src, lo_scr, hi_scr in ((k_t, klo_scr, khi_scr), (v[:, sl], vlo_scr, vhi_scr)):
            for hd, (lo, hi) in zip((2 * t, 2 * t + 1), _split_head_pair(src, lane_lo_t)):
                hsl = slice(hd * LANES, (hd + 1) * LANES)
                lo_scr[BLOCK:BLOCK + ts, hsl] = lo.astype(BF16)
                hi_scr[BLOCK:BLOCK + ts, hsl] = hi.astype(BF16)

    u = proj(OFF_CC, D_MODEL) * proj(OFF_CX, D_MODEL)
    u_scr[SUBLANES:SUBLANES + ts, :] = u
    conv = (convw_ref[0:1, :] * u_scr[SUBLANES - 2:SUBLANES - 2 + ts, :]
            + convw_ref[1:2, :] * u_scr[SUBLANES - 1:SUBLANES - 1 + ts, :]
            + convw_ref[2:3, :] * u)
    u_scr[0:SUBLANES, :] = u_scr[ts:ts + SUBLANES, :]
    conv_y = (proj(OFF_CB, D_MODEL) * conv).astype(BF16)
    gate_c = _sigmoid(proj(OFF_GC, D_MODEL))
    merged = gate_c * jnp.dot(conv_y, w_co_ref[...], preferred_element_type=F32)

    lane = lax.broadcasted_iota(jnp.int32, (BLOCK, LANES), 1)
    rowi = lax.broadcasted_iota(jnp.int32, (BLOCK, LANES), 0)
    from_cur = lane <= rowi
    lane_lo = lane < HEAD_DIM
    lane2 = lax.broadcasted_iota(jnp.int32, (2 * BLOCK, LANES), 1)
    ones_lo = jnp.where(lane2 < HEAD_DIM, 1.0, 0.0).astype(BF16)
    ones_hi = jnp.where(lane2 < HEAD_DIM, 0.0, 1.0).astype(BF16)
    nt_dims = (((1,), (1,)), ((), ()))

    units = [(j, kv) for j in range(n_blk) for kv in range(N_KV_HEADS)]

    def q_slices(kv):
        return [slice((2 * kv + pr) * LANES, (2 * kv + pr + 1) * LANES) for pr in range(2)]

    def scores(j, kv):
        r0 = j * BLOCK
        kv_sl = slice(kv * LANES, (kv + 1) * LANES)
        q2 = jnp.concatenate([q_scr[r0:r0 + BLOCK, sl] for sl in q_slices(kv)], axis=0)
        k_rhs = jnp.concatenate([klo_scr[r0:r0 + 2 * BLOCK, kv_sl],
                                 khi_scr[r0:r0 + 2 * BLOCK, kv_sl]], axis=0)
        return lax.dot_general(q2, k_rhs, nt_dims, preferred_element_type=F32)

    s_next = scores(*units[0])
    gate_a_halves = []
    for n, (j, kv) in enumerate(units):
        r0 = j * BLOCK
        kv_sl = slice(kv * LANES, (kv + 1) * LANES)
        q_sl = q_slices(kv)
        s = s_next
        if n + 1 < len(units):
            s_next = scores(*units[n + 1])
        if n + 2 >= len(units):
            half = n + 2 - len(units)
            gate_a_halves.append(
                _sigmoid(proj(OFF_GA + half * (D_MODEL // 2), D_MODEL // 2)))
        v_rhs = jnp.concatenate(
            [jnp.concatenate([vlo_scr[r0:r0 + 2 * BLOCK, kv_sl], ones_lo], axis=1),
             jnp.concatenate([vhi_scr[r0:r0 + 2 * BLOCK, kv_sl], ones_hi], axis=1)], axis=0)
        p_rows, sink_terms = [], []
        for pr in range(2):
            rows = slice(pr * BLOCK, (pr + 1) * BLOCK)
            p_tiles, e_sink = [], []
            for half in range(2):
                hd = kv * GROUP + 2 * pr + half
                s_prev = s[rows, (2 * half) * BLOCK:(2 * half + 1) * BLOCK]
                s_cur = s[rows, (2 * half + 1) * BLOCK:(2 * half + 2) * BLOCK]
                if j == 0:
                    s_prev = jnp.where(s_idx > 0, s_prev, NEG_INF)
                t = jnp.where(from_cur, s_cur, s_prev)
                m = jnp.max(t, axis=-1, keepdims=True)
                p = jnp.exp2(t - m).astype(BF16)
                zero = jnp.zeros_like(p)
                p_tiles += [jnp.where(from_cur, zero, p), jnp.where(from_cur, p, zero)]
                e_sink.append(jnp.exp2(sinks_ref[hd] * LOG2E - m))
            p_rows.append(jnp.concatenate(p_tiles, axis=1))
            sink_terms.append(jnp.where(lane_lo, e_sink[0], e_sink[1]))
        o = jnp.dot(jnp.concatenate(p_rows, axis=0), v_rhs, preferred_element_type=F32)
        for pr in range(2):
            rows = slice(pr * BLOCK, (pr + 1) * BLOCK)
            den = o[rows, LANES:2 * LANES] + sink_terms[pr]
            attn_scr[r0:r0 + BLOCK, q_sl[pr]] = (o[rows, 0:LANES] / den).astype(BF16)

    for scr in kv_scrs:
        scr[0:BLOCK, :] = scr[ts:ts + BLOCK, :]

    attn_out = jnp.dot(attn_scr[...], w_ao_ref[...], preferred_element_type=F32)
    merged = merged + jnp.concatenate(gate_a_halves, axis=1) * attn_out
    out_ref[...] = x + jnp.dot(merged.astype(BF16), w_o_ref[...], preferred_element_type=F32)


def _ffn_kernel(x_ref, g_ref, gf_ref, w_gu_ref, w_d_ref, out_ref, act_scr, *, final_norm):
    x = x_ref[...]
    h = _rms_norm(x, g_ref[...]).astype(BF16)
    for off, width in FF_CHUNKS:
        gate = jnp.dot(h, w_gu_ref[:, off:off + width], preferred_element_type=F32)
        up = jnp.dot(h, w_gu_ref[:, D_FF + off:D_FF + off + width], preferred_element_type=F32)
        act_scr[:, off:off + width] = (gate * _sigmoid(gate) * up).astype(BF16)
    y = x + jnp.dot(act_scr[...], w_d_ref[...], preferred_element_type=F32)
    if final_norm:
        y = _rms_norm(y, gf_ref[...])
    out_ref[...] = y


def _resident(shape):
    return pl.BlockSpec(shape, lambda b, s: (0,) * len(shape), pipeline_mode=pl.Buffered(1))


def _rope_tables(seq):
    half = ROT_DIM // 2
    d = jnp.arange(LANES) % HEAD_DIM
    inv_freq = ROPE_THETA ** (-(2 * (d % half)).astype(F32) / ROT_DIM)
    ang = jnp.arange(seq, dtype=F32)[:, None] * inv_freq[None, :]
    cos, sin = jnp.cos(ang), jnp.sin(ang)
    c = jnp.where(d < ROT_DIM, cos, 1.0)
    s_next = jnp.where(d < half, -sin, 0.0)
    s_prev = jnp.where((d >= half) & (d < ROT_DIM), sin, 0.0)
    kind = jnp.arange(3)[:, None, None]
    return jnp.where(kind == 0, c, jnp.where(kind == 1, s_next, s_prev))


def _mixer(x, g, rope, conv_w, sinks, layer, w_in, w_co, w_ao, w_o, w_gu, w_d):
    b, s, d = x.shape
    ts = SEQ_TILE
    n_s = s // ts
    gu_rows = w_gu.shape[0] // (b * n_s)
    d_rows = 2 * w_d.shape[0] // (b * n_s)
    assert gu_rows % BF16_SUBLANES == 0 and d_rows % BF16_SUBLANES == 0
    gu_slab = pl.BlockSpec((gu_rows, w_gu.shape[1]), lambda i, j: (i * n_s + j, 0))
    d_slab = pl.BlockSpec((d_rows, w_d.shape[1]), lambda i, j: ((i * n_s + j) // 2, 0))
    tile = pl.BlockSpec((None, ts, d), lambda i, j: (i, j, 0))
    kv_scratch = pltpu.VMEM((BLOCK + ts, N_KV_HEADS * LANES), BF16)
    return pl.pallas_call(
        _mixer_kernel,
        grid=(b, n_s),
        in_specs=[
            pl.BlockSpec(memory_space=pltpu.SMEM),
            tile,
            _resident((1, d)),
            pl.BlockSpec((3, ts, LANES), lambda i, j: (0, j, 0)),
            pl.BlockSpec((None, CONV_K, d), lambda i, j: (layer, 0, 0),
                         pipeline_mode=pl.Buffered(1)),
            pl.BlockSpec(memory_space=pl.ANY),
            pl.BlockSpec(memory_space=pl.ANY),
            pl.BlockSpec(memory_space=pl.ANY),
            pl.BlockSpec(memory_space=pl.ANY),
            gu_slab,
            d_slab,
        ],
        out_specs=[tile, gu_slab, d_slab],
        out_shape=[jax.ShapeDtypeStruct(x.shape, x.dtype),
                   jax.ShapeDtypeStruct(w_gu.shape, BF16),
                   jax.ShapeDtypeStruct(w_d.shape, BF16)],
        scratch_shapes=[
            pltpu.VMEM(w_in.shape, BF16),
            pltpu.VMEM(w_co.shape, BF16),
            pltpu.VMEM(w_ao.shape, BF16),
            pltpu.VMEM(w_o.shape, BF16),
            pltpu.VMEM((STAGE_SLOTS, d, STAGE_COLS), F32),
            pltpu.SemaphoreType.DMA((STAGE_SLOTS,)),
            pltpu.VMEM((SUBLANES + ts, d), F32),
            pltpu.VMEM((ts, D_ATTN), BF16),
            kv_scratch, kv_scratch, kv_scratch, kv_scratch,
            pltpu.VMEM((ts, D_ATTN), BF16),
        ],
        compiler_params=pltpu.CompilerParams(
            dimension_semantics=("arbitrary", "arbitrary"),
            vmem_limit_bytes=VMEM_LIMIT_BYTES),
        name="mixer",
    )(sinks, x, g, rope, conv_w, w_in, w_co, w_ao, w_o, w_gu, w_d)


def _ffn(x, g, g_final, w_gu, w_d, final_norm):
    b, s, d = x.shape
    ts = FFN_SEQ_TILE
    tile = pl.BlockSpec((None, ts, d), lambda i, j: (i, j, 0))
    return pl.pallas_call(
        functools.partial(_ffn_kernel, final_norm=final_norm),
        grid=(b, s // ts),
        in_specs=[tile, _resident((1, d)), _resident((1, d)),
                  _resident(w_gu.shape), _resident(w_d.shape)],
        out_specs=tile,
        out_shape=jax.ShapeDtypeStruct(x.shape, x.dtype),
        scratch_shapes=[pltpu.VMEM((ts, D_FF), BF16)],
        compiler_params=pltpu.CompilerParams(
            dimension_semantics=("arbitrary", "arbitrary"),
            vmem_limit_bytes=VMEM_LIMIT_BYTES),
        name="ffn",
    )(x, g, g_final, w_gu, w_d)


def kernel(x, g_mix, w_in, conv_w, attn_sinks, w_conv_out, w_attn_out, w_o,
           g_ffn, w_gate_up, w_down, g_final):
    b, s, d = x.shape
    depth = w_in.shape[0]
    assert d == D_MODEL and s % SEQ_TILE == 0 and s % FFN_SEQ_TILE == 0
    assert w_in.shape[-1] == N_IN and w_gate_up.shape[-1] == 2 * D_FF
    assert N_IN % STAGE_COLS == 0 and d % STAGE_COLS == 0
    rope = _rope_tables(s)
    g_fin = g_final.reshape(1, d)
    for l in range(depth):
        x, w_gu, w_d = _mixer(
            x, g_mix[l].reshape(1, d), rope, conv_w, attn_sinks[l], l,
            w_in[l], w_conv_out[l], w_attn_out[l], w_o[l],
            w_gate_up[l], w_down[l])
        x = _ffn(x, g_ffn[l].reshape(1, d), g_fin, w_gu, w_d, final_norm=(l == depth - 1))
    return x
```

```python
import functools
import math

import jax
import jax.numpy as jnp
from jax import lax
from jax.experimental import pallas as pl
from jax.experimental.pallas import tpu as pltpu

D_MODEL = 1024
CONV_K = 3
HEAD_DIM = 64
N_HEADS = 16
N_KV_HEADS = 4
GROUP = N_HEADS // N_KV_HEADS
D_ATTN = N_HEADS * HEAD_DIM
D_KV = N_KV_HEADS * HEAD_DIM
WINDOW = 128
BLOCK = 128
ROT_DIM = HEAD_DIM // 4
ROPE_THETA = 500000.0
ATTN_SCALE = 1.0 / math.sqrt(HEAD_DIM)
LOG2E = math.log2(math.e)
NEG_INF = -1e30
D_FF = 2816
EPS = 1e-5

OFF_CB = 0
OFF_CC = OFF_CB + D_MODEL
OFF_CX = OFF_CC + D_MODEL
OFF_Q = OFF_CX + D_MODEL
OFF_K = OFF_Q + D_ATTN
OFF_V = OFF_K + D_KV
OFF_GC = OFF_V + D_KV
OFF_GA = OFF_GC + D_MODEL
N_IN = OFF_GA + D_MODEL

LANES = 128
SUBLANES = 8
BF16_SUBLANES = 16
SEQ_TILE = 512
FFN_SEQ_TILE = 1024
STAGE_COLS = 512
STAGE_SLOTS = 4
FF_CHUNKS = ((0, 1024), (1024, 1024), (2048, 768))
VMEM_LIMIT_BYTES = 58 * 1024 * 1024

F32 = jnp.float32
BF16 = jnp.bfloat16

assert WINDOW == BLOCK and 2 * HEAD_DIM == LANES and GROUP == 4


def _rms_norm(x, g):
    ms = jnp.mean(x * x, axis=-1, keepdims=True)
    return x * lax.rsqrt(ms + EPS) * g


def _sigmoid(x):
    return 0.5 * jnp.tanh(0.5 * x) + 0.5


def _rope(t, cos, sin_next, sin_prev):
    nxt = pltpu.roll(t, LANES - ROT_DIM // 2, 1)
    prv = pltpu.roll(t, ROT_DIM // 2, 1)
    return t * cos + nxt * sin_next + prv * sin_prev


def _split_head_pair(t, lane_lo):
    sw = pltpu.roll(t, HEAD_DIM, 1)
    zero = jnp.zeros_like(t)
    even = (jnp.where(lane_lo, t, zero), jnp.where(lane_lo, zero, sw))
    odd = (jnp.where(lane_lo, sw, zero), jnp.where(lane_lo, zero, t))
    return even, odd


def _stage_weights(pairs, stage_ref, sem_ref):
    slots = stage_ref.shape[0]
    jobs = [(src, dst, c * STAGE_COLS)
            for src, dst in pairs for c in range(src.shape[1] // STAGE_COLS)]

    def slab_copy(n):
        src, _, col = jobs[n]
        return pltpu.make_async_copy(
            src.at[:, pl.ds(col, STAGE_COLS)], stage_ref.at[n % slots], sem_ref.at[n % slots])

    for n in range(min(slots - 1, len(jobs))):
        slab_copy(n).start()
    for n, (_, dst, col) in enumerate(jobs):
        if n + slots - 1 < len(jobs):
            slab_copy(n + slots - 1).start()
        slab_copy(n).wait()
        dst[:, col:col + STAGE_COLS] = stage_ref[n % slots].astype(BF16)


def _mixer_kernel(sinks_ref, x_ref, g_ref, rope_ref, convw_ref, w_in_hbm,
                  w_co_hbm, w_ao_hbm, w_o_hbm, w_gu_f32_ref, w_d_f32_ref,
                  out_ref, w_gu_bf16_ref, w_d_bf16_ref,
                  w_in_ref, w_co_ref, w_ao_ref, w_o_ref, stage_ref, stage_sem,
                  u_scr, q_scr, klo_scr, khi_scr, vlo_scr, vhi_scr, attn_scr):
    ts = x_ref.shape[0]
    n_blk = ts // BLOCK
    s_idx = pl.program_id(1)
    kv_scrs = (klo_scr, khi_scr, vlo_scr, vhi_scr)

    @pl.when((pl.program_id(0) == 0) & (s_idx == 0))
    def _():
        _stage_weights(((w_in_hbm, w_in_ref), (w_co_hbm, w_co_ref),
                        (w_ao_hbm, w_ao_ref), (w_o_hbm, w_o_ref)), stage_ref, stage_sem)

    @pl.when(s_idx == 0)
    def _():
        u_scr[0:SUBLANES, :] = jnp.zeros((SUBLANES, D_MODEL), F32)
        for scr in kv_scrs:
            scr[0:BLOCK, :] = jnp.zeros((BLOCK, N_KV_HEADS * LANES), BF16)

    w_gu_bf16_ref[...] = w_gu_f32_ref[...].astype(BF16)
    w_d_bf16_ref[...] = w_d_f32_ref[...].astype(BF16)

    x = x_ref[...]
    h = _rms_norm(x, g_ref[...]).astype(BF16)

    def proj(off, width):
        return jnp.dot(h, w_in_ref[:, off:off + width], preferred_element_type=F32)

    q_tabs = [rope_ref[i] * (ATTN_SCALE * LOG2E) for i in range(3)]
    q = proj(OFF_Q, D_ATTN)
    for t in range(D_ATTN // LANES):
        sl = slice(t * LANES, (t + 1) * LANES)
        q_scr[:, sl] = _rope(q[:, sl], *q_tabs).astype(BF16)
    lane_lo_t = lax.broadcasted_iota(jnp.int32, (ts, LANES), 1) < HEAD_DIM
    k = proj(OFF_K, D_KV)
    v = proj(OFF_V, D_KV)
    for t in range(D_KV // LANES):
        sl = slice(t * LANES, (t + 1) * LANES)
        k_t = _rope(k[:, sl], rope_ref[0], rope_ref[1], rope_ref[2])
        for src, lo_scr, hi_scr in ((k_t, klo_scr, khi_scr), (v[:, sl], vlo_scr, vhi_scr)):
            for hd, (lo, hi) in zip((2 * t, 2 * t + 1), _split_head_pair(src, lane_lo_t)):
                hsl = slice(hd * LANES, (hd + 1) * LANES)
                lo_scr[BLOCK:BLOCK + ts, hsl] = lo.astype(BF16)
                hi_scr[BLOCK:BLOCK + ts, hsl] = hi.astype(BF16)

    u = proj(OFF_CC, D_MODEL) * proj(OFF_CX, D_MODEL)
    u_scr[SUBLANES:SUBLANES + ts, :] = u
    conv = (convw_ref[0:1, :] * u_scr[SUBLANES - 2:SUBLANES - 2 + ts, :]
            + convw_ref[1:2, :] * u_scr[SUBLANES - 1:SUBLANES - 1 + ts, :]
            + convw_ref[2:3, :] * u)
    u_scr[0:SUBLANES, :] = u_scr[ts:ts + SUBLANES, :]
    conv_y = (proj(OFF_CB, D_MODEL) * conv).astype(BF16)
    gate_c = _sigmoid(proj(OFF_GC, D_MODEL))
    merged = gate_c * jnp.dot(conv_y, w_co_ref[...], preferred_element_type=F32)

    lane = lax.broadcasted_iota(jnp.int32, (BLOCK, LANES), 1)
    rowi = lax.broadcasted_iota(jnp.int32, (BLOCK, LANES), 0)
    from_cur = lane <= rowi
    lane_lo = lane < HEAD_DIM
    lane2 = lax.broadcasted_iota(jnp.int32, (2 * BLOCK, LANES), 1)
    ones_lo = jnp.where(lane2 < HEAD_DIM, 1.0, 0.0).astype(BF16)
    ones_hi = jnp.where(lane2 < HEAD_DIM, 0.0, 1.0).astype(BF16)
    nt_dims = (((1,), (1,)), ((), ()))

    units = [(j, kv) for j in range(n_blk) for kv in range(N_KV_HEADS)]

    def q_slices(kv):
        return [slice((2 * kv + pr) * LANES, (2 * kv + pr + 1) * LANES) for pr in range(2)]

    def scores(j, kv):
        r0 = j * BLOCK
        kv_sl = slice(kv * LANES, (kv + 1) * LANES)
        q2 = jnp.concatenate([q_scr[r0:r0 + BLOCK, sl] for sl in q_slices(kv)], axis=0)
        k_rhs = jnp.concatenate([klo_scr[r0:r0 + 2 * BLOCK, kv_sl],
                                 khi_scr[r0:r0 + 2 * BLOCK, kv_sl]], axis=0)
        return lax.dot_general(q2, k_rhs, nt_dims, preferred_element_type=F32)

    s_next = scores(*units[0])
    for n, (j, kv) in enumerate(units):
        r0 = j * BLOCK
        kv_sl = slice(kv * LANES, (kv + 1) * LANES)
        q_sl = q_slices(kv)
        s = s_next
        if n + 1 < len(units):
            s_next = scores(*units[n + 1])
        else:
            gate_a_halves = [_sigmoid(proj(OFF_GA, D_MODEL // 2))]
        v_rhs = jnp.concatenate(
            [jnp.concatenate([vlo_scr[r0:r0 + 2 * BLOCK, kv_sl], ones_lo], axis=1),
             jnp.concatenate([vhi_scr[r0:r0 + 2 * BLOCK, kv_sl], ones_hi], axis=1)], axis=0)
        p_rows, sink_terms = [], []
        for pr in range(2):
            rows = slice(pr * BLOCK, (pr + 1) * BLOCK)
            p_tiles, e_sink = [], []
            for half in range(2):
                hd = kv * GROUP + 2 * pr + half
                s_prev = s[rows, (2 * half) * BLOCK:(2 * half + 1) * BLOCK]
                s_cur = s[rows, (2 * half + 1) * BLOCK:(2 * half + 2) * BLOCK]
                if j == 0:
                    s_prev = jnp.where(s_idx > 0, s_prev, NEG_INF)
                t = jnp.where(from_cur, s_cur, s_prev)
                m = jnp.max(t, axis=-1, keepdims=True)
                p = jnp.exp2(t - m).astype(BF16)
                zero = jnp.zeros_like(p)
                p_tiles += [jnp.where(from_cur, zero, p), jnp.where(from_cur, p, zero)]
                e_sink.append(jnp.exp2(sinks_ref[hd] * LOG2E - m))
            p_rows.append(jnp.concatenate(p_tiles, axis=1))
            sink_terms.append(jnp.where(lane_lo, e_sink[0], e_sink[1]))
        o = jnp.dot(jnp.concatenate(p_rows, axis=0), v_rhs, preferred_element_type=F32)
        if n + 1 == len(units):
            gate_a_halves.append(_sigmoid(proj(OFF_GA + D_MODEL // 2, D_MODEL // 2)))
        for pr in range(2):
            rows = slice(pr * BLOCK, (pr + 1) * BLOCK)
            den = o[rows, LANES:2 * LANES] + sink_terms[pr]
            attn_scr[r0:r0 + BLOCK, q_sl[pr]] = (o[rows, 0:LANES] / den).astype(BF16)

    for scr in kv_scrs:
        scr[0:BLOCK, :] = scr[ts:ts + BLOCK, :]

    attn_out = jnp.dot(attn_scr[...], w_ao_ref[...], preferred_element_type=F32)
    merged = merged + jnp.concatenate(gate_a_halves, axis=1) * attn_out
    out_ref[...] = x + jnp.dot(merged.astype(BF16), w_o_ref[...], preferred_element_type=F32)


def _ffn_kernel(x_ref, g_ref, gf_ref, w_gu_ref, w_d_ref, out_ref, act_scr, *, final_norm):
    x = x_ref[...]
    h = _rms_norm(x, g_ref[...]).astype(BF16)
    for off, width in FF_CHUNKS:
        gate = jnp.dot(h, w_gu_ref[:, off:off + width], preferred_element_type=F32)
        up = jnp.dot(h, w_gu_ref[:, D_FF + off:D_FF + off + width], preferred_element_type=F32)
        act_scr[:, off:off + width] = (gate * _sigmoid(gate) * up).astype(BF16)
    y = x + jnp.dot(act_scr[...], w_d_ref[...], preferred_element_type=F32)
    if final_norm:
        y = _rms_norm(y, gf_ref[...])
    out_ref[...] = y


def _resident(shape):
    return pl.BlockSpec(shape, lambda b, s: (0,) * len(shape), pipeline_mode=pl.Buffered(1))


def _rope_tables(seq):
    half = ROT_DIM // 2
    d = jnp.arange(LANES) % HEAD_DIM
    inv_freq = ROPE_THETA ** (-(2 * (d % half)).astype(F32) / ROT_DIM)
    ang = jnp.arange(seq, dtype=F32)[:, None] * inv_freq[None, :]
    cos, sin = jnp.cos(ang), jnp.sin(ang)
    c = jnp.where(d < ROT_DIM, cos, 1.0)
    s_next = jnp.where(d < half, -sin, 0.0)
    s_prev = jnp.where((d >= half) & (d < ROT_DIM), sin, 0.0)
    kind = jnp.arange(3)[:, None, None]
    return jnp.where(kind == 0, c, jnp.where(kind == 1, s_next, s_prev))


def _mixer(x, g, rope, conv_w, sinks, layer, w_in, w_co, w_ao, w_o, w_gu, w_d):
    b, s, d = x.shape
    ts = SEQ_TILE
    n_s = s // ts
    gu_rows = w_gu.shape[0] // (b * n_s)
    d_rows = 2 * w_d.shape[0] // (b * n_s)
    assert gu_rows % BF16_SUBLANES == 0 and d_rows % BF16_SUBLANES == 0
    gu_slab = pl.BlockSpec((gu_rows, w_gu.shape[1]), lambda i, j: (i * n_s + j, 0))
    d_slab = pl.BlockSpec((d_rows, w_d.shape[1]), lambda i, j: ((i * n_s + j) // 2, 0))
    tile = pl.BlockSpec((None, ts, d), lambda i, j: (i, j, 0))
    kv_scratch = pltpu.VMEM((BLOCK + ts, N_KV_HEADS * LANES), BF16)
    return pl.pallas_call(
        _mixer_kernel,
        grid=(b, n_s),
        in_specs=[
            pl.BlockSpec(memory_space=pltpu.SMEM),
            tile,
            _resident((1, d)),
            pl.BlockSpec((3, ts, LANES), lambda i, j: (0, j, 0)),
            pl.BlockSpec((None, CONV_K, d), lambda i, j: (layer, 0, 0),
                         pipeline_mode=pl.Buffered(1)),
            pl.BlockSpec(memory_space=pl.ANY),
            pl.BlockSpec(memory_space=pl.ANY),
            pl.BlockSpec(memory_space=pl.ANY),
            pl.BlockSpec(memory_space=pl.ANY),
            gu_slab,
            d_slab,
        ],
        out_specs=[tile, gu_slab, d_slab],
        out_shape=[jax.ShapeDtypeStruct(x.shape, x.dtype),
                   jax.ShapeDtypeStruct(w_gu.shape, BF16),
                   jax.ShapeDtypeStruct(w_d.shape, BF16)],
        scratch_shapes=[
            pltpu.VMEM(w_in.shape, BF16),
            pltpu.VMEM(w_co.shape, BF16),
            pltpu.VMEM(w_ao.shape, BF16),
            pltpu.VMEM(w_o.shape, BF16),
            pltpu.VMEM((STAGE_SLOTS, d, STAGE_COLS), F32),
            pltpu.SemaphoreType.DMA((STAGE_SLOTS,)),
            pltpu.VMEM((SUBLANES + ts, d), F32),
            pltpu.VMEM((ts, D_ATTN), BF16),
            kv_scratch, kv_scratch, kv_scratch, kv_scratch,
            pltpu.VMEM((ts, D_ATTN), BF16),
        ],
        compiler_params=pltpu.CompilerParams(
            dimension_semantics=("arbitrary", "arbitrary"),
            vmem_limit_bytes=VMEM_LIMIT_BYTES),
        name="mixer",
    )(sinks, x, g, rope, conv_w, w_in, w_co, w_ao, w_o, w_gu, w_d)


def _ffn(x, g, g_final, w_gu, w_d, final_norm):
    b, s, d = x.shape
    ts = FFN_SEQ_TILE
    tile = pl.BlockSpec((None, ts, d), lambda i, j: (i, j, 0))
    return pl.pallas_call(
        functools.partial(_ffn_kernel, final_norm=final_norm),
        grid=(b, s // ts),
        in_specs=[tile, _resident((1, d)), _resident((1, d)),
                  _resident(w_gu.shape), _resident(w_d.shape)],
        out_specs=tile,
        out_shape=jax.ShapeDtypeStruct(x.shape, x.dtype),
        scratch_shapes=[pltpu.VMEM((ts, D_FF), BF16)],
        compiler_params=pltpu.CompilerParams(
            dimension_semantics=("arbitrary", "arbitrary"),
            vmem_limit_bytes=VMEM_LIMIT_BYTES),
        name="ffn",
    )(x, g, g_final, w_gu, w_d)


def kernel(x, g_mix, w_in, conv_w, attn_sinks, w_conv_out, w_attn_out, w_o,
           g_ffn, w_gate_up, w_down, g_final):
    b, s, d = x.shape
    depth = w_in.shape[0]
    assert d == D_MODEL and s % SEQ_TILE == 0 and s % FFN_SEQ_TILE == 0
    assert w_in.shape[-1] == N_IN and w_gate_up.shape[-1] == 2 * D_FF
    assert N_IN % STAGE_COLS == 0 and d % STAGE_COLS == 0
    rope = _rope_tables(s)
    g_fin = g_final.reshape(1, d)
    for l in range(depth):
        x, w_gu, w_d = _mixer(
            x, g_mix[l].reshape(1, d), rope, conv_w, attn_sinks[l], l,
            w_in[l], w_conv_out[l], w_attn_out[l], w_o[l],
            w_gate_up[l], w_down[l])
        x = _ffn(x, g_ffn[l].reshape(1, d), g_fin, w_gu, w_d, final_norm=(l == depth - 1))
    return x
```

```python
import functools
import math

import jax
import jax.numpy as jnp
from jax import lax
from jax.experimental import pallas as pl
from jax.experimental.pallas import tpu as pltpu

D_MODEL = 1024
CONV_K = 3
HEAD_DIM = 64
N_HEADS = 16
N_KV_HEADS = 4
GROUP = N_HEADS // N_KV_HEADS
D_ATTN = N_HEADS * HEAD_DIM
D_KV = N_KV_HEADS * HEAD_DIM
WINDOW = 128
BLOCK = 128
ROT_DIM = HEAD_DIM // 4
ROPE_THETA = 500000.0
ATTN_SCALE = 1.0 / math.sqrt(HEAD_DIM)
LOG2E = math.log2(math.e)
NEG_INF = -1e30
D_FF = 2816
EPS = 1e-5

OFF_CB = 0
OFF_CC = OFF_CB + D_MODEL
OFF_CX = OFF_CC + D_MODEL
OFF_Q = OFF_CX + D_MODEL
OFF_K = OFF_Q + D_ATTN
OFF_V = OFF_K + D_KV
OFF_GC = OFF_V + D_KV
OFF_GA = OFF_GC + D_MODEL
N_IN = OFF_GA + D_MODEL

LANES = 128
SUBLANES = 8
BF16_SUBLANES = 16
SEQ_TILE = 512
FFN_SEQ_TILE = 1024
STAGE_COLS = 512
STAGE_SLOTS = 4
FF_CHUNKS = ((0, 1024), (1024, 1024), (2048, 768))
VMEM_LIMIT_BYTES = 58 * 1024 * 1024

F32 = jnp.float32
BF16 = jnp.bfloat16

assert WINDOW == BLOCK and 2 * HEAD_DIM == LANES and GROUP == 4


def _rms_norm(x, g):
    ms = jnp.mean(x * x, axis=-1, keepdims=True)
    return x * lax.rsqrt(ms + EPS) * g


def _sigmoid(x):
    return 0.5 * jnp.tanh(0.5 * x) + 0.5


def _silu(x):
    h = 0.5 * x
    return h * jnp.tanh(h) + h


def _rope(t, cos, sin_next, sin_prev):
    nxt = pltpu.roll(t, LANES - ROT_DIM // 2, 1)
    prv = pltpu.roll(t, ROT_DIM // 2, 1)
    return t * cos + nxt * sin_next + prv * sin_prev


def _split_head_pair(t, lane_lo):
    sw = pltpu.roll(t, HEAD_DIM, 1)
    zero = jnp.zeros_like(t)
    even = (jnp.where(lane_lo, t, zero), jnp.where(lane_lo, zero, sw))
    odd = (jnp.where(lane_lo, sw, zero), jnp.where(lane_lo, zero, t))
    return even, odd


def _stage_weights(pairs, stage_ref, sem_ref):
    slots = stage_ref.shape[0]
    jobs = [(src, dst, c * STAGE_COLS)
            for src, dst in pairs for c in range(src.shape[1] // STAGE_COLS)]

    def slab_copy(n):
        src, _, col = jobs[n]
        return pltpu.make_async_copy(
            src.at[:, pl.ds(col, STAGE_COLS)], stage_ref.at[n % slots], sem_ref.at[n % slots])

    for n in range(min(slots - 1, len(jobs))):
        slab_copy(n).start()
    for n, (_, dst, col) in enumerate(jobs):
        if n + slots - 1 < len(jobs):
            slab_copy(n + slots - 1).start()
        slab_copy(n).wait()
        dst[:, col:col + STAGE_COLS] = stage_ref[n % slots].astype(BF16)


def _mixer_kernel(sinks_ref, x_ref, g_ref, rope_ref, convw_ref, w_in_hbm,
                  w_co_hbm, w_ao_hbm, w_o_hbm, w_gu_f32_ref, w_d_f32_ref,
                  out_ref, w_gu_bf16_ref, w_d_bf16_ref,
                  w_in_ref, w_co_ref, w_ao_ref, w_o_ref, stage_ref, stage_sem,
                  u_scr, q_scr, klo_scr, khi_scr, vlo_scr, vhi_scr, attn_scr):
    ts = x_ref.shape[0]
    n_blk = ts // BLOCK
    s_idx = pl.program_id(1)
    kv_scrs = (klo_scr, khi_scr, vlo_scr, vhi_scr)

    @pl.when((pl.program_id(0) == 0) & (s_idx == 0))
    def _():
        _stage_weights(((w_in_hbm, w_in_ref), (w_co_hbm, w_co_ref),
                        (w_ao_hbm, w_ao_ref), (w_o_hbm, w_o_ref)), stage_ref, stage_sem)

    @pl.when(s_idx == 0)
    def _():
        u_scr[0:SUBLANES, :] = jnp.zeros((SUBLANES, D_MODEL), F32)
        for scr in kv_scrs:
            scr[0:BLOCK, :] = jnp.zeros((BLOCK, N_KV_HEADS * LANES), BF16)

    w_gu_bf16_ref[...] = w_gu_f32_ref[...].astype(BF16)
    w_d_bf16_ref[...] = w_d_f32_ref[...].astype(BF16)

    x = x_ref[...]
    h = _rms_norm(x, g_ref[...]).astype(BF16)

    def proj(off, width):
        return jnp.dot(h, w_in_ref[:, off:off + width], preferred_element_type=F32)

    q_tabs = [rope_ref[i] * (ATTN_SCALE * LOG2E) for i in range(3)]
    q = proj(OFF_Q, D_ATTN)
    for t in range(D_ATTN // LANES):
        sl = slice(t * LANES, (t + 1) * LANES)
        q_scr[:, sl] = _rope(q[:, sl], *q_tabs).astype(BF16)
    lane_lo_t = lax.broadcasted_iota(jnp.int32, (ts, LANES), 1) < HEAD_DIM
    k = proj(OFF_K, D_KV)
    v = proj(OFF_V, D_KV)
    for t in range(D_KV // LANES):
        sl = slice(t * LANES, (t + 1) * LANES)
        k_t = _rope(k[:, sl], rope_ref[0], rope_ref[1], rope_ref[2])
        for src, lo_scr, hi_scr in ((k_t, klo_scr, khi_scr), (v[:, sl], vlo_scr, vhi_scr)):
            for hd, (lo, hi) in zip((2 * t, 2 * t + 1), _split_head_pair(src, lane_lo_t)):
                hsl = slice(hd * LANES, (hd + 1) * LANES)
                lo_scr[BLOCK:BLOCK + ts, hsl] = lo.astype(BF16)
                hi_scr[BLOCK:BLOCK + ts, hsl] = hi.astype(BF16)

    u = proj(OFF_CC, D_MODEL) * proj(OFF_CX, D_MODEL)
    u_scr[SUBLANES:SUBLANES + ts, :] = u
    conv = (convw_ref[0:1, :] * u_scr[SUBLANES - 2:SUBLANES - 2 + ts, :]
            + convw_ref[1:2, :] * u_scr[SUBLANES - 1:SUBLANES - 1 + ts, :]
            + convw_ref[2:3, :] * u)
    u_scr[0:SUBLANES, :] = u_scr[ts:ts + SUBLANES, :]
    conv_y = (proj(OFF_CB, D_MODEL) * conv).astype(BF16)
    gate_c = _sigmoid(proj(OFF_GC, D_MODEL))
    merged = gate_c * jnp.dot(conv_y, w_co_ref[...], preferred_element_type=F32)

    lane = lax.broadcasted_iota(jnp.int32, (BLOCK, LANES), 1)
    rowi = lax.broadcasted_iota(jnp.int32, (BLOCK, LANES), 0)
    from_cur = lane <= rowi
    lane_lo = lane < HEAD_DIM
    lane2 = lax.broadcasted_iota(jnp.int32, (2 * BLOCK, LANES), 1)
    ones_lo = jnp.where(lane2 < HEAD_DIM, 1.0, 0.0).astype(BF16)
    ones_hi = jnp.where(lane2 < HEAD_DIM, 0.0, 1.0).astype(BF16)
    nt_dims = (((1,), (1,)), ((), ()))

    units = [(j, kv) for j in range(n_blk) for kv in range(N_KV_HEADS)]

    def q_slices(kv):
        return [slice((2 * kv + pr) * LANES, (2 * kv + pr + 1) * LANES) for pr in range(2)]

    def scores(j, kv):
        r0 = j * BLOCK
        kv_sl = slice(kv * LANES, (kv + 1) * LANES)
        q2 = jnp.concatenate([q_scr[r0:r0 + BLOCK, sl] for sl in q_slices(kv)], axis=0)
        k_rhs = jnp.concatenate([klo_scr[r0:r0 + 2 * BLOCK, kv_sl],
                                 khi_scr[r0:r0 + 2 * BLOCK, kv_sl]], axis=0)
        return lax.dot_general(q2, k_rhs, nt_dims, preferred_element_type=F32)

    s_next = scores(*units[0])
    for n, (j, kv) in enumerate(units):
        r0 = j * BLOCK
        kv_sl = slice(kv * LANES, (kv + 1) * LANES)
        q_sl = q_slices(kv)
        s = s_next
        if n + 1 < len(units):
            s_next = scores(*units[n + 1])
        else:
            gate_a_halves = [_sigmoid(proj(OFF_GA, D_MODEL // 2))]
        v_rhs = jnp.concatenate(
            [jnp.concatenate([vlo_scr[r0:r0 + 2 * BLOCK, kv_sl], ones_lo], axis=1),
             jnp.concatenate([vhi_scr[r0:r0 + 2 * BLOCK, kv_sl], ones_hi], axis=1)], axis=0)
        p_rows, sink_terms = [], []
        for pr in range(2):
            rows = slice(pr * BLOCK, (pr + 1) * BLOCK)
            p_tiles, e_sink = [], []
            for half in range(2):
                hd = kv * GROUP + 2 * pr + half
                s_prev = s[rows, (2 * half) * BLOCK:(2 * half + 1) * BLOCK]
                s_cur = s[rows, (2 * half + 1) * BLOCK:(2 * half + 2) * BLOCK]
                if j == 0:
                    s_prev = jnp.where(s_idx > 0, s_prev, NEG_INF)
                t = jnp.where(from_cur, s_cur, s_prev)
                m = jnp.max(t, axis=-1, keepdims=True)
                p = jnp.exp2(t - m).astype(BF16)
                zero = jnp.zeros_like(p)
                p_tiles += [jnp.where(from_cur, zero, p), jnp.where(from_cur, p, zero)]
                e_sink.append(jnp.exp2(sinks_ref[hd] * LOG2E - m))
            p_rows.append(jnp.concatenate(p_tiles, axis=1))
            sink_terms.append(jnp.where(lane_lo, e_sink[0], e_sink[1]))
        o = jnp.dot(jnp.concatenate(p_rows, axis=0), v_rhs, preferred_element_type=F32)
        if n + 1 == len(units):
            gate_a_halves.append(_sigmoid(proj(OFF_GA + D_MODEL // 2, D_MODEL // 2)))
        for pr in range(2):
            rows = slice(pr * BLOCK, (pr + 1) * BLOCK)
            den = o[rows, LANES:2 * LANES] + sink_terms[pr]
            attn_scr[r0:r0 + BLOCK, q_sl[pr]] = (o[rows, 0:LANES] / den).astype(BF16)

    for scr in kv_scrs:
        scr[0:BLOCK, :] = scr[ts:ts + BLOCK, :]

    attn_out = jnp.dot(attn_scr[...], w_ao_ref[...], preferred_element_type=F32)
    merged = merged + jnp.concatenate(gate_a_halves, axis=1) * attn_out
    out_ref[...] = x + jnp.dot(merged.astype(BF16), w_o_ref[...], preferred_element_type=F32)


def _ffn_kernel(x_ref, g_ref, gf_ref, w_gu_ref, w_d_ref, out_ref, act_scr, *, final_norm):
    x = x_ref[...]
    h = _rms_norm(x, g_ref[...]).astype(BF16)
    for off, width in FF_CHUNKS:
        gate = jnp.dot(h, w_gu_ref[:, off:off + width], preferred_element_type=F32)
        up = jnp.dot(h, w_gu_ref[:, D_FF + off:D_FF + off + width], preferred_element_type=F32)
        act_scr[:, off:off + width] = (_silu(gate) * up).astype(BF16)
    y = x + jnp.dot(act_scr[...], w_d_ref[...], preferred_element_type=F32)
    if final_norm:
        y = _rms_norm(y, gf_ref[...])
    out_ref[...] = y


def _resident(shape):
    return pl.BlockSpec(shape, lambda b, s: (0,) * len(shape), pipeline_mode=pl.Buffered(1))


def _rope_tables(seq):
    half = ROT_DIM // 2
    d = jnp.arange(LANES) % HEAD_DIM
    inv_freq = ROPE_THETA ** (-(2 * (d % half)).astype(F32) / ROT_DIM)
    ang = jnp.arange(seq, dtype=F32)[:, None] * inv_freq[None, :]
    cos, sin = jnp.cos(ang), jnp.sin(ang)
    c = jnp.where(d < ROT_DIM, cos, 1.0)
    s_next = jnp.where(d < half, -sin, 0.0)
    s_prev = jnp.where((d >= half) & (d < ROT_DIM), sin, 0.0)
    kind = jnp.arange(3)[:, None, None]
    return jnp.where(kind == 0, c, jnp.where(kind == 1, s_next, s_prev))


def _mixer(x, g, rope, conv_w, sinks, layer, w_in, w_co, w_ao, w_o, w_gu, w_d):
    b, s, d = x.shape
    ts = SEQ_TILE
    n_s = s // ts
    gu_rows = w_gu.shape[0] // (b * n_s)
    d_rows = 2 * w_d.shape[0] // (b * n_s)
    assert gu_rows % BF16_SUBLANES == 0 and d_rows % BF16_SUBLANES == 0
    gu_slab = pl.BlockSpec((gu_rows, w_gu.shape[1]), lambda i, j: (i * n_s + j, 0))
    d_slab = pl.BlockSpec((d_rows, w_d.shape[1]), lambda i, j: ((i * n_s + j) // 2, 0))
    tile = pl.BlockSpec((None, ts, d), lambda i, j: (i, j, 0))
    kv_scratch = pltpu.VMEM((BLOCK + ts, N_KV_HEADS * LANES), BF16)
    return pl.pallas_call(
        _mixer_kernel,
        grid=(b, n_s),
        in_specs=[
            pl.BlockSpec(memory_space=pltpu.SMEM),
            tile,
            _resident((1, d)),
            pl.BlockSpec((3, ts, LANES), lambda i, j: (0, j, 0)),
            pl.BlockSpec((None, CONV_K, d), lambda i, j: (layer, 0, 0),
                         pipeline_mode=pl.Buffered(1)),
            pl.BlockSpec(memory_space=pl.ANY),
            pl.BlockSpec(memory_space=pl.ANY),
            pl.BlockSpec(memory_space=pl.ANY),
            pl.BlockSpec(memory_space=pl.ANY),
            gu_slab,
            d_slab,
        ],
        out_specs=[tile, gu_slab, d_slab],
        out_shape=[jax.ShapeDtypeStruct(x.shape, x.dtype),
                   jax.ShapeDtypeStruct(w_gu.shape, BF16),
                   jax.ShapeDtypeStruct(w_d.shape, BF16)],
        scratch_shapes=[
            pltpu.VMEM(w_in.shape, BF16),
            pltpu.VMEM(w_co.shape, BF16),
            pltpu.VMEM(w_ao.shape, BF16),
            pltpu.VMEM(w_o.shape, BF16),
            pltpu.VMEM((STAGE_SLOTS, d, STAGE_COLS), F32),
            pltpu.SemaphoreType.DMA((STAGE_SLOTS,)),
            pltpu.VMEM((SUBLANES + ts, d), F32),
            pltpu.VMEM((ts, D_ATTN), BF16),
            kv_scratch, kv_scratch, kv_scratch, kv_scratch,
            pltpu.VMEM((ts, D_ATTN), BF16),
        ],
        compiler_params=pltpu.CompilerParams(
            dimension_semantics=("arbitrary", "arbitrary"),
            vmem_limit_bytes=VMEM_LIMIT_BYTES),
        name="mixer",
    )(sinks, x, g, rope, conv_w, w_in, w_co, w_ao, w_o, w_gu, w_d)


def _ffn(x, g, g_final, w_gu, w_d, final_norm):
    b, s, d = x.shape
    ts = FFN_SEQ_TILE
    tile = pl.BlockSpec((None, ts, d), lambda i, j: (i, j, 0))
    return pl.pallas_call(
        functools.partial(_ffn_kernel, final_norm=final_norm),
        grid=(b, s // ts),
        in_specs=[tile, _resident((1, d)), _resident((1, d)),
                  _resident(w_gu.shape), _resident(w_d.shape)],
        out_specs=tile,
        out_shape=jax.ShapeDtypeStruct(x.shape, x.dtype),
        scratch_shapes=[pltpu.VMEM((ts, D_FF), BF16)],
        compiler_params=pltpu.CompilerParams(
            dimension_semantics=("arbitrary", "arbitrary"),
            vmem_limit_bytes=VMEM_LIMIT_BYTES),
        name="ffn",
    )(x, g, g_final, w_gu, w_d)


def kernel(x, g_mix, w_in, conv_w, attn_sinks, w_conv_out, w_attn_out, w_o,
           g_ffn, w_gate_up, w_down, g_final):
    b, s, d = x.shape
    depth = w_in.shape[0]
    assert d == D_MODEL and s % SEQ_TILE == 0 and s % FFN_SEQ_TILE == 0
    assert w_in.shape[-1] == N_IN and w_gate_up.shape[-1] == 2 * D_FF
    assert N_IN % STAGE_COLS == 0 and d % STAGE_COLS == 0
    rope = _rope_tables(s)
    g_fin = g_final.reshape(1, d)
    for l in range(depth):
        x, w_gu, w_d = _mixer(
            x, g_mix[l].reshape(1, d), rope, conv_w, attn_sinks[l], l,
            w_in[l], w_conv_out[l], w_attn_out[l], w_o[l],
            w_gate_up[l], w_down[l])
        x = _ffn(x, g_ffn[l].reshape(1, d), g_fin, w_gu, w_d, final_norm=(l == depth - 1))
    return x
```

```python
import functools
import math

import jax
import jax.numpy as jnp
from jax import lax
from jax.experimental import pallas as pl
from jax.experimental.pallas import tpu as pltpu

D_MODEL = 1024
CONV_K = 3
HEAD_DIM = 64
N_HEADS = 16
N_KV_HEADS = 4
GROUP = N_HEADS // N_KV_HEADS
D_ATTN = N_HEADS * HEAD_DIM
D_KV = N_KV_HEADS * HEAD_DIM
WINDOW = 128
BLOCK = 128
ROT_DIM = HEAD_DIM // 4
ROPE_THETA = 500000.0
ATTN_SCALE = 1.0 / math.sqrt(HEAD_DIM)
LOG2E = math.log2(math.e)
NEG_INF = -1e30
D_FF = 2816
EPS = 1e-5

OFF_CB = 0
OFF_CC = OFF_CB + D_MODEL
OFF_CX = OFF_CC + D_MODEL
OFF_Q = OFF_CX + D_MODEL
OFF_K = OFF_Q + D_ATTN
OFF_V = OFF_K + D_KV
OFF_GC = OFF_V + D_KV
OFF_GA = OFF_GC + D_MODEL
N_IN = OFF_GA + D_MODEL

LANES = 128
SUBLANES = 8
BF16_SUBLANES = 16
SEQ_TILE = 512
MIXER_STEP_TILES = 2
FFN_SEQ_TILE = 1024
STAGE_COLS = 256
STAGE_SLOTS = 4
FF_CHUNKS = ((0, 1024), (1024, 1024), (2048, 768))
VMEM_LIMIT_BYTES = 62 * 1024 * 1024

F32 = jnp.float32
BF16 = jnp.bfloat16

assert WINDOW == BLOCK and 2 * HEAD_DIM == LANES and GROUP == 4


def _rms_norm(x, g):
    ms = jnp.mean(x * x, axis=-1, keepdims=True)
    return x * lax.rsqrt(ms + EPS) * g


def _sigmoid(x):
    return 0.5 * jnp.tanh(0.5 * x) + 0.5


def _silu(x):
    h = 0.5 * x
    return h * jnp.tanh(h) + h


def _rope(t, cos, sin_next, sin_prev):
    nxt = pltpu.roll(t, LANES - ROT_DIM // 2, 1)
    prv = pltpu.roll(t, ROT_DIM // 2, 1)
    return t * cos + nxt * sin_next + prv * sin_prev


def _split_head_pair(t, lane_lo):
    sw = pltpu.roll(t, HEAD_DIM, 1)
    zero = jnp.zeros_like(t)
    even = (jnp.where(lane_lo, t, zero), jnp.where(lane_lo, zero, sw))
    odd = (jnp.where(lane_lo, sw, zero), jnp.where(lane_lo, zero, t))
    return even, odd


def _stage_weights(pairs, stage_ref, sem_ref):
    slots = stage_ref.shape[0]
    jobs = [(src, dst, c * STAGE_COLS)
            for src, dst in pairs for c in range(src.shape[1] // STAGE_COLS)]

    def slab_copy(n):
        src, _, col = jobs[n]
        return pltpu.make_async_copy(
            src.at[:, pl.ds(col, STAGE_COLS)], stage_ref.at[n % slots], sem_ref.at[n % slots])

    for n in range(min(slots - 1, len(jobs))):
        slab_copy(n).start()
    for n, (_, dst, col) in enumerate(jobs):
        if n + slots - 1 < len(jobs):
            slab_copy(n + slots - 1).start()
        slab_copy(n).wait()
        dst[:, col:col + STAGE_COLS] = stage_ref[n % slots].astype(BF16)


def _mixer_kernel(sinks_ref, x_ref, g_ref, rope_ref, convw_ref, w_in_hbm,
                  w_co_hbm, w_ao_hbm, w_o_hbm, w_gu_f32_ref, w_d_f32_ref,
                  out_ref, w_gu_bf16_ref, w_d_bf16_ref,
                  w_in_ref, w_co_ref, w_ao_ref, w_o_ref, stage_ref, stage_sem,
                  u_scr, q_scr, klo_scr, khi_scr, vlo_scr, vhi_scr, attn_scr):
    ts = SEQ_TILE
    s_idx = pl.program_id(1)
    kv_scrs = (klo_scr, khi_scr, vlo_scr, vhi_scr)

    @pl.when((pl.program_id(0) == 0) & (s_idx == 0))
    def _():
        _stage_weights(((w_in_hbm, w_in_ref), (w_co_hbm, w_co_ref),
                        (w_ao_hbm, w_ao_ref), (w_o_hbm, w_o_ref)), stage_ref, stage_sem)

    @pl.when(s_idx == 0)
    def _():
        u_scr[0:SUBLANES, :] = jnp.zeros((SUBLANES, D_MODEL), F32)
        for scr in kv_scrs:
            scr[0:BLOCK, :] = jnp.zeros((BLOCK, N_KV_HEADS * LANES), BF16)

    w_gu_bf16_ref[...] = w_gu_f32_ref[...].astype(BF16)
    w_d_bf16_ref[...] = w_d_f32_ref[...].astype(BF16)

    for sub in range(x_ref.shape[0] // ts):
        _mixer_tile(sub, s_idx, sinks_ref, x_ref, g_ref, rope_ref, convw_ref,
                    w_in_ref, w_co_ref, w_ao_ref, w_o_ref, out_ref,
                    u_scr, q_scr, kv_scrs, attn_scr)


def _mixer_tile(sub, s_idx, sinks_ref, x_ref, g_ref, rope_ref, convw_ref,
                w_in_ref, w_co_ref, w_ao_ref, w_o_ref, out_ref,
                u_scr, q_scr, kv_scrs, attn_scr):
    ts = SEQ_TILE
    n_blk = ts // BLOCK
    klo_scr, khi_scr, vlo_scr, vhi_scr = kv_scrs
    tile_rows = slice(sub * ts, (sub + 1) * ts)
    tabs = [rope_ref[i, tile_rows, :] for i in range(3)]

    x = x_ref[tile_rows, :]
    h = _rms_norm(x, g_ref[...]).astype(BF16)

    def proj(off, width):
        return jnp.dot(h, w_in_ref[:, off:off + width], preferred_element_type=F32)

    q_tabs = [t * (ATTN_SCALE * LOG2E) for t in tabs]
    q = proj(OFF_Q, D_ATTN)
    for t in range(D_ATTN // LANES):
        sl = slice(t * LANES, (t + 1) * LANES)
        q_scr[:, sl] = _rope(q[:, sl], *q_tabs).astype(BF16)
    lane_lo_t = lax.broadcasted_iota(jnp.int32, (ts, LANES), 1) < HEAD_DIM
    k = proj(OFF_K, D_KV)
    v = proj(OFF_V, D_KV)
    for t in range(D_KV // LANES):
        sl = slice(t * LANES, (t + 1) * LANES)
        k_t = _rope(k[:, sl], *tabs)
        for src, lo_scr, hi_scr in ((k_t, klo_scr, khi_scr), (v[:, sl], vlo_scr, vhi_scr)):
            for hd, (lo, hi) in zip((2 * t, 2 * t + 1), _split_head_pair(src, lane_lo_t)):
                hsl = slice(hd * LANES, (hd + 1) * LANES)
                lo_scr[BLOCK:BLOCK + ts, hsl] = lo.astype(BF16)
                hi_scr[BLOCK:BLOCK + ts, hsl] = hi.astype(BF16)

    u = proj(OFF_CC, D_MODEL) * proj(OFF_CX, D_MODEL)
    u_scr[SUBLANES:SUBLANES + ts, :] = u
    conv = (convw_ref[0:1, :] * u_scr[SUBLANES - 2:SUBLANES - 2 + ts, :]
            + convw_ref[1:2, :] * u_scr[SUBLANES - 1:SUBLANES - 1 + ts, :]
            + convw_ref[2:3, :] * u)
    u_scr[0:SUBLANES, :] = u_scr[ts:ts + SUBLANES, :]
    conv_y = (proj(OFF_CB, D_MODEL) * conv).astype(BF16)
    gate_c = _sigmoid(proj(OFF_GC, D_MODEL))
    merged = gate_c * jnp.dot(conv_y, w_co_ref[...], preferred_element_type=F32)

    lane = lax.broadcasted_iota(jnp.int32, (BLOCK, LANES), 1)
    rowi = lax.broadcasted_iota(jnp.int32, (BLOCK, LANES), 0)
    from_cur = lane <= rowi
    lane_lo = lane < HEAD_DIM
    lane2 = lax.broadcasted_iota(jnp.int32, (2 * BLOCK, LANES), 1)
    ones_lo = jnp.where(lane2 < HEAD_DIM, 1.0, 0.0).astype(BF16)
    ones_hi = jnp.where(lane2 < HEAD_DIM, 0.0, 1.0).astype(BF16)
    nt_dims = (((1,), (1,)), ((), ()))

    units = [(j, kv) for j in range(n_blk) for kv in range(N_KV_HEADS)]

    def q_slices(kv):
        return [slice((2 * kv + pr) * LANES, (2 * kv + pr + 1) * LANES) for pr in range(2)]

    def scores(j, kv):
        r0 = j * BLOCK
        kv_sl = slice(kv * LANES, (kv + 1) * LANES)
        q2 = jnp.concatenate([q_scr[r0:r0 + BLOCK, sl] for sl in q_slices(kv)], axis=0)
        k_rhs = jnp.concatenate([klo_scr[r0:r0 + 2 * BLOCK, kv_sl],
                                 khi_scr[r0:r0 + 2 * BLOCK, kv_sl]], axis=0)
        return lax.dot_general(q2, k_rhs, nt_dims, preferred_element_type=F32)

    s_next = scores(*units[0])
    for n, (j, kv) in enumerate(units):
        r0 = j * BLOCK
        kv_sl = slice(kv * LANES, (kv + 1) * LANES)
        q_sl = q_slices(kv)
        s = s_next
        if n + 1 < len(units):
            s_next = scores(*units[n + 1])
        else:
            gate_a_halves = [_sigmoid(proj(OFF_GA, D_MODEL // 2))]
        v_rhs = jnp.concatenate(
            [jnp.concatenate([vlo_scr[r0:r0 + 2 * BLOCK, kv_sl], ones_lo], axis=1),
             jnp.concatenate([vhi_scr[r0:r0 + 2 * BLOCK, kv_sl], ones_hi], axis=1)], axis=0)
        p_rows, sink_terms = [], []
        for pr in range(2):
            rows = slice(pr * BLOCK, (pr + 1) * BLOCK)
            p_tiles, e_sink = [], []
            for half in range(2):
                hd = kv * GROUP + 2 * pr + half
                s_prev = s[rows, (2 * half) * BLOCK:(2 * half + 1) * BLOCK]
                s_cur = s[rows, (2 * half + 1) * BLOCK:(2 * half + 2) * BLOCK]
                if j == 0 and sub == 0:
                    s_prev = jnp.where(s_idx > 0, s_prev, NEG_INF)
                t = jnp.where(from_cur, s_cur, s_prev)
                m = jnp.max(t, axis=-1, keepdims=True)
                p = jnp.exp2(t - m).astype(BF16)
                zero = jnp.zeros_like(p)
                p_tiles += [jnp.where(from_cur, zero, p), jnp.where(from_cur, p, zero)]
                e_sink.append(jnp.exp2(sinks_ref[hd] * LOG2E - m))
            p_rows.append(jnp.concatenate(p_tiles, axis=1))
            sink_terms.append(jnp.where(lane_lo, e_sink[0], e_sink[1]))
        o = jnp.dot(jnp.concatenate(p_rows, axis=0), v_rhs, preferred_element_type=F32)
        if n + 1 == len(units):
            gate_a_halves.append(_sigmoid(proj(OFF_GA + D_MODEL // 2, D_MODEL // 2)))
        for pr in range(2):
            rows = slice(pr * BLOCK, (pr + 1) * BLOCK)
            den = o[rows, LANES:2 * LANES] + sink_terms[pr]
            attn_scr[r0:r0 + BLOCK, q_sl[pr]] = (o[rows, 0:LANES] / den).astype(BF16)

    for scr in kv_scrs:
        scr[0:BLOCK, :] = scr[ts:ts + BLOCK, :]

    attn_out = jnp.dot(attn_scr[...], w_ao_ref[...], preferred_element_type=F32)
    merged = merged + jnp.concatenate(gate_a_halves, axis=1) * attn_out
    out_ref[tile_rows, :] = x + jnp.dot(
        merged.astype(BF16), w_o_ref[...], preferred_element_type=F32)


def _ffn_kernel(x_ref, g_ref, gf_ref, w_gu_ref, w_d_ref, out_ref, act_scr, *, final_norm):
    x = x_ref[...]
    h = _rms_norm(x, g_ref[...]).astype(BF16)
    for off, width in FF_CHUNKS:
        gate = jnp.dot(h, w_gu_ref[:, off:off + width], preferred_element_type=F32)
        up = jnp.dot(h, w_gu_ref[:, D_FF + off:D_FF + off + width], preferred_element_type=F32)
        act_scr[:, off:off + width] = (_silu(gate) * up).astype(BF16)
    y = x + jnp.dot(act_scr[...], w_d_ref[...], preferred_element_type=F32)
    if final_norm:
        y = _rms_norm(y, gf_ref[...])
    out_ref[...] = y


def _resident(shape):
    return pl.BlockSpec(shape, lambda b, s: (0,) * len(shape), pipeline_mode=pl.Buffered(1))


def _rope_tables(seq):
    half = ROT_DIM // 2
    d = jnp.arange(LANES) % HEAD_DIM
    inv_freq = ROPE_THETA ** (-(2 * (d % half)).astype(F32) / ROT_DIM)
    ang = jnp.arange(seq, dtype=F32)[:, None] * inv_freq[None, :]
    cos, sin = jnp.cos(ang), jnp.sin(ang)
    c = jnp.where(d < ROT_DIM, cos, 1.0)
    s_next = jnp.where(d < half, -sin, 0.0)
    s_prev = jnp.where((d >= half) & (d < ROT_DIM), sin, 0.0)
    kind = jnp.arange(3)[:, None, None]
    return jnp.where(kind == 0, c, jnp.where(kind == 1, s_next, s_prev))


def _mixer(x, g, rope, conv_w, sinks, layer, w_in, w_co, w_ao, w_o, w_gu, w_d):
    b, s, d = x.shape
    ts = SEQ_TILE
    step_rows = MIXER_STEP_TILES * ts
    n_s = s // step_rows
    gu_rows = w_gu.shape[0] // (b * n_s)
    d_rows = w_d.shape[0] // (b * n_s)
    assert gu_rows % BF16_SUBLANES == 0 and d_rows % BF16_SUBLANES == 0
    gu_slab = pl.BlockSpec((gu_rows, w_gu.shape[1]), lambda i, j: (i * n_s + j, 0))
    d_slab = pl.BlockSpec((d_rows, w_d.shape[1]), lambda i, j: (i * n_s + j, 0))
    tile = pl.BlockSpec((None, step_rows, d), lambda i, j: (i, j, 0))
    kv_scratch = pltpu.VMEM((BLOCK + ts, N_KV_HEADS * LANES), BF16)
    return pl.pallas_call(
        _mixer_kernel,
        grid=(b, n_s),
        in_specs=[
            pl.BlockSpec(memory_space=pltpu.SMEM),
            tile,
            _resident((1, d)),
            pl.BlockSpec((3, step_rows, LANES), lambda i, j: (0, j, 0)),
            pl.BlockSpec((None, CONV_K, d), lambda i, j: (layer, 0, 0),
                         pipeline_mode=pl.Buffered(1)),
            pl.BlockSpec(memory_space=pl.ANY),
            pl.BlockSpec(memory_space=pl.ANY),
            pl.BlockSpec(memory_space=pl.ANY),
            pl.BlockSpec(memory_space=pl.ANY),
            gu_slab,
            d_slab,
        ],
        out_specs=[tile, gu_slab, d_slab],
        out_shape=[jax.ShapeDtypeStruct(x.shape, x.dtype),
                   jax.ShapeDtypeStruct(w_gu.shape, BF16),
                   jax.ShapeDtypeStruct(w_d.shape, BF16)],
        scratch_shapes=[
            pltpu.VMEM(w_in.shape, BF16),
            pltpu.VMEM(w_co.shape, BF16),
            pltpu.VMEM(w_ao.shape, BF16),
            pltpu.VMEM(w_o.shape, BF16),
            pltpu.VMEM((STAGE_SLOTS, d, STAGE_COLS), F32),
            pltpu.SemaphoreType.DMA((STAGE_SLOTS,)),
            pltpu.VMEM((SUBLANES + ts, d), F32),
            pltpu.VMEM((ts, D_ATTN), BF16),
            kv_scratch, kv_scratch, kv_scratch, kv_scratch,
            pltpu.VMEM((ts, D_ATTN), BF16),
        ],
        compiler_params=pltpu.CompilerParams(
            dimension_semantics=("arbitrary", "arbitrary"),
            vmem_limit_bytes=VMEM_LIMIT_BYTES),
        name="mixer",
    )(sinks, x, g, rope, conv_w, w_in, w_co, w_ao, w_o, w_gu, w_d)


def _ffn(x, g, g_final, w_gu, w_d, final_norm):
    b, s, d = x.shape
    ts = FFN_SEQ_TILE
    tile = pl.BlockSpec((None, ts, d), lambda i, j: (i, j, 0))
    return pl.pallas_call(
        functools.partial(_ffn_kernel, final_norm=final_norm),
        grid=(b, s // ts),
        in_specs=[tile, _resident((1, d)), _resident((1, d)),
                  _resident(w_gu.shape), _resident(w_d.shape)],
        out_specs=tile,
        out_shape=jax.ShapeDtypeStruct(x.shape, x.dtype),
        scratch_shapes=[pltpu.VMEM((ts, D_FF), BF16)],
        compiler_params=pltpu.CompilerParams(
            dimension_semantics=("arbitrary", "arbitrary"),
            vmem_limit_bytes=VMEM_LIMIT_BYTES),
        name="ffn",
    )(x, g, g_final, w_gu, w_d)


def kernel(x, g_mix, w_in, conv_w, attn_sinks, w_conv_out, w_attn_out, w_o,
           g_ffn, w_gate_up, w_down, g_final):
    b, s, d = x.shape
    depth = w_in.shape[0]
    assert d == D_MODEL and s % (MIXER_STEP_TILES * SEQ_TILE) == 0 and s % FFN_SEQ_TILE == 0
    assert w_in.shape[-1] == N_IN and w_gate_up.shape[-1] == 2 * D_FF
    assert N_IN % STAGE_COLS == 0 and d % STAGE_COLS == 0
    rope = _rope_tables(s)
    g_fin = g_final.reshape(1, d)
    for l in range(depth):
        x, w_gu, w_d = _mixer(
            x, g_mix[l].reshape(1, d), rope, conv_w, attn_sinks[l], l,
            w_in[l], w_conv_out[l], w_attn_out[l], w_o[l],
            w_gate_up[l], w_down[l])
        x = _ffn(x, g_ffn[l].reshape(1, d), g_fin, w_gu, w_d, final_norm=(l == depth - 1))
    return x
```

```python
import functools
import math

import jax
import jax.numpy as jnp
import numpy as np
from jax import lax
from jax.experimental import pallas as pl
from jax.experimental.pallas import tpu as pltpu

D_MODEL = 1024
CONV_K = 3
HEAD_DIM = 64
N_HEADS = 16
N_KV_HEADS = 4
GROUP = N_HEADS // N_KV_HEADS
D_ATTN = N_HEADS * HEAD_DIM
D_KV = N_KV_HEADS * HEAD_DIM
WINDOW = 128
BLOCK = 128
ROT_DIM = HEAD_DIM // 4
ROPE_THETA = 500000.0
ATTN_SCALE = 1.0 / math.sqrt(HEAD_DIM)
LOG2E = math.log2(math.e)
NEG_INF = -1e30
D_FF = 2816
EPS = 1e-5

OFF_CB = 0
OFF_CC = OFF_CB + D_MODEL
OFF_CX = OFF_CC + D_MODEL
OFF_Q = OFF_CX + D_MODEL
OFF_K = OFF_Q + D_ATTN
OFF_V = OFF_K + D_KV
OFF_GC = OFF_V + D_KV
OFF_GA = OFF_GC + D_MODEL
N_IN = OFF_GA + D_MODEL

LANES = 128
SUBLANES = 8
BF16_SUBLANES = 16
SEQ_TILE = 512
FFN_SEQ_TILE = 1024
STAGE_COLS = 512
STAGE_SLOTS = 4
FF_CHUNKS = ((0, 1024), (1024, 1024), (2048, 768))
VMEM_LIMIT_BYTES = 58 * 1024 * 1024

F32 = jnp.float32
BF16 = jnp.bfloat16

assert WINDOW == BLOCK and 2 * HEAD_DIM == LANES and GROUP == 4


def _rms_norm(x, g):
    ms = jnp.mean(x * x, axis=-1, keepdims=True)
    return x * lax.rsqrt(ms + EPS) * g


def _sigmoid(x):
    return 0.5 * jnp.tanh(0.5 * x) + 0.5


def _silu(x):
    h = 0.5 * x
    return h * jnp.tanh(h) + h


def _rope(t, cos, sin_next, sin_prev):
    nxt = pltpu.roll(t, LANES - ROT_DIM // 2, 1)
    prv = pltpu.roll(t, ROT_DIM // 2, 1)
    return t * cos + nxt * sin_next + prv * sin_prev


def _split_head_pair(t, lane_lo):
    sw = pltpu.roll(t, HEAD_DIM, 1)
    zero = jnp.zeros_like(t)
    even = (jnp.where(lane_lo, t, zero), jnp.where(lane_lo, zero, sw))
    odd = (jnp.where(lane_lo, sw, zero), jnp.where(lane_lo, zero, t))
    return even, odd


def _stage_weights(pairs, stage_ref, sem_ref):
    slots = stage_ref.shape[0]
    jobs = [(src, dst, c * STAGE_COLS)
            for src, dst in pairs for c in range(src.shape[1] // STAGE_COLS)]

    def slab_copy(n):
        src, _, col = jobs[n]
        return pltpu.make_async_copy(
            src.at[:, pl.ds(col, STAGE_COLS)], stage_ref.at[n % slots], sem_ref.at[n % slots])

    for n in range(min(slots - 1, len(jobs))):
        slab_copy(n).start()
    for n, (_, dst, col) in enumerate(jobs):
        if n + slots - 1 < len(jobs):
            slab_copy(n + slots - 1).start()
        slab_copy(n).wait()
        dst[:, col:col + STAGE_COLS] = stage_ref[n % slots].astype(BF16)


def _mixer_kernel(sinks_ref, x_ref, g_ref, rope_ref, convw_ref, w_in_hbm,
                  w_co_hbm, w_ao_hbm, w_o_hbm, w_gu_f32_ref, w_d_f32_ref,
                  out_ref, w_gu_bf16_ref, w_d_bf16_ref,
                  w_in_ref, w_co_ref, w_ao_ref, w_o_ref, stage_ref, stage_sem,
                  u_scr, q_scr, klo_scr, khi_scr, vlo_scr, vhi_scr, attn_scr):
    ts = x_ref.shape[0]
    n_blk = ts // BLOCK
    s_idx = pl.program_id(1)
    kv_scrs = (klo_scr, khi_scr, vlo_scr, vhi_scr)

    @pl.when((pl.program_id(0) == 0) & (s_idx == 0))
    def _():
        _stage_weights(((w_in_hbm, w_in_ref), (w_co_hbm, w_co_ref),
                        (w_ao_hbm, w_ao_ref), (w_o_hbm, w_o_ref)), stage_ref, stage_sem)

    @pl.when(s_idx == 0)
    def _():
        u_scr[0:SUBLANES, :] = jnp.zeros((SUBLANES, D_MODEL), F32)
        for scr in kv_scrs:
            scr[0:BLOCK, :] = jnp.zeros((BLOCK, N_KV_HEADS * LANES), BF16)

    w_gu_bf16_ref[...] = w_gu_f32_ref[...].astype(BF16)
    w_d_bf16_ref[...] = w_d_f32_ref[...].astype(BF16)

    x = x_ref[...]
    h = _rms_norm(x, g_ref[...]).astype(BF16)

    def proj(off, width):
        return jnp.dot(h, w_in_ref[:, off:off + width], preferred_element_type=F32)

    q_tabs = [rope_ref[i] * (ATTN_SCALE * LOG2E) for i in range(3)]
    q = proj(OFF_Q, D_ATTN)
    for t in range(D_ATTN // LANES):
        sl = slice(t * LANES, (t + 1) * LANES)
        q_scr[:, sl] = _rope(q[:, sl], *q_tabs).astype(BF16)
    lane_lo_t = lax.broadcasted_iota(jnp.int32, (ts, LANES), 1) < HEAD_DIM
    k = proj(OFF_K, D_KV)
    v = proj(OFF_V, D_KV)
    for t in range(D_KV // LANES):
        sl = slice(t * LANES, (t + 1) * LANES)
        k_t = _rope(k[:, sl], rope_ref[0], rope_ref[1], rope_ref[2])
        for src, lo_scr, hi_scr in ((k_t, klo_scr, khi_scr), (v[:, sl], vlo_scr, vhi_scr)):
            for hd, (lo, hi) in zip((2 * t, 2 * t + 1), _split_head_pair(src, lane_lo_t)):
                hsl = slice(hd * LANES, (hd + 1) * LANES)
                lo_scr[BLOCK:BLOCK + ts, hsl] = lo.astype(BF16)
                hi_scr[BLOCK:BLOCK + ts, hsl] = hi.astype(BF16)

    u = proj(OFF_CC, D_MODEL) * proj(OFF_CX, D_MODEL)
    u_scr[SUBLANES:SUBLANES + ts, :] = u
    conv = (convw_ref[0:1, :] * u_scr[SUBLANES - 2:SUBLANES - 2 + ts, :]
            + convw_ref[1:2, :] * u_scr[SUBLANES - 1:SUBLANES - 1 + ts, :]
            + convw_ref[2:3, :] * u)
    u_scr[0:SUBLANES, :] = u_scr[ts:ts + SUBLANES, :]
    conv_y = (proj(OFF_CB, D_MODEL) * conv).astype(BF16)
    gate_c = _sigmoid(proj(OFF_GC, D_MODEL))
    merged = gate_c * jnp.dot(conv_y, w_co_ref[...], preferred_element_type=F32)

    lane = lax.broadcasted_iota(jnp.int32, (BLOCK, LANES), 1)
    rowi = lax.broadcasted_iota(jnp.int32, (BLOCK, LANES), 0)
    from_cur = lane <= rowi
    lane_lo = lane < HEAD_DIM
    lane2 = lax.broadcasted_iota(jnp.int32, (2 * BLOCK, LANES), 1)
    ones_lo = jnp.where(lane2 < HEAD_DIM, 1.0, 0.0).astype(BF16)
    ones_hi = jnp.where(lane2 < HEAD_DIM, 0.0, 1.0).astype(BF16)
    nt_dims = (((1,), (1,)), ((), ()))

    units = [(j, kv) for j in range(n_blk) for kv in range(N_KV_HEADS)]

    def q_slices(kv):
        return [slice((2 * kv + pr) * LANES, (2 * kv + pr + 1) * LANES) for pr in range(2)]

    def scores(j, kv):
        r0 = j * BLOCK
        kv_sl = slice(kv * LANES, (kv + 1) * LANES)
        q2 = jnp.concatenate([q_scr[r0:r0 + BLOCK, sl] for sl in q_slices(kv)], axis=0)
        k_rhs = jnp.concatenate([klo_scr[r0:r0 + 2 * BLOCK, kv_sl],
                                 khi_scr[r0:r0 + 2 * BLOCK, kv_sl]], axis=0)
        return lax.dot_general(q2, k_rhs, nt_dims, preferred_element_type=F32)

    s_next = scores(*units[0])
    for n, (j, kv) in enumerate(units):
        r0 = j * BLOCK
        kv_sl = slice(kv * LANES, (kv + 1) * LANES)
        q_sl = q_slices(kv)
        s = s_next
        if n + 1 < len(units):
            s_next = scores(*units[n + 1])
        else:
            gate_a_halves = [_sigmoid(proj(OFF_GA, D_MODEL // 2))]
        v_rhs = jnp.concatenate(
            [jnp.concatenate([vlo_scr[r0:r0 + 2 * BLOCK, kv_sl], ones_lo], axis=1),
             jnp.concatenate([vhi_scr[r0:r0 + 2 * BLOCK, kv_sl], ones_hi], axis=1)], axis=0)
        p_rows, sink_terms = [], []
        for pr in range(2):
            rows = slice(pr * BLOCK, (pr + 1) * BLOCK)
            p_tiles, e_sink = [], []
            for half in range(2):
                hd = kv * GROUP + 2 * pr + half
                s_prev = s[rows, (2 * half) * BLOCK:(2 * half + 1) * BLOCK]
                s_cur = s[rows, (2 * half + 1) * BLOCK:(2 * half + 2) * BLOCK]
                if j == 0:
                    s_prev = jnp.where(s_idx > 0, s_prev, NEG_INF)
                t = jnp.where(from_cur, s_cur, s_prev)
                m = jnp.max(t, axis=-1, keepdims=True)
                p = jnp.exp2(t - m).astype(BF16)
                zero = jnp.zeros_like(p)
                p_tiles += [jnp.where(from_cur, zero, p), jnp.where(from_cur, p, zero)]
                e_sink.append(jnp.exp2(sinks_ref[hd] * LOG2E - m))
            p_rows.append(jnp.concatenate(p_tiles, axis=1))
            sink_terms.append(jnp.where(lane_lo, e_sink[0], e_sink[1]))
        o = jnp.dot(jnp.concatenate(p_rows, axis=0), v_rhs, preferred_element_type=F32)
        if n + 1 == len(units):
            gate_a_halves.append(_sigmoid(proj(OFF_GA + D_MODEL // 2, D_MODEL // 2)))
        for pr in range(2):
            rows = slice(pr * BLOCK, (pr + 1) * BLOCK)
            den = o[rows, LANES:2 * LANES] + sink_terms[pr]
            attn_scr[r0:r0 + BLOCK, q_sl[pr]] = (o[rows, 0:LANES] / den).astype(BF16)

    for scr in kv_scrs:
        scr[0:BLOCK, :] = scr[ts:ts + BLOCK, :]

    attn_out = jnp.dot(attn_scr[...], w_ao_ref[...], preferred_element_type=F32)
    merged = merged + jnp.concatenate(gate_a_halves, axis=1) * attn_out
    out_ref[...] = x + jnp.dot(merged.astype(BF16), w_o_ref[...], preferred_element_type=F32)


def _ffn_kernel(x_ref, g_ref, gf_ref, w_gu_ref, w_d_ref, out_ref, act_scr, *, final_norm):
    x = x_ref[...]
    h = _rms_norm(x, g_ref[...]).astype(BF16)
    for off, width in FF_CHUNKS:
        gate = jnp.dot(h, w_gu_ref[:, off:off + width], preferred_element_type=F32)
        up = jnp.dot(h, w_gu_ref[:, D_FF + off:D_FF + off + width], preferred_element_type=F32)
        act_scr[:, off:off + width] = (_silu(gate) * up).astype(BF16)
    y = x + jnp.dot(act_scr[...], w_d_ref[...], preferred_element_type=F32)
    if final_norm:
        y = _rms_norm(y, gf_ref[...])
    out_ref[...] = y


def _resident(shape):
    return pl.BlockSpec(shape, lambda b, s: (0,) * len(shape), pipeline_mode=pl.Buffered(1))


def _rope_tables(seq):
    half = ROT_DIM // 2
    d = np.arange(LANES) % HEAD_DIM
    inv_freq = np.float32(ROPE_THETA) ** (-(2 * (d % half)).astype(np.float32) / np.float32(ROT_DIM))
    ang = np.arange(seq, dtype=np.float32)[:, None] * inv_freq[None, :].astype(np.float32)
    cos, sin = np.cos(ang), np.sin(ang)
    c = np.where(d < ROT_DIM, cos, 1.0)
    s_next = np.where(d < half, -sin, 0.0)
    s_prev = np.where((d >= half) & (d < ROT_DIM), sin, 0.0)
    return jnp.asarray(np.stack([c, s_next, s_prev]).astype(np.float32))


def _mixer(x, g, rope, conv_w, sinks, layer, w_in, w_co, w_ao, w_o, w_gu, w_d):
    b, s, d = x.shape
    ts = SEQ_TILE
    n_s = s // ts
    gu_rows = w_gu.shape[0] // (b * n_s)
    d_rows = 2 * w_d.shape[0] // (b * n_s)
    assert gu_rows % BF16_SUBLANES == 0 and d_rows % BF16_SUBLANES == 0
    gu_slab = pl.BlockSpec((gu_rows, w_gu.shape[1]), lambda i, j: (i * n_s + j, 0))
    d_slab = pl.BlockSpec((d_rows, w_d.shape[1]), lambda i, j: ((i * n_s + j) // 2, 0))
    tile = pl.BlockSpec((None, ts, d), lambda i, j: (i, j, 0))
    kv_scratch = pltpu.VMEM((BLOCK + ts, N_KV_HEADS * LANES), BF16)
    return pl.pallas_call(
        _mixer_kernel,
        grid=(b, n_s),
        in_specs=[
            pl.BlockSpec(memory_space=pltpu.SMEM),
            tile,
            _resident((1, d)),
            pl.BlockSpec((3, ts, LANES), lambda i, j: (0, j, 0)),
            pl.BlockSpec((None, CONV_K, d), lambda i, j: (layer, 0, 0),
                         pipeline_mode=pl.Buffered(1)),
            pl.BlockSpec(memory_space=pl.ANY),
            pl.BlockSpec(memory_space=pl.ANY),
            pl.BlockSpec(memory_space=pl.ANY),
            pl.BlockSpec(memory_space=pl.ANY),
            gu_slab,
            d_slab,
        ],
        out_specs=[tile, gu_slab, d_slab],
        out_shape=[jax.ShapeDtypeStruct(x.shape, x.dtype),
                   jax.ShapeDtypeStruct(w_gu.shape, BF16),
                   jax.ShapeDtypeStruct(w_d.shape, BF16)],
        scratch_shapes=[
            pltpu.VMEM(w_in.shape, BF16),
            pltpu.VMEM(w_co.shape, BF16),
            pltpu.VMEM(w_ao.shape, BF16),
            pltpu.VMEM(w_o.shape, BF16),
            pltpu.VMEM((STAGE_SLOTS, d, STAGE_COLS), F32),
            pltpu.SemaphoreType.DMA((STAGE_SLOTS,)),
            pltpu.VMEM((SUBLANES + ts, d), F32),
            pltpu.VMEM((ts, D_ATTN), BF16),
            kv_scratch, kv_scratch, kv_scratch, kv_scratch,
            pltpu.VMEM((ts, D_ATTN), BF16),
        ],
        compiler_params=pltpu.CompilerParams(
            dimension_semantics=("arbitrary", "arbitrary"),
            vmem_limit_bytes=VMEM_LIMIT_BYTES),
        name="mixer",
    )(sinks, x, g, rope, conv_w, w_in, w_co, w_ao, w_o, w_gu, w_d)


def _ffn(x, g, g_final, w_gu, w_d, final_norm):
    b, s, d = x.shape
    ts = FFN_SEQ_TILE
    tile = pl.BlockSpec((None, ts, d), lambda i, j: (i, j, 0))
    return pl.pallas_call(
        functools.partial(_ffn_kernel, final_norm=final_norm),
        grid=(b, s // ts),
        in_specs=[tile, _resident((1, d)), _resident((1, d)),
                  _resident(w_gu.shape), _resident(w_d.shape)],
        out_specs=tile,
        out_shape=jax.ShapeDtypeStruct(x.shape, x.dtype),
        scratch_shapes=[pltpu.VMEM((ts, D_FF), BF16)],
        compiler_params=pltpu.CompilerParams(
            dimension_semantics=("arbitrary", "arbitrary"),
            vmem_limit_bytes=VMEM_LIMIT_BYTES),
        name="ffn",
    )(x, g, g_final, w_gu, w_d)


def kernel(x, g_mix, w_in, conv_w, attn_sinks, w_conv_out, w_attn_out, w_o,
           g_ffn, w_gate_up, w_down, g_final):
    b, s, d = x.shape
    depth = w_in.shape[0]
    assert d == D_MODEL and s % SEQ_TILE == 0 and s % FFN_SEQ_TILE == 0
    assert w_in.shape[-1] == N_IN and w_gate_up.shape[-1] == 2 * D_FF
    assert N_IN % STAGE_COLS == 0 and d % STAGE_COLS == 0
    rope = _rope_tables(s)
    g_fin = g_final.reshape(1, d)
    for l in range(depth):
        x, w_gu, w_d = _mixer(
            x, g_mix[l].reshape(1, d), rope, conv_w, attn_sinks[l], l,
            w_in[l], w_conv_out[l], w_attn_out[l], w_o[l],
            w_gate_up[l], w_down[l])
        x = _ffn(x, g_ffn[l].reshape(1, d), g_fin, w_gu, w_d, final_norm=(l == depth - 1))
    return x
```

```python
import functools
import math

import jax
import jax.numpy as jnp
import numpy as np
from jax import lax
from jax.experimental import pallas as pl
from jax.experimental.pallas import tpu as pltpu

D_MODEL = 1024
CONV_K = 3
HEAD_DIM = 64
N_HEADS = 16
N_KV_HEADS = 4
GROUP = N_HEADS // N_KV_HEADS
D_ATTN = N_HEADS * HEAD_DIM
D_KV = N_KV_HEADS * HEAD_DIM
WINDOW = 128
BLOCK = 128
ROT_DIM = HEAD_DIM // 4
ROPE_THETA = 500000.0
ATTN_SCALE = 1.0 / math.sqrt(HEAD_DIM)
LOG2E = math.log2(math.e)
NEG_INF = -1e30
D_FF = 2816
EPS = 1e-5

OFF_CB = 0
OFF_CC = OFF_CB + D_MODEL
OFF_CX = OFF_CC + D_MODEL
OFF_Q = OFF_CX + D_MODEL
OFF_K = OFF_Q + D_ATTN
OFF_V = OFF_K + D_KV
OFF_GC = OFF_V + D_KV
OFF_GA = OFF_GC + D_MODEL
N_IN = OFF_GA + D_MODEL

LANES = 128
SUBLANES = 8
BF16_SUBLANES = 16
SEQ_TILE = 512
FFN_SEQ_TILE = 1024
STAGE_COLS = 512
STAGE_SLOTS = 4
FF_CHUNKS = ((0, 1024), (1024, 1024), (2048, 768))
VMEM_LIMIT_BYTES = 58 * 1024 * 1024

F32 = jnp.float32
BF16 = jnp.bfloat16

assert WINDOW == BLOCK and 2 * HEAD_DIM == LANES and GROUP == 4


def _rms_norm(x, g):
    ms = jnp.mean(x * x, axis=-1, keepdims=True)
    return x * lax.rsqrt(ms + EPS) * g


def _sigmoid(x):
    return 0.5 * jnp.tanh(0.5 * x) + 0.5


def _silu(x):
    h = 0.5 * x
    return h * jnp.tanh(h) + h


def _rope(t, cos, sin_next, sin_prev):
    nxt = pltpu.roll(t, LANES - ROT_DIM // 2, 1)
    prv = pltpu.roll(t, ROT_DIM // 2, 1)
    return t * cos + nxt * sin_next + prv * sin_prev


def _split_head_pair(t, lane_lo):
    sw = pltpu.roll(t, HEAD_DIM, 1)
    zero = jnp.zeros_like(t)
    even = (jnp.where(lane_lo, t, zero), jnp.where(lane_lo, zero, sw))
    odd = (jnp.where(lane_lo, sw, zero), jnp.where(lane_lo, zero, t))
    return even, odd


def _stage_weights(pairs, stage_ref, sem_ref):
    slots = stage_ref.shape[0]
    jobs = [(src, dst, c * STAGE_COLS)
            for src, dst in pairs for c in range(src.shape[1] // STAGE_COLS)]

    def slab_copy(n):
        src, _, col = jobs[n]
        return pltpu.make_async_copy(
            src.at[:, pl.ds(col, STAGE_COLS)], stage_ref.at[n % slots], sem_ref.at[n % slots])

    for n in range(min(slots - 1, len(jobs))):
        slab_copy(n).start()
    for n, (_, dst, col) in enumerate(jobs):
        if n + slots - 1 < len(jobs):
            slab_copy(n + slots - 1).start()
        slab_copy(n).wait()
        dst[:, col:col + STAGE_COLS] = stage_ref[n % slots].astype(BF16)


def _mixer_kernel(sinks_ref, x_ref, g_ref, rope_ref, convw_ref, w_in_hbm,
                  w_co_hbm, w_ao_hbm, w_o_hbm, w_gu_f32_ref, w_d_f32_ref,
                  out_ref, w_gu_bf16_ref, w_d_bf16_ref,
                  w_in_ref, w_co_ref, w_ao_ref, w_o_ref, stage_ref, stage_sem,
                  u_scr, q_scr, klo_scr, khi_scr, vlo_scr, vhi_scr, attn_scr):
    ts = x_ref.shape[0]
    n_blk = ts // BLOCK
    s_idx = pl.program_id(1)
    kv_scrs = (klo_scr, khi_scr, vlo_scr, vhi_scr)

    @pl.when((pl.program_id(0) == 0) & (s_idx == 0))
    def _():
        _stage_weights(((w_in_hbm, w_in_ref), (w_co_hbm, w_co_ref),
                        (w_ao_hbm, w_ao_ref), (w_o_hbm, w_o_ref)), stage_ref, stage_sem)

    @pl.when(s_idx == 0)
    def _():
        u_scr[0:SUBLANES, :] = jnp.zeros((SUBLANES, D_MODEL), F32)
        for scr in kv_scrs:
            scr[0:BLOCK, :] = jnp.zeros((BLOCK, N_KV_HEADS * LANES), BF16)

    w_gu_bf16_ref[...] = w_gu_f32_ref[...].astype(BF16)
    w_d_bf16_ref[...] = w_d_f32_ref[...].astype(BF16)

    x = x_ref[...]
    h = _rms_norm(x, g_ref[...]).astype(BF16)

    def proj(off, width):
        return jnp.dot(h, w_in_ref[:, off:off + width], preferred_element_type=F32)

    q_tabs = [rope_ref[i] * (ATTN_SCALE * LOG2E) for i in range(3)]
    q = proj(OFF_Q, D_ATTN)
    for t in range(D_ATTN // LANES):
        sl = slice(t * LANES, (t + 1) * LANES)
        q_scr[:, sl] = _rope(q[:, sl], *q_tabs).astype(BF16)
    lane_lo_t = lax.broadcasted_iota(jnp.int32, (ts, LANES), 1) < HEAD_DIM
    k = proj(OFF_K, D_KV)
    v = proj(OFF_V, D_KV)
    for t in range(D_KV // LANES):
        sl = slice(t * LANES, (t + 1) * LANES)
        k_t = _rope(k[:, sl], rope_ref[0], rope_ref[1], rope_ref[2])
        for src, lo_scr, hi_scr in ((k_t, klo_scr, khi_scr), (v[:, sl], vlo_scr, vhi_scr)):
            for hd, (lo, hi) in zip((2 * t, 2 * t + 1), _split_head_pair(src, lane_lo_t)):
                hsl = slice(hd * LANES, (hd + 1) * LANES)
                lo_scr[BLOCK:BLOCK + ts, hsl] = lo.astype(BF16)
                hi_scr[BLOCK:BLOCK + ts, hsl] = hi.astype(BF16)

    u = proj(OFF_CC, D_MODEL) * proj(OFF_CX, D_MODEL)
    u_scr[SUBLANES:SUBLANES + ts, :] = u
    conv = (convw_ref[0:1, :] * u_scr[SUBLANES - 2:SUBLANES - 2 + ts, :]
            + convw_ref[1:2, :] * u_scr[SUBLANES - 1:SUBLANES - 1 + ts, :]
            + convw_ref[2:3, :] * u)
    u_scr[0:SUBLANES, :] = u_scr[ts:ts + SUBLANES, :]
    conv_y = (proj(OFF_CB, D_MODEL) * conv).astype(BF16)
    gate_c = _sigmoid(proj(OFF_GC, D_MODEL))
    merged = gate_c * jnp.dot(conv_y, w_co_ref[...], preferred_element_type=F32)

    lane = lax.broadcasted_iota(jnp.int32, (BLOCK, LANES), 1)
    rowi = lax.broadcasted_iota(jnp.int32, (BLOCK, LANES), 0)
    from_cur = lane <= rowi
    lane_lo = lane < HEAD_DIM
    lane2 = lax.broadcasted_iota(jnp.int32, (2 * BLOCK, LANES), 1)
    ones_lo = jnp.where(lane2 < HEAD_DIM, 1.0, 0.0).astype(BF16)
    ones_hi = jnp.where(lane2 < HEAD_DIM, 0.0, 1.0).astype(BF16)
    nt_dims = (((1,), (1,)), ((), ()))

    units = [(j, kv) for j in range(n_blk) for kv in range(N_KV_HEADS)]

    def q_slices(kv):
        return [slice((2 * kv + pr) * LANES, (2 * kv + pr + 1) * LANES) for pr in range(2)]

    def scores(j, kv):
        r0 = j * BLOCK
        kv_sl = slice(kv * LANES, (kv + 1) * LANES)
        q2 = jnp.concatenate([q_scr[r0:r0 + BLOCK, sl] for sl in q_slices(kv)], axis=0)
        k_rhs = jnp.concatenate([klo_scr[r0:r0 + 2 * BLOCK, kv_sl],
                                 khi_scr[r0:r0 + 2 * BLOCK, kv_sl]], axis=0)
        return lax.dot_general(q2, k_rhs, nt_dims, preferred_element_type=F32)

    s_next = scores(*units[0])
    for n, (j, kv) in enumerate(units):
        r0 = j * BLOCK
        kv_sl = slice(kv * LANES, (kv + 1) * LANES)
        q_sl = q_slices(kv)
        s = s_next
        if n + 1 < len(units):
            s_next = scores(*units[n + 1])
        else:
            gate_a_halves = [_sigmoid(proj(OFF_GA, D_MODEL // 2))]
        v_rhs = jnp.concatenate(
            [jnp.concatenate([vlo_scr[r0:r0 + 2 * BLOCK, kv_sl], ones_lo], axis=1),
             jnp.concatenate([vhi_scr[r0:r0 + 2 * BLOCK, kv_sl], ones_hi], axis=1)], axis=0)
        p_rows, sink_terms = [], []
        for pr in range(2):
            rows = slice(pr * BLOCK, (pr + 1) * BLOCK)
            p_tiles, e_sink = [], []
            for half in range(2):
                hd = kv * GROUP + 2 * pr + half
                s_prev = s[rows, (2 * half) * BLOCK:(2 * half + 1) * BLOCK]
                s_cur = s[rows, (2 * half + 1) * BLOCK:(2 * half + 2) * BLOCK]
                if j == 0:
                    s_prev = jnp.where(s_idx > 0, s_prev, NEG_INF)
                t = jnp.where(from_cur, s_cur, s_prev)
                m = jnp.max(t, axis=-1, keepdims=True)
                p = jnp.exp2(t - m).astype(BF16)
                zero = jnp.zeros_like(p)
                p_tiles += [jnp.where(from_cur, zero, p), jnp.where(from_cur, p, zero)]
                e_sink.append(jnp.exp2(sinks_ref[hd] * LOG2E - m))
            p_rows.append(jnp.concatenate(p_tiles, axis=1))
            sink_terms.append(jnp.where(lane_lo, e_sink[0], e_sink[1]))
        o = jnp.dot(jnp.concatenate(p_rows, axis=0), v_rhs, preferred_element_type=F32)
        if n + 1 == len(units):
            gate_a_halves.append(_sigmoid(proj(OFF_GA + D_MODEL // 2, D_MODEL // 2)))
        for pr in range(2):
            rows = slice(pr * BLOCK, (pr + 1) * BLOCK)
            den = o[rows, LANES:2 * LANES] + sink_terms[pr]
            attn_scr[r0:r0 + BLOCK, q_sl[pr]] = (o[rows, 0:LANES] / den).astype(BF16)

    for scr in kv_scrs:
        scr[0:BLOCK, :] = scr[ts:ts + BLOCK, :]

    attn_out = jnp.dot(attn_scr[...], w_ao_ref[...], preferred_element_type=F32)
    merged = merged + jnp.concatenate(gate_a_halves, axis=1) * attn_out
    out_ref[...] = x + jnp.dot(merged.astype(BF16), w_o_ref[...], preferred_element_type=F32)


def _ffn_kernel(x_ref, g_ref, gf_ref, w_gu_ref, w_d_ref, out_hbm,
                act_scr, y_scr, stage_ref, out_sem, *, final_norm, n_seq_tiles):
    ts = x_ref.shape[0]
    step = pl.program_id(0) * n_seq_tiles + pl.program_id(1)
    n_steps = pl.num_programs(0) * n_seq_tiles
    slot = step % 2

    def out_copy(from_slot, tile):
        rows = pl.ds(pl.multiple_of((tile % n_seq_tiles) * ts, ts), ts)
        return pltpu.make_async_copy(
            stage_ref.at[from_slot], out_hbm.at[tile // n_seq_tiles, rows, :],
            out_sem.at[from_slot])

    def closing(y):
        return _rms_norm(y, gf_ref[...]) if final_norm else y

    @pl.when(step == 0)
    def _():
        y_scr[...] = jnp.zeros(y_scr.shape, F32)

    @pl.when(step >= 3)
    def _():
        out_copy(slot, step - 3).wait()

    x = x_ref[...]
    h = _rms_norm(x, g_ref[...]).astype(BF16)
    for n, (off, width) in enumerate(FF_CHUNKS):
        gate = jnp.dot(h, w_gu_ref[:, off:off + width], preferred_element_type=F32)
        up = jnp.dot(h, w_gu_ref[:, D_FF + off:D_FF + off + width], preferred_element_type=F32)
        act_scr[:, off:off + width] = (_silu(gate) * up).astype(BF16)
        if n == 0:
            stage_ref[slot] = closing(y_scr[...])
    y_scr[...] = x + jnp.dot(act_scr[...], w_d_ref[...], preferred_element_type=F32)

    @pl.when(step >= 1)
    def _():
        out_copy(slot, step - 1).start()

    @pl.when(step == n_steps - 1)
    def _():
        last = n_steps - 1
        out_copy(1 - slot, last - 2).wait()
        stage_ref[1 - slot] = closing(y_scr[...])
        out_copy(1 - slot, last).start()
        out_copy(slot, last - 1).wait()
        out_copy(1 - slot, last).wait()


def _resident(shape):
    return pl.BlockSpec(shape, lambda b, s: (0,) * len(shape), pipeline_mode=pl.Buffered(1))


def _rope_tables(seq):
    half = ROT_DIM // 2
    d = np.arange(LANES) % HEAD_DIM
    inv_freq = np.float32(ROPE_THETA) ** (-(2 * (d % half)).astype(np.float32) / np.float32(ROT_DIM))
    ang = np.arange(seq, dtype=np.float32)[:, None] * inv_freq[None, :].astype(np.float32)
    cos, sin = np.cos(ang), np.sin(ang)
    c = np.where(d < ROT_DIM, cos, 1.0)
    s_next = np.where(d < half, -sin, 0.0)
    s_prev = np.where((d >= half) & (d < ROT_DIM), sin, 0.0)
    return jnp.asarray(np.stack([c, s_next, s_prev]).astype(np.float32))


def _mixer(x, g, rope, conv_w, sinks, layer, w_in, w_co, w_ao, w_o, w_gu, w_d):
    b, s, d = x.shape
    ts = SEQ_TILE
    n_s = s // ts
    gu_rows = w_gu.shape[0] // (b * n_s)
    d_rows = 2 * w_d.shape[0] // (b * n_s)
    assert gu_rows % BF16_SUBLANES == 0 and d_rows % BF16_SUBLANES == 0
    gu_slab = pl.BlockSpec((gu_rows, w_gu.shape[1]), lambda i, j: (i * n_s + j, 0))
    d_slab = pl.BlockSpec((d_rows, w_d.shape[1]), lambda i, j: ((i * n_s + j) // 2, 0))
    tile = pl.BlockSpec((None, ts, d), lambda i, j: (i, j, 0))
    kv_scratch = pltpu.VMEM((BLOCK + ts, N_KV_HEADS * LANES), BF16)
    return pl.pallas_call(
        _mixer_kernel,
        grid=(b, n_s),
        in_specs=[
            pl.BlockSpec(memory_space=pltpu.SMEM),
            tile,
            _resident((1, d)),
            pl.BlockSpec((3, ts, LANES), lambda i, j: (0, j, 0)),
            pl.BlockSpec((None, CONV_K, d), lambda i, j: (layer, 0, 0),
                         pipeline_mode=pl.Buffered(1)),
            pl.BlockSpec(memory_space=pl.ANY),
            pl.BlockSpec(memory_space=pl.ANY),
            pl.BlockSpec(memory_space=pl.ANY),
            pl.BlockSpec(memory_space=pl.ANY),
            gu_slab,
            d_slab,
        ],
        out_specs=[tile, gu_slab, d_slab],
        out_shape=[jax.ShapeDtypeStruct(x.shape, x.dtype),
                   jax.ShapeDtypeStruct(w_gu.shape, BF16),
                   jax.ShapeDtypeStruct(w_d.shape, BF16)],
        scratch_shapes=[
            pltpu.VMEM(w_in.shape, BF16),
            pltpu.VMEM(w_co.shape, BF16),
            pltpu.VMEM(w_ao.shape, BF16),
            pltpu.VMEM(w_o.shape, BF16),
            pltpu.VMEM((STAGE_SLOTS, d, STAGE_COLS), F32),
            pltpu.SemaphoreType.DMA((STAGE_SLOTS,)),
            pltpu.VMEM((SUBLANES + ts, d), F32),
            pltpu.VMEM((ts, D_ATTN), BF16),
            kv_scratch, kv_scratch, kv_scratch, kv_scratch,
            pltpu.VMEM((ts, D_ATTN), BF16),
        ],
        compiler_params=pltpu.CompilerParams(
            dimension_semantics=("arbitrary", "arbitrary"),
            vmem_limit_bytes=VMEM_LIMIT_BYTES),
        name="mixer",
    )(sinks, x, g, rope, conv_w, w_in, w_co, w_ao, w_o, w_gu, w_d)


def _ffn(x, g, g_final, w_gu, w_d, final_norm):
    b, s, d = x.shape
    ts = FFN_SEQ_TILE
    n_s = s // ts
    assert b * n_s >= 4
    tile = pl.BlockSpec((None, ts, d), lambda i, j: (i, j, 0))
    return pl.pallas_call(
        functools.partial(_ffn_kernel, final_norm=final_norm, n_seq_tiles=n_s),
        grid=(b, n_s),
        in_specs=[tile, _resident((1, d)), _resident((1, d)),
                  _resident(w_gu.shape), _resident(w_d.shape)],
        out_specs=pl.BlockSpec(memory_space=pl.ANY),
        out_shape=jax.ShapeDtypeStruct(x.shape, x.dtype),
        scratch_shapes=[pltpu.VMEM((ts, D_FF), BF16),
                        pltpu.VMEM((ts, d), F32),
                        pltpu.VMEM((2, ts, d), F32),
                        pltpu.SemaphoreType.DMA((2,))],
        compiler_params=pltpu.CompilerParams(
            dimension_semantics=("arbitrary", "arbitrary"),
            vmem_limit_bytes=VMEM_LIMIT_BYTES),
        name="ffn",
    )(x, g, g_final, w_gu, w_d)


def kernel(x, g_mix, w_in, conv_w, attn_sinks, w_conv_out, w_attn_out, w_o,
           g_ffn, w_gate_up, w_down, g_final):
    b, s, d = x.shape
    depth = w_in.shape[0]
    assert d == D_MODEL and s % SEQ_TILE == 0 and s % FFN_SEQ_TILE == 0
    assert w_in.shape[-1] == N_IN and w_gate_up.shape[-1] == 2 * D_FF
    assert N_IN % STAGE_COLS == 0 and d % STAGE_COLS == 0
    rope = _rope_tables(s)
    g_fin = g_final.reshape(1, d)
    for l in range(depth):
        x, w_gu, w_d = _mixer(
            x, g_mix[l].reshape(1, d), rope, conv_w, attn_sinks[l], l,
            w_in[l], w_conv_out[l], w_attn_out[l], w_o[l],
            w_gate_up[l], w_down[l])
        x = _ffn(x, g_ffn[l].reshape(1, d), g_fin, w_gu, w_d, final_norm=(l == depth - 1))
    return x
```

```python
import functools
import math

import jax
import jax.numpy as jnp
import numpy as np
from jax import lax
from jax.experimental import pallas as pl
from jax.experimental.pallas import tpu as pltpu

D_MODEL = 1024
CONV_K = 3
HEAD_DIM = 64
N_HEADS = 16
N_KV_HEADS = 4
GROUP = N_HEADS // N_KV_HEADS
D_ATTN = N_HEADS * HEAD_DIM
D_KV = N_KV_HEADS * HEAD_DIM
WINDOW = 128
BLOCK = 128
ROT_DIM = HEAD_DIM // 4
ROPE_THETA = 500000.0
ATTN_SCALE = 1.0 / math.sqrt(HEAD_DIM)
LOG2E = math.log2(math.e)
NEG_INF = -1e30
D_FF = 2816
EPS = 1e-5

OFF_CB = 0
OFF_CC = OFF_CB + D_MODEL
OFF_CX = OFF_CC + D_MODEL
OFF_Q = OFF_CX + D_MODEL
OFF_K = OFF_Q + D_ATTN
OFF_V = OFF_K + D_KV
OFF_GC = OFF_V + D_KV
OFF_GA = OFF_GC + D_MODEL
N_IN = OFF_GA + D_MODEL

LANES = 128
SUBLANES = 8
BF16_SUBLANES = 16
SEQ_TILE = 512
FFN_SEQ_TILE = 1024
STAGE_COLS = 512
STAGE_SLOTS = 4
FF_CHUNKS = ((0, 1024), (1024, 1024), (2048, 768))
VMEM_LIMIT_BYTES = 58 * 1024 * 1024

F32 = jnp.float32
BF16 = jnp.bfloat16

assert WINDOW == BLOCK and 2 * HEAD_DIM == LANES and GROUP == 4


def _rms_norm(x, g):
    ms = jnp.mean(x * x, axis=-1, keepdims=True)
    return x * lax.rsqrt(ms + EPS) * g


def _sigmoid(x):
    return 0.5 * jnp.tanh(0.5 * x) + 0.5


def _silu(x):
    h = 0.5 * x
    return h * jnp.tanh(h) + h


def _rope(t, cos, sin_next, sin_prev):
    nxt = pltpu.roll(t, LANES - ROT_DIM // 2, 1)
    prv = pltpu.roll(t, ROT_DIM // 2, 1)
    return t * cos + nxt * sin_next + prv * sin_prev


def _split_head_pair(t, lane_lo):
    sw = pltpu.roll(t, HEAD_DIM, 1)
    zero = jnp.zeros_like(t)
    even = (jnp.where(lane_lo, t, zero), jnp.where(lane_lo, zero, sw))
    odd = (jnp.where(lane_lo, sw, zero), jnp.where(lane_lo, zero, t))
    return even, odd


def _stage_weights(pairs, stage_ref, sem_ref):
    slots = stage_ref.shape[0]
    jobs = [(src, dst, c * STAGE_COLS)
            for src, dst in pairs for c in range(src.shape[1] // STAGE_COLS)]

    def slab_copy(n):
        src, _, col = jobs[n]
        return pltpu.make_async_copy(
            src.at[:, pl.ds(col, STAGE_COLS)], stage_ref.at[n % slots], sem_ref.at[n % slots])

    for n in range(min(slots - 1, len(jobs))):
        slab_copy(n).start()
    for n, (_, dst, col) in enumerate(jobs):
        if n + slots - 1 < len(jobs):
            slab_copy(n + slots - 1).start()
        slab_copy(n).wait()
        dst[:, col:col + STAGE_COLS] = stage_ref[n % slots].astype(BF16)


def _mixer_kernel(sinks_ref, x_ref, g_ref, rope_ref, convw_ref, w_in_hbm,
                  w_co_hbm, w_ao_hbm, w_o_hbm, w_gu_f32_ref, w_d_f32_ref,
                  out_ref, w_gu_bf16_ref, w_d_bf16_ref,
                  w_in_ref, w_co_ref, w_ao_ref, w_o_ref, stage_ref, stage_sem,
                  u_scr, q_scr, klo_scr, khi_scr, vlo_scr, vhi_scr, attn_scr):
    ts = x_ref.shape[0]
    n_blk = ts // BLOCK
    s_idx = pl.program_id(1)
    kv_scrs = (klo_scr, khi_scr, vlo_scr, vhi_scr)

    @pl.when((pl.program_id(0) == 0) & (s_idx == 0))
    def _():
        _stage_weights(((w_in_hbm, w_in_ref), (w_co_hbm, w_co_ref),
                        (w_ao_hbm, w_ao_ref), (w_o_hbm, w_o_ref)), stage_ref, stage_sem)

    @pl.when(s_idx == 0)
    def _():
        u_scr[0:SUBLANES, :] = jnp.zeros((SUBLANES, D_MODEL), F32)
        for scr in kv_scrs:
            scr[0:BLOCK, :] = jnp.zeros((BLOCK, N_KV_HEADS * LANES), BF16)

    w_gu_bf16_ref[...] = w_gu_f32_ref[...].astype(BF16)
    w_d_bf16_ref[...] = w_d_f32_ref[...].astype(BF16)

    x = x_ref[...]
    h = _rms_norm(x, g_ref[...]).astype(BF16)

    def proj(off, width):
        return jnp.dot(h, w_in_ref[:, off:off + width], preferred_element_type=F32)

    seq_rows = pl.ds(pl.multiple_of(s_idx * ts, ts), ts)
    tabs = [rope_ref[i, seq_rows, :] for i in range(3)]
    q_tabs = [t * (ATTN_SCALE * LOG2E) for t in tabs]
    q = proj(OFF_Q, D_ATTN)
    for t in range(D_ATTN // LANES):
        sl = slice(t * LANES, (t + 1) * LANES)
        q_scr[:, sl] = _rope(q[:, sl], *q_tabs).astype(BF16)
    lane_lo_t = lax.broadcasted_iota(jnp.int32, (ts, LANES), 1) < HEAD_DIM
    k = proj(OFF_K, D_KV)
    v = proj(OFF_V, D_KV)
    for t in range(D_KV // LANES):
        sl = slice(t * LANES, (t + 1) * LANES)
        k_t = _rope(k[:, sl], *tabs)
        for src, lo_scr, hi_scr in ((k_t, klo_scr, khi_scr), (v[:, sl], vlo_scr, vhi_scr)):
            for hd, (lo, hi) in zip((2 * t, 2 * t + 1), _split_head_pair(src, lane_lo_t)):
                hsl = slice(hd * LANES, (hd + 1) * LANES)
                lo_scr[BLOCK:BLOCK + ts, hsl] = lo.astype(BF16)
                hi_scr[BLOCK:BLOCK + ts, hsl] = hi.astype(BF16)

    u = proj(OFF_CC, D_MODEL) * proj(OFF_CX, D_MODEL)
    u_scr[SUBLANES:SUBLANES + ts, :] = u
    conv = (convw_ref[0:1, :] * u_scr[SUBLANES - 2:SUBLANES - 2 + ts, :]
            + convw_ref[1:2, :] * u_scr[SUBLANES - 1:SUBLANES - 1 + ts, :]
            + convw_ref[2:3, :] * u)
    u_scr[0:SUBLANES, :] = u_scr[ts:ts + SUBLANES, :]
    conv_y = (proj(OFF_CB, D_MODEL) * conv).astype(BF16)
    gate_c = _sigmoid(proj(OFF_GC, D_MODEL))
    merged = gate_c * jnp.dot(conv_y, w_co_ref[...], preferred_element_type=F32)

    lane = lax.broadcasted_iota(jnp.int32, (BLOCK, LANES), 1)
    rowi = lax.broadcasted_iota(jnp.int32, (BLOCK, LANES), 0)
    from_cur = lane <= rowi
    lane_lo = lane < HEAD_DIM
    lane2 = lax.broadcasted_iota(jnp.int32, (2 * BLOCK, LANES), 1)
    ones_lo = jnp.where(lane2 < HEAD_DIM, 1.0, 0.0).astype(BF16)
    ones_hi = jnp.where(lane2 < HEAD_DIM, 0.0, 1.0).astype(BF16)
    nt_dims = (((1,), (1,)), ((), ()))

    units = [(j, kv) for j in range(n_blk) for kv in range(N_KV_HEADS)]

    def q_slices(kv):
        return [slice((2 * kv + pr) * LANES, (2 * kv + pr + 1) * LANES) for pr in range(2)]

    def scores(j, kv):
        r0 = j * BLOCK
        kv_sl = slice(kv * LANES, (kv + 1) * LANES)
        q2 = jnp.concatenate([q_scr[r0:r0 + BLOCK, sl] for sl in q_slices(kv)], axis=0)
        k_rhs = jnp.concatenate([klo_scr[r0:r0 + 2 * BLOCK, kv_sl],
                                 khi_scr[r0:r0 + 2 * BLOCK, kv_sl]], axis=0)
        return lax.dot_general(q2, k_rhs, nt_dims, preferred_element_type=F32)

    s_next = scores(*units[0])
    for n, (j, kv) in enumerate(units):
        r0 = j * BLOCK
        kv_sl = slice(kv * LANES, (kv + 1) * LANES)
        q_sl = q_slices(kv)
        s = s_next
        if n + 1 < len(units):
            s_next = scores(*units[n + 1])
        else:
            gate_a_halves = [_sigmoid(proj(OFF_GA, D_MODEL // 2))]
        v_rhs = jnp.concatenate(
            [jnp.concatenate([vlo_scr[r0:r0 + 2 * BLOCK, kv_sl], ones_lo], axis=1),
             jnp.concatenate([vhi_scr[r0:r0 + 2 * BLOCK, kv_sl], ones_hi], axis=1)], axis=0)
        p_rows, sink_terms = [], []
        for pr in range(2):
            rows = slice(pr * BLOCK, (pr + 1) * BLOCK)
            p_tiles, e_sink = [], []
            for half in range(2):
                hd = kv * GROUP + 2 * pr + half
                s_prev = s[rows, (2 * half) * BLOCK:(2 * half + 1) * BLOCK]
                s_cur = s[rows, (2 * half + 1) * BLOCK:(2 * half + 2) * BLOCK]
                if j == 0:
                    s_prev = jnp.where(s_idx > 0, s_prev, NEG_INF)
                t = jnp.where(from_cur, s_cur, s_prev)
                m = jnp.max(t, axis=-1, keepdims=True)
                p = jnp.exp2(t - m).astype(BF16)
                zero = jnp.zeros_like(p)
                p_tiles += [jnp.where(from_cur, zero, p), jnp.where(from_cur, p, zero)]
                e_sink.append(jnp.exp2(sinks_ref[hd] * LOG2E - m))
            p_rows.append(jnp.concatenate(p_tiles, axis=1))
            sink_terms.append(jnp.where(lane_lo, e_sink[0], e_sink[1]))
        o = jnp.dot(jnp.concatenate(p_rows, axis=0), v_rhs, preferred_element_type=F32)
        if n + 1 == len(units):
            gate_a_halves.append(_sigmoid(proj(OFF_GA + D_MODEL // 2, D_MODEL // 2)))
        for pr in range(2):
            rows = slice(pr * BLOCK, (pr + 1) * BLOCK)
            den = o[rows, LANES:2 * LANES] + sink_terms[pr]
            attn_scr[r0:r0 + BLOCK, q_sl[pr]] = (o[rows, 0:LANES] / den).astype(BF16)

    for scr in kv_scrs:
        scr[0:BLOCK, :] = scr[ts:ts + BLOCK, :]

    attn_out = jnp.dot(attn_scr[...], w_ao_ref[...], preferred_element_type=F32)
    merged = merged + jnp.concatenate(gate_a_halves, axis=1) * attn_out
    out_ref[...] = x + jnp.dot(merged.astype(BF16), w_o_ref[...], preferred_element_type=F32)


def _ffn_kernel(x_ref, g_ref, gf_ref, w_gu_ref, w_d_ref, out_ref, act_scr, *, final_norm):
    x = x_ref[...]
    h = _rms_norm(x, g_ref[...]).astype(BF16)
    for off, width in FF_CHUNKS:
        gate = jnp.dot(h, w_gu_ref[:, off:off + width], preferred_element_type=F32)
        up = jnp.dot(h, w_gu_ref[:, D_FF + off:D_FF + off + width], preferred_element_type=F32)
        act_scr[:, off:off + width] = (_silu(gate) * up).astype(BF16)
    y = x + jnp.dot(act_scr[...], w_d_ref[...], preferred_element_type=F32)
    if final_norm:
        y = _rms_norm(y, gf_ref[...])
    out_ref[...] = y


def _resident(shape):
    return pl.BlockSpec(shape, lambda b, s: (0,) * len(shape), pipeline_mode=pl.Buffered(1))


def _rope_tables(seq):
    half = ROT_DIM // 2
    d = np.arange(LANES) % HEAD_DIM
    inv_freq = np.float32(ROPE_THETA) ** (-(2 * (d % half)).astype(np.float32) / np.float32(ROT_DIM))
    ang = np.arange(seq, dtype=np.float32)[:, None] * inv_freq[None, :].astype(np.float32)
    cos, sin = np.cos(ang), np.sin(ang)
    c = np.where(d < ROT_DIM, cos, 1.0)
    s_next = np.where(d < half, -sin, 0.0)
    s_prev = np.where((d >= half) & (d < ROT_DIM), sin, 0.0)
    return jnp.asarray(np.stack([c, s_next, s_prev]).astype(np.float32))


def _mixer(x, g, rope, conv_w, sinks, layer, w_in, w_co, w_ao, w_o, w_gu, w_d):
    b, s, d = x.shape
    ts = SEQ_TILE
    n_s = s // ts
    gu_rows = w_gu.shape[0] // (b * n_s)
    d_rows = 2 * w_d.shape[0] // (b * n_s)
    assert gu_rows % BF16_SUBLANES == 0 and d_rows % BF16_SUBLANES == 0
    gu_slab = pl.BlockSpec((gu_rows, w_gu.shape[1]), lambda i, j: (i * n_s + j, 0))
    d_slab = pl.BlockSpec((d_rows, w_d.shape[1]), lambda i, j: ((i * n_s + j) // 2, 0))
    tile = pl.BlockSpec((None, ts, d), lambda i, j: (i, j, 0))
    kv_scratch = pltpu.VMEM((BLOCK + ts, N_KV_HEADS * LANES), BF16)
    return pl.pallas_call(
        _mixer_kernel,
        grid=(b, n_s),
        in_specs=[
            pl.BlockSpec(memory_space=pltpu.SMEM),
            tile,
            _resident((1, d)),
            _resident(rope.shape),
            pl.BlockSpec((None, CONV_K, d), lambda i, j: (layer, 0, 0),
                         pipeline_mode=pl.Buffered(1)),
            pl.BlockSpec(memory_space=pl.ANY),
            pl.BlockSpec(memory_space=pl.ANY),
            pl.BlockSpec(memory_space=pl.ANY),
            pl.BlockSpec(memory_space=pl.ANY),
            gu_slab,
            d_slab,
        ],
        out_specs=[tile, gu_slab, d_slab],
        out_shape=[jax.ShapeDtypeStruct(x.shape, x.dtype),
                   jax.ShapeDtypeStruct(w_gu.shape, BF16),
                   jax.ShapeDtypeStruct(w_d.shape, BF16)],
        scratch_shapes=[
            pltpu.VMEM(w_in.shape, BF16),
            pltpu.VMEM(w_co.shape, BF16),
            pltpu.VMEM(w_ao.shape, BF16),
            pltpu.VMEM(w_o.shape, BF16),
            pltpu.VMEM((STAGE_SLOTS, d, STAGE_COLS), F32),
            pltpu.SemaphoreType.DMA((STAGE_SLOTS,)),
            pltpu.VMEM((SUBLANES + ts, d), F32),
            pltpu.VMEM((ts, D_ATTN), BF16),
            kv_scratch, kv_scratch, kv_scratch, kv_scratch,
            pltpu.VMEM((ts, D_ATTN), BF16),
        ],
        compiler_params=pltpu.CompilerParams(
            dimension_semantics=("arbitrary", "arbitrary"),
            vmem_limit_bytes=VMEM_LIMIT_BYTES),
        name="mixer",
    )(sinks, x, g, rope, conv_w, w_in, w_co, w_ao, w_o, w_gu, w_d)


def _ffn(x, g, g_final, w_gu, w_d, final_norm):
    b, s, d = x.shape
    ts = FFN_SEQ_TILE
    tile = pl.BlockSpec((None, ts, d), lambda i, j: (i, j, 0))
    return pl.pallas_call(
        functools.partial(_ffn_kernel, final_norm=final_norm),
        grid=(b, s // ts),
        in_specs=[tile, _resident((1, d)), _resident((1, d)),
                  _resident(w_gu.shape), _resident(w_d.shape)],
        out_specs=tile,
        out_shape=jax.ShapeDtypeStruct(x.shape, x.dtype),
        scratch_shapes=[pltpu.VMEM((ts, D_FF), BF16)],
        compiler_params=pltpu.CompilerParams(
            dimension_semantics=("arbitrary", "arbitrary"),
            vmem_limit_bytes=VMEM_LIMIT_BYTES),
        name="ffn",
    )(x, g, g_final, w_gu, w_d)


def kernel(x, g_mix, w_in, conv_w, attn_sinks, w_conv_out, w_attn_out, w_o,
           g_ffn, w_gate_up, w_down, g_final):
    b, s, d = x.shape
    depth = w_in.shape[0]
    assert d == D_MODEL and s % SEQ_TILE == 0 and s % FFN_SEQ_TILE == 0
    assert w_in.shape[-1] == N_IN and w_gate_up.shape[-1] == 2 * D_FF
    assert N_IN % STAGE_COLS == 0 and d % STAGE_COLS == 0
    rope = _rope_tables(s)
    g_fin = g_final.reshape(1, d)
    for l in range(depth):
        x, w_gu, w_d = _mixer(
            x, g_mix[l].reshape(1, d), rope, conv_w, attn_sinks[l], l,
            w_in[l], w_conv_out[l], w_attn_out[l], w_o[l],
            w_gate_up[l], w_down[l])
        x = _ffn(x, g_ffn[l].reshape(1, d), g_fin, w_gu, w_d, final_norm=(l == depth - 1))
    return x
```

```python
import functools
import math

import jax
import jax.numpy as jnp
import numpy as np
from jax import lax
from jax.experimental import pallas as pl
from jax.experimental.pallas import tpu as pltpu

D_MODEL = 1024
CONV_K = 3
HEAD_DIM = 64
N_HEADS = 16
N_KV_HEADS = 4
GROUP = N_HEADS // N_KV_HEADS
D_ATTN = N_HEADS * HEAD_DIM
D_KV = N_KV_HEADS * HEAD_DIM
WINDOW = 128
BLOCK = 128
ROT_DIM = HEAD_DIM // 4
ROPE_THETA = 500000.0
ATTN_SCALE = 1.0 / math.sqrt(HEAD_DIM)
LOG2E = math.log2(math.e)
NEG_INF = -1e30
D_FF = 2816
EPS = 1e-5

OFF_CB = 0
OFF_CC = OFF_CB + D_MODEL
OFF_CX = OFF_CC + D_MODEL
OFF_Q = OFF_CX + D_MODEL
OFF_K = OFF_Q + D_ATTN
OFF_V = OFF_K + D_KV
OFF_GC = OFF_V + D_KV
OFF_GA = OFF_GC + D_MODEL
N_IN = OFF_GA + D_MODEL

LANES = 128
SUBLANES = 8
BF16_SUBLANES = 16
SEQ_TILE = 512
FFN_SEQ_TILE = 1024
STAGE_COLS = 512
STAGE_SLOTS = 4
FF_CHUNKS = ((0, 1024), (1024, 1024), (2048, 768))
VMEM_LIMIT_BYTES = 58 * 1024 * 1024

F32 = jnp.float32
BF16 = jnp.bfloat16

assert WINDOW == BLOCK and 2 * HEAD_DIM == LANES and GROUP == 4


def _rms_norm(x, g):
    ms = jnp.mean(x * x, axis=-1, keepdims=True)
    return x * lax.rsqrt(ms + EPS) * g


def _inv_rms(x):
    return lax.rsqrt(jnp.mean(x * x, axis=-1, keepdims=True) + EPS)


def _as_column(row, start, n):
    d = row.shape[1]
    lane = lax.broadcasted_iota(jnp.int32, (n, d), 1)
    idx = lax.broadcasted_iota(jnp.int32, (n, d), 0) + start
    return jnp.sum(jnp.where(lane == idx, row, 0.0), axis=-1, keepdims=True)


def _sigmoid(x):
    return 0.5 * jnp.tanh(0.5 * x) + 0.5


def _silu(x):
    h = 0.5 * x
    return h * jnp.tanh(h) + h


def _rope(t, cos, sin_next, sin_prev):
    nxt = pltpu.roll(t, LANES - ROT_DIM // 2, 1)
    prv = pltpu.roll(t, ROT_DIM // 2, 1)
    return t * cos + nxt * sin_next + prv * sin_prev


def _split_head_pair(t, lane_lo):
    sw = pltpu.roll(t, HEAD_DIM, 1)
    zero = jnp.zeros_like(t)
    even = (jnp.where(lane_lo, t, zero), jnp.where(lane_lo, zero, sw))
    odd = (jnp.where(lane_lo, sw, zero), jnp.where(lane_lo, zero, t))
    return even, odd


def _stage_weights(pairs, stage_ref, sem_ref):
    slots = stage_ref.shape[0]
    jobs = [(src, dst, c * STAGE_COLS, gain)
            for src, dst, gain in pairs for c in range(src.shape[1] // STAGE_COLS)]

    def slab_copy(n):
        src, _, col, _ = jobs[n]
        return pltpu.make_async_copy(
            src.at[:, pl.ds(col, STAGE_COLS)], stage_ref.at[n % slots], sem_ref.at[n % slots])

    for n in range(min(slots - 1, len(jobs))):
        slab_copy(n).start()
    for n, (_, dst, col, gain) in enumerate(jobs):
        if n + slots - 1 < len(jobs):
            slab_copy(n + slots - 1).start()
        slab_copy(n).wait()
        slab = stage_ref[n % slots]
        if gain is not None:
            slab = slab * gain
        dst[:, col:col + STAGE_COLS] = slab.astype(BF16)


def _mixer_kernel(sinks_ref, x_ref, g_ref, g_ffn_ref, rope_ref, convw_ref, w_in_hbm,
                  w_co_hbm, w_ao_hbm, w_o_hbm, w_gu_f32_ref, w_d_f32_ref,
                  out_ref, w_gu_bf16_ref, w_d_bf16_ref,
                  w_in_ref, w_co_ref, w_ao_ref, w_o_ref, stage_ref, stage_sem,
                  u_scr, q_scr, klo_scr, khi_scr, vlo_scr, vhi_scr, attn_scr):
    ts = x_ref.shape[0]
    n_blk = ts // BLOCK
    s_idx = pl.program_id(1)
    kv_scrs = (klo_scr, khi_scr, vlo_scr, vhi_scr)

    @pl.when((pl.program_id(0) == 0) & (s_idx == 0))
    def _():
        g_col = jnp.concatenate([_as_column(g_ref[...], c * LANES, LANES)
                                 for c in range(D_MODEL // LANES)], axis=0)
        _stage_weights(((w_in_hbm, w_in_ref, g_col), (w_co_hbm, w_co_ref, None),
                        (w_ao_hbm, w_ao_ref, None), (w_o_hbm, w_o_ref, None)),
                       stage_ref, stage_sem)

    @pl.when(s_idx == 0)
    def _():
        u_scr[0:SUBLANES, :] = jnp.zeros((SUBLANES, D_MODEL), F32)
        for scr in kv_scrs:
            scr[0:BLOCK, :] = jnp.zeros((BLOCK, N_KV_HEADS * LANES), BF16)

    gu_rows = w_gu_f32_ref.shape[0]
    slab_row0 = (pl.program_id(0) * pl.num_programs(1) + s_idx) * gu_rows
    w_gu_bf16_ref[...] = (w_gu_f32_ref[...]
                          * _as_column(g_ffn_ref[...], slab_row0, gu_rows)).astype(BF16)
    w_d_bf16_ref[...] = w_d_f32_ref[...].astype(BF16)

    x = x_ref[...]
    h = (x * _inv_rms(x)).astype(BF16)

    def proj(off, width):
        return jnp.dot(h, w_in_ref[:, off:off + width], preferred_element_type=F32)

    q_tabs = [rope_ref[i] * (ATTN_SCALE * LOG2E) for i in range(3)]
    q = proj(OFF_Q, D_ATTN)
    for t in range(D_ATTN // LANES):
        sl = slice(t * LANES, (t + 1) * LANES)
        q_scr[:, sl] = _rope(q[:, sl], *q_tabs).astype(BF16)
    lane_lo_t = lax.broadcasted_iota(jnp.int32, (ts, LANES), 1) < HEAD_DIM
    k = proj(OFF_K, D_KV)
    v = proj(OFF_V, D_KV)
    for t in range(D_KV // LANES):
        sl = slice(t * LANES, (t + 1) * LANES)
        k_t = _rope(k[:, sl], rope_ref[0], rope_ref[1], rope_ref[2])
        for src, lo_scr, hi_scr in ((k_t, klo_scr, khi_scr), (v[:, sl], vlo_scr, vhi_scr)):
            for hd, (lo, hi) in zip((2 * t, 2 * t + 1), _split_head_pair(src, lane_lo_t)):
                hsl = slice(hd * LANES, (hd + 1) * LANES)
                lo_scr[BLOCK:BLOCK + ts, hsl] = lo.astype(BF16)
                hi_scr[BLOCK:BLOCK + ts, hsl] = hi.astype(BF16)

    u = proj(OFF_CC, D_MODEL) * proj(OFF_CX, D_MODEL)
    u_scr[SUBLANES:SUBLANES + ts, :] = u
    conv = (convw_ref[0:1, :] * u_scr[SUBLANES - 2:SUBLANES - 2 + ts, :]
            + convw_ref[1:2, :] * u_scr[SUBLANES - 1:SUBLANES - 1 + ts, :]
            + convw_ref[2:3, :] * u)
    u_scr[0:SUBLANES, :] = u_scr[ts:ts + SUBLANES, :]
    conv_y = (proj(OFF_CB, D_MODEL) * conv).astype(BF16)
    gate_c = _sigmoid(proj(OFF_GC, D_MODEL))
    merged = gate_c * jnp.dot(conv_y, w_co_ref[...], preferred_element_type=F32)

    lane = lax.broadcasted_iota(jnp.int32, (BLOCK, LANES), 1)
    rowi = lax.broadcasted_iota(jnp.int32, (BLOCK, LANES), 0)
    from_cur = lane <= rowi
    lane_lo = lane < HEAD_DIM
    lane2 = lax.broadcasted_iota(jnp.int32, (2 * BLOCK, LANES), 1)
    ones_lo = jnp.where(lane2 < HEAD_DIM, 1.0, 0.0).astype(BF16)
    ones_hi = jnp.where(lane2 < HEAD_DIM, 0.0, 1.0).astype(BF16)
    nt_dims = (((1,), (1,)), ((), ()))

    units = [(j, kv) for j in range(n_blk) for kv in range(N_KV_HEADS)]

    def q_slices(kv):
        return [slice((2 * kv + pr) * LANES, (2 * kv + pr + 1) * LANES) for pr in range(2)]

    def scores(j, kv):
        r0 = j * BLOCK
        kv_sl = slice(kv * LANES, (kv + 1) * LANES)
        q2 = jnp.concatenate([q_scr[r0:r0 + BLOCK, sl] for sl in q_slices(kv)], axis=0)
        k_rhs = jnp.concatenate([klo_scr[r0:r0 + 2 * BLOCK, kv_sl],
                                 khi_scr[r0:r0 + 2 * BLOCK, kv_sl]], axis=0)
        return lax.dot_general(q2, k_rhs, nt_dims, preferred_element_type=F32)

    s_next = scores(*units[0])
    for n, (j, kv) in enumerate(units):
        r0 = j * BLOCK
        kv_sl = slice(kv * LANES, (kv + 1) * LANES)
        q_sl = q_slices(kv)
        s = s_next
        if n + 1 < len(units):
            s_next = scores(*units[n + 1])
        else:
            gate_a_halves = [_sigmoid(proj(OFF_GA, D_MODEL // 2))]
        v_rhs = jnp.concatenate(
            [jnp.concatenate([vlo_scr[r0:r0 + 2 * BLOCK, kv_sl], ones_lo], axis=1),
             jnp.concatenate([vhi_scr[r0:r0 + 2 * BLOCK, kv_sl], ones_hi], axis=1)], axis=0)
        p_rows, sink_terms = [], []
        for pr in range(2):
            rows = slice(pr * BLOCK, (pr + 1) * BLOCK)
            p_tiles, e_sink = [], []
            for half in range(2):
                hd = kv * GROUP + 2 * pr + half
                s_prev = s[rows, (2 * half) * BLOCK:(2 * half + 1) * BLOCK]
                s_cur = s[rows, (2 * half + 1) * BLOCK:(2 * half + 2) * BLOCK]
                if j == 0:
                    s_prev = jnp.where(s_idx > 0, s_prev, NEG_INF)
                t = jnp.where(from_cur, s_cur, s_prev)
                m = jnp.max(t, axis=-1, keepdims=True)
                p = jnp.exp2(t - m).astype(BF16)
                zero = jnp.zeros_like(p)
                p_tiles += [jnp.where(from_cur, zero, p), jnp.where(from_cur, p, zero)]
                e_sink.append(jnp.exp2(sinks_ref[hd] * LOG2E - m))
            p_rows.append(jnp.concatenate(p_tiles, axis=1))
            sink_terms.append(jnp.where(lane_lo, e_sink[0], e_sink[1]))
        o = jnp.dot(jnp.concatenate(p_rows, axis=0), v_rhs, preferred_element_type=F32)
        if n + 1 == len(units):
            gate_a_halves.append(_sigmoid(proj(OFF_GA + D_MODEL // 2, D_MODEL // 2)))
        for pr in range(2):
            rows = slice(pr * BLOCK, (pr + 1) * BLOCK)
            den = o[rows, LANES:2 * LANES] + sink_terms[pr]
            attn_scr[r0:r0 + BLOCK, q_sl[pr]] = (o[rows, 0:LANES] / den).astype(BF16)

    for scr in kv_scrs:
        scr[0:BLOCK, :] = scr[ts:ts + BLOCK, :]

    attn_out = jnp.dot(attn_scr[...], w_ao_ref[...], preferred_element_type=F32)
    merged = merged + jnp.concatenate(gate_a_halves, axis=1) * attn_out
    out_ref[...] = x + jnp.dot(merged.astype(BF16), w_o_ref[...], preferred_element_type=F32)


def _ffn_kernel(x_ref, gf_ref, w_gu_ref, w_d_ref, out_ref, act_scr, *, final_norm):
    x = x_ref[...]
    h = (x * _inv_rms(x)).astype(BF16)
    for off, width in FF_CHUNKS:
        gate = jnp.dot(h, w_gu_ref[:, off:off + width], preferred_element_type=F32)
        up = jnp.dot(h, w_gu_ref[:, D_FF + off:D_FF + off + width], preferred_element_type=F32)
        act_scr[:, off:off + width] = (_silu(gate) * up).astype(BF16)
    y = x + jnp.dot(act_scr[...], w_d_ref[...], preferred_element_type=F32)
    if final_norm:
        y = _rms_norm(y, gf_ref[...])
    out_ref[...] = y


def _resident(shape):
    return pl.BlockSpec(shape, lambda b, s: (0,) * len(shape), pipeline_mode=pl.Buffered(1))


def _rope_tables(seq):
    half = ROT_DIM // 2
    d = np.arange(LANES) % HEAD_DIM
    inv_freq = np.float32(ROPE_THETA) ** (-(2 * (d % half)).astype(np.float32) / np.float32(ROT_DIM))
    ang = np.arange(seq, dtype=np.float32)[:, None] * inv_freq[None, :].astype(np.float32)
    cos, sin = np.cos(ang), np.sin(ang)
    c = np.where(d < ROT_DIM, cos, 1.0)
    s_next = np.where(d < half, -sin, 0.0)
    s_prev = np.where((d >= half) & (d < ROT_DIM), sin, 0.0)
    return jnp.asarray(np.stack([c, s_next, s_prev]).astype(np.float32))


def _mixer(x, g, g_ffn, rope, conv_w, sinks, layer, w_in, w_co, w_ao, w_o, w_gu, w_d):
    b, s, d = x.shape
    ts = SEQ_TILE
    n_s = s // ts
    gu_rows = w_gu.shape[0] // (b * n_s)
    d_rows = 2 * w_d.shape[0] // (b * n_s)
    assert gu_rows % BF16_SUBLANES == 0 and d_rows % BF16_SUBLANES == 0
    gu_slab = pl.BlockSpec((gu_rows, w_gu.shape[1]), lambda i, j: (i * n_s + j, 0))
    d_slab = pl.BlockSpec((d_rows, w_d.shape[1]), lambda i, j: ((i * n_s + j) // 2, 0))
    tile = pl.BlockSpec((None, ts, d), lambda i, j: (i, j, 0))
    kv_scratch = pltpu.VMEM((BLOCK + ts, N_KV_HEADS * LANES), BF16)
    return pl.pallas_call(
        _mixer_kernel,
        grid=(b, n_s),
        in_specs=[
            pl.BlockSpec(memory_space=pltpu.SMEM),
            tile,
            _resident((1, d)),
            _resident((1, d)),
            pl.BlockSpec((3, ts, LANES), lambda i, j: (0, j, 0)),
            pl.BlockSpec((None, CONV_K, d), lambda i, j: (layer, 0, 0),
                         pipeline_mode=pl.Buffered(1)),
            pl.BlockSpec(memory_space=pl.ANY),
            pl.BlockSpec(memory_space=pl.ANY),
            pl.BlockSpec(memory_space=pl.ANY),
            pl.BlockSpec(memory_space=pl.ANY),
            gu_slab,
            d_slab,
        ],
        out_specs=[tile, gu_slab, d_slab],
        out_shape=[jax.ShapeDtypeStruct(x.shape, x.dtype),
                   jax.ShapeDtypeStruct(w_gu.shape, BF16),
                   jax.ShapeDtypeStruct(w_d.shape, BF16)],
        scratch_shapes=[
            pltpu.VMEM(w_in.shape, BF16),
            pltpu.VMEM(w_co.shape, BF16),
            pltpu.VMEM(w_ao.shape, BF16),
            pltpu.VMEM(w_o.shape, BF16),
            pltpu.VMEM((STAGE_SLOTS, d, STAGE_COLS), F32),
            pltpu.SemaphoreType.DMA((STAGE_SLOTS,)),
            pltpu.VMEM((SUBLANES + ts, d), F32),
            pltpu.VMEM((ts, D_ATTN), BF16),
            kv_scratch, kv_scratch, kv_scratch, kv_scratch,
            pltpu.VMEM((ts, D_ATTN), BF16),
        ],
        compiler_params=pltpu.CompilerParams(
            dimension_semantics=("arbitrary", "arbitrary"),
            vmem_limit_bytes=VMEM_LIMIT_BYTES),
        name="mixer",
    )(sinks, x, g, g_ffn, rope, conv_w, w_in, w_co, w_ao, w_o, w_gu, w_d)


def _ffn(x, g_final, w_gu, w_d, final_norm):
    b, s, d = x.shape
    ts = FFN_SEQ_TILE
    tile = pl.BlockSpec((None, ts, d), lambda i, j: (i, j, 0))
    return pl.pallas_call(
        functools.partial(_ffn_kernel, final_norm=final_norm),
        grid=(b, s // ts),
        in_specs=[tile, _resident((1, d)), _resident(w_gu.shape), _resident(w_d.shape)],
        out_specs=tile,
        out_shape=jax.ShapeDtypeStruct(x.shape, x.dtype),
        scratch_shapes=[pltpu.VMEM((ts, D_FF), BF16)],
        compiler_params=pltpu.CompilerParams(
            dimension_semantics=("arbitrary", "arbitrary"),
            vmem_limit_bytes=VMEM_LIMIT_BYTES),
        name="ffn",
    )(x, g_final, w_gu, w_d)


def kernel(x, g_mix, w_in, conv_w, attn_sinks, w_conv_out, w_attn_out, w_o,
           g_ffn, w_gate_up, w_down, g_final):
    b, s, d = x.shape
    depth = w_in.shape[0]
    assert d == D_MODEL and s % SEQ_TILE == 0 and s % FFN_SEQ_TILE == 0
    assert w_in.shape[-1] == N_IN and w_gate_up.shape[-1] == 2 * D_FF
    assert N_IN % STAGE_COLS == 0 and d % STAGE_COLS == 0
    rope = _rope_tables(s)
    g_fin = g_final.reshape(1, d)
    for l in range(depth):
        x, w_gu, w_d = _mixer(
            x, g_mix[l].reshape(1, d), g_ffn[l].reshape(1, d), rope, conv_w, attn_sinks[l], l,
            w_in[l], w_conv_out[l], w_attn_out[l], w_o[l],
            w_gate_up[l], w_down[l])
        x = _ffn(x, g_fin, w_gu, w_d, final_norm=(l == depth - 1))
    return x
```

```python
import functools
import math

import jax
import jax.numpy as jnp
import numpy as np
from jax import lax
from jax.experimental import pallas as pl
from jax.experimental.pallas import tpu as pltpu

D_MODEL = 1024
CONV_K = 3
HEAD_DIM = 64
N_HEADS = 16
N_KV_HEADS = 4
GROUP = N_HEADS // N_KV_HEADS
D_ATTN = N_HEADS * HEAD_DIM
D_KV = N_KV_HEADS * HEAD_DIM
WINDOW = 128
BLOCK = 128
ROT_DIM = HEAD_DIM // 4
ROPE_THETA = 500000.0
ATTN_SCALE = 1.0 / math.sqrt(HEAD_DIM)
LOG2E = math.log2(math.e)
NEG_INF = -1e30
D_FF = 2816
EPS = 1e-5

OFF_CB = 0
OFF_CC = OFF_CB + D_MODEL
OFF_CX = OFF_CC + D_MODEL
OFF_Q = OFF_CX + D_MODEL
OFF_K = OFF_Q + D_ATTN
OFF_V = OFF_K + D_KV
OFF_GC = OFF_V + D_KV
OFF_GA = OFF_GC + D_MODEL
N_IN = OFF_GA + D_MODEL

LANES = 128
SUBLANES = 8
BF16_SUBLANES = 16
SEQ_TILE = 512
FFN_SEQ_TILE = 1024
STAGE_COLS = 512
STAGE_SLOTS = 4
FF_CHUNKS = ((0, 1024), (1024, 1024), (2048, 768))
VMEM_LIMIT_BYTES = 58 * 1024 * 1024

F32 = jnp.float32
BF16 = jnp.bfloat16

assert WINDOW == BLOCK and 2 * HEAD_DIM == LANES and GROUP == 4


def _rms_norm(x, g):
    ms = jnp.mean(x * x, axis=-1, keepdims=True)
    return x * lax.rsqrt(ms + EPS) * g


def _sigmoid(x):
    return 0.5 * jnp.tanh(0.5 * x) + 0.5


def _silu(x):
    h = 0.5 * x
    return h * jnp.tanh(h) + h


def _rope(t, cos, sin_next, sin_prev):
    nxt = pltpu.roll(t, LANES - ROT_DIM // 2, 1)
    prv = pltpu.roll(t, ROT_DIM // 2, 1)
    return t * cos + nxt * sin_next + prv * sin_prev


def _split_head_pair(t, lane_lo):
    sw = pltpu.roll(t, HEAD_DIM, 1)
    zero = jnp.zeros_like(t)
    even = (jnp.where(lane_lo, t, zero), jnp.where(lane_lo, zero, sw))
    odd = (jnp.where(lane_lo, sw, zero), jnp.where(lane_lo, zero, t))
    return even, odd


def _stage_weights(pairs, stage_ref, sem_ref):
    slots = stage_ref.shape[0]
    jobs = [(src, dst, c * STAGE_COLS)
            for src, dst in pairs for c in range(src.shape[1] // STAGE_COLS)]

    def slab_copy(n):
        src, _, col = jobs[n]
        return pltpu.make_async_copy(
            src.at[:, pl.ds(col, STAGE_COLS)], stage_ref.at[n % slots], sem_ref.at[n % slots])

    for n in range(min(slots - 1, len(jobs))):
        slab_copy(n).start()
    for n, (_, dst, col) in enumerate(jobs):
        if n + slots - 1 < len(jobs):
            slab_copy(n + slots - 1).start()
        slab_copy(n).wait()
        dst[:, col:col + STAGE_COLS] = stage_ref[n % slots].astype(BF16)


def _mixer_kernel(sinks_ref, x_ref, g_ref, rope_ref, convw_ref, w_in_hbm,
                  w_co_hbm, w_ao_hbm, w_o_hbm, w_gu_f32_ref, w_d_f32_ref,
                  out_ref, w_gu_bf16_ref, w_d_bf16_ref,
                  w_in_ref, w_co_ref, w_ao_ref, w_o_ref, stage_ref, stage_sem,
                  u_scr, q_scr, klo_scr, khi_scr, vlo_scr, vhi_scr, attn_scr):
    ts = x_ref.shape[0]
    n_blk = ts // BLOCK
    s_idx = pl.program_id(1)
    kv_scrs = (klo_scr, khi_scr, vlo_scr, vhi_scr)

    @pl.when((pl.program_id(0) == 0) & (s_idx == 0))
    def _():
        _stage_weights(((w_in_hbm, w_in_ref), (w_co_hbm, w_co_ref),
                        (w_ao_hbm, w_ao_ref), (w_o_hbm, w_o_ref)), stage_ref, stage_sem)

    @pl.when(s_idx == 0)
    def _():
        u_scr[0:SUBLANES, :] = jnp.zeros((SUBLANES, D_MODEL), F32)
        for scr in kv_scrs:
            scr[0:BLOCK, :] = jnp.zeros((BLOCK, N_KV_HEADS * LANES), BF16)

    w_gu_bf16_ref[...] = w_gu_f32_ref[...].astype(BF16)
    w_d_bf16_ref[...] = w_d_f32_ref[...].astype(BF16)

    x = x_ref[...]
    h = _rms_norm(x, g_ref[...]).astype(BF16)

    def proj(off, width):
        return jnp.dot(h, w_in_ref[:, off:off + width], preferred_element_type=F32)

    q_tabs = [rope_ref[i] * (ATTN_SCALE * LOG2E) for i in range(3)]
    q = proj(OFF_Q, D_ATTN)
    for t in range(D_ATTN // LANES):
        sl = slice(t * LANES, (t + 1) * LANES)
        q_scr[:, sl] = _rope(q[:, sl], *q_tabs).astype(BF16)
    lane_lo_t = lax.broadcasted_iota(jnp.int32, (ts, LANES), 1) < HEAD_DIM
    kv = proj(OFF_K, 2 * D_KV)
    k, v = kv[:, :D_KV], kv[:, D_KV:]
    for t in range(D_KV // LANES):
        sl = slice(t * LANES, (t + 1) * LANES)
        k_t = _rope(k[:, sl], rope_ref[0], rope_ref[1], rope_ref[2])
        for src, lo_scr, hi_scr in ((k_t, klo_scr, khi_scr), (v[:, sl], vlo_scr, vhi_scr)):
            for hd, (lo, hi) in zip((2 * t, 2 * t + 1), _split_head_pair(src, lane_lo_t)):
                hsl = slice(hd * LANES, (hd + 1) * LANES)
                lo_scr[BLOCK:BLOCK + ts, hsl] = lo.astype(BF16)
                hi_scr[BLOCK:BLOCK + ts, hsl] = hi.astype(BF16)

    u = proj(OFF_CC, D_MODEL) * proj(OFF_CX, D_MODEL)
    u_scr[SUBLANES:SUBLANES + ts, :] = u
    conv = (convw_ref[0:1, :] * u_scr[SUBLANES - 2:SUBLANES - 2 + ts, :]
            + convw_ref[1:2, :] * u_scr[SUBLANES - 1:SUBLANES - 1 + ts, :]
            + convw_ref[2:3, :] * u)
    u_scr[0:SUBLANES, :] = u_scr[ts:ts + SUBLANES, :]
    conv_y = (proj(OFF_CB, D_MODEL) * conv).astype(BF16)
    gate_c = _sigmoid(proj(OFF_GC, D_MODEL))
    merged = gate_c * jnp.dot(conv_y, w_co_ref[...], preferred_element_type=F32)

    lane = lax.broadcasted_iota(jnp.int32, (BLOCK, LANES), 1)
    rowi = lax.broadcasted_iota(jnp.int32, (BLOCK, LANES), 0)
    from_cur = lane <= rowi
    lane_lo = lane < HEAD_DIM
    lane2 = lax.broadcasted_iota(jnp.int32, (2 * BLOCK, LANES), 1)
    ones_lo = jnp.where(lane2 < HEAD_DIM, 1.0, 0.0).astype(BF16)
    ones_hi = jnp.where(lane2 < HEAD_DIM, 0.0, 1.0).astype(BF16)
    nt_dims = (((1,), (1,)), ((), ()))

    units = [(j, kv) for j in range(n_blk) for kv in range(N_KV_HEADS)]

    def q_slices(kv):
        return [slice((2 * kv + pr) * LANES, (2 * kv + pr + 1) * LANES) for pr in range(2)]

    def scores(j, kv):
        r0 = j * BLOCK
        kv_sl = slice(kv * LANES, (kv + 1) * LANES)
        q2 = jnp.concatenate([q_scr[r0:r0 + BLOCK, sl] for sl in q_slices(kv)], axis=0)
        k_rhs = jnp.concatenate([klo_scr[r0:r0 + 2 * BLOCK, kv_sl],
                                 khi_scr[r0:r0 + 2 * BLOCK, kv_sl]], axis=0)
        return lax.dot_general(q2, k_rhs, nt_dims, preferred_element_type=F32)

    s_next = scores(*units[0])
    for n, (j, kv) in enumerate(units):
        r0 = j * BLOCK
        kv_sl = slice(kv * LANES, (kv + 1) * LANES)
        q_sl = q_slices(kv)
        s = s_next
        if n + 1 < len(units):
            s_next = scores(*units[n + 1])
        else:
            gate_a_halves = [_sigmoid(proj(OFF_GA, D_MODEL // 2))]
        v_rhs = jnp.concatenate(
            [jnp.concatenate([vlo_scr[r0:r0 + 2 * BLOCK, kv_sl], ones_lo], axis=1),
             jnp.concatenate([vhi_scr[r0:r0 + 2 * BLOCK, kv_sl], ones_hi], axis=1)], axis=0)
        p_rows, sink_terms = [], []
        for pr in range(2):
            rows = slice(pr * BLOCK, (pr + 1) * BLOCK)
            p_tiles, e_sink = [], []
            for half in range(2):
                hd = kv * GROUP + 2 * pr + half
                s_prev = s[rows, (2 * half) * BLOCK:(2 * half + 1) * BLOCK]
                s_cur = s[rows, (2 * half + 1) * BLOCK:(2 * half + 2) * BLOCK]
                if j == 0:
                    s_prev = jnp.where(s_idx > 0, s_prev, NEG_INF)
                t = jnp.where(from_cur, s_cur, s_prev)
                m = jnp.max(t, axis=-1, keepdims=True)
                p = jnp.exp2(t - m).astype(BF16)
                zero = jnp.zeros_like(p)
                p_tiles += [jnp.where(from_cur, zero, p), jnp.where(from_cur, p, zero)]
                e_sink.append(jnp.exp2(sinks_ref[hd] * LOG2E - m))
            p_rows.append(jnp.concatenate(p_tiles, axis=1))
            sink_terms.append(jnp.where(lane_lo, e_sink[0], e_sink[1]))
        o = jnp.dot(jnp.concatenate(p_rows, axis=0), v_rhs, preferred_element_type=F32)
        if n + 1 == len(units):
            gate_a_halves.append(_sigmoid(proj(OFF_GA + D_MODEL // 2, D_MODEL // 2)))
        for pr in range(2):
            rows = slice(pr * BLOCK, (pr + 1) * BLOCK)
            den = o[rows, LANES:2 * LANES] + sink_terms[pr]
            attn_scr[r0:r0 + BLOCK, q_sl[pr]] = (o[rows, 0:LANES] / den).astype(BF16)

    for scr in kv_scrs:
        scr[0:BLOCK, :] = scr[ts:ts + BLOCK, :]

    attn_out = jnp.dot(attn_scr[...], w_ao_ref[...], preferred_element_type=F32)
    merged = merged + jnp.concatenate(gate_a_halves, axis=1) * attn_out
    out_ref[...] = x + jnp.dot(merged.astype(BF16), w_o_ref[...], preferred_element_type=F32)


def _ffn_kernel(x_ref, g_ref, gf_ref, w_gu_ref, w_d_ref, out_ref, act_scr, *, final_norm):
    x = x_ref[...]
    h = _rms_norm(x, g_ref[...]).astype(BF16)
    for off, width in FF_CHUNKS:
        gate = jnp.dot(h, w_gu_ref[:, off:off + width], preferred_element_type=F32)
        up = jnp.dot(h, w_gu_ref[:, D_FF + off:D_FF + off + width], preferred_element_type=F32)
        act_scr[:, off:off + width] = (_silu(gate) * up).astype(BF16)
    y = x + jnp.dot(act_scr[...], w_d_ref[...], preferred_element_type=F32)
    if final_norm:
        y = _rms_norm(y, gf_ref[...])
    out_ref[...] = y


def _resident(shape):
    return pl.BlockSpec(shape, lambda b, s: (0,) * len(shape), pipeline_mode=pl.Buffered(1))


def _rope_tables(seq):
    half = ROT_DIM // 2
    d = np.arange(LANES) % HEAD_DIM
    inv_freq = np.float32(ROPE_THETA) ** (-(2 * (d % half)).astype(np.float32) / np.float32(ROT_DIM))
    ang = np.arange(seq, dtype=np.float32)[:, None] * inv_freq[None, :].astype(np.float32)
    cos, sin = np.cos(ang), np.sin(ang)
    c = np.where(d < ROT_DIM, cos, 1.0)
    s_next = np.where(d < half, -sin, 0.0)
    s_prev = np.where((d >= half) & (d < ROT_DIM), sin, 0.0)
    return jnp.asarray(np.stack([c, s_next, s_prev]).astype(np.float32))


def _mixer(x, g, rope, conv_w, sinks, layer, w_in, w_co, w_ao, w_o, w_gu, w_d):
    b, s, d = x.shape
    ts = SEQ_TILE
    n_s = s // ts
    gu_rows = w_gu.shape[0] // (b * n_s)
    d_rows = 2 * w_d.shape[0] // (b * n_s)
    assert gu_rows % BF16_SUBLANES == 0 and d_rows % BF16_SUBLANES == 0
    gu_slab = pl.BlockSpec((gu_rows, w_gu.shape[1]), lambda i, j: (i * n_s + j, 0))
    d_slab = pl.BlockSpec((d_rows, w_d.shape[1]), lambda i, j: ((i * n_s + j) // 2, 0))
    tile = pl.BlockSpec((None, ts, d), lambda i, j: (i, j, 0))
    kv_scratch = pltpu.VMEM((BLOCK + ts, N_KV_HEADS * LANES), BF16)
    return pl.pallas_call(
        _mixer_kernel,
        grid=(b, n_s),
        in_specs=[
            pl.BlockSpec(memory_space=pltpu.SMEM),
            tile,
            _resident((1, d)),
            pl.BlockSpec((3, ts, LANES), lambda i, j: (0, j, 0)),
            pl.BlockSpec((None, CONV_K, d), lambda i, j: (layer, 0, 0),
                         pipeline_mode=pl.Buffered(1)),
            pl.BlockSpec(memory_space=pl.ANY),
            pl.BlockSpec(memory_space=pl.ANY),
            pl.BlockSpec(memory_space=pl.ANY),
            pl.BlockSpec(memory_space=pl.ANY),
            gu_slab,
            d_slab,
        ],
        out_specs=[tile, gu_slab, d_slab],
        out_shape=[jax.ShapeDtypeStruct(x.shape, x.dtype),
                   jax.ShapeDtypeStruct(w_gu.shape, BF16),
                   jax.ShapeDtypeStruct(w_d.shape, BF16)],
        scratch_shapes=[
            pltpu.VMEM(w_in.shape, BF16),
            pltpu.VMEM(w_co.shape, BF16),
            pltpu.VMEM(w_ao.shape, BF16),
            pltpu.VMEM(w_o.shape, BF16),
            pltpu.VMEM((STAGE_SLOTS, d, STAGE_COLS), F32),
            pltpu.SemaphoreType.DMA((STAGE_SLOTS,)),
            pltpu.VMEM((SUBLANES + ts, d), F32),
            pltpu.VMEM((ts, D_ATTN), BF16),
            kv_scratch, kv_scratch, kv_scratch, kv_scratch,
            pltpu.VMEM((ts, D_ATTN), BF16),
        ],
        compiler_params=pltpu.CompilerParams(
            dimension_semantics=("arbitrary", "arbitrary"),
            vmem_limit_bytes=VMEM_LIMIT_BYTES),
        name="mixer",
    )(sinks, x, g, rope, conv_w, w_in, w_co, w_ao, w_o, w_gu, w_d)


def _ffn(x, g, g_final, w_gu, w_d, final_norm):
    b, s, d = x.shape
    ts = FFN_SEQ_TILE
    tile = pl.BlockSpec((None, ts, d), lambda i, j: (i, j, 0))
    return pl.pallas_call(
        functools.partial(_ffn_kernel, final_norm=final_norm),
        grid=(b, s // ts),
        in_specs=[tile, _resident((1, d)), _resident((1, d)),
                  _resident(w_gu.shape), _resident(w_d.shape)],
        out_specs=tile,
        out_shape=jax.ShapeDtypeStruct(x.shape, x.dtype),
        scratch_shapes=[pltpu.VMEM((ts, D_FF), BF16)],
        compiler_params=pltpu.CompilerParams(
            dimension_semantics=("parallel", "parallel"),
            vmem_limit_bytes=VMEM_LIMIT_BYTES),
        name="ffn",
    )(x, g, g_final, w_gu, w_d)


def kernel(x, g_mix, w_in, conv_w, attn_sinks, w_conv_out, w_attn_out, w_o,
           g_ffn, w_gate_up, w_down, g_final):
    b, s, d = x.shape
    depth = w_in.shape[0]
    assert d == D_MODEL and s % SEQ_TILE == 0 and s % FFN_SEQ_TILE == 0
    assert w_in.shape[-1] == N_IN and w_gate_up.shape[-1] == 2 * D_FF
    assert N_IN % STAGE_COLS == 0 and d % STAGE_COLS == 0
    rope = _rope_tables(s)
    g_fin = g_final.reshape(1, d)
    for l in range(depth):
        x, w_gu, w_d = _mixer(
            x, g_mix[l].reshape(1, d), rope, conv_w, attn_sinks[l], l,
            w_in[l], w_conv_out[l], w_attn_out[l], w_o[l],
            w_gate_up[l], w_down[l])
        x = _ffn(x, g_ffn[l].reshape(1, d), g_fin, w_gu, w_d, final_norm=(l == depth - 1))
    return x
```

```python
import functools
import math

import jax
import jax.numpy as jnp
import numpy as np
from jax import lax
from jax.experimental import pallas as pl
from jax.experimental.pallas import tpu as pltpu

D_MODEL = 1024
CONV_K = 3
HEAD_DIM = 64
N_HEADS = 16
N_KV_HEADS = 4
GROUP = N_HEADS // N_KV_HEADS
D_ATTN = N_HEADS * HEAD_DIM
D_KV = N_KV_HEADS * HEAD_DIM
WINDOW = 128
BLOCK = 128
ROT_DIM = HEAD_DIM // 4
ROPE_THETA = 500000.0
ATTN_SCALE = 1.0 / math.sqrt(HEAD_DIM)
LOG2E = math.log2(math.e)
NEG_INF = -1e30
D_FF = 2816
EPS = 1e-5

OFF_CB = 0
OFF_CC = OFF_CB + D_MODEL
OFF_CX = OFF_CC + D_MODEL
OFF_Q = OFF_CX + D_MODEL
OFF_K = OFF_Q + D_ATTN
OFF_V = OFF_K + D_KV
OFF_GC = OFF_V + D_KV
OFF_GA = OFF_GC + D_MODEL
N_IN = OFF_GA + D_MODEL

LANES = 128
SUBLANES = 8
BF16_SUBLANES = 16
SEQ_TILE = 512
FFN_SEQ_TILE = 1024
STAGE_COLS = 512
STAGE_SLOTS = 4
FF_CHUNKS = ((0, 1024), (1024, 1024), (2048, 768))
VMEM_LIMIT_BYTES = 58 * 1024 * 1024

F32 = jnp.float32
BF16 = jnp.bfloat16

assert WINDOW == BLOCK and 2 * HEAD_DIM == LANES and GROUP == 4


def _rms_norm(x, g):
    ms = jnp.mean(x * x, axis=-1, keepdims=True)
    return x * lax.rsqrt(ms + EPS) * g


def _sigmoid(x):
    return 0.5 * jnp.tanh(0.5 * x) + 0.5


def _silu(x):
    h = 0.5 * x
    return h * jnp.tanh(h) + h


def _rope(t, cos, sin_next, sin_prev):
    nxt = pltpu.roll(t, LANES - ROT_DIM // 2, 1)
    prv = pltpu.roll(t, ROT_DIM // 2, 1)
    return t * cos + nxt * sin_next + prv * sin_prev


def _split_head_pair(t, lane_lo):
    sw = pltpu.roll(t, HEAD_DIM, 1)
    zero = jnp.zeros_like(t)
    even = (jnp.where(lane_lo, t, zero), jnp.where(lane_lo, zero, sw))
    odd = (jnp.where(lane_lo, sw, zero), jnp.where(lane_lo, zero, t))
    return even, odd


def _stage_weights(pairs, stage_ref, sem_ref):
    slots = stage_ref.shape[0]
    jobs = [(src, dst, c * STAGE_COLS)
            for src, dst in pairs for c in range(src.shape[1] // STAGE_COLS)]

    def slab_copy(n):
        src, _, col = jobs[n]
        return pltpu.make_async_copy(
            src.at[:, pl.ds(col, STAGE_COLS)], stage_ref.at[n % slots], sem_ref.at[n % slots])

    for n in range(min(slots - 1, len(jobs))):
        slab_copy(n).start()
    for n, (_, dst, col) in enumerate(jobs):
        if n + slots - 1 < len(jobs):
            slab_copy(n + slots - 1).start()
        slab_copy(n).wait()
        dst[:, col:col + STAGE_COLS] = stage_ref[n % slots].astype(BF16)


def _mixer_kernel(sinks_ref, x_ref, g_ref, rope_ref, convw_hbm, w_in_hbm,
                  w_co_hbm, w_ao_hbm, w_o_hbm, w_gu_f32_ref, w_d_f32_ref,
                  out_ref, w_gu_bf16_ref, w_d_bf16_ref,
                  w_in_ref, w_co_ref, w_ao_ref, w_o_ref, stage_ref, stage_sem,
                  u_scr, q_scr, klo_scr, khi_scr, vlo_scr, vhi_scr, attn_scr,
                  convw_ref, convw_sem, *, layer):
    ts = x_ref.shape[0]
    n_blk = ts // BLOCK
    s_idx = pl.program_id(1)
    kv_scrs = (klo_scr, khi_scr, vlo_scr, vhi_scr)

    @pl.when((pl.program_id(0) == 0) & (s_idx == 0))
    def _():
        taps = pltpu.make_async_copy(convw_hbm.at[layer], convw_ref, convw_sem.at[0])
        taps.start()
        _stage_weights(((w_in_hbm, w_in_ref), (w_co_hbm, w_co_ref),
                        (w_ao_hbm, w_ao_ref), (w_o_hbm, w_o_ref)), stage_ref, stage_sem)
        taps.wait()

    @pl.when(s_idx == 0)
    def _():
        u_scr[0:SUBLANES, :] = jnp.zeros((SUBLANES, D_MODEL), F32)
        for scr in kv_scrs:
            scr[0:BLOCK, :] = jnp.zeros((BLOCK, N_KV_HEADS * LANES), BF16)

    w_gu_bf16_ref[...] = w_gu_f32_ref[...].astype(BF16)
    w_d_bf16_ref[...] = w_d_f32_ref[...].astype(BF16)

    x = x_ref[...]
    h = _rms_norm(x, g_ref[...]).astype(BF16)

    def proj(off, width):
        return jnp.dot(h, w_in_ref[:, off:off + width], preferred_element_type=F32)

    q_tabs = [rope_ref[i] * (ATTN_SCALE * LOG2E) for i in range(3)]
    q = proj(OFF_Q, D_ATTN)
    for t in range(D_ATTN // LANES):
        sl = slice(t * LANES, (t + 1) * LANES)
        q_scr[:, sl] = _rope(q[:, sl], *q_tabs).astype(BF16)
    lane_lo_t = lax.broadcasted_iota(jnp.int32, (ts, LANES), 1) < HEAD_DIM
    k = proj(OFF_K, D_KV)
    v = proj(OFF_V, D_KV)
    for t in range(D_KV // LANES):
        sl = slice(t * LANES, (t + 1) * LANES)
        k_t = _rope(k[:, sl], rope_ref[0], rope_ref[1], rope_ref[2])
        for src, lo_scr, hi_scr in ((k_t, klo_scr, khi_scr), (v[:, sl], vlo_scr, vhi_scr)):
            for hd, (lo, hi) in zip((2 * t, 2 * t + 1), _split_head_pair(src, lane_lo_t)):
                hsl = slice(hd * LANES, (hd + 1) * LANES)
                lo_scr[BLOCK:BLOCK + ts, hsl] = lo.astype(BF16)
                hi_scr[BLOCK:BLOCK + ts, hsl] = hi.astype(BF16)

    u = proj(OFF_CC, D_MODEL) * proj(OFF_CX, D_MODEL)
    u_scr[SUBLANES:SUBLANES + ts, :] = u
    conv = (convw_ref[0:1, :] * u_scr[SUBLANES - 2:SUBLANES - 2 + ts, :]
            + convw_ref[1:2, :] * u_scr[SUBLANES - 1:SUBLANES - 1 + ts, :]
            + convw_ref[2:3, :] * u)
    u_scr[0:SUBLANES, :] = u_scr[ts:ts + SUBLANES, :]
    conv_y = (proj(OFF_CB, D_MODEL) * conv).astype(BF16)
    gate_c = _sigmoid(proj(OFF_GC, D_MODEL))
    merged = gate_c * jnp.dot(conv_y, w_co_ref[...], preferred_element_type=F32)

    lane = lax.broadcasted_iota(jnp.int32, (BLOCK, LANES), 1)
    rowi = lax.broadcasted_iota(jnp.int32, (BLOCK, LANES), 0)
    from_cur = lane <= rowi
    lane_lo = lane < HEAD_DIM
    lane2 = lax.broadcasted_iota(jnp.int32, (2 * BLOCK, LANES), 1)
    ones_lo = jnp.where(lane2 < HEAD_DIM, 1.0, 0.0).astype(BF16)
    ones_hi = jnp.where(lane2 < HEAD_DIM, 0.0, 1.0).astype(BF16)
    nt_dims = (((1,), (1,)), ((), ()))

    units = [(j, kv) for j in range(n_blk) for kv in range(N_KV_HEADS)]

    def q_slices(kv):
        return [slice((2 * kv + pr) * LANES, (2 * kv + pr + 1) * LANES) for pr in range(2)]

    def scores(j, kv):
        r0 = j * BLOCK
        kv_sl = slice(kv * LANES, (kv + 1) * LANES)
        q2 = jnp.concatenate([q_scr[r0:r0 + BLOCK, sl] for sl in q_slices(kv)], axis=0)
        k_rhs = jnp.concatenate([klo_scr[r0:r0 + 2 * BLOCK, kv_sl],
                                 khi_scr[r0:r0 + 2 * BLOCK, kv_sl]], axis=0)
        return lax.dot_general(q2, k_rhs, nt_dims, preferred_element_type=F32)

    s_next = scores(*units[0])
    for n, (j, kv) in enumerate(units):
        r0 = j * BLOCK
        kv_sl = slice(kv * LANES, (kv + 1) * LANES)
        q_sl = q_slices(kv)
        s = s_next
        if n + 1 < len(units):
            s_next = scores(*units[n + 1])
        else:
            gate_a_halves = [_sigmoid(proj(OFF_GA, D_MODEL // 2))]
        v_rhs = jnp.concatenate(
            [jnp.concatenate([vlo_scr[r0:r0 + 2 * BLOCK, kv_sl], ones_lo], axis=1),
             jnp.concatenate([vhi_scr[r0:r0 + 2 * BLOCK, kv_sl], ones_hi], axis=1)], axis=0)
        p_rows, sink_terms = [], []
        for pr in range(2):
            rows = slice(pr * BLOCK, (pr + 1) * BLOCK)
            p_tiles, e_sink = [], []
            for half in range(2):
                hd = kv * GROUP + 2 * pr + half
                s_prev = s[rows, (2 * half) * BLOCK:(2 * half + 1) * BLOCK]
                s_cur = s[rows, (2 * half + 1) * BLOCK:(2 * half + 2) * BLOCK]
                if j == 0:
                    s_prev = jnp.where(s_idx > 0, s_prev, NEG_INF)
                t = jnp.where(from_cur, s_cur, s_prev)
                m = jnp.max(t, axis=-1, keepdims=True)
                p = jnp.exp2(t - m).astype(BF16)
                zero = jnp.zeros_like(p)
                p_tiles += [jnp.where(from_cur, zero, p), jnp.where(from_cur, p, zero)]
                e_sink.append(jnp.exp2(sinks_ref[hd] * LOG2E - m))
            p_rows.append(jnp.concatenate(p_tiles, axis=1))
            sink_terms.append(jnp.where(lane_lo, e_sink[0], e_sink[1]))
        o = jnp.dot(jnp.concatenate(p_rows, axis=0), v_rhs, preferred_element_type=F32)
        if n + 1 == len(units):
            gate_a_halves.append(_sigmoid(proj(OFF_GA + D_MODEL // 2, D_MODEL // 2)))
        for pr in range(2):
            rows = slice(pr * BLOCK, (pr + 1) * BLOCK)
            den = o[rows, LANES:2 * LANES] + sink_terms[pr]
            attn_scr[r0:r0 + BLOCK, q_sl[pr]] = (o[rows, 0:LANES] / den).astype(BF16)

    for scr in kv_scrs:
        scr[0:BLOCK, :] = scr[ts:ts + BLOCK, :]

    attn_out = jnp.dot(attn_scr[...], w_ao_ref[...], preferred_element_type=F32)
    merged = merged + jnp.concatenate(gate_a_halves, axis=1) * attn_out
    out_ref[...] = x + jnp.dot(merged.astype(BF16), w_o_ref[...], preferred_element_type=F32)


def _ffn_kernel(x_ref, g_ref, gf_ref, w_gu_ref, w_d_ref, out_ref, act_scr, *, final_norm):
    x = x_ref[...]
    h = _rms_norm(x, g_ref[...]).astype(BF16)
    for off, width in FF_CHUNKS:
        gate = jnp.dot(h, w_gu_ref[:, off:off + width], preferred_element_type=F32)
        up = jnp.dot(h, w_gu_ref[:, D_FF + off:D_FF + off + width], preferred_element_type=F32)
        act_scr[:, off:off + width] = (_silu(gate) * up).astype(BF16)
    y = x + jnp.dot(act_scr[...], w_d_ref[...], preferred_element_type=F32)
    if final_norm:
        y = _rms_norm(y, gf_ref[...])
    out_ref[...] = y


def _resident(shape):
    return pl.BlockSpec(shape, lambda b, s: (0,) * len(shape), pipeline_mode=pl.Buffered(1))


def _rope_tables(seq):
    half = ROT_DIM // 2
    d = np.arange(LANES) % HEAD_DIM
    inv_freq = np.float32(ROPE_THETA) ** (-(2 * (d % half)).astype(np.float32) / np.float32(ROT_DIM))
    ang = np.arange(seq, dtype=np.float32)[:, None] * inv_freq[None, :].astype(np.float32)
    cos, sin = np.cos(ang), np.sin(ang)
    c = np.where(d < ROT_DIM, cos, 1.0)
    s_next = np.where(d < half, -sin, 0.0)
    s_prev = np.where((d >= half) & (d < ROT_DIM), sin, 0.0)
    return jnp.asarray(np.stack([c, s_next, s_prev]).astype(np.float32))


def _mixer(x, g, rope, conv_w, sinks, layer, w_in, w_co, w_ao, w_o, w_gu, w_d):
    b, s, d = x.shape
    ts = SEQ_TILE
    n_s = s // ts
    gu_rows = w_gu.shape[0] // (b * n_s)
    d_rows = 2 * w_d.shape[0] // (b * n_s)
    assert gu_rows % BF16_SUBLANES == 0 and d_rows % BF16_SUBLANES == 0
    gu_slab = pl.BlockSpec((gu_rows, w_gu.shape[1]), lambda i, j: (i * n_s + j, 0))
    d_slab = pl.BlockSpec((d_rows, w_d.shape[1]), lambda i, j: ((i * n_s + j) // 2, 0))
    tile = pl.BlockSpec((None, ts, d), lambda i, j: (i, j, 0))
    kv_scratch = pltpu.VMEM((BLOCK + ts, N_KV_HEADS * LANES), BF16)
    return pl.pallas_call(
        functools.partial(_mixer_kernel, layer=layer),
        grid=(b, n_s),
        in_specs=[
            pl.BlockSpec(memory_space=pltpu.SMEM),
            tile,
            _resident((1, d)),
            pl.BlockSpec((3, ts, LANES), lambda i, j: (0, j, 0)),
            pl.BlockSpec(memory_space=pl.ANY),
            pl.BlockSpec(memory_space=pl.ANY),
            pl.BlockSpec(memory_space=pl.ANY),
            pl.BlockSpec(memory_space=pl.ANY),
            pl.BlockSpec(memory_space=pl.ANY),
            gu_slab,
            d_slab,
        ],
        out_specs=[tile, gu_slab, d_slab],
        out_shape=[jax.ShapeDtypeStruct(x.shape, x.dtype),
                   jax.ShapeDtypeStruct(w_gu.shape, BF16),
                   jax.ShapeDtypeStruct(w_d.shape, BF16)],
        scratch_shapes=[
            pltpu.VMEM(w_in.shape, BF16),
            pltpu.VMEM(w_co.shape, BF16),
            pltpu.VMEM(w_ao.shape, BF16),
            pltpu.VMEM(w_o.shape, BF16),
            pltpu.VMEM((STAGE_SLOTS, d, STAGE_COLS), F32),
            pltpu.SemaphoreType.DMA((STAGE_SLOTS,)),
            pltpu.VMEM((SUBLANES + ts, d), F32),
            pltpu.VMEM((ts, D_ATTN), BF16),
            kv_scratch, kv_scratch, kv_scratch, kv_scratch,
            pltpu.VMEM((ts, D_ATTN), BF16),
            pltpu.VMEM((CONV_K, d), F32),
            pltpu.SemaphoreType.DMA((1,)),
        ],
        compiler_params=pltpu.CompilerParams(
            dimension_semantics=("arbitrary", "arbitrary"),
            vmem_limit_bytes=VMEM_LIMIT_BYTES),
        name="mixer",
    )(sinks, x, g, rope, conv_w, w_in, w_co, w_ao, w_o, w_gu, w_d)


def _ffn(x, g, g_final, w_gu, w_d, final_norm):
    b, s, d = x.shape
    ts = FFN_SEQ_TILE
    tile = pl.BlockSpec((None, ts, d), lambda i, j: (i, j, 0))
    return pl.pallas_call(
        functools.partial(_ffn_kernel, final_norm=final_norm),
        grid=(b, s // ts),
        in_specs=[tile, _resident((1, d)), _resident((1, d)),
                  _resident(w_gu.shape), _resident(w_d.shape)],
        out_specs=tile,
        out_shape=jax.ShapeDtypeStruct(x.shape, x.dtype),
        scratch_shapes=[pltpu.VMEM((ts, D_FF), BF16)],
        compiler_params=pltpu.CompilerParams(
            dimension_semantics=("arbitrary", "arbitrary"),
            vmem_limit_bytes=VMEM_LIMIT_BYTES),
        name="ffn",
    )(x, g, g_final, w_gu, w_d)


def kernel(x, g_mix, w_in, conv_w, attn_sinks, w_conv_out, w_attn_out, w_o,
           g_ffn, w_gate_up, w_down, g_final):
    b, s, d = x.shape
    depth = w_in.shape[0]
    assert d == D_MODEL and s % SEQ_TILE == 0 and s % FFN_SEQ_TILE == 0
    assert w_in.shape[-1] == N_IN and w_gate_up.shape[-1] == 2 * D_FF
    assert N_IN % STAGE_COLS == 0 and d % STAGE_COLS == 0
    rope = _rope_tables(s)
    g_fin = g_final.reshape(1, d)
    for l in range(depth):
        x, w_gu, w_d = _mixer(
            x, g_mix[l].reshape(1, d), rope, conv_w, attn_sinks[l], l,
            w_in[l], w_conv_out[l], w_attn_out[l], w_o[l],
            w_gate_up[l], w_down[l])
        x = _ffn(x, g_ffn[l].reshape(1, d), g_fin, w_gu, w_d, final_norm=(l == depth - 1))
    return x
```

```python
import functools
import math

import jax
import jax.numpy as jnp
import numpy as np
from jax import lax
from jax.experimental import pallas as pl
from jax.experimental.pallas import tpu as pltpu

D_MODEL = 1024
CONV_K = 3
HEAD_DIM = 64
N_HEADS = 16
N_KV_HEADS = 4
GROUP = N_HEADS // N_KV_HEADS
D_ATTN = N_HEADS * HEAD_DIM
D_KV = N_KV_HEADS * HEAD_DIM
WINDOW = 128
BLOCK = 128
ROT_DIM = HEAD_DIM // 4
ROPE_THETA = 500000.0
ATTN_SCALE = 1.0 / math.sqrt(HEAD_DIM)
LOG2E = math.log2(math.e)
NEG_INF = -1e30
D_FF = 2816
EPS = 1e-5

OFF_CB = 0
OFF_CC = OFF_CB + D_MODEL
OFF_CX = OFF_CC + D_MODEL
OFF_Q = OFF_CX + D_MODEL
OFF_K = OFF_Q + D_ATTN
OFF_V = OFF_K + D_KV
OFF_GC = OFF_V + D_KV
OFF_GA = OFF_GC + D_MODEL
N_IN = OFF_GA + D_MODEL

LANES = 128
SUBLANES = 8
BF16_SUBLANES = 16
SEQ_TILE = 512
FFN_SEQ_TILE = 1024
STAGE_COLS = 512
STAGE_SLOTS = 4
FF_CHUNKS = ((0, 1536), (1536, 1280))
VMEM_LIMIT_BYTES = 58 * 1024 * 1024

F32 = jnp.float32
BF16 = jnp.bfloat16

assert WINDOW == BLOCK and 2 * HEAD_DIM == LANES and GROUP == 4


def _rms_norm(x, g):
    ms = jnp.mean(x * x, axis=-1, keepdims=True)
    return x * lax.rsqrt(ms + EPS) * g


def _sigmoid(x):
    return 0.5 * jnp.tanh(0.5 * x) + 0.5


def _silu(x):
    h = 0.5 * x
    return h * jnp.tanh(h) + h


def _rope(t, cos, sin_next, sin_prev):
    nxt = pltpu.roll(t, LANES - ROT_DIM // 2, 1)
    prv = pltpu.roll(t, ROT_DIM // 2, 1)
    return t * cos + nxt * sin_next + prv * sin_prev


def _split_head_pair(t, lane_lo):
    sw = pltpu.roll(t, HEAD_DIM, 1)
    zero = jnp.zeros_like(t)
    even = (jnp.where(lane_lo, t, zero), jnp.where(lane_lo, zero, sw))
    odd = (jnp.where(lane_lo, sw, zero), jnp.where(lane_lo, zero, t))
    return even, odd


def _stage_weights(pairs, stage_ref, sem_ref):
    slots = stage_ref.shape[0]
    jobs = [(src, dst, c * STAGE_COLS)
            for src, dst in pairs for c in range(src.shape[1] // STAGE_COLS)]

    def slab_copy(n):
        src, _, col = jobs[n]
        return pltpu.make_async_copy(
            src.at[:, pl.ds(col, STAGE_COLS)], stage_ref.at[n % slots], sem_ref.at[n % slots])

    for n in range(min(slots - 1, len(jobs))):
        slab_copy(n).start()
    for n, (_, dst, col) in enumerate(jobs):
        if n + slots - 1 < len(jobs):
            slab_copy(n + slots - 1).start()
        slab_copy(n).wait()
        dst[:, col:col + STAGE_COLS] = stage_ref[n % slots].astype(BF16)


def _mixer_kernel(sinks_ref, x_ref, g_ref, rope_ref, convw_ref, w_in_hbm,
                  w_co_hbm, w_ao_hbm, w_o_hbm, w_gu_f32_ref, w_d_f32_ref,
                  out_ref, w_gu_bf16_ref, w_d_bf16_ref,
                  w_in_ref, w_co_ref, w_ao_ref, w_o_ref, stage_ref, stage_sem,
                  u_scr, q_scr, klo_scr, khi_scr, vlo_scr, vhi_scr, attn_scr):
    ts = x_ref.shape[0]
    n_blk = ts // BLOCK
    s_idx = pl.program_id(1)
    kv_scrs = (klo_scr, khi_scr, vlo_scr, vhi_scr)

    @pl.when((pl.program_id(0) == 0) & (s_idx == 0))
    def _():
        _stage_weights(((w_in_hbm, w_in_ref), (w_co_hbm, w_co_ref),
                        (w_ao_hbm, w_ao_ref), (w_o_hbm, w_o_ref)), stage_ref, stage_sem)

    @pl.when(s_idx == 0)
    def _():
        u_scr[0:SUBLANES, :] = jnp.zeros((SUBLANES, D_MODEL), F32)
        for scr in kv_scrs:
            scr[0:BLOCK, :] = jnp.zeros((BLOCK, N_KV_HEADS * LANES), BF16)

    w_gu_bf16_ref[...] = w_gu_f32_ref[...].astype(BF16)
    w_d_bf16_ref[...] = w_d_f32_ref[...].astype(BF16)

    x = x_ref[...]
    h = _rms_norm(x, g_ref[...]).astype(BF16)

    def proj(off, width):
        return jnp.dot(h, w_in_ref[:, off:off + width], preferred_element_type=F32)

    u = proj(OFF_CC, D_MODEL) * proj(OFF_CX, D_MODEL)
    u_scr[SUBLANES:SUBLANES + ts, :] = u
    conv = (convw_ref[0:1, :] * u_scr[SUBLANES - 2:SUBLANES - 2 + ts, :]
            + convw_ref[1:2, :] * u_scr[SUBLANES - 1:SUBLANES - 1 + ts, :]
            + convw_ref[2:3, :] * u)
    u_scr[0:SUBLANES, :] = u_scr[ts:ts + SUBLANES, :]

    q_tabs = [rope_ref[i] * (ATTN_SCALE * LOG2E) for i in range(3)]
    q = proj(OFF_Q, D_ATTN)
    for t in range(D_ATTN // LANES):
        sl = slice(t * LANES, (t + 1) * LANES)
        q_scr[:, sl] = _rope(q[:, sl], *q_tabs).astype(BF16)
    lane_lo_t = lax.broadcasted_iota(jnp.int32, (ts, LANES), 1) < HEAD_DIM
    k = proj(OFF_K, D_KV)
    v = proj(OFF_V, D_KV)
    for t in range(D_KV // LANES):
        sl = slice(t * LANES, (t + 1) * LANES)
        k_t = _rope(k[:, sl], rope_ref[0], rope_ref[1], rope_ref[2])
        for src, lo_scr, hi_scr in ((k_t, klo_scr, khi_scr), (v[:, sl], vlo_scr, vhi_scr)):
            for hd, (lo, hi) in zip((2 * t, 2 * t + 1), _split_head_pair(src, lane_lo_t)):
                hsl = slice(hd * LANES, (hd + 1) * LANES)
                lo_scr[BLOCK:BLOCK + ts, hsl] = lo.astype(BF16)
                hi_scr[BLOCK:BLOCK + ts, hsl] = hi.astype(BF16)

    conv_y = (proj(OFF_CB, D_MODEL) * conv).astype(BF16)
    gate_c = _sigmoid(proj(OFF_GC, D_MODEL))
    merged = gate_c * jnp.dot(conv_y, w_co_ref[...], preferred_element_type=F32)

    lane = lax.broadcasted_iota(jnp.int32, (BLOCK, LANES), 1)
    rowi = lax.broadcasted_iota(jnp.int32, (BLOCK, LANES), 0)
    from_cur = lane <= rowi
    lane_lo = lane < HEAD_DIM
    lane2 = lax.broadcasted_iota(jnp.int32, (2 * BLOCK, LANES), 1)
    ones_lo = jnp.where(lane2 < HEAD_DIM, 1.0, 0.0).astype(BF16)
    ones_hi = jnp.where(lane2 < HEAD_DIM, 0.0, 1.0).astype(BF16)
    nt_dims = (((1,), (1,)), ((), ()))

    units = [(j, kv) for j in range(n_blk) for kv in range(N_KV_HEADS)]

    def q_slices(kv):
        return [slice((2 * kv + pr) * LANES, (2 * kv + pr + 1) * LANES) for pr in range(2)]

    def scores(j, kv):
        r0 = j * BLOCK
        kv_sl = slice(kv * LANES, (kv + 1) * LANES)
        q2 = jnp.concatenate([q_scr[r0:r0 + BLOCK, sl] for sl in q_slices(kv)], axis=0)
        k_rhs = jnp.concatenate([klo_scr[r0:r0 + 2 * BLOCK, kv_sl],
                                 khi_scr[r0:r0 + 2 * BLOCK, kv_sl]], axis=0)
        return lax.dot_general(q2, k_rhs, nt_dims, preferred_element_type=F32)

    s_next = scores(*units[0])
    for n, (j, kv) in enumerate(units):
        r0 = j * BLOCK
        kv_sl = slice(kv * LANES, (kv + 1) * LANES)
        q_sl = q_slices(kv)
        s = s_next
        if n + 1 < len(units):
            s_next = scores(*units[n + 1])
        else:
            gate_a_halves = [_sigmoid(proj(OFF_GA, D_MODEL // 2))]
        v_rhs = jnp.concatenate(
            [jnp.concatenate([vlo_scr[r0:r0 + 2 * BLOCK, kv_sl], ones_lo], axis=1),
             jnp.concatenate([vhi_scr[r0:r0 + 2 * BLOCK, kv_sl], ones_hi], axis=1)], axis=0)
        p_rows, sink_terms = [], []
        for pr in range(2):
            rows = slice(pr * BLOCK, (pr + 1) * BLOCK)
            p_tiles, e_sink = [], []
            for half in range(2):
                hd = kv * GROUP + 2 * pr + half
                s_prev = s[rows, (2 * half) * BLOCK:(2 * half + 1) * BLOCK]
                s_cur = s[rows, (2 * half + 1) * BLOCK:(2 * half + 2) * BLOCK]
                if j == 0:
                    s_prev = jnp.where(s_idx > 0, s_prev, NEG_INF)
                t = jnp.where(from_cur, s_cur, s_prev)
                m = jnp.max(t, axis=-1, keepdims=True)
                p = jnp.exp2(t - m).astype(BF16)
                zero = jnp.zeros_like(p)
                p_tiles += [jnp.where(from_cur, zero, p), jnp.where(from_cur, p, zero)]
                e_sink.append(jnp.exp2(sinks_ref[hd] * LOG2E - m))
            p_rows.append(jnp.concatenate(p_tiles, axis=1))
            sink_terms.append(jnp.where(lane_lo, e_sink[0], e_sink[1]))
        o = jnp.dot(jnp.concatenate(p_rows, axis=0), v_rhs, preferred_element_type=F32)
        if n + 1 == len(units):
            gate_a_halves.append(_sigmoid(proj(OFF_GA + D_MODEL // 2, D_MODEL // 2)))
        for pr in range(2):
            rows = slice(pr * BLOCK, (pr + 1) * BLOCK)
            den = o[rows, LANES:2 * LANES] + sink_terms[pr]
            attn_scr[r0:r0 + BLOCK, q_sl[pr]] = (o[rows, 0:LANES] / den).astype(BF16)

    for scr in kv_scrs:
        scr[0:BLOCK, :] = scr[ts:ts + BLOCK, :]

    attn_out = jnp.dot(attn_scr[...], w_ao_ref[...], preferred_element_type=F32)
    merged = merged + jnp.concatenate(gate_a_halves, axis=1) * attn_out
    out_ref[...] = x + jnp.dot(merged.astype(BF16), w_o_ref[...], preferred_element_type=F32)


def _ffn_kernel(x_ref, g_ref, gf_ref, w_gu_ref, w_d_ref, out_ref, act_scr, *, final_norm):
    x = x_ref[...]
    h = _rms_norm(x, g_ref[...]).astype(BF16)
    for off, width in FF_CHUNKS:
        gate = jnp.dot(h, w_gu_ref[:, off:off + width], preferred_element_type=F32)
        up = jnp.dot(h, w_gu_ref[:, D_FF + off:D_FF + off + width], preferred_element_type=F32)
        act_scr[:, off:off + width] = (_silu(gate) * up).astype(BF16)
    d = x.shape[1]
    col_halves = (slice(0, d // 2), slice(d // 2, d))
    ys = [x[:, c] + jnp.dot(act_scr[...], w_d_ref[:, c], preferred_element_type=F32)
          for c in col_halves]
    if final_norm:
        sq = sum(jnp.sum(y * y, axis=-1, keepdims=True) for y in ys)
        inv_rms = lax.rsqrt(sq / d + EPS)
        ys = [y * inv_rms * gf_ref[:, c] for y, c in zip(ys, col_halves)]
    for y, c in zip(ys, col_halves):
        out_ref[:, c] = y


def _resident(shape):
    return pl.BlockSpec(shape, lambda b, s: (0,) * len(shape), pipeline_mode=pl.Buffered(1))


def _rope_tables(seq):
    half = ROT_DIM // 2
    d = np.arange(LANES) % HEAD_DIM
    inv_freq = np.float32(ROPE_THETA) ** (-(2 * (d % half)).astype(np.float32) / np.float32(ROT_DIM))
    ang = np.arange(seq, dtype=np.float32)[:, None] * inv_freq[None, :].astype(np.float32)
    cos, sin = np.cos(ang), np.sin(ang)
    c = np.where(d < ROT_DIM, cos, 1.0)
    s_next = np.where(d < half, -sin, 0.0)
    s_prev = np.where((d >= half) & (d < ROT_DIM), sin, 0.0)
    return jnp.asarray(np.stack([c, s_next, s_prev]).astype(np.float32))


def _mixer(x, g, rope, conv_w, sinks, layer, w_in, w_co, w_ao, w_o, w_gu, w_d):
    b, s, d = x.shape
    ts = SEQ_TILE
    n_s = s // ts
    gu_rows = w_gu.shape[0] // (b * n_s)
    d_rows = 2 * w_d.shape[0] // (b * n_s)
    assert gu_rows % BF16_SUBLANES == 0 and d_rows % BF16_SUBLANES == 0
    gu_slab = pl.BlockSpec((gu_rows, w_gu.shape[1]), lambda i, j: (i * n_s + j, 0))
    d_slab = pl.BlockSpec((d_rows, w_d.shape[1]), lambda i, j: ((i * n_s + j) // 2, 0))
    tile = pl.BlockSpec((None, ts, d), lambda i, j: (i, j, 0))
    kv_scratch = pltpu.VMEM((BLOCK + ts, N_KV_HEADS * LANES), BF16)
    return pl.pallas_call(
        _mixer_kernel,
        grid=(b, n_s),
        in_specs=[
            pl.BlockSpec(memory_space=pltpu.SMEM),
            tile,
            _resident((1, d)),
            pl.BlockSpec((3, ts, LANES), lambda i, j: (0, j, 0)),
            pl.BlockSpec((None, CONV_K, d), lambda i, j: (layer, 0, 0),
                         pipeline_mode=pl.Buffered(1)),
            pl.BlockSpec(memory_space=pl.ANY),
            pl.BlockSpec(memory_space=pl.ANY),
            pl.BlockSpec(memory_space=pl.ANY),
            pl.BlockSpec(memory_space=pl.ANY),
            gu_slab,
            d_slab,
        ],
        out_specs=[tile, gu_slab, d_slab],
        out_shape=[jax.ShapeDtypeStruct(x.shape, x.dtype),
                   jax.ShapeDtypeStruct(w_gu.shape, BF16),
                   jax.ShapeDtypeStruct(w_d.shape, BF16)],
        scratch_shapes=[
            pltpu.VMEM(w_in.shape, BF16),
            pltpu.VMEM(w_co.shape, BF16),
            pltpu.VMEM(w_ao.shape, BF16),
            pltpu.VMEM(w_o.shape, BF16),
            pltpu.VMEM((STAGE_SLOTS, d, STAGE_COLS), F32),
            pltpu.SemaphoreType.DMA((STAGE_SLOTS,)),
            pltpu.VMEM((SUBLANES + ts, d), F32),
            pltpu.VMEM((ts, D_ATTN), BF16),
            kv_scratch, kv_scratch, kv_scratch, kv_scratch,
            pltpu.VMEM((ts, D_ATTN), BF16),
        ],
        compiler_params=pltpu.CompilerParams(
            dimension_semantics=("arbitrary", "arbitrary"),
            vmem_limit_bytes=VMEM_LIMIT_BYTES),
        name="mixer",
    )(sinks, x, g, rope, conv_w, w_in, w_co, w_ao, w_o, w_gu, w_d)


def _ffn(x, g, g_final, w_gu, w_d, final_norm):
    b, s, d = x.shape
    ts = FFN_SEQ_TILE
    tile = pl.BlockSpec((None, ts, d), lambda i, j: (i, j, 0))
    return pl.pallas_call(
        functools.partial(_ffn_kernel, final_norm=final_norm),
        grid=(b, s // ts),
        in_specs=[tile, _resident((1, d)), _resident((1, d)),
                  _resident(w_gu.shape), _resident(w_d.shape)],
        out_specs=tile,
        out_shape=jax.ShapeDtypeStruct(x.shape, x.dtype),
        scratch_shapes=[pltpu.VMEM((ts, D_FF), BF16)],
        compiler_params=pltpu.CompilerParams(
            dimension_semantics=("arbitrary", "arbitrary"),
            vmem_limit_bytes=VMEM_LIMIT_BYTES),
        name="ffn",
    )(x, g, g_final, w_gu, w_d)


def kernel(x, g_mix, w_in, conv_w, attn_sinks, w_conv_out, w_attn_out, w_o,
           g_ffn, w_gate_up, w_down, g_final):
    b, s, d = x.shape
    depth = w_in.shape[0]
    assert d == D_MODEL and s % SEQ_TILE == 0 and s % FFN_SEQ_TILE == 0
    assert w_in.shape[-1] == N_IN and w_gate_up.shape[-1] == 2 * D_FF
    assert N_IN % STAGE_COLS == 0 and d % STAGE_COLS == 0
    rope = _rope_tables(s)
    g_fin = g_final.reshape(1, d)
    for l in range(depth):
        x, w_gu, w_d = _mixer(
            x, g_mix[l].reshape(1, d), rope, conv_w, attn_sinks[l], l,
            w_in[l], w_conv_out[l], w_attn_out[l], w_o[l],
            w_gate_up[l], w_down[l])
        x = _ffn(x, g_ffn[l].reshape(1, d), g_fin, w_gu, w_d, final_norm=(l == depth - 1))
    return x
```

```python
import functools
import math

import jax
import jax.numpy as jnp
import numpy as np
from jax import lax
from jax.experimental import pallas as pl
from jax.experimental.pallas import tpu as pltpu

D_MODEL = 1024
CONV_K = 3
HEAD_DIM = 64
N_HEADS = 16
N_KV_HEADS = 4
GROUP = N_HEADS // N_KV_HEADS
D_ATTN = N_HEADS * HEAD_DIM
D_KV = N_KV_HEADS * HEAD_DIM
WINDOW = 128
BLOCK = 128
ROT_DIM = HEAD_DIM // 4
ROPE_THETA = 500000.0
ATTN_SCALE = 1.0 / math.sqrt(HEAD_DIM)
LOG2E = math.log2(math.e)
NEG_INF = -1e30
D_FF = 2816
EPS = 1e-5

OFF_CB = 0
OFF_CC = OFF_CB + D_MODEL
OFF_CX = OFF_CC + D_MODEL
OFF_Q = OFF_CX + D_MODEL
OFF_K = OFF_Q + D_ATTN
OFF_V = OFF_K + D_KV
OFF_GC = OFF_V + D_KV
OFF_GA = OFF_GC + D_MODEL
N_IN = OFF_GA + D_MODEL

LANES = 128
SUBLANES = 8
BF16_SUBLANES = 16
SEQ_TILE = 512
FFN_SEQ_TILE = 1024
STAGE_COLS = 512
STAGE_SLOTS = 4
FF_CHUNKS = ((0, 1024), (1024, 1024), (2048, 768))
VMEM_LIMIT_BYTES = 58 * 1024 * 1024

F32 = jnp.float32
BF16 = jnp.bfloat16

assert WINDOW == BLOCK and 2 * HEAD_DIM == LANES and GROUP == 4


def _rms_norm(x, g):
    ms = jnp.mean(x * x, axis=-1, keepdims=True)
    return x * lax.rsqrt(ms + EPS) * g


def _sigmoid(x):
    return 0.5 * jnp.tanh(0.5 * x) + 0.5


def _silu(x):
    h = 0.5 * x
    return h * jnp.tanh(h) + h


def _rope(t, cos, sin_next, sin_prev):
    nxt = pltpu.roll(t, LANES - ROT_DIM // 2, 1)
    prv = pltpu.roll(t, ROT_DIM // 2, 1)
    return t * cos + nxt * sin_next + prv * sin_prev


def _split_head_pair(t, lane_lo):
    sw = pltpu.roll(t, HEAD_DIM, 1)
    zero = jnp.zeros_like(t)
    even = (jnp.where(lane_lo, t, zero), jnp.where(lane_lo, zero, sw))
    odd = (jnp.where(lane_lo, sw, zero), jnp.where(lane_lo, zero, t))
    return even, odd


def _stage_weights(pairs, stage_ref, sem_ref):
    slots = stage_ref.shape[0]
    jobs = [(src, dst, c * STAGE_COLS)
            for src, dst in pairs for c in range(src.shape[1] // STAGE_COLS)]

    def slab_copy(n):
        src, _, col = jobs[n]
        return pltpu.make_async_copy(
            src.at[:, pl.ds(col, STAGE_COLS)], stage_ref.at[n % slots], sem_ref.at[n % slots])

    for n in range(min(slots - 1, len(jobs))):
        slab_copy(n).start()
    for n, (_, dst, col) in enumerate(jobs):
        if n + slots - 1 < len(jobs):
            slab_copy(n + slots - 1).start()
        slab_copy(n).wait()
        dst[:, col:col + STAGE_COLS] = stage_ref[n % slots].astype(BF16)


def _mixer_kernel(sinks_ref, x_ref, g_ref, rope_ref, convw_ref, w_in_hbm,
                  w_co_hbm, w_ao_hbm, w_o_hbm, w_gu_f32_ref, w_d_f32_ref,
                  out_ref, w_gu_bf16_ref, w_d_bf16_ref,
                  w_in_ref, w_co_ref, w_ao_ref, w_o_ref, stage_ref, stage_sem,
                  u_scr, q_scr, klo_scr, khi_scr, vlo_scr, vhi_scr, attn_scr):
    ts = x_ref.shape[0]
    n_blk = ts // BLOCK
    s_idx = pl.program_id(1)
    kv_scrs = (klo_scr, khi_scr, vlo_scr, vhi_scr)

    @pl.when((pl.program_id(0) == 0) & (s_idx == 0))
    def _():
        _stage_weights(((w_in_hbm, w_in_ref), (w_co_hbm, w_co_ref),
                        (w_ao_hbm, w_ao_ref), (w_o_hbm, w_o_ref)), stage_ref, stage_sem)

    @pl.when(s_idx == 0)
    def _():
        u_scr[0:SUBLANES, :] = jnp.zeros((SUBLANES, D_MODEL), F32)
        for scr in kv_scrs:
            scr[0:BLOCK, :] = jnp.zeros((BLOCK, N_KV_HEADS * LANES), BF16)

    w_gu_bf16_ref[...] = w_gu_f32_ref[...].astype(BF16)
    w_d_bf16_ref[...] = w_d_f32_ref[...].astype(BF16)

    x = x_ref[...]
    h = _rms_norm(x, g_ref[...]).astype(BF16)

    def proj(off, width):
        return jnp.dot(h, w_in_ref[:, off:off + width], preferred_element_type=F32)

    q_tabs = [rope_ref[i] * (ATTN_SCALE * LOG2E) for i in range(3)]
    q = proj(OFF_Q, D_ATTN)
    for t in range(D_ATTN // LANES):
        sl = slice(t * LANES, (t + 1) * LANES)
        q_scr[:, sl] = _rope(q[:, sl], *q_tabs).astype(BF16)
    lane_lo_t = lax.broadcasted_iota(jnp.int32, (ts, LANES), 1) < HEAD_DIM
    k = proj(OFF_K, D_KV)
    v = proj(OFF_V, D_KV)
    for t in range(D_KV // LANES):
        sl = slice(t * LANES, (t + 1) * LANES)
        k_t = _rope(k[:, sl], rope_ref[0], rope_ref[1], rope_ref[2])
        for src, lo_scr, hi_scr in ((k_t, klo_scr, khi_scr), (v[:, sl], vlo_scr, vhi_scr)):
            for hd, (lo, hi) in zip((2 * t, 2 * t + 1), _split_head_pair(src, lane_lo_t)):
                hsl = slice(hd * LANES, (hd + 1) * LANES)
                lo_scr[BLOCK:BLOCK + ts, hsl] = lo.astype(BF16)
                hi_scr[BLOCK:BLOCK + ts, hsl] = hi.astype(BF16)

    u = proj(OFF_CC, D_MODEL) * proj(OFF_CX, D_MODEL)
    u_scr[SUBLANES:SUBLANES + ts, :] = u
    conv = (convw_ref[0:1, :] * u_scr[SUBLANES - 2:SUBLANES - 2 + ts, :]
            + convw_ref[1:2, :] * u_scr[SUBLANES - 1:SUBLANES - 1 + ts, :]
            + convw_ref[2:3, :] * u)
    u_scr[0:SUBLANES, :] = u_scr[ts:ts + SUBLANES, :]
    conv_y = (proj(OFF_CB, D_MODEL) * conv).astype(BF16)
    gate_c = _sigmoid(proj(OFF_GC, D_MODEL))
    merged = gate_c * jnp.dot(conv_y, w_co_ref[...], preferred_element_type=F32)

    lane = lax.broadcasted_iota(jnp.int32, (BLOCK, LANES), 1)
    rowi = lax.broadcasted_iota(jnp.int32, (BLOCK, LANES), 0)
    from_cur = lane <= rowi
    lane_lo = lane < HEAD_DIM
    lane2 = lax.broadcasted_iota(jnp.int32, (2 * BLOCK, LANES), 1)
    ones_lo = jnp.where(lane2 < HEAD_DIM, 1.0, 0.0).astype(BF16)
    ones_hi = jnp.where(lane2 < HEAD_DIM, 0.0, 1.0).astype(BF16)
    nt_dims = (((1,), (1,)), ((), ()))

    units = [(j, kv) for j in range(n_blk) for kv in range(N_KV_HEADS)]

    def q_slices(kv):
        return [slice((2 * kv + pr) * LANES, (2 * kv + pr + 1) * LANES) for pr in range(2)]

    def scores(j, kv):
        r0 = j * BLOCK
        kv_sl = slice(kv * LANES, (kv + 1) * LANES)
        q2 = jnp.concatenate([q_scr[r0:r0 + BLOCK, sl] for sl in q_slices(kv)], axis=0)
        k_rhs = jnp.concatenate([klo_scr[r0:r0 + 2 * BLOCK, kv_sl],
                                 khi_scr[r0:r0 + 2 * BLOCK, kv_sl]], axis=0)
        return lax.dot_general(q2, k_rhs, nt_dims, preferred_element_type=F32)

    s_next = scores(*units[0])
    for n, (j, kv) in enumerate(units):
        r0 = j * BLOCK
        kv_sl = slice(kv * LANES, (kv + 1) * LANES)
        q_sl = q_slices(kv)
        s = s_next
        if n + 1 < len(units):
            s_next = scores(*units[n + 1])
        else:
            gate_a_halves = [_sigmoid(proj(OFF_GA, D_MODEL // 2))]
        v_rhs = jnp.concatenate(
            [jnp.concatenate([vlo_scr[r0:r0 + 2 * BLOCK, kv_sl], ones_lo], axis=1),
             jnp.concatenate([vhi_scr[r0:r0 + 2 * BLOCK, kv_sl], ones_hi], axis=1)], axis=0)
        p_rows, sink_terms = [], []
        for pr in range(2):
            rows = slice(pr * BLOCK, (pr + 1) * BLOCK)
            p_tiles, e_sink = [], []
            for half in range(2):
                hd = kv * GROUP + 2 * pr + half
                s_prev = s[rows, (2 * half) * BLOCK:(2 * half + 1) * BLOCK]
                s_cur = s[rows, (2 * half + 1) * BLOCK:(2 * half + 2) * BLOCK]
                if j == 0:
                    s_prev = jnp.where(s_idx > 0, s_prev, NEG_INF)
                t = jnp.where(from_cur, s_cur, s_prev)
                m = jnp.max(t, axis=-1, keepdims=True)
                p = jnp.exp2(t - m).astype(BF16)
                zero = jnp.zeros_like(p)
                p_tiles += [jnp.where(from_cur, zero, p), jnp.where(from_cur, p, zero)]
                e_sink.append(jnp.exp2(sinks_ref[hd] * LOG2E - m))
            p_rows.append(jnp.concatenate(p_tiles, axis=1))
            sink_terms.append(jnp.where(lane_lo, e_sink[0], e_sink[1]))
        o = jnp.dot(jnp.concatenate(p_rows, axis=0), v_rhs, preferred_element_type=F32)
        if n + 1 == len(units):
            gate_a_halves.append(_sigmoid(proj(OFF_GA + D_MODEL // 2, D_MODEL // 2)))
        for pr in range(2):
            rows = slice(pr * BLOCK, (pr + 1) * BLOCK)
            den = o[rows, LANES:2 * LANES] + sink_terms[pr]
            attn_scr[r0:r0 + BLOCK, q_sl[pr]] = (o[rows, 0:LANES] / den).astype(BF16)

    for scr in kv_scrs:
        scr[0:BLOCK, :] = scr[ts:ts + BLOCK, :]

    attn_out = jnp.dot(attn_scr[...], w_ao_ref[...], preferred_element_type=F32)
    merged = merged + jnp.concatenate(gate_a_halves, axis=1) * attn_out
    out_ref[...] = x + jnp.dot(merged.astype(BF16), w_o_ref[...], preferred_element_type=F32)


def _ffn_kernel(x_ref, g_ref, gf_ref, w_gu_ref, w_d_ref, out_ref, act_scr, *, final_norm):
    x = x_ref[...]
    h = _rms_norm(x, g_ref[...]).astype(BF16)
    y = x
    for n, (off, width) in enumerate(FF_CHUNKS):
        gate = jnp.dot(h, w_gu_ref[:, off:off + width], preferred_element_type=F32)
        up = jnp.dot(h, w_gu_ref[:, D_FF + off:D_FF + off + width], preferred_element_type=F32)
        if n == len(FF_CHUNKS) - 1 and off:
            y = y + jnp.dot(act_scr[:, :off], w_d_ref[:off, :], preferred_element_type=F32)
        act_scr[:, off:off + width] = (_silu(gate) * up).astype(BF16)
    last = FF_CHUNKS[-1][0]
    y = y + jnp.dot(act_scr[:, last:], w_d_ref[last:, :], preferred_element_type=F32)
    if final_norm:
        y = _rms_norm(y, gf_ref[...])
    out_ref[...] = y


def _resident(shape):
    return pl.BlockSpec(shape, lambda b, s: (0,) * len(shape), pipeline_mode=pl.Buffered(1))


def _rope_tables(seq):
    half = ROT_DIM // 2
    d = np.arange(LANES) % HEAD_DIM
    inv_freq = np.float32(ROPE_THETA) ** (-(2 * (d % half)).astype(np.float32) / np.float32(ROT_DIM))
    ang = np.arange(seq, dtype=np.float32)[:, None] * inv_freq[None, :].astype(np.float32)
    cos, sin = np.cos(ang), np.sin(ang)
    c = np.where(d < ROT_DIM, cos, 1.0)
    s_next = np.where(d < half, -sin, 0.0)
    s_prev = np.where((d >= half) & (d < ROT_DIM), sin, 0.0)
    return jnp.asarray(np.stack([c, s_next, s_prev]).astype(np.float32))


def _mixer(x, g, rope, conv_w, sinks, layer, w_in, w_co, w_ao, w_o, w_gu, w_d):
    b, s, d = x.shape
    ts = SEQ_TILE
    n_s = s // ts
    gu_rows = w_gu.shape[0] // (b * n_s)
    d_rows = 2 * w_d.shape[0] // (b * n_s)
    assert gu_rows % BF16_SUBLANES == 0 and d_rows % BF16_SUBLANES == 0
    gu_slab = pl.BlockSpec((gu_rows, w_gu.shape[1]), lambda i, j: (i * n_s + j, 0))
    d_slab = pl.BlockSpec((d_rows, w_d.shape[1]), lambda i, j: ((i * n_s + j) // 2, 0))
    tile = pl.BlockSpec((None, ts, d), lambda i, j: (i, j, 0))
    kv_scratch = pltpu.VMEM((BLOCK + ts, N_KV_HEADS * LANES), BF16)
    return pl.pallas_call(
        _mixer_kernel,
        grid=(b, n_s),
        in_specs=[
            pl.BlockSpec(memory_space=pltpu.SMEM),
            tile,
            _resident((1, d)),
            pl.BlockSpec((3, ts, LANES), lambda i, j: (0, j, 0)),
            pl.BlockSpec((None, CONV_K, d), lambda i, j: (layer, 0, 0),
                         pipeline_mode=pl.Buffered(1)),
            pl.BlockSpec(memory_space=pl.ANY),
            pl.BlockSpec(memory_space=pl.ANY),
            pl.BlockSpec(memory_space=pl.ANY),
            pl.BlockSpec(memory_space=pl.ANY),
            gu_slab,
            d_slab,
        ],
        out_specs=[tile, gu_slab, d_slab],
        out_shape=[jax.ShapeDtypeStruct(x.shape, x.dtype),
                   jax.ShapeDtypeStruct(w_gu.shape, BF16),
                   jax.ShapeDtypeStruct(w_d.shape, BF16)],
        scratch_shapes=[
            pltpu.VMEM(w_in.shape, BF16),
            pltpu.VMEM(w_co.shape, BF16),
            pltpu.VMEM(w_ao.shape, BF16),
            pltpu.VMEM(w_o.shape, BF16),
            pltpu.VMEM((STAGE_SLOTS, d, STAGE_COLS), F32),
            pltpu.SemaphoreType.DMA((STAGE_SLOTS,)),
            pltpu.VMEM((SUBLANES + ts, d), F32),
            pltpu.VMEM((ts, D_ATTN), BF16),
            kv_scratch, kv_scratch, kv_scratch, kv_scratch,
            pltpu.VMEM((ts, D_ATTN), BF16),
        ],
        compiler_params=pltpu.CompilerParams(
            dimension_semantics=("arbitrary", "arbitrary"),
            vmem_limit_bytes=VMEM_LIMIT_BYTES),
        name="mixer",
    )(sinks, x, g, rope, conv_w, w_in, w_co, w_ao, w_o, w_gu, w_d)


def _ffn(x, g, g_final, w_gu, w_d, final_norm):
    b, s, d = x.shape
    ts = FFN_SEQ_TILE
    tile = pl.BlockSpec((None, ts, d), lambda i, j: (i, j, 0))
    return pl.pallas_call(
        functools.partial(_ffn_kernel, final_norm=final_norm),
        grid=(b, s // ts),
        in_specs=[tile, _resident((1, d)), _resident((1, d)),
                  _resident(w_gu.shape), _resident(w_d.shape)],
        out_specs=tile,
        out_shape=jax.ShapeDtypeStruct(x.shape, x.dtype),
        scratch_shapes=[pltpu.VMEM((ts, D_FF), BF16)],
        compiler_params=pltpu.CompilerParams(
            dimension_semantics=("arbitrary", "arbitrary"),
            vmem_limit_bytes=VMEM_LIMIT_BYTES),
        name="ffn",
    )(x, g, g_final, w_gu, w_d)


def kernel(x, g_mix, w_in, conv_w, attn_sinks, w_conv_out, w_attn_out, w_o,
           g_ffn, w_gate_up, w_down, g_final):
    b, s, d = x.shape
    depth = w_in.shape[0]
    assert d == D_MODEL and s % SEQ_TILE == 0 and s % FFN_SEQ_TILE == 0
    assert w_in.shape[-1] == N_IN and w_gate_up.shape[-1] == 2 * D_FF
    assert N_IN % STAGE_COLS == 0 and d % STAGE_COLS == 0
    rope = _rope_tables(s)
    g_fin = g_final.reshape(1, d)
    for l in range(depth):
        x, w_gu, w_d = _mixer(
            x, g_mix[l].reshape(1, d), rope, conv_w, attn_sinks[l], l,
            w_in[l], w_conv_out[l], w_attn_out[l], w_o[l],
            w_gate_up[l], w_down[l])
        x = _ffn(x, g_ffn[l].reshape(1, d), g_fin, w_gu, w_d, final_norm=(l == depth - 1))
    return x
```

```python
import functools
import math

import jax
import jax.numpy as jnp
import numpy as np
from jax import lax
from jax.experimental import pallas as pl
from jax.experimental.pallas import tpu as pltpu

D_MODEL = 1024
CONV_K = 3
HEAD_DIM = 64
N_HEADS = 16
N_KV_HEADS = 4
GROUP = N_HEADS // N_KV_HEADS
D_ATTN = N_HEADS * HEAD_DIM
D_KV = N_KV_HEADS * HEAD_DIM
WINDOW = 128
BLOCK = 128
ROT_DIM = HEAD_DIM // 4
ROPE_THETA = 500000.0
ATTN_SCALE = 1.0 / math.sqrt(HEAD_DIM)
LOG2E = math.log2(math.e)
NEG_INF = -1e30
D_FF = 2816
EPS = 1e-5

OFF_CB = 0
OFF_CC = OFF_CB + D_MODEL
OFF_CX = OFF_CC + D_MODEL
OFF_Q = OFF_CX + D_MODEL
OFF_K = OFF_Q + D_ATTN
OFF_V = OFF_K + D_KV
OFF_GC = OFF_V + D_KV
OFF_GA = OFF_GC + D_MODEL
N_IN = OFF_GA + D_MODEL

LANES = 128
SUBLANES = 8
BF16_SUBLANES = 16
SEQ_TILE = 512
FFN_SEQ_TILE = 1024
STAGE_COLS = 512
STAGE_SLOTS = 4
FF_CHUNKS = ((0, 1024), (1024, 1024), (2048, 768))
VMEM_LIMIT_BYTES = 58 * 1024 * 1024

F32 = jnp.float32
BF16 = jnp.bfloat16

assert WINDOW == BLOCK and 2 * HEAD_DIM == LANES and GROUP == 4


def _rms_norm(x, g):
    ms = jnp.mean(x * x, axis=-1, keepdims=True)
    return x * lax.rsqrt(ms + EPS) * g


def _sigmoid(x):
    return 0.5 * jnp.tanh(0.5 * x) + 0.5


def _silu(x):
    h = 0.5 * x
    return h * jnp.tanh(h) + h


def _rope(t, cos, sin_next, sin_prev):
    nxt = pltpu.roll(t, LANES - ROT_DIM // 2, 1)
    prv = pltpu.roll(t, ROT_DIM // 2, 1)
    return t * cos + nxt * sin_next + prv * sin_prev


def _split_head_pair(t, lane_lo):
    sw = pltpu.roll(t, HEAD_DIM, 1)
    zero = jnp.zeros_like(t)
    even = (jnp.where(lane_lo, t, zero), jnp.where(lane_lo, zero, sw))
    odd = (jnp.where(lane_lo, sw, zero), jnp.where(lane_lo, zero, t))
    return even, odd


def _stage_weights(pairs, stage_ref, sem_ref):
    slots = stage_ref.shape[0]
    jobs = [(src, dst, c * STAGE_COLS)
            for src, dst in pairs for c in range(src.shape[1] // STAGE_COLS)]

    def slab_copy(n):
        src, _, col = jobs[n]
        return pltpu.make_async_copy(
            src.at[:, pl.ds(col, STAGE_COLS)], stage_ref.at[n % slots], sem_ref.at[n % slots])

    for n in range(min(slots - 1, len(jobs))):
        slab_copy(n).start()
    for n, (_, dst, col) in enumerate(jobs):
        if n + slots - 1 < len(jobs):
            slab_copy(n + slots - 1).start()
        slab_copy(n).wait()
        dst[:, col:col + STAGE_COLS] = stage_ref[n % slots].astype(BF16)


def _mixer_kernel(sinks_ref, x_ref, g_ref, rope_ref, convw_ref, w_in_hbm,
                  w_co_hbm, w_ao_hbm, w_o_hbm, w_gu_f32_ref, w_d_f32_ref,
                  out_ref, w_gu_bf16_ref, w_d_bf16_ref,
                  w_in_ref, w_co_ref, w_ao_ref, w_o_ref, stage_ref, stage_sem,
                  u_scr, q_scr, klo_scr, khi_scr, vlo_scr, vhi_scr, attn_scr):
    ts = x_ref.shape[0]
    n_blk = ts // BLOCK
    s_idx = pl.program_id(1)
    kv_scrs = (klo_scr, khi_scr, vlo_scr, vhi_scr)

    @pl.when((pl.program_id(0) == 0) & (s_idx == 0))
    def _():
        _stage_weights(((w_in_hbm, w_in_ref), (w_co_hbm, w_co_ref),
                        (w_ao_hbm, w_ao_ref), (w_o_hbm, w_o_ref)), stage_ref, stage_sem)

    @pl.when(s_idx == 0)
    def _():
        u_scr[0:SUBLANES, :] = jnp.zeros((SUBLANES, D_MODEL), F32)
        for scr in kv_scrs:
            scr[0:BLOCK, :] = jnp.zeros((BLOCK, N_KV_HEADS * LANES), BF16)

    x = x_ref[...]
    h = _rms_norm(x, g_ref[...]).astype(BF16)

    def proj(off, width):
        return jnp.dot(h, w_in_ref[:, off:off + width], preferred_element_type=F32)

    u = proj(OFF_CC, D_MODEL) * proj(OFF_CX, D_MODEL)
    u_scr[SUBLANES:SUBLANES + ts, :] = u
    conv = (convw_ref[0:1, :] * u_scr[SUBLANES - 2:SUBLANES - 2 + ts, :]
            + convw_ref[1:2, :] * u_scr[SUBLANES - 1:SUBLANES - 1 + ts, :]
            + convw_ref[2:3, :] * u)
    u_scr[0:SUBLANES, :] = u_scr[ts:ts + SUBLANES, :]

    q_tabs = [rope_ref[i] * (ATTN_SCALE * LOG2E) for i in range(3)]
    q = proj(OFF_Q, D_ATTN)
    for t in range(D_ATTN // LANES):
        sl = slice(t * LANES, (t + 1) * LANES)
        q_scr[:, sl] = _rope(q[:, sl], *q_tabs).astype(BF16)
    lane_lo_t = lax.broadcasted_iota(jnp.int32, (ts, LANES), 1) < HEAD_DIM
    k = proj(OFF_K, D_KV)
    v = proj(OFF_V, D_KV)
    for t in range(D_KV // LANES):
        sl = slice(t * LANES, (t + 1) * LANES)
        k_t = _rope(k[:, sl], rope_ref[0], rope_ref[1], rope_ref[2])
        for src, lo_scr, hi_scr in ((k_t, klo_scr, khi_scr), (v[:, sl], vlo_scr, vhi_scr)):
            for hd, (lo, hi) in zip((2 * t, 2 * t + 1), _split_head_pair(src, lane_lo_t)):
                hsl = slice(hd * LANES, (hd + 1) * LANES)
                lo_scr[BLOCK:BLOCK + ts, hsl] = lo.astype(BF16)
                hi_scr[BLOCK:BLOCK + ts, hsl] = hi.astype(BF16)

    conv_y = (proj(OFF_CB, D_MODEL) * conv).astype(BF16)
    gate_c = _sigmoid(proj(OFF_GC, D_MODEL))
    merged = gate_c * jnp.dot(conv_y, w_co_ref[...], preferred_element_type=F32)

    lane = lax.broadcasted_iota(jnp.int32, (BLOCK, LANES), 1)
    rowi = lax.broadcasted_iota(jnp.int32, (BLOCK, LANES), 0)
    from_cur = lane <= rowi
    lane_lo = lane < HEAD_DIM
    lane2 = lax.broadcasted_iota(jnp.int32, (2 * BLOCK, LANES), 1)
    ones_lo = jnp.where(lane2 < HEAD_DIM, 1.0, 0.0).astype(BF16)
    ones_hi = jnp.where(lane2 < HEAD_DIM, 0.0, 1.0).astype(BF16)
    nt_dims = (((1,), (1,)), ((), ()))

    units = [(j, kv) for j in range(n_blk) for kv in range(N_KV_HEADS)]

    def q_slices(kv):
        return [slice((2 * kv + pr) * LANES, (2 * kv + pr + 1) * LANES) for pr in range(2)]

    def scores(j, kv):
        r0 = j * BLOCK
        kv_sl = slice(kv * LANES, (kv + 1) * LANES)
        q2 = jnp.concatenate([q_scr[r0:r0 + BLOCK, sl] for sl in q_slices(kv)], axis=0)
        k_rhs = jnp.concatenate([klo_scr[r0:r0 + 2 * BLOCK, kv_sl],
                                 khi_scr[r0:r0 + 2 * BLOCK, kv_sl]], axis=0)
        return lax.dot_general(q2, k_rhs, nt_dims, preferred_element_type=F32)

    s_next = scores(*units[0])
    for n, (j, kv) in enumerate(units):
        r0 = j * BLOCK
        kv_sl = slice(kv * LANES, (kv + 1) * LANES)
        q_sl = q_slices(kv)
        s = s_next
        if n + 1 < len(units):
            s_next = scores(*units[n + 1])
        else:
            gate_a_halves = [_sigmoid(proj(OFF_GA, D_MODEL // 2))]
        v_rhs = jnp.concatenate(
            [jnp.concatenate([vlo_scr[r0:r0 + 2 * BLOCK, kv_sl], ones_lo], axis=1),
             jnp.concatenate([vhi_scr[r0:r0 + 2 * BLOCK, kv_sl], ones_hi], axis=1)], axis=0)
        p_rows, sink_terms = [], []
        for pr in range(2):
            rows = slice(pr * BLOCK, (pr + 1) * BLOCK)
            p_tiles, e_sink = [], []
            for half in range(2):
                hd = kv * GROUP + 2 * pr + half
                s_prev = s[rows, (2 * half) * BLOCK:(2 * half + 1) * BLOCK]
                s_cur = s[rows, (2 * half + 1) * BLOCK:(2 * half + 2) * BLOCK]
                if j == 0:
                    s_prev = jnp.where(s_idx > 0, s_prev, NEG_INF)
                t = jnp.where(from_cur, s_cur, s_prev)
                m = jnp.max(t, axis=-1, keepdims=True)
                p = jnp.exp2(t - m).astype(BF16)
                zero = jnp.zeros_like(p)
                p_tiles += [jnp.where(from_cur, zero, p), jnp.where(from_cur, p, zero)]
                e_sink.append(jnp.exp2(sinks_ref[hd] * LOG2E - m))
            p_rows.append(jnp.concatenate(p_tiles, axis=1))
            sink_terms.append(jnp.where(lane_lo, e_sink[0], e_sink[1]))
        o = jnp.dot(jnp.concatenate(p_rows, axis=0), v_rhs, preferred_element_type=F32)
        if n + 1 == len(units):
            gate_a_halves.append(_sigmoid(proj(OFF_GA + D_MODEL // 2, D_MODEL // 2)))
        for pr in range(2):
            rows = slice(pr * BLOCK, (pr + 1) * BLOCK)
            den = o[rows, LANES:2 * LANES] + sink_terms[pr]
            attn_scr[r0:r0 + BLOCK, q_sl[pr]] = (o[rows, 0:LANES] / den).astype(BF16)

    for scr in kv_scrs:
        scr[0:BLOCK, :] = scr[ts:ts + BLOCK, :]

    w_gu_bf16_ref[...] = w_gu_f32_ref[...].astype(BF16)
    w_d_bf16_ref[...] = w_d_f32_ref[...].astype(BF16)

    attn_out = jnp.dot(attn_scr[...], w_ao_ref[...], preferred_element_type=F32)
    merged = merged +jnp.concatenate(gate_a_halves, axis=1) * attn_out
    out_ref[...] = x + jnp.dot(merged.astype(BF16), w_o_ref[...], preferred_element_type=F32)


def _ffn_kernel(x_ref, g_ref, gf_ref, w_gu_ref, w_d_ref, out_ref, act_scr, *, final_norm):
    x = x_ref[...]
    h = _rms_norm(x, g_ref[...]).astype(BF16)
    for off, width in FF_CHUNKS:
        gate = jnp.dot(h, w_gu_ref[:, off:off + width], preferred_element_type=F32)
        up = jnp.dot(h, w_gu_ref[:, D_FF + off:D_FF + off + width], preferred_element_type=F32)
        act_scr[:, off:off + width] = (_silu(gate) * up).astype(BF16)
    y = x + jnp.dot(act_scr[...], w_d_ref[...], preferred_element_type=F32)
    if final_norm:
        y = _rms_norm(y, gf_ref[...])
    out_ref[...] = y


def _resident(shape):
    return pl.BlockSpec(shape, lambda b, s: (0,) * len(shape), pipeline_mode=pl.Buffered(1))


def _rope_tables(seq):
    half = ROT_DIM // 2
    d = np.arange(LANES) % HEAD_DIM
    inv_freq = np.float32(ROPE_THETA) ** (-(2 * (d % half)).astype(np.float32) / np.float32(ROT_DIM))
    ang = np.arange(seq, dtype=np.float32)[:, None] * inv_freq[None, :].astype(np.float32)
    cos, sin = np.cos(ang), np.sin(ang)
    c = np.where(d < ROT_DIM, cos, 1.0)
    s_next = np.where(d < half, -sin, 0.0)
    s_prev = np.where((d >= half) & (d < ROT_DIM), sin, 0.0)
    return jnp.asarray(np.stack([c, s_next, s_prev]).astype(np.float32))


def _mixer(x, g, rope, conv_w, sinks, layer, w_in, w_co, w_ao, w_o, w_gu, w_d):
    b, s, d = x.shape
    ts = SEQ_TILE
    n_s = s // ts
    gu_rows = w_gu.shape[0] // (b * n_s)
    d_rows = 2 * w_d.shape[0] // (b * n_s)
    assert gu_rows % BF16_SUBLANES == 0 and d_rows % BF16_SUBLANES == 0
    gu_slab = pl.BlockSpec((gu_rows, w_gu.shape[1]), lambda i, j: (i * n_s + j, 0))
    d_slab = pl.BlockSpec((d_rows, w_d.shape[1]), lambda i, j: ((i * n_s + j) // 2, 0))
    tile = pl.BlockSpec((None, ts, d), lambda i, j: (i, j, 0))
    kv_scratch = pltpu.VMEM((BLOCK + ts, N_KV_HEADS * LANES), BF16)
    return pl.pallas_call(
        _mixer_kernel,
        grid=(b, n_s),
        in_specs=[
            pl.BlockSpec(memory_space=pltpu.SMEM),
            tile,
            _resident((1, d)),
            pl.BlockSpec((3, ts, LANES), lambda i, j: (0, j, 0)),
            pl.BlockSpec((None, CONV_K, d), lambda i, j: (layer, 0, 0),
                         pipeline_mode=pl.Buffered(1)),
            pl.BlockSpec(memory_space=pl.ANY),
            pl.BlockSpec(memory_space=pl.ANY),
            pl.BlockSpec(memory_space=pl.ANY),
            pl.BlockSpec(memory_space=pl.ANY),
            gu_slab,
            d_slab,
        ],
        out_specs=[tile, gu_slab, d_slab],
        out_shape=[jax.ShapeDtypeStruct(x.shape, x.dtype),
                   jax.ShapeDtypeStruct(w_gu.shape, BF16),
                   jax.ShapeDtypeStruct(w_d.shape, BF16)],
        scratch_shapes=[
            pltpu.VMEM(w_in.shape, BF16),
            pltpu.VMEM(w_co.shape, BF16),
            pltpu.VMEM(w_ao.shape, BF16),
            pltpu.VMEM(w_o.shape, BF16),
            pltpu.VMEM((STAGE_SLOTS, d, STAGE_COLS), F32),
            pltpu.SemaphoreType.DMA((STAGE_SLOTS,)),
            pltpu.VMEM((SUBLANES + ts, d), F32),
            pltpu.VMEM((ts, D_ATTN), BF16),
            kv_scratch, kv_scratch, kv_scratch, kv_scratch,
            pltpu.VMEM((ts, D_ATTN), BF16),
        ],
        compiler_params=pltpu.CompilerParams(
            dimension_semantics=("arbitrary", "arbitrary"),
            vmem_limit_bytes=VMEM_LIMIT_BYTES),
        name="mixer",
    )(sinks, x, g, rope, conv_w, w_in, w_co, w_ao, w_o, w_gu, w_d)


def _ffn(x, g, g_final, w_gu, w_d, final_norm):
    b, s, d = x.shape
    ts = FFN_SEQ_TILE
    tile = pl.BlockSpec((None, ts, d), lambda i, j: (i, j, 0))
    return pl.pallas_call(
        functools.partial(_ffn_kernel, final_norm=final_norm),
        grid=(b, s // ts),
        in_specs=[tile, _resident((1, d)), _resident((1, d)),
                  _resident(w_gu.shape), _resident(w_d.shape)],
        out_specs=tile,
        out_shape=jax.ShapeDtypeStruct(x.shape, x.dtype),
        scratch_shapes=[pltpu.VMEM((ts, D_FF), BF16)],
        compiler_params=pltpu.CompilerParams(
            dimension_semantics=("arbitrary", "arbitrary"),
            vmem_limit_bytes=VMEM_LIMIT_BYTES),
        name="ffn",
    )(x, g, g_final, w_gu, w_d)


def kernel(x, g_mix, w_in, conv_w, attn_sinks, w_conv_out, w_attn_out, w_o,
           g_ffn, w_gate_up, w_down, g_final):
    b, s, d = x.shape
    depth = w_in.shape[0]
    assert d == D_MODEL and s % SEQ_TILE == 0 and s % FFN_SEQ_TILE == 0
    assert w_in.shape[-1] == N_IN and w_gate_up.shape[-1] == 2 * D_FF
    assert N_IN % STAGE_COLS == 0 and d % STAGE_COLS == 0
    rope = _rope_tables(s)
    g_fin = g_final.reshape(1, d)
    for l in range(depth):
        x, w_gu, w_d = _mixer(
            x, g_mix[l].reshape(1, d), rope, conv_w, attn_sinks[l], l,
            w_in[l], w_conv_out[l], w_attn_out[l], w_o[l],
            w_gate_up[l], w_down[l])
        x = _ffn(x, g_ffn[l].reshape(1, d), g_fin, w_gu, w_d, final_norm=(l == depth - 1))
    return x
```
